```python
import math
import jax
import jax.numpy as jnp
from jax import lax
import numpy as np

D_MODEL = 2048
BATCH = 8
SEQ = 4096
DEPTH = 4

HEAD_DIM = 128
MIX_WIDTH = D_MODEL
N_HEADS_TOTAL = MIX_WIDTH // HEAD_DIM
N_HEADS_SB = N_HEADS_TOTAL // 2
N_HEADS_DIL = N_HEADS_TOTAL - N_HEADS_SB
SB_WIDTH = N_HEADS_SB * HEAD_DIM
DIL_WIDTH = N_HEADS_DIL * HEAD_DIM
IN_WIDTH = 3 * (SB_WIDTH + DIL_WIDTH)
IN_SPLITS = (SB_WIDTH, 2 * SB_WIDTH, 3 * SB_WIDTH, 3 * SB_WIDTH + DIL_WIDTH, 3 * SB_WIDTH + 2 * DIL_WIDTH)
D_FF = -(-8 * D_MODEL // (3 * 256)) * 256
PLI_DIM = 256
BLOCK = 128
DIL_GROUPS = ((128, 1), (512, 4), (2048, 16))
NUM_BUCKETS = 32
MAX_DISTANCE = 2048
RMS_EPS = 1e-6

kernel_name = "hybrid_stickbreak_dilated_sandwich_trunk"


def rms_norm(x, gain):
    xf = x.astype(jnp.float32)
    y = xf * lax.rsqrt(jnp.mean(xf * xf, axis=-1, keepdims=True) + RMS_EPS)
    return (y * gain.astype(jnp.float32)).astype(x.dtype)


def t5_bucket(dist):
    max_exact = NUM_BUCKETS // 2
    d = jnp.maximum(dist, 1).astype(jnp.float32)
    large = max_exact + (jnp.log(d / max_exact) / math.log(MAX_DISTANCE / max_exact)
                         * (NUM_BUCKETS - max_exact)).astype(jnp.int32)
    large = jnp.minimum(large, NUM_BUCKETS - 1)
    return jnp.where(dist < max_exact, dist, large)


def stick_breaking_attention(q, k, v):
    b, s, h, e = q.shape
    nblk = s // BLOCK
    scale = 1.0 / math.sqrt(e)
    kt = k.transpose(0, 2, 1, 3)
    vt = v.transpose(0, 2, 1, 3).astype(jnp.float32)
    qb = q.reshape(b, nblk, BLOCK, h, e).transpose(1, 0, 3, 2, 4)
    starts = jnp.arange(nblk, dtype=jnp.int32) * BLOCK
    key_pos = jnp.arange(s, dtype=jnp.int32)

    def one_block(args):
        qi, t0 = args
        z = jnp.einsum('bhqe,bhke->bhqk', qi, kt).astype(jnp.float32) * scale
        q_pos = t0 + jnp.arange(BLOCK, dtype=jnp.int32)
        earlier = key_pos[None, :] < q_pos[:, None]
        log_keep = jnp.where(earlier, jax.nn.log_sigmoid(-z), 0.0)
        later = lax.cumsum(log_keep, axis=3, reverse=True) - log_keep
        a = jnp.where(earlier, jnp.exp(jax.nn.log_sigmoid(z) + later), 0.0)
        return jnp.einsum('bhqk,bhke->bhqe', a, vt)

    o = lax.map(one_block, (qb, starts))
    return o.transpose(1, 0, 3, 2, 4).reshape(b, s, h, e).astype(q.dtype)


def dilated_branch(q, k, v, rel_bias, window, dilation):
    b, s, h, e = q.shape
    span = dilation * BLOCK
    s_pad = -(-s // span) * span
    nb = s_pad // span
    n_back = window // dilation
    scale = 1.0 / math.sqrt(e)

    def to_sub(t):
        t = jnp.pad(t, ((0, 0), (0, s_pad - s), (0, 0), (0, 0)))
        return t.reshape(b, nb, BLOCK, dilation, h, e).transpose(0, 3, 4, 1, 2, 5)

    def with_prev(t):
        prev = jnp.pad(t[:, :, :, :-1], ((0, 0), (0, 0), (0, 0), (1, 0), (0, 0), (0, 0)))
        return jnp.concatenate([prev, t], axis=4)

    qs = to_sub(q)
    kk = with_prev(to_sub(k))
    vv = with_prev(to_sub(v)).astype(jnp.float32)
    logits = jnp.einsum('brhnqe,brhnke->brhnqk', qs, kk).astype(jnp.float32) * scale

    qi = jnp.arange(BLOCK, dtype=jnp.int32)[:, None]
    ki = jnp.arange(2 * BLOCK, dtype=jnp.int32)[None, :]
    rel = BLOCK + qi - ki
    band = (rel >= 0) & (rel <= n_back)
    bias = rel_bias.astype(jnp.float32)[t5_bucket(jnp.maximum(rel, 0) * dilation)]
    bias = bias.transpose(2, 0, 1)[:, None]
    has_prev = (jnp.arange(nb)[:, None, None] > 0) | (ki[None] >= BLOCK)
    valid = band[None] & has_prev

    logits = jnp.where(valid, logits + bias, -jnp.inf)
    m = jnp.max(logits, axis=-1, keepdims=True)
    p = jnp.exp(logits - m)
    denom = jnp.sum(p, axis=-1, keepdims=True)
    o = jnp.einsum('brhnqk,brhnke->brhnqe', p, vv) / denom
    lse = m + jnp.log(denom)

    def from_sub(t):
        return t.transpose(0, 3, 4, 1, 2, 5).reshape(b, s_pad, h, t.shape[-1])[:, :s]

    return from_sub(o), from_sub(lse)[..., 0]


def dilated_mixture(q, k, v, rel_bias):
    outs, lses = [], []
    for window, dilation in DIL_GROUPS:
        o, l = dilated_branch(q, k, v, rel_bias, window, dilation)
        outs.append(o)
        lses.append(l)
    w = jax.nn.softmax(jnp.stack(lses, axis=0), axis=0)
    o = jnp.sum(w[..., None] * jnp.stack(outs, axis=0), axis=0)
    return o.astype(q.dtype)


def _fwd_setup_inputs(seed: int = 0) -> dict:
    key = jax.random.key(seed)
    ks = jax.random.split(key, 15)

    def normal(k, shape, scale):
        return jax.random.normal(k, shape, jnp.float32) * scale

    def gain(k, shape):
        return 1.0 + normal(k, shape, 0.01)

    return {
        "x": normal(ks[0], (BATCH, SEQ, D_MODEL), 1.0),
        "p": normal(ks[1], (DEPTH, BATCH, SEQ, PLI_DIM), 1.0),
        "ln_mix_pre": gain(ks[2], (DEPTH, D_MODEL)),
        "w_in": normal(ks[3], (DEPTH, D_MODEL, IN_WIDTH), D_MODEL ** -0.5),
        "ln_head": gain(ks[4], (DEPTH, MIX_WIDTH)),
        "w_out": normal(ks[5], (DEPTH, MIX_WIDTH, D_MODEL), MIX_WIDTH ** -0.5),
        "ln_mix_post": gain(ks[6], (DEPTH, D_MODEL)),
        "rel_bias": normal(ks[7], (NUM_BUCKETS, N_HEADS_DIL), 0.5),
        "ln_ffn_pre": gain(ks[8], (DEPTH, D_MODEL)),
        "w_gate_up": normal(ks[9], (DEPTH, D_MODEL, 2 * D_FF), D_MODEL ** -0.5),
        "w_down": normal(ks[10], (DEPTH, D_FF, D_MODEL), D_FF ** -0.5),
        "ln_ffn_post": gain(ks[11], (DEPTH, D_MODEL)),
        "ln_pli": gain(ks[12], (DEPTH, D_MODEL)),
        "w_pli_gate": normal(ks[13], (DEPTH, D_MODEL, D_MODEL), D_MODEL ** -0.5),
        "w_pli_proj": normal(ks[14], (DEPTH, PLI_DIM, D_MODEL), PLI_DIM ** -0.5),
    }


def _fwd_reference(x, p, ln_mix_pre, w_in, ln_head, w_out, ln_mix_post, rel_bias,
              ln_ffn_pre, w_gate_up, w_down, ln_ffn_post, ln_pli, w_pli_gate, w_pli_proj):
    b, s, _ = x.shape
    for i in range(DEPTH):
        h = rms_norm(x, ln_mix_pre[i])
        proj = h @ w_in[i]
        q_sb, k_sb, v_sb, q_dl, k_dl, v_dl = jnp.split(proj, list(IN_SPLITS), axis=-1)
        sb_shape = (b, s, N_HEADS_SB, HEAD_DIM)
        dl_shape = (b, s, N_HEADS_DIL, HEAD_DIM)
        o_sb = stick_breaking_attention(q_sb.reshape(sb_shape), k_sb.reshape(sb_shape), v_sb.reshape(sb_shape))
        o_dl = dilated_mixture(q_dl.reshape(dl_shape), k_dl.reshape(dl_shape), v_dl.reshape(dl_shape), rel_bias)
        o = jnp.concatenate([o_sb, o_dl], axis=2)
        o = rms_norm(o, ln_head[i].reshape(N_HEADS_TOTAL, HEAD_DIM)).reshape(b, s, MIX_WIDTH)
        x = x + rms_norm(o @ w_out[i], ln_mix_post[i])
        h = rms_norm(x, ln_ffn_pre[i])
        g, u = jnp.split(h @ w_gate_up[i], 2, axis=-1)
        f = (jax.nn.silu(g) * u) @ w_down[i]
        x = x + rms_norm(f, ln_ffn_post[i])
        gate = jax.nn.sigmoid(rms_norm(x, ln_pli[i]) @ w_pli_gate[i])
        x = x + gate * (p[i] @ w_pli_proj[i])
    return x


import jax as _jax
import jax.numpy as _jnp

TWIN_FORMAT = 'train_step'
FWD_PARAMS = ['x', 'p', 'ln_mix_pre', 'w_in', 'ln_head', 'w_out', 'ln_mix_post', 'rel_bias', 'ln_ffn_pre', 'w_gate_up', 'w_down', 'ln_ffn_post', 'ln_pli', 'w_pli_gate', 'w_pli_proj']
TWIN_WEIGHTS = ['ln_mix_pre', 'w_in', 'ln_head', 'w_out', 'ln_mix_post', 'rel_bias', 'ln_ffn_pre', 'w_gate_up', 'w_down', 'ln_ffn_post', 'ln_pli', 'w_pli_gate', 'w_pli_proj']
TWIN_DIFF_INPUT = 'x'
TWIN_INPUTS = ['x', 'p', 'ln_mix_pre', 'w_in', 'ln_head', 'w_out', 'ln_mix_post', 'rel_bias', 'ln_ffn_pre', 'w_gate_up', 'w_down', 'ln_ffn_post', 'ln_pli', 'w_pli_gate', 'w_pli_proj', 'loss_target', 'm_ln_mix_pre', 'm_w_in', 'm_ln_head', 'm_w_out', 'm_ln_mix_post', 'm_rel_bias', 'm_ln_ffn_pre', 'm_w_gate_up', 'm_w_down', 'm_ln_ffn_post', 'm_ln_pli', 'm_w_pli_gate', 'm_w_pli_proj', 'v_ln_mix_pre', 'v_w_in', 'v_ln_head', 'v_w_out', 'v_ln_mix_post', 'v_rel_bias', 'v_ln_ffn_pre', 'v_w_gate_up', 'v_w_down', 'v_ln_ffn_post', 'v_ln_pli', 'v_w_pli_gate', 'v_w_pli_proj']
TWIN_OUTPUTS = ['loss', 'grad_x', 'grad_ln_mix_pre', 'grad_w_in', 'grad_ln_head', 'grad_w_out', 'grad_ln_mix_post', 'grad_rel_bias', 'grad_ln_ffn_pre', 'grad_w_gate_up', 'grad_w_down', 'grad_ln_ffn_post', 'grad_ln_pli', 'grad_w_pli_gate', 'grad_w_pli_proj', 'delta_ln_mix_pre', 'delta_w_in', 'delta_ln_head', 'delta_w_out', 'delta_ln_mix_post', 'delta_rel_bias', 'delta_ln_ffn_pre', 'delta_w_gate_up', 'delta_w_down', 'delta_ln_ffn_post', 'delta_ln_pli', 'delta_w_pli_gate', 'delta_w_pli_proj', 'new_m_ln_mix_pre', 'new_m_w_in', 'new_m_ln_head', 'new_m_w_out', 'new_m_ln_mix_post', 'new_m_rel_bias', 'new_m_ln_ffn_pre', 'new_m_w_gate_up', 'new_m_w_down', 'new_m_ln_ffn_post', 'new_m_ln_pli', 'new_m_w_pli_gate', 'new_m_w_pli_proj', 'new_v_ln_mix_pre', 'new_v_w_in', 'new_v_ln_head', 'new_v_w_out', 'new_v_ln_mix_post', 'new_v_rel_bias', 'new_v_ln_ffn_pre', 'new_v_w_gate_up', 'new_v_w_down', 'new_v_ln_ffn_post', 'new_v_ln_pli', 'new_v_w_pli_gate', 'new_v_w_pli_proj']
TWIN_LEAF_KINDS = {'loss': 'loss', 'grad_x': 'grad_x', 'grad_ln_mix_pre': 'grad_w', 'grad_w_in': 'grad_w', 'grad_ln_head': 'grad_w', 'grad_w_out': 'grad_w', 'grad_ln_mix_post': 'grad_w', 'grad_rel_bias': 'grad_w', 'grad_ln_ffn_pre': 'grad_w', 'grad_w_gate_up': 'grad_w', 'grad_w_down': 'grad_w', 'grad_ln_ffn_post': 'grad_w', 'grad_ln_pli': 'grad_w', 'grad_w_pli_gate': 'grad_w', 'grad_w_pli_proj': 'grad_w', 'delta_ln_mix_pre': 'delta_w', 'delta_w_in': 'delta_w', 'delta_ln_head': 'delta_w', 'delta_w_out': 'delta_w', 'delta_ln_mix_post': 'delta_w', 'delta_rel_bias': 'delta_w', 'delta_ln_ffn_pre': 'delta_w', 'delta_w_gate_up': 'delta_w', 'delta_w_down': 'delta_w', 'delta_ln_ffn_post': 'delta_w', 'delta_ln_pli': 'delta_w', 'delta_w_pli_gate': 'delta_w', 'delta_w_pli_proj': 'delta_w', 'new_m_ln_mix_pre': 'new_m', 'new_m_w_in': 'new_m', 'new_m_ln_head': 'new_m', 'new_m_w_out': 'new_m', 'new_m_ln_mix_post': 'new_m', 'new_m_rel_bias': 'new_m', 'new_m_ln_ffn_pre': 'new_m', 'new_m_w_gate_up': 'new_m', 'new_m_w_down': 'new_m', 'new_m_ln_ffn_post': 'new_m', 'new_m_ln_pli': 'new_m', 'new_m_w_pli_gate': 'new_m', 'new_m_w_pli_proj': 'new_m', 'new_v_ln_mix_pre': 'new_v', 'new_v_w_in': 'new_v', 'new_v_ln_head': 'new_v', 'new_v_w_out': 'new_v', 'new_v_ln_mix_post': 'new_v', 'new_v_rel_bias': 'new_v', 'new_v_ln_ffn_pre': 'new_v', 'new_v_w_gate_up': 'new_v', 'new_v_w_down': 'new_v', 'new_v_ln_ffn_post': 'new_v', 'new_v_ln_pli': 'new_v', 'new_v_w_pli_gate': 'new_v', 'new_v_w_pli_proj': 'new_v'}


def _forward(args):
    return _fwd_reference(*[args[k] for k in FWD_PARAMS])


def _output_shape():
    out = _jax.eval_shape(lambda: _forward(_fwd_setup_inputs(0)))
    return out.shape, out.dtype

N_MICROBATCH = 1
ADAM_LR = 0.001
ADAM_B1 = 0.9
ADAM_B2 = 0.999
ADAM_EPS = 1e-08
ADAM_WD = 0.01
ADAM_STEP = 10
PER_EXAMPLE_BATCH_AXIS = {'x': 0, 'p': 1, 'loss_target': 0}
SHARED_INPUTS = []
_WEIGHT_DTYPES = {'ln_mix_pre': _jnp.float32, 'w_in': _jnp.float32, 'ln_head': _jnp.float32, 'w_out': _jnp.float32, 'ln_mix_post': _jnp.float32, 'rel_bias': _jnp.float32, 'ln_ffn_pre': _jnp.float32, 'w_gate_up': _jnp.float32, 'w_down': _jnp.float32, 'ln_ffn_post': _jnp.float32, 'ln_pli': _jnp.float32, 'w_pli_gate': _jnp.float32, 'w_pli_proj': _jnp.float32}
MOMENT_SCALE = {'ln_mix_pre': 2.662479e+00, 'w_in': 1.556147e+00, 'ln_head': 2.726510e+00, 'w_out': 2.663728e+00, 'ln_mix_post': 1.555888e+01, 'rel_bias': 3.439915e+00, 'ln_ffn_pre': 1.041887e+00, 'w_gate_up': 4.255762e-01, 'w_down': 7.945344e-01, 'ln_ffn_post': 1.567422e+01, 'ln_pli': 3.973654e-01, 'w_pli_gate': 1.368670e-01, 'w_pli_proj': 3.054155e-01}


def _to_microbatches(a, axis):
    t = _jnp.moveaxis(a, axis, 0)
    t = t.reshape((N_MICROBATCH, t.shape[0] // N_MICROBATCH) + t.shape[1:])
    return _jnp.moveaxis(t, 1, axis + 1)


def setup_inputs(seed: int = 0) -> dict:
    inp = _fwd_setup_inputs(seed)
    key = _jax.random.fold_in(_jax.random.key(seed), 7919)
    shape, _ = _output_shape()
    out = dict(inp)
    out["loss_target"] = _jax.random.normal(_jax.random.fold_in(key, 0), shape, _jnp.float32)
    for i, name in enumerate(TWIN_WEIGHTS):
        w = inp[name].astype(_jnp.float32)
        if MOMENT_SCALE is None:
            s = _jnp.sqrt(_jnp.mean(_jnp.square(w)) + 1e-30)
        else:
            s = MOMENT_SCALE[name]
        km, kv = _jax.random.split(_jax.random.fold_in(key, i + 1))
        out[name] = w
        out["m_" + name] = s * _jax.random.normal(km, w.shape, _jnp.float32)
        out["v_" + name] = (s * s) * _jax.random.uniform(kv, w.shape, _jnp.float32, 0.5, 1.5)
    if N_MICROBATCH > 1:
        for name, axis in PER_EXAMPLE_BATCH_AXIS.items():
            out[name] = _to_microbatches(out[name], axis)
    return {'x': out['x'], 'p': out['p'], 'ln_mix_pre': out['ln_mix_pre'], 'w_in': out['w_in'], 'ln_head': out['ln_head'], 'w_out': out['w_out'], 'ln_mix_post': out['ln_mix_post'], 'rel_bias': out['rel_bias'], 'ln_ffn_pre': out['ln_ffn_pre'], 'w_gate_up': out['w_gate_up'], 'w_down': out['w_down'], 'ln_ffn_post': out['ln_ffn_post'], 'ln_pli': out['ln_pli'], 'w_pli_gate': out['w_pli_gate'], 'w_pli_proj': out['w_pli_proj'], 'loss_target': out['loss_target'], 'm_ln_mix_pre': out['m_ln_mix_pre'], 'm_w_in': out['m_w_in'], 'm_ln_head': out['m_ln_head'], 'm_w_out': out['m_w_out'], 'm_ln_mix_post': out['m_ln_mix_post'], 'm_rel_bias': out['m_rel_bias'], 'm_ln_ffn_pre': out['m_ln_ffn_pre'], 'm_w_gate_up': out['m_w_gate_up'], 'm_w_down': out['m_w_down'], 'm_ln_ffn_post': out['m_ln_ffn_post'], 'm_ln_pli': out['m_ln_pli'], 'm_w_pli_gate': out['m_w_pli_gate'], 'm_w_pli_proj': out['m_w_pli_proj'], 'v_ln_mix_pre': out['v_ln_mix_pre'], 'v_w_in': out['v_w_in'], 'v_ln_head': out['v_ln_head'], 'v_w_out': out['v_w_out'], 'v_ln_mix_post': out['v_ln_mix_post'], 'v_rel_bias': out['v_rel_bias'], 'v_ln_ffn_pre': out['v_ln_ffn_pre'], 'v_w_gate_up': out['v_w_gate_up'], 'v_w_down': out['v_w_down'], 'v_ln_ffn_post': out['v_ln_ffn_post'], 'v_ln_pli': out['v_ln_pli'], 'v_w_pli_gate': out['v_w_pli_gate'], 'v_w_pli_proj': out['v_w_pli_proj']}


def _loss(weights, diff, rest, loss_target):
    with _jax.named_scope("forward"):
        args = {**rest, TWIN_DIFF_INPUT: diff, **{k: w.astype(_WEIGHT_DTYPES[k]) for k, w in weights.items()}}
        y = _forward(args)
    with _jax.named_scope("loss_head"):
        err = _jnp.square(y.astype(_jnp.float32) - loss_target)
        return 0.5 * _jnp.sum(_jnp.mean(err, axis=-1)) if err.ndim else 0.5 * err


def _adamw(w, g, m, v):
    m = ADAM_B1 * m + (1.0 - ADAM_B1) * g
    v = ADAM_B2 * v + (1.0 - ADAM_B2) * _jnp.square(g)
    m_hat = m / (1.0 - ADAM_B1 ** ADAM_STEP)
    v_hat = v / (1.0 - ADAM_B2 ** ADAM_STEP)
    delta = -ADAM_LR * (m_hat / (_jnp.sqrt(v_hat) + ADAM_EPS) + ADAM_WD * w)
    return delta, m, v


def reference(x, p, ln_mix_pre, w_in, ln_head, w_out, ln_mix_post, rel_bias, ln_ffn_pre, w_gate_up, w_down, ln_ffn_post, ln_pli, w_pli_gate, w_pli_proj, loss_target, m_ln_mix_pre, m_w_in, m_ln_head, m_w_out, m_ln_mix_post, m_rel_bias, m_ln_ffn_pre, m_w_gate_up, m_w_down, m_ln_ffn_post, m_ln_pli, m_w_pli_gate, m_w_pli_proj, v_ln_mix_pre, v_w_in, v_ln_head, v_w_out, v_ln_mix_post, v_rel_bias, v_ln_ffn_pre, v_w_gate_up, v_w_down, v_ln_ffn_post, v_ln_pli, v_w_pli_gate, v_w_pli_proj):
    given = dict(x=x, p=p, ln_mix_pre=ln_mix_pre, w_in=w_in, ln_head=ln_head, w_out=w_out, ln_mix_post=ln_mix_post, rel_bias=rel_bias, ln_ffn_pre=ln_ffn_pre, w_gate_up=w_gate_up, w_down=w_down, ln_ffn_post=ln_ffn_post, ln_pli=ln_pli, w_pli_gate=w_pli_gate, w_pli_proj=w_pli_proj, loss_target=loss_target, m_ln_mix_pre=m_ln_mix_pre, m_w_in=m_w_in, m_ln_head=m_ln_head, m_w_out=m_w_out, m_ln_mix_post=m_ln_mix_post, m_rel_bias=m_rel_bias, m_ln_ffn_pre=m_ln_ffn_pre, m_w_gate_up=m_w_gate_up, m_w_down=m_w_down, m_ln_ffn_post=m_ln_ffn_post, m_ln_pli=m_ln_pli, m_w_pli_gate=m_w_pli_gate, m_w_pli_proj=m_w_pli_proj, v_ln_mix_pre=v_ln_mix_pre, v_w_in=v_w_in, v_ln_head=v_ln_head, v_w_out=v_w_out, v_ln_mix_post=v_ln_mix_post, v_rel_bias=v_rel_bias, v_ln_ffn_pre=v_ln_ffn_pre, v_w_gate_up=v_w_gate_up, v_w_down=v_w_down, v_ln_ffn_post=v_ln_ffn_post, v_ln_pli=v_ln_pli, v_w_pli_gate=v_w_pli_gate, v_w_pli_proj=v_w_pli_proj)
    weights = {n: given[n] for n in TWIN_WEIGHTS}
    shared = {n: given[n] for n in SHARED_INPUTS}
    per_example = {n: given[n] for n in ['x', 'p']}
    grad_fn = _jax.value_and_grad(_loss, argnums=(0, 1))

    def one_microbatch(ex, loss_target):
        ex = dict(ex)
        diff = ex.pop(TWIN_DIFF_INPUT)
        return grad_fn(weights, diff, {**shared, **ex}, loss_target)

    if N_MICROBATCH == 1:
        loss, (grad_w, grad_x) = one_microbatch(per_example, given["loss_target"])
    else:
        def body(carry, xs):
            loss_sum, grad_sum = carry
            l_k, (gw_k, gx_k) = one_microbatch(xs[0], xs[1])
            with _jax.named_scope("update"):
                return (loss_sum + l_k, _jax.tree.map(_jnp.add, grad_sum, gw_k)), gx_k

        init = (_jnp.zeros((), _jnp.float32), _jax.tree.map(_jnp.zeros_like, weights))
        (loss, grad_w), grad_x = _jax.lax.scan(body, init, (per_example, given["loss_target"]))
    with _jax.named_scope("update"):
        delta_w, new_m, new_v = {}, {}, {}
        for n in TWIN_WEIGHTS:
            delta_w[n], new_m[n], new_v[n] = _adamw(weights[n], grad_w[n], given["m_" + n], given["v_" + n])
    return (loss, grad_x, *[grad_w[n] for n in TWIN_WEIGHTS], *[delta_w[n] for n in TWIN_WEIGHTS],
            *[new_m[n] for n in TWIN_WEIGHTS], *[new_v[n] for n in TWIN_WEIGHTS])
```

```python
import functools
import math

import jax
import jax.numpy as jnp
from jax import lax
from jax.experimental import pallas as pl
from jax.experimental.pallas import tpu as pltpu

F32 = jnp.float32
BF16 = jnp.bfloat16

HEAD_DIM = 128
RMS_EPS = 1e-6
DILATIONS = (1, 4, 16)
NUM_BUCKETS = 32
MAX_DISTANCE = 2048
NEG = -1e30
N_CHIPS = 4
N_DEV = 8

ADAM_LR = 0.001
ADAM_B1 = 0.9
ADAM_B2 = 0.999
ADAM_EPS = 1e-08
ADAM_WD = 0.01
ADAM_STEP = 10

V7X_VMEM_LIMIT = 48 * 1024 * 1024
LANE = 128

NN = (((1,), (0,)), ((), ()))
NT = (((1,), (1,)), ((), ()))
TN = (((0,), (0,)), ((), ()))
MESH = pl.DeviceIdType.MESH

BIG = ("w_in", "w_out", "w_gate_up", "w_down", "w_pli_gate", "w_pli_proj")
COL_SHARDED = {"w_in": True, "w_out": False, "w_gate_up": True, "w_down": False,
               "w_pli_gate": False, "w_pli_proj": True}
SMALL = ("ln_mix_pre", "ln_head", "ln_mix_post", "ln_ffn_pre", "ln_ffn_post", "ln_pli")
WEIGHTS = ("ln_mix_pre", "w_in", "ln_head", "w_out", "ln_mix_post", "rel_bias", "ln_ffn_pre",
           "w_gate_up", "w_down", "ln_ffn_post", "ln_pli", "w_pli_gate", "w_pli_proj")


def _tile(n, cap):
    t = min(n, cap) // LANE * LANE
    while t >= LANE:
        if n % t == 0:
            return t
        t -= LANE
    return n


def _params(sem=None):
    return pltpu.CompilerParams(dimension_semantics=sem, vmem_limit_bytes=V7X_VMEM_LIMIT)


def _rowwise(fn, rows, vecs, outs, sums, name, tr):
    s = rows[0].shape[0]
    nr, nv, no, ns = len(rows), len(vecs), len(outs), len(sums)

    def body(*refs):
        ins = [r[...] for r in refs[:nr + nv]]
        res = fn(*ins)
        out_refs = refs[nr + nv:nr + nv + no]
        sum_refs = refs[nr + nv + no:]
        for o_ref, val in zip(out_refs, res[:no]):
            o_ref[...] = val.astype(o_ref.dtype)
        if ns:
            @pl.when(pl.program_id(0) == 0)
            def _():
                for s_ref in sum_refs:
                    s_ref[...] = jnp.zeros_like(s_ref)
            for s_ref, val in zip(sum_refs, res[no:]):
                s_ref[...] += jnp.sum(val, axis=0, keepdims=True)

    in_specs = [pl.BlockSpec((tr, r.shape[1]), lambda i: (i, 0)) for r in rows]
    in_specs += [pl.BlockSpec(v.shape, lambda i: (0, 0)) for v in vecs]
    out_specs = [pl.BlockSpec((tr, c), lambda i: (i, 0)) for c, _ in outs]
    out_specs += [pl.BlockSpec((1, c), lambda i: (0, 0)) for c in sums]
    out_shape = [jax.ShapeDtypeStruct((s, c), dt) for c, dt in outs]
    out_shape += [jax.ShapeDtypeStruct((1, c), F32) for c in sums]
    return pl.pallas_call(
        body, grid=(s // tr,), in_specs=in_specs, out_specs=out_specs, out_shape=out_shape,
        compiler_params=_params(("arbitrary",) if ns else ("parallel",)), name=name,
    )(*rows, *vecs)


def _rms_r(x):
    return lax.rsqrt(jnp.mean(x * x, axis=-1, keepdims=True) + RMS_EPS)


def _rms_bwd(x, g, dy):
    r = _rms_r(x)
    u = dy * g
    dx = r * (u - x * (r * r) * jnp.mean(u * x, axis=-1, keepdims=True))
    return dx, dy * x * r


def _sigmoid(z):
    return 1.0 / (1.0 + jnp.exp(-z))


def _norm_in(x, g, tr):
    d = x.shape[1]
    return _rowwise(lambda x, g: (x * _rms_r(x) * g,), [x], [g], [(d, BF16)], [], "norm_in", tr)[0]


def _norm_in_bwd(dx_res, dh, x, g, tr):
    d = x.shape[1]

    def fn(dx_res, dh, x, g):
        dx, dg = _rms_bwd(x, g, dh)
        return dx_res + dx, dg
    return _rowwise(fn, [dx_res, dh, x], [g], [(d, F32)], [d], "norm_in_bwd", tr)


def _headnorm(o_sb, o_dl, g, tr):
    d = g.shape[1]

    def fn(o_sb, o_dl, g):
        o = jnp.concatenate([o_sb, o_dl], axis=1)
        parts = []
        for h in range(d // HEAD_DIM):
            sl = slice(h * HEAD_DIM, (h + 1) * HEAD_DIM)
            oh = o[:, sl]
            parts.append(oh * _rms_r(oh) * g[:, sl])
        return (jnp.concatenate(parts, axis=1),)
    return _rowwise(fn, [o_sb, o_dl], [g], [(d, BF16)], [], "headnorm", tr)[0]


def _headnorm_bwd(don, o_sb, o_dl, g, tr):
    d = g.shape[1]
    n_sb = o_sb.shape[1]

    def fn(don, o_sb, o_dl, g):
        o = jnp.concatenate([o_sb, o_dl], axis=1)
        dos, dgs = [], []
        for h in range(d // HEAD_DIM):
            sl = slice(h * HEAD_DIM, (h + 1) * HEAD_DIM)
            dx, dg = _rms_bwd(o[:, sl], g[:, sl], don[:, sl])
            dos.append(dx)
            dgs.append(dg)
        do = jnp.concatenate(dos, axis=1)
        return do[:, :n_sb], do[:, n_sb:], jnp.concatenate(dgs, axis=1)
    return _rowwise(fn, [don, o_sb, o_dl], [g], [(n_sb, F32), (d - n_sb, F32)], [d], "headnorm_bwd", tr)


def _res_norm(x, y, g_post, g_pre, name, tr):
    d = x.shape[1]

    def fn(x, y, g_post, g_pre):
        x2 = x + y * _rms_r(y) * g_post
        return x2, x2 * _rms_r(x2) * g_pre
    return _rowwise(fn, [x, y], [g_post, g_pre], [(d, F32), (d, BF16)], [], name, tr)


def _res_norm_bwd(dx_res, dh, x2, y, g_pre, g_post, name, tr):
    d = x2.shape[1]

    def fn(dx_res, dh, x2, y, g_pre, g_post):
        dxa, dg_pre = _rms_bwd(x2, g_pre, dh)
        dx2 = dx_res + dxa
        dy, dg_post = _rms_bwd(y, g_post, dx2)
        return dx2, dy, dg_pre, dg_post
    return _rowwise(fn, [dx_res, dh, x2, y], [g_pre, g_post], [(d, F32), (d, BF16)], [d, d], name, tr)


def _swiglu(gu, tr):
    ff = gu.shape[1] // 2

    def fn(gu):
        g, u = gu[:, :ff], gu[:, ff:]
        return (g * _sigmoid(g) * u,)
    return _rowwise(fn, [gu], [], [(ff, BF16)], [], "swiglu", tr)[0]


def _swiglu_bwd(dact, gu, tr):
    ff = gu.shape[1] // 2

    def fn(dact, gu):
        g, u = gu[:, :ff], gu[:, ff:]
        sg = _sigmoid(g)
        dg = dact * u * sg * (1.0 + g * (1.0 - sg))
        du = dact * g * sg
        return (jnp.concatenate([dg, du], axis=1),)
    return _rowwise(fn, [dact, gu], [], [(2 * ff, BF16)], [], "swiglu_bwd", tr)[0]


def _pli_out(x3, gl, pp, g_next, tr):
    d = x3.shape[1]

    def fn(x3, gl, pp, g):
        x4 = x3 + _sigmoid(gl) * pp
        return x4, x4 * _rms_r(x4) * g
    return _rowwise(fn, [x3, gl, pp], [g_next], [(d, F32), (d, BF16)], [], "pli_out", tr)


def _loss_head(x3, gl, pp, target, tr):
    d = x3.shape[1]

    def fn(x3, gl, pp, t):
        err = x3 + _sigmoid(gl) * pp - t
        sq = err * err
        part = sq[:, :LANE]
        for k in range(1, d // LANE):
            part = part + sq[:, k * LANE:(k + 1) * LANE]
        return err * (1.0 / d), part
    return _rowwise(fn, [x3, gl, pp, target], [], [(d, F32)], [LANE], "loss_head", tr)


def _pli_bwd(dx, gl, pp, tr):
    d = dx.shape[1]

    def fn(dx, gl, pp):
        gate = _sigmoid(gl)
        return dx * gate, dx * pp * gate * (1.0 - gate)
    return _rowwise(fn, [dx, gl, pp], [], [(d, BF16), (d, BF16)], [], "pli_bwd", tr)


def _cast_bf16(w, name):
    depth, r, c = w.shape
    w2 = w.reshape(depth * r, c)
    tr = r // 2 if r * c * 4 > (4 << 20) else r
    while tr * c * 4 > (4 << 20) and tr % 32 == 0:
        tr //= 2
    out = _rowwise(lambda a: (a,), [w2], [], [(c, BF16)], [], "cast_" + name, tr)[0]
    return out.reshape(depth, r, c)


def _sum8(recv, name):
    _, h, c = recv.shape
    tr = h
    while tr * c * 2 * N_DEV > (8 << 20) and tr % 32 == 0:
        tr //= 2

    def body(r_ref, o_ref):
        acc = r_ref[0].astype(F32)
        for i in range(1, N_DEV):
            acc = acc + r_ref[i].astype(F32)
        o_ref[...] = acc

    return pl.pallas_call(
        body, grid=(h // tr,), in_specs=[pl.BlockSpec((N_DEV, tr, c), lambda i: (0, i, 0))],
        out_specs=pl.BlockSpec((tr, c), lambda i: (i, 0)), out_shape=jax.ShapeDtypeStruct((h, c), F32),
        compiler_params=_params(("parallel",)), name="sum8_" + name,
    )(recv)


def _adamw(w, g, m, v, name):
    shape = w.shape
    if w.ndim == 3:
        w, g, m, v = (a.reshape(shape[0] * shape[1], shape[2]) for a in (w, g, m, v))
    r, c = w.shape
    tr = r
    while tr * c * 4 > (1 << 20) and tr % 16 == 0:
        tr //= 2

    def body(w_ref, g_ref, m_ref, v_ref, d_ref, m2_ref, v2_ref):
        g = g_ref[...]
        m2 = ADAM_B1 * m_ref[...] + (1.0 - ADAM_B1) * g
        v2 = ADAM_B2 * v_ref[...] + (1.0 - ADAM_B2) * (g * g)
        m_hat = m2 / (1.0 - ADAM_B1 ** ADAM_STEP)
        v_hat = v2 / (1.0 - ADAM_B2 ** ADAM_STEP)
        d_ref[...] = -ADAM_LR * (m_hat / (jnp.sqrt(v_hat) + ADAM_EPS) + ADAM_WD * w_ref[...])
        m2_ref[...] = m2
        v2_ref[...] = v2

    spec = pl.BlockSpec((tr, c), lambda i: (i, 0))
    res = pl.pallas_call(
        body, grid=(r // tr,), in_specs=[spec] * 4, out_specs=[spec] * 3,
        out_shape=[jax.ShapeDtypeStruct((r, c), F32)] * 3,
        compiler_params=_params(("parallel",)), name="adamw_" + name,
    )(w, g, m, v)
    return tuple(a.reshape(shape) for a in res)


def _mm(a, b, grid, a_spec, b_spec, o_spec, o_shape, o_dtype, dims, acc_shape, name):
    nk = grid[2]

    def body(a_ref, b_ref, o_ref, acc_ref):
        k = pl.program_id(2)

        @pl.when(k == 0)
        def _():
            acc_ref[...] = jnp.zeros_like(acc_ref)

        acc_ref[...] += lax.dot_general(a_ref[...].astype(BF16), b_ref[...].astype(BF16), dims,
                                        preferred_element_type=F32)

        @pl.when(k == nk - 1)
        def _():
            o_ref[...] = acc_ref[...].astype(o_ref.dtype)

    return pl.pallas_call(
        body, grid=grid, in_specs=[a_spec, b_spec], out_specs=o_spec,
        out_shape=jax.ShapeDtypeStruct(o_shape, o_dtype),
        scratch_shapes=[pltpu.VMEM(acc_shape, F32)],
        compiler_params=_params(("parallel", "parallel", "arbitrary")), name=name,
    )(a, b)


def _mm_fwd(a, wg, col, name, tm=1024, tn_cap=1536, tk_cap=1024):
    m, k = a.shape
    ns, r, c = wg.shape
    tm = min(tm, m)
    if col:
        n, tn, tk = ns * c, _tile(c, tn_cap), _tile(k, tk_cap)
        per = c // tn
        b_spec = pl.BlockSpec((None, tk, tn), lambda j, i, kk: (j // per, kk, j % per))
    else:
        n, tn, tk = c, _tile(c, tn_cap), _tile(r, tk_cap)
        per = r // tk
        b_spec = pl.BlockSpec((None, tk, tn), lambda j, i, kk: (kk // per, kk % per, j))
    return _mm(a, wg, (n // tn, m // tm, k // tk), pl.BlockSpec((tm, tk), lambda j, i, kk: (i, kk)), b_spec,
               pl.BlockSpec((tm, tn), lambda j, i, kk: (i, j)), (m, n), F32, NN, (tm, tn), name)


def _mm_dgrad(dc, wg, col, name, tm=1024, to_cap=1536, tc_cap=1024):
    m, n = dc.shape
    ns, r, c = wg.shape
    tm = min(tm, m)
    if col:
        kout, to, tc = r, _tile(r, to_cap), _tile(c, tc_cap)
        per = c // tc
        b_spec = pl.BlockSpec((None, to, tc), lambda j, i, kk: (kk // per, j, kk % per))
    else:
        kout, to, tc = ns * r, _tile(r, to_cap), _tile(c, tc_cap)
        per = r // to
        b_spec = pl.BlockSpec((None, to, tc), lambda j, i, kk: (j // per, j % per, kk))
    return _mm(dc, wg, (kout // to, m // tm, n // tc), pl.BlockSpec((tm, tc), lambda j, i, kk: (i, kk)), b_spec,
               pl.BlockSpec((tm, to), lambda j, i, kk: (i, j)), (m, kout), F32, NT, (tm, to), name)


def _mm_wgrad(a, dc, col, name, ti_cap=1536, tn_cap=1536, tkm=1024):
    m, k = a.shape
    n = dc.shape[1]
    tkm = min(tkm, m)
    if col:
        r, c = k, n // N_CHIPS
        ti, tn = _tile(r, ti_cap), _tile(c, tn_cap)
        per = c // tn
        o_spec = pl.BlockSpec((None, ti, tn), lambda i, j, kk: (j // per, i, j % per))
    else:
        r, c = k // N_CHIPS, n
        ti, tn = _tile(r, ti_cap), _tile(c, tn_cap)
        per = r // ti
        o_spec = pl.BlockSpec((None, ti, tn), lambda i, j, kk: (i // per, i % per, j))
    return _mm(a, dc, (k // ti, n // tn, m // tkm), pl.BlockSpec((tkm, ti), lambda i, j, kk: (kk, i)),
               pl.BlockSpec((tkm, tn), lambda i, j, kk: (kk, j)), o_spec, (N_CHIPS, r, c), BF16, TN, (ti, tn), name)


def _log_keep(z):
    return -(jnp.maximum(z, 0.0) + jnp.log(1.0 + jnp.exp(-jnp.abs(z))))


def _split_dot(x, t):
    hi = x.astype(BF16)
    lo = (x - hi.astype(F32)).astype(BF16)
    return (lax.dot_general(hi, t, NN, preferred_element_type=F32)
            + lax.dot_general(lo, t, NN, preferred_element_type=F32))


def _sb_fwd(proj, n_sb, bq):
    s = proj.shape[0]
    scale = 1.0 / math.sqrt(HEAD_DIM)

    def body(q_ref, k_ref, v_ref, o_ref, lt_ref):
        i = pl.program_id(1)
        q = q_ref[...].astype(BF16)
        row = lax.broadcasted_iota(jnp.int32, (bq, bq), 0)
        col = lax.broadcasted_iota(jnp.int32, (bq, bq), 1)
        later_in_block = (row > col).astype(BF16)
        keep = col < row

        def block(j, carry, diagonal):
            c, acc = carry
            ks = pl.multiple_of(j * bq, bq)
            kb = k_ref[pl.ds(ks, bq), :].astype(BF16)
            vb = v_ref[pl.ds(ks, bq), :].astype(BF16)
            z = lax.dot_general(q, kb, NT, preferred_element_type=F32) * scale
            lk = _log_keep(z)
            if diagonal:
                lk = jnp.where(keep, lk, 0.0)
            e = z + lk + _split_dot(lk, later_in_block) + c
            if diagonal:
                e = jnp.where(keep, e, NEG)
            a = jnp.exp(e)
            acc = acc + lax.dot_general(a.astype(BF16), vb, NN, preferred_element_type=F32)
            return c + jnp.sum(lk, axis=1, keepdims=True), acc

        carry = block(i, (jnp.zeros((bq, 1), F32), jnp.zeros((bq, HEAD_DIM), F32)), True)
        carry = lax.fori_loop(0, i, lambda t, cr: block(i - 1 - t, cr, False), carry)
        o_ref[...] = carry[1]
        lt_ref[...] = jnp.broadcast_to(carry[0], (bq, HEAD_DIM))

    blk = pl.BlockSpec((bq, HEAD_DIM), lambda h, i: (i, h))
    shp = jax.ShapeDtypeStruct((s, n_sb * HEAD_DIM), F32)
    return pl.pallas_call(
        body, grid=(n_sb, s // bq),
        in_specs=[blk,
                  pl.BlockSpec((s, HEAD_DIM), lambda h, i: (0, n_sb + h)),
                  pl.BlockSpec((s, HEAD_DIM), lambda h, i: (0, 2 * n_sb + h))],
        out_specs=[blk, blk], out_shape=[shp, shp],
        compiler_params=_params(("parallel", "parallel")), name="sb_fwd",
    )(proj, proj, proj)


def _sb_bwd(proj, lt, do, n_sb, bq):
    s = proj.shape[0]
    nq = s // bq
    scale = 1.0 / math.sqrt(HEAD_DIM)

    def body(q_ref, k_ref, v_ref, lt_ref, do_ref, dq_ref, dk_ref, dv_ref, dk_acc, dv_acc):
        i = pl.program_id(1)

        @pl.when(i == 0)
        def _():
            dk_acc[...] = jnp.zeros_like(dk_acc)
            dv_acc[...] = jnp.zeros_like(dv_acc)

        q = q_ref[...].astype(BF16)
        do_b = do_ref[...].astype(BF16)
        ltot = jnp.max(lt_ref[...], axis=1, keepdims=True)
        row = lax.broadcasted_iota(jnp.int32, (bq, bq), 0)
        col = lax.broadcasted_iota(jnp.int32, (bq, bq), 1)
        upto_in_block = (row <= col).astype(BF16)
        before_in_block = (row < col).astype(BF16)
        keep = col < row

        def block(j, carry, diagonal):
            pk, pg, dq = carry
            ks = pl.multiple_of(j * bq, bq)
            kb = k_ref[pl.ds(ks, bq), :].astype(BF16)
            vb = v_ref[pl.ds(ks, bq), :].astype(BF16)
            z = lax.dot_general(q, kb, NT, preferred_element_type=F32) * scale
            lk = _log_keep(z)
            if diagonal:
                lk = jnp.where(keep, lk, 0.0)
            e = z + lk + ((ltot - pk) - _split_dot(lk, upto_in_block))
            if diagonal:
                e = jnp.where(keep, e, NEG)
            a = jnp.exp(e)
            da = lax.dot_general(do_b, vb, NT, preferred_element_type=F32)
            g = a * da
            dv_acc[pl.ds(ks, bq), :] += lax.dot_general(a.astype(BF16), do_b, TN, preferred_element_type=F32)
            before = pg + _split_dot(g, before_in_block)
            dz = g * jnp.exp(lk) - before * jnp.exp(z + lk)
            if diagonal:
                dz = jnp.where(keep, dz, 0.0)
            dz_b = (dz * scale).astype(BF16)
            dq = dq + lax.dot_general(dz_b, kb, NN, preferred_element_type=F32)
            dk_acc[pl.ds(ks, bq), :] += lax.dot_general(dz_b, q, TN, preferred_element_type=F32)
            return (pk + jnp.sum(lk, axis=1, keepdims=True), pg + jnp.sum(g, axis=1, keepdims=True), dq)

        zero = jnp.zeros((bq, 1), F32)
        carry = lax.fori_loop(0, i, lambda j, cr: block(j, cr, False), (zero, zero, jnp.zeros((bq, HEAD_DIM), F32)))
        carry = block(i, carry, True)
        dq_ref[...] = carry[2].astype(dq_ref.dtype)

        @pl.when(i == nq - 1)
        def _():
            dk_ref[...] = dk_acc[...].astype(dk_ref.dtype)
            dv_ref[...] = dv_acc[...].astype(dv_ref.dtype)

    blk = pl.BlockSpec((bq, HEAD_DIM), lambda h, i: (i, h))
    full = pl.BlockSpec((s, HEAD_DIM), lambda h, i: (0, h))
    shp = jax.ShapeDtypeStruct((s, n_sb * HEAD_DIM), BF16)
    return pl.pallas_call(
        body, grid=(n_sb, nq),
        in_specs=[blk,
                  pl.BlockSpec((s, HEAD_DIM), lambda h, i: (0, n_sb + h)),
                  pl.BlockSpec((s, HEAD_DIM), lambda h, i: (0, 2 * n_sb + h)),
                  blk, blk],
        out_specs=[blk, full, full], out_shape=[shp, shp, shp],
        scratch_shapes=[pltpu.VMEM((s, HEAD_DIM), F32), pltpu.VMEM((s, HEAD_DIM), F32)],
        compiler_params=_params(("parallel", "arbitrary")), name="sb_bwd",
    )(proj, proj, proj, lt, do)


def _t5_bucket(dist):
    max_exact = NUM_BUCKETS // 2
    d = jnp.maximum(dist, 1).astype(F32)
    large = max_exact + (jnp.log(d / max_exact) / math.log(MAX_DISTANCE / max_exact)
                         * (NUM_BUCKETS - max_exact)).astype(jnp.int32)
    large = jnp.minimum(large, NUM_BUCKETS - 1)
    return jnp.where(dist < max_exact, dist, large)


def _dil_tables(rel_bias):
    qi = jnp.arange(HEAD_DIM, dtype=jnp.int32)[:, None]
    ki = jnp.arange(2 * HEAD_DIM, dtype=jnp.int32)[None, :]
    rel = HEAD_DIM + qi - ki
    band = (rel >= 0) & (rel <= HEAD_DIM)
    biases, buckets = [], []
    for d in DILATIONS:
        bucket = _t5_bucket(jnp.maximum(rel, 0) * d)
        onehot = (bucket[:, :, None] == jnp.arange(NUM_BUCKETS, dtype=jnp.int32)).astype(F32)
        bias = jnp.einsum("qkb,bh->hqk", onehot, rel_bias.astype(F32), precision=lax.Precision.HIGHEST)
        biases.append(jnp.where(band[None], bias, NEG))
        buckets.append(jnp.where(band, bucket, -1).astype(F32))
    return jnp.stack(biases, axis=1), jnp.stack(buckets, axis=0)


def _sub_rows(ref, start, d):
    if d == 1:
        return ref[pl.ds(pl.multiple_of(start, HEAD_DIM), HEAD_DIM), :]
    return ref[pl.ds(start, HEAD_DIM, stride=d), :]


def _sub_idx(start, d):
    if d == 1:
        return pl.ds(pl.multiple_of(start, HEAD_DIM), HEAD_DIM)
    return pl.ds(start, HEAD_DIM, stride=d)


def _dil_logits(q_ref, k_ref, bm, n, cur, prv, d, scale):
    qb = _sub_rows(q_ref, cur, d).astype(BF16)
    kk = jnp.concatenate([_sub_rows(k_ref, prv, d), _sub_rows(k_ref, cur, d)], axis=0).astype(BF16)
    sc = lax.dot_general(qb, kk, NT, preferred_element_type=F32) * scale + bm
    colk = lax.broadcasted_iota(jnp.int32, sc.shape, 1)
    sc = jnp.where((colk >= HEAD_DIM) | (n > 0), sc, NEG)
    return qb, kk, sc


def _dil_fwd(proj, bm, n_sb, n_dl):
    s = proj.shape[0]
    scale = 1.0 / math.sqrt(HEAD_DIM)
    chunk = min(s, 512)

    def body(q_ref, k_ref, v_ref, bm_ref, o_ref, l_ref, ob0, ob1, ob2, lb0, lb1, lb2):
        obs, lbs = (ob0, ob1, ob2), (lb0, lb1, lb2)
        for b, d in enumerate(DILATIONS):
            nb = s // (HEAD_DIM * d)

            def step(idx, _, b=b, d=d, nb=nb):
                r, n = idx // nb, idx % nb
                cur = n * (HEAD_DIM * d) + r
                prv = jnp.maximum(n - 1, 0) * (HEAD_DIM * d) + r
                _, _, sc = _dil_logits(q_ref, k_ref, bm_ref[b], n, cur, prv, d, scale)
                vv = jnp.concatenate([_sub_rows(v_ref, prv, d), _sub_rows(v_ref, cur, d)], axis=0).astype(BF16)
                mx = jnp.max(sc, axis=1, keepdims=True)
                pr = jnp.exp(sc - mx)
                den = jnp.sum(pr, axis=1, keepdims=True)
                o = lax.dot_general(pr.astype(BF16), vv, NN, preferred_element_type=F32) / den
                obs[b][_sub_idx(cur, d), :] = o
                lbs[b][_sub_idx(cur, d), :] = jnp.broadcast_to(mx + jnp.log(den), (HEAD_DIM, HEAD_DIM))
                return 0

            lax.fori_loop(0, s // HEAD_DIM, step, 0)

        for ci in range(s // chunk):
            sl = pl.ds(ci * chunk, chunk)
            l0, l1, l2 = lb0[sl, :], lb1[sl, :], lb2[sl, :]
            mx = jnp.maximum(jnp.maximum(l0, l1), l2)
            w0, w1, w2 = jnp.exp(l0 - mx), jnp.exp(l1 - mx), jnp.exp(l2 - mx)
            tot = w0 + w1 + w2
            o_ref[sl, :] = (w0 * ob0[sl, :] + w1 * ob1[sl, :] + w2 * ob2[sl, :]) / tot
            l_ref[sl, :] = mx + jnp.log(tot)

    base = 3 * n_sb
    full = pl.BlockSpec((s, HEAD_DIM), lambda h: (0, h))
    shp = jax.ShapeDtypeStruct((s, n_dl * HEAD_DIM), F32)
    return pl.pallas_call(
        body, grid=(n_dl,),
        in_specs=[pl.BlockSpec((s, HEAD_DIM), lambda h: (0, base + h)),
                  pl.BlockSpec((s, HEAD_DIM), lambda h: (0, base + n_dl + h)),
                  pl.BlockSpec((s, HEAD_DIM), lambda h: (0, base + 2 * n_dl + h)),
                  pl.BlockSpec((None, 3, HEAD_DIM, 2 * HEAD_DIM), lambda h: (h, 0, 0, 0))],
        out_specs=[full, full], out_shape=[shp, shp],
        scratch_shapes=[pltpu.VMEM((s, HEAD_DIM), F32)] * 6,
        compiler_params=_params(("parallel",)), name="dil_fwd",
    )(proj, proj, proj, bm)


def _dil_bwd(proj, do, o, lse, bm, n_sb, n_dl):
    s = proj.shape[0]
    scale = 1.0 / math.sqrt(HEAD_DIM)
    chunk = min(s, 512)

    def body(q_ref, k_ref, v_ref, do_ref, o_ref, l_ref, bm_ref, dq_ref, dk_ref, dv_ref, ds_ref, dq_s, dk_s, dv_s):
        dq_s[...] = jnp.zeros_like(dq_s)
        dk_s[...] = jnp.zeros_like(dk_s)
        dv_s[...] = jnp.zeros_like(dv_s)
        ds_ref[...] = jnp.zeros_like(ds_ref)
        for b, d in enumerate(DILATIONS):
            nb = s // (HEAD_DIM * d)

            def step(idx, _, b=b, d=d, nb=nb):
                r, n = idx // nb, idx % nb
                cur = n * (HEAD_DIM * d) + r
                prv = jnp.maximum(n - 1, 0) * (HEAD_DIM * d) + r
                qb, kk, sc = _dil_logits(q_ref, k_ref, bm_ref[b], n, cur, prv, d, scale)
                vv = jnp.concatenate([_sub_rows(v_ref, prv, d), _sub_rows(v_ref, cur, d)], axis=0).astype(BF16)
                do_f = _sub_rows(do_ref, cur, d)
                do_b = do_f.astype(BF16)
                delta = jnp.sum(do_f * _sub_rows(o_ref, cur, d), axis=1, keepdims=True)
                lr = _sub_rows(l_ref, cur, d)
                w = jnp.exp(sc - jnp.concatenate([lr, lr], axis=1))
                dp = lax.dot_general(do_b, vv, NT, preferred_element_type=F32)
                ds = w * (dp - delta)
                ds_ref[b] += ds
                ds_b = (ds * scale).astype(BF16)
                dv_blk = lax.dot_general(w.astype(BF16), do_b, TN, preferred_element_type=F32)
                dk_blk = lax.dot_general(ds_b, qb, TN, preferred_element_type=F32)
                ci, pi = _sub_idx(cur, d), _sub_idx(prv, d)
                dq_s[ci, :] += lax.dot_general(ds_b, kk, NN, preferred_element_type=F32)
                dk_s[ci, :] += dk_blk[HEAD_DIM:]
                dv_s[ci, :] += dv_blk[HEAD_DIM:]
                dk_s[pi, :] += dk_blk[:HEAD_DIM]
                dv_s[pi, :] += dv_blk[:HEAD_DIM]
                return 0

            lax.fori_loop(0, s // HEAD_DIM, step, 0)

        for ci in range(s // chunk):
            sl = pl.ds(ci * chunk, chunk)
            dq_ref[sl, :] = dq_s[sl, :].astype(dq_ref.dtype)
            dk_ref[sl, :] = dk_s[sl, :].astype(dk_ref.dtype)
            dv_ref[sl, :] = dv_s[sl, :].astype(dv_ref.dtype)

    base = 3 * n_sb
    full = pl.BlockSpec((s, HEAD_DIM), lambda h: (0, h))
    tab = pl.BlockSpec((None, 3, HEAD_DIM, 2 * HEAD_DIM), lambda h: (h, 0, 0, 0))
    shp = jax.ShapeDtypeStruct((s, n_dl * HEAD_DIM), BF16)
    return pl.pallas_call(
        body, grid=(n_dl,),
        in_specs=[pl.BlockSpec((s, HEAD_DIM), lambda h: (0, base + h)),
                  pl.BlockSpec((s, HEAD_DIM), lambda h: (0, base + n_dl + h)),
                  pl.BlockSpec((s, HEAD_DIM), lambda h: (0, base + 2 * n_dl + h)),
                  full, full, full, tab],
        out_specs=[full, full, full, tab],
        out_shape=[shp, shp, shp, jax.ShapeDtypeStruct((n_dl, 3, HEAD_DIM, 2 * HEAD_DIM), F32)],
        scratch_shapes=[pltpu.VMEM((s, HEAD_DIM), F32)] * 3,
        compiler_params=_params(("parallel",)), name="dil_bwd",
    )(proj, proj, proj, do, o, lse, bm)


def _rel_bias_grad(ds_all, buckets):
    depth, n_dl = ds_all.shape[:2]
    rows = -(-n_dl // 8) * 8

    def body(ds_ref, bk_ref, o_ref):
        lane = lax.broadcasted_iota(jnp.int32, (1, LANE), 1)

        def one_bucket(bkt, acc):
            fb = bkt.astype(F32)
            out = []
            for h in range(n_dl):
                val = jnp.zeros((1, 1), F32)
                for b in range(3):
                    tot = ds_ref[0, h, b]
                    for l in range(1, depth):
                        tot = tot + ds_ref[l, h, b]
                    val = val + jnp.sum(jnp.where(bk_ref[b] == fb, tot, 0.0), keepdims=True)
                out.append(jnp.where(lane == bkt, val, 0.0))
            out += [jnp.zeros((1, LANE), F32)] * (rows - n_dl)
            return acc + jnp.concatenate(out, axis=0)

        o_ref[...] = lax.fori_loop(0, NUM_BUCKETS, one_bucket, jnp.zeros((rows, LANE), F32))

    return pl.pallas_call(
        body, out_shape=jax.ShapeDtypeStruct((rows, LANE), F32),
        in_specs=[pl.BlockSpec(memory_space=pltpu.VMEM)] * 2, out_specs=pl.BlockSpec(memory_space=pltpu.VMEM),
        compiler_params=_params(), name="rel_bias_grad",
    )(ds_all, buckets)


def _place():
    return lax.axis_index("x"), lax.axis_index("y"), lax.axis_index("c")


def _flip(v, bit):
    return 1 - v if bit else v


HBM_SPEC = pl.BlockSpec(memory_space=pl.ANY)


def _gather_layer(shards, layer):
    nw = len(shards)

    def body(*refs):
        ins, outs = refs[:nw], refs[nw:2 * nw]
        local_sem, send1, recv1, send2, recv2 = refs[2 * nw:]
        x, y, c = _place()
        mine = 2 * x + y
        local, sends = [], []
        for w in range(nw):
            src = ins[w].at[layer]
            cp = pltpu.make_async_copy(src, outs[w].at[mine], local_sem.at[w])
            cp.start()
            local.append(cp)
            h = src.shape[0] // 2
            half = pl.ds(c * h, h)
            for k in (1, 2, 3):
                px, py = _flip(x, k >> 1), _flip(y, k & 1)
                cp = pltpu.make_async_remote_copy(
                    src_ref=src.at[half], dst_ref=outs[w].at[mine, half], send_sem=send1.at[w, k],
                    recv_sem=recv1.at[w, k], device_id=(px, py, c), device_id_type=MESH)
                cp.start()
                sends.append(cp)
        for w in range(nw):
            h = outs[w].shape[1] // 2
            half = pl.ds(c * h, h)
            for k in (1, 2, 3):
                theirs = 2 * _flip(x, k >> 1) + _flip(y, k & 1)
                landed = outs[w].at[theirs, half]
                pltpu.make_async_remote_copy(
                    src_ref=landed, dst_ref=landed, send_sem=send1.at[w, k], recv_sem=recv1.at[w, k],
                    device_id=(x, y, c), device_id_type=MESH).wait_recv()
                cp = pltpu.make_async_remote_copy(
                    src_ref=landed, dst_ref=landed, send_sem=send2.at[w, k], recv_sem=recv2.at[w, k],
                    device_id=(x, y, 1 - c), device_id_type=MESH)
                cp.start()
                sends.append(cp)
        for w in range(nw):
            h = outs[w].shape[1] // 2
            other = pl.ds((1 - c) * h, h)
            for k in (1, 2, 3):
                theirs = 2 * _flip(x, k >> 1) + _flip(y, k & 1)
                passed = outs[w].at[theirs, other]
                pltpu.make_async_remote_copy(
                    src_ref=passed, dst_ref=passed, send_sem=send2.at[w, k], recv_sem=recv2.at[w, k],
                    device_id=(x, y, c), device_id_type=MESH).wait_recv()
        for cp in sends:
            cp.wait_send()
        for cp in local:
            cp.wait()

    sems = [pltpu.SemaphoreType.DMA((nw,))] + [pltpu.SemaphoreType.DMA((nw, 4))] * 4
    return pl.pallas_call(
        body, out_shape=[jax.ShapeDtypeStruct((N_CHIPS,) + a.shape[1:], a.dtype) for a in shards],
        in_specs=[HBM_SPEC] * nw, out_specs=[HBM_SPEC] * nw, scratch_shapes=sems,
        name=f"gather_weights_{layer}",
    )(*shards)


def _scatter_grads(grads):
    nw = len(grads)

    def body(*refs):
        ins, outs = refs[:nw], refs[nw:2 * nw]
        local_sem, send, recv = refs[2 * nw:]
        x, y, c = _place()
        me = 4 * x + 2 * y + c
        local, sends = [], []
        for w in range(nw):
            h = ins[w].shape[1] // 2
            cp = pltpu.make_async_copy(ins[w].at[2 * x + y, pl.ds(c * h, h)], outs[w].at[me], local_sem.at[w])
            cp.start()
            local.append(cp)
            for k in range(1, N_DEV):
                px, py, pc = _flip(x, k >> 2), _flip(y, (k >> 1) & 1), _flip(c, k & 1)
                cp = pltpu.make_async_remote_copy(
                    src_ref=ins[w].at[2 * px + py, pl.ds(pc * h, h)], dst_ref=outs[w].at[me],
                    send_sem=send.at[w, k], recv_sem=recv.at[w, k], device_id=(px, py, pc), device_id_type=MESH)
                cp.start()
                sends.append(cp)
        for w in range(nw):
            for k in range(1, N_DEV):
                px, py, pc = _flip(x, k >> 2), _flip(y, (k >> 1) & 1), _flip(c, k & 1)
                slot = outs[w].at[4 * px + 2 * py + pc]
                pltpu.make_async_remote_copy(
                    src_ref=slot, dst_ref=slot, send_sem=send.at[w, k], recv_sem=recv.at[w, k],
                    device_id=(x, y, c), device_id_type=MESH).wait_recv()
        for cp in sends:
            cp.wait_send()
        for cp in local:
            cp.wait()

    sems = [pltpu.SemaphoreType.DMA((nw,))] + [pltpu.SemaphoreType.DMA((nw, N_DEV))] * 2
    return pl.pallas_call(
        body, out_shape=[jax.ShapeDtypeStruct((N_DEV, a.shape[1] // 2, a.shape[2]), a.dtype) for a in grads],
        in_specs=[HBM_SPEC] * nw, out_specs=[HBM_SPEC] * nw, scratch_shapes=sems,
        name="scatter_grads",
    )(*grads)


def _join_halves(halves):
    nw = len(halves)

    def body(*refs):
        ins, outs = refs[:nw], refs[nw:2 * nw]
        local_sem, send, recv = refs[2 * nw:]
        x, y, c = _place()
        copies = []
        for w in range(nw):
            h = ins[w].shape[0]
            cp = pltpu.make_async_copy(ins[w], outs[w].at[pl.ds(c * h, h)], local_sem.at[w])
            cp.start()
            copies.append(cp)
            cp = pltpu.make_async_remote_copy(
                src_ref=ins[w], dst_ref=outs[w].at[pl.ds(c * h, h)], send_sem=send.at[w], recv_sem=recv.at[w],
                device_id=(x, y, 1 - c), device_id_type=MESH)
            cp.start()
            copies.append(cp)
        for w in range(nw):
            h = ins[w].shape[0]
            theirs = outs[w].at[pl.ds((1 - c) * h, h)]
            pltpu.make_async_remote_copy(
                src_ref=theirs, dst_ref=theirs, send_sem=send.at[w], recv_sem=recv.at[w],
                device_id=(x, y, c), device_id_type=MESH).wait_recv()
        for w in range(nw):
            copies[2 * w + 1].wait_send()
            copies[2 * w].wait()

    sems = [pltpu.SemaphoreType.DMA((nw,))] * 3
    return pl.pallas_call(
        body, out_shape=[jax.ShapeDtypeStruct((2 * a.shape[0], a.shape[1]), a.dtype) for a in halves],
        in_specs=[HBM_SPEC] * nw, out_specs=[HBM_SPEC] * nw, scratch_shapes=sems,
        name="join_halves",
    )(*halves)


def _allreduce_small(v):
    rows, c = v.shape

    def body(v_ref, o_ref, buf, local_sem, send, recv):
        x, y, cc = _place()
        me = 4 * x + 2 * y + cc
        own = pltpu.make_async_copy(v_ref, buf.at[me], local_sem)
        own.start()
        sends = []
        for k in range(1, N_DEV):
            px, py, pc = _flip(x, k >> 2), _flip(y, (k >> 1) & 1), _flip(cc, k & 1)
            cp = pltpu.make_async_remote_copy(
                src_ref=v_ref, dst_ref=buf.at[me], send_sem=send.at[k], recv_sem=recv.at[k],
                device_id=(px, py, pc), device_id_type=MESH)
            cp.start()
            sends.append(cp)
        for k in range(1, N_DEV):
            px, py, pc = _flip(x, k >> 2), _flip(y, (k >> 1) & 1), _flip(cc, k & 1)
            slot = buf.at[4 * px + 2 * py + pc]
            pltpu.make_async_remote_copy(
                src_ref=slot, dst_ref=slot, send_sem=send.at[k], recv_sem=recv.at[k],
                device_id=(x, y, cc), device_id_type=MESH).wait_recv()
        for cp in sends:
            cp.wait_send()
        own.wait()
        acc = buf[0]
        for i in range(1, N_DEV):
            acc = acc + buf[i]
        o_ref[...] = acc

    return pl.pallas_call(
        body, out_shape=jax.ShapeDtypeStruct((rows, c), F32),
        in_specs=[pl.BlockSpec(memory_space=pltpu.VMEM)], out_specs=pl.BlockSpec(memory_space=pltpu.VMEM),
        scratch_shapes=[pltpu.VMEM((N_DEV, rows, c), F32), pltpu.SemaphoreType.DMA,
                        pltpu.SemaphoreType.DMA((N_DEV,)), pltpu.SemaphoreType.DMA((N_DEV,))],
        compiler_params=_params(),
        name="allreduce_small",
    )(v)


def kernel(x, p, ln_mix_pre, w_in, ln_head, w_out, ln_mix_post, rel_bias, ln_ffn_pre, w_gate_up, w_down, ln_ffn_post, ln_pli, w_pli_gate, w_pli_proj, loss_target, m_ln_mix_pre, m_w_in, m_ln_head, m_w_out, m_ln_mix_post, m_rel_bias, m_ln_ffn_pre, m_w_gate_up, m_w_down, m_ln_ffn_post, m_ln_pli, m_w_pli_gate, m_w_pli_proj, v_ln_mix_pre, v_w_in, v_ln_head, v_w_out, v_ln_mix_post, v_rel_bias, v_ln_ffn_pre, v_w_gate_up, v_w_down, v_ln_ffn_post, v_ln_pli, v_w_pli_gate, v_w_pli_proj):
    weights = dict(ln_mix_pre=ln_mix_pre, w_in=w_in, ln_head=ln_head, w_out=w_out, ln_mix_post=ln_mix_post,
                   rel_bias=rel_bias, ln_ffn_pre=ln_ffn_pre, w_gate_up=w_gate_up, w_down=w_down,
                   ln_ffn_post=ln_ffn_post, ln_pli=ln_pli, w_pli_gate=w_pli_gate, w_pli_proj=w_pli_proj)
    mom1 = dict(ln_mix_pre=m_ln_mix_pre, w_in=m_w_in, ln_head=m_ln_head, w_out=m_w_out, ln_mix_post=m_ln_mix_post,
                rel_bias=m_rel_bias, ln_ffn_pre=m_ln_ffn_pre, w_gate_up=m_w_gate_up, w_down=m_w_down,
                ln_ffn_post=m_ln_ffn_post, ln_pli=m_ln_pli, w_pli_gate=m_w_pli_gate, w_pli_proj=m_w_pli_proj)
    mom2 = dict(ln_mix_pre=v_ln_mix_pre, w_in=v_w_in, ln_head=v_ln_head, w_out=v_w_out, ln_mix_post=v_ln_mix_post,
                rel_bias=v_rel_bias, ln_ffn_pre=v_ln_ffn_pre, w_gate_up=v_w_gate_up, w_down=v_w_down,
                ln_ffn_post=v_ln_ffn_post, ln_pli=v_ln_pli, w_pli_gate=v_w_pli_gate, w_pli_proj=v_w_pli_proj)

    _, seq, d_model = x.shape
    depth = w_in.shape[0]
    n_heads = d_model // HEAD_DIM
    n_sb = n_heads // 2
    n_dl = n_heads - n_sb
    assert seq % (HEAD_DIM * DILATIONS[-1]) == 0 and d_model % (2 * HEAD_DIM) == 0
    bq = 256
    tr = 128
    tr_ff = 64

    xs = x[0]
    target = loss_target[0]
    gain = {n: [weights[n][l][None, :] for l in range(depth)] for n in SMALL}
    shards = [_cast_bf16(weights[n], n) for n in BIG]
    bias_mask, buckets = _dil_tables(rel_bias)

    saved = []
    h1 = _norm_in(xs, gain["ln_mix_pre"][0], tr)
    xin = xs
    for l in range(depth):
        wg = dict(zip(BIG, _gather_layer(shards, l)))
        proj = _mm_fwd(h1, wg["w_in"], True, "mm_in", tn_cap=768, tk_cap=2048)
        o_sb, lt_sb = _sb_fwd(proj, n_sb, bq)
        o_dl, lse_dl = _dil_fwd(proj, bias_mask, n_sb, n_dl)
        on = _headnorm(o_sb, o_dl, gain["ln_head"][l], tr)
        y = _mm_fwd(on, wg["w_out"], False, "mm_out", tn_cap=1024)
        x2, h2 = _res_norm(xin, y, gain["ln_mix_post"][l], gain["ln_ffn_pre"][l], "post_attn", tr)
        gu = _mm_fwd(h2, wg["w_gate_up"], True, "mm_gate_up")
        act = _swiglu(gu, tr_ff)
        f = _mm_fwd(act, wg["w_down"], False, "mm_down", tn_cap=1024, tk_cap=1536)
        x3, h3 = _res_norm(x2, f, gain["ln_ffn_post"][l], gain["ln_pli"][l], "post_ffn", tr)
        gl = _mm_fwd(h3, wg["w_pli_gate"], False, "mm_pli_gate", tn_cap=1024)
        pl_in = p[l, 0]
        pp = _mm_fwd(pl_in, wg["w_pli_proj"], True, "mm_pli_proj")
        saved.append(dict(wg=wg, x=xin, h1=h1, proj=proj, o_sb=o_sb, lt_sb=lt_sb, o_dl=o_dl, lse_dl=lse_dl, on=on, y=y, x2=x2,
                          h2=h2, gu=gu, act=act, f=f, x3=x3, h3=h3, gl=gl, pp=pp, p=pl_in))
        if l + 1 < depth:
            xin, h1 = _pli_out(x3, gl, pp, gain["ln_mix_pre"][l + 1], tr)
        else:
            dx, loss_part = _loss_head(x3, gl, pp, target, tr)

    grad_big = {n: [None] * depth for n in BIG}
    grad_gain = {n: [None] * depth for n in SMALL}
    ds_layers = [None] * depth
    for l in reversed(range(depth)):
        sv = saved[l]
        wg = sv["wg"]
        dpp, dgl = _pli_bwd(dx, sv["gl"], sv["pp"], tr)
        dw = {}
        dw["w_pli_proj"] = _mm_wgrad(sv["p"], dpp, True, "wg_pli_proj")
        dw["w_pli_gate"] = _mm_wgrad(sv["h3"], dgl, False, "wg_pli_gate")
        dh3 = _mm_dgrad(dgl, wg["w_pli_gate"], False, "dg_pli_gate")
        dx3, df, grad_gain["ln_pli"][l], grad_gain["ln_ffn_post"][l] = _res_norm_bwd(
            dx, dh3, sv["x3"], sv["f"], gain["ln_pli"][l], gain["ln_ffn_post"][l], "post_ffn_bwd", tr)
        dw["w_down"] = _mm_wgrad(sv["act"], df, False, "wg_down")
        dact = _mm_dgrad(df, wg["w_down"], False, "dg_down")
        dgu = _swiglu_bwd(dact, sv["gu"], tr_ff)
        dw["w_gate_up"] = _mm_wgrad(sv["h2"], dgu, True, "wg_gate_up")
        dh2 = _mm_dgrad(dgu, wg["w_gate_up"], True, "dg_gate_up", tc_cap=1536)
        dx2, dy, grad_gain["ln_ffn_pre"][l], grad_gain["ln_mix_post"][l] = _res_norm_bwd(
            dx3, dh2, sv["x2"], sv["y"], gain["ln_ffn_pre"][l], gain["ln_mix_post"][l], "post_attn_bwd", tr)
        dw["w_out"] = _mm_wgrad(sv["on"], dy, False, "wg_out")
        don = _mm_dgrad(dy, wg["w_out"], False, "dg_out")
        do_sb, do_dl, grad_gain["ln_head"][l] = _headnorm_bwd(don, sv["o_sb"], sv["o_dl"], gain["ln_head"][l], tr)
        dq_s, dk_s, dv_s = _sb_bwd(sv["proj"], sv["lt_sb"], do_sb, n_sb, bq)
        dq_d, dk_d, dv_d, ds_layers[l] = _dil_bwd(sv["proj"], do_dl, sv["o_dl"], sv["lse_dl"], bias_mask, n_sb, n_dl)
        dproj = jnp.concatenate([dq_s, dk_s, dv_s, dq_d, dk_d, dv_d], axis=1)
        dw["w_in"] = _mm_wgrad(sv["h1"], dproj, True, "wg_in")
        dh1 = _mm_dgrad(dproj, wg["w_in"], True, "dg_in", tc_cap=1536)
        dx, grad_gain["ln_mix_pre"][l] = _norm_in_bwd(dx2, dh1, sv["x"], gain["ln_mix_pre"][l], tr)
        pieces = _scatter_grads([dw[n] for n in BIG])
        halves = [_sum8(pc, n) for pc, n in zip(pieces, BIG)]
        for n, g in zip(BIG, _join_halves(halves)):
            grad_big[n][l] = g

    db = _rel_bias_grad(jnp.stack(ds_layers, axis=0), buckets)
    rb_flat = db[:n_dl, :NUM_BUCKETS].T.reshape(1, NUM_BUCKETS * n_dl)
    def widen(v):
        return jnp.pad(v, ((0, 0), (0, d_model - v.shape[1])))
    small_rows = [grad_gain[n][l] for n in SMALL for l in range(depth)] + [widen(rb_flat), widen(loss_part)]
    n_rows = len(small_rows)
    small = jnp.concatenate(small_rows + [jnp.zeros((-n_rows % 8, d_model), F32)], axis=0)
    total = _allreduce_small(small)
    grads = {}
    for i, n in enumerate(SMALL):
        grads[n] = total[i * depth:(i + 1) * depth]
    grads["rel_bias"] = total[len(SMALL) * depth, :NUM_BUCKETS * n_dl].reshape(NUM_BUCKETS, n_dl)
    loss = (0.5 / d_model) * jnp.sum(total[len(SMALL) * depth + 1, :LANE])
    for n in BIG:
        grads[n] = jnp.stack(grad_big[n], axis=0)

    delta, new_m, new_v = {}, {}, {}
    for n in WEIGHTS:
        delta[n], new_m[n], new_v[n] = _adamw(weights[n], grads[n], mom1[n], mom2[n], n)
    return (loss, dx[None], *[grads[n] for n in WEIGHTS], *[delta[n] for n in WEIGHTS],
            *[new_m[n] for n in WEIGHTS], *[new_v[n] for n in WEIGHTS])
```

```python
import functools
import math

import jax
import jax.numpy as jnp
from jax import lax
from jax.experimental import pallas as pl
from jax.experimental.pallas import tpu as pltpu

F32 = jnp.float32
BF16 = jnp.bfloat16

HEAD_DIM = 128
RMS_EPS = 1e-6
DILATIONS = (1, 4, 16)
NUM_BUCKETS = 32
MAX_DISTANCE = 2048
NEG = -1e30
N_CHIPS = 4
N_DEV = 8
JOIN_CHUNKS = 8

ADAM_LR = 0.001
ADAM_B1 = 0.9
ADAM_B2 = 0.999
ADAM_EPS = 1e-08
ADAM_WD = 0.01
ADAM_STEP = 10

V7X_VMEM_LIMIT = 48 * 1024 * 1024
LANE = 128

NN = (((1,), (0,)), ((), ()))
NT = (((1,), (1,)), ((), ()))
TN = (((0,), (0,)), ((), ()))
MESH = pl.DeviceIdType.MESH

BIG = ("w_in", "w_out", "w_gate_up", "w_down", "w_pli_gate", "w_pli_proj")
COL_SHARDED = {"w_in": True, "w_out": False, "w_gate_up": True, "w_down": False,
               "w_pli_gate": False, "w_pli_proj": True}
SMALL = ("ln_mix_pre", "ln_head", "ln_mix_post", "ln_ffn_pre", "ln_ffn_post", "ln_pli")
WEIGHTS = ("ln_mix_pre", "w_in", "ln_head", "w_out", "ln_mix_post", "rel_bias", "ln_ffn_pre",
           "w_gate_up", "w_down", "ln_ffn_post", "ln_pli", "w_pli_gate", "w_pli_proj")


def _tile(n, cap):
    t = min(n, cap) // LANE * LANE
    while t >= LANE:
        if n % t == 0:
            return t
        t -= LANE
    return n


def _params(sem=None):
    return pltpu.CompilerParams(dimension_semantics=sem, vmem_limit_bytes=V7X_VMEM_LIMIT)


def _rowwise(fn, rows, vecs, outs, sums, name, tr):
    s = rows[0].shape[0]
    nr, nv, no, ns = len(rows), len(vecs), len(outs), len(sums)

    def body(*refs):
        ins = [r[...] for r in refs[:nr + nv]]
        res = fn(*ins)
        out_refs = refs[nr + nv:nr + nv + no]
        sum_refs = refs[nr + nv + no:]
        for o_ref, val in zip(out_refs, res[:no]):
            o_ref[...] = val.astype(o_ref.dtype)
        if ns:
            @pl.when(pl.program_id(0) == 0)
            def _():
                for s_ref in sum_refs:
                    s_ref[...] = jnp.zeros_like(s_ref)
            for s_ref, val in zip(sum_refs, res[no:]):
                s_ref[...] += jnp.sum(val, axis=0, keepdims=True)

    in_specs = [pl.BlockSpec((tr, r.shape[1]), lambda i: (i, 0)) for r in rows]
    in_specs += [pl.BlockSpec(v.shape, lambda i: (0, 0)) for v in vecs]
    out_specs = [pl.BlockSpec((tr, c), lambda i: (i, 0)) for c, _ in outs]
    out_specs += [pl.BlockSpec((1, c), lambda i: (0, 0)) for c in sums]
    out_shape = [jax.ShapeDtypeStruct((s, c), dt) for c, dt in outs]
    out_shape += [jax.ShapeDtypeStruct((1, c), F32) for c in sums]
    return pl.pallas_call(
        body, grid=(s // tr,), in_specs=in_specs, out_specs=out_specs, out_shape=out_shape,
        compiler_params=_params(("arbitrary",) if ns else ("parallel",)), name=name,
    )(*rows, *vecs)


def _rms_r(x):
    return lax.rsqrt(jnp.mean(x * x, axis=-1, keepdims=True) + RMS_EPS)


def _rms_bwd(x, g, dy):
    r = _rms_r(x)
    u = dy * g
    dx = r * (u - x * (r * r) * jnp.mean(u * x, axis=-1, keepdims=True))
    return dx, dy * x * r


def _sigmoid(z):
    return 1.0 / (1.0 + jnp.exp(-z))


def _norm_in(x, g, tr):
    d = x.shape[1]
    return _rowwise(lambda x, g: (x * _rms_r(x) * g,), [x], [g], [(d, BF16)], [], "norm_in", tr)[0]


def _norm_in_bwd(dx_res, dh, x, g, tr):
    d = x.shape[1]

    def fn(dx_res, dh, x, g):
        dx, dg = _rms_bwd(x, g, dh)
        return dx_res + dx, dg
    return _rowwise(fn, [dx_res, dh, x], [g], [(d, F32)], [d], "norm_in_bwd", tr)


def _headnorm(o_sb, o_dl, g, tr):
    d = g.shape[1]

    def fn(o_sb, o_dl, g):
        o = jnp.concatenate([o_sb, o_dl], axis=1)
        parts = []
        for h in range(d // HEAD_DIM):
            sl = slice(h * HEAD_DIM, (h + 1) * HEAD_DIM)
            oh = o[:, sl]
            parts.append(oh * _rms_r(oh) * g[:, sl])
        return (jnp.concatenate(parts, axis=1),)
    return _rowwise(fn, [o_sb, o_dl], [g], [(d, BF16)], [], "headnorm", tr)[0]


def _headnorm_bwd(don, o_sb, o_dl, g, tr):
    d = g.shape[1]
    n_sb = o_sb.shape[1]

    def fn(don, o_sb, o_dl, g):
        o = jnp.concatenate([o_sb, o_dl], axis=1)
        dos, dgs = [], []
        for h in range(d // HEAD_DIM):
            sl = slice(h * HEAD_DIM, (h + 1) * HEAD_DIM)
            dx, dg = _rms_bwd(o[:, sl], g[:, sl], don[:, sl])
            dos.append(dx)
            dgs.append(dg)
        do = jnp.concatenate(dos, axis=1)
        return do[:, :n_sb], do[:, n_sb:], jnp.concatenate(dgs, axis=1)
    return _rowwise(fn, [don, o_sb, o_dl], [g], [(n_sb, F32), (d - n_sb, F32)], [d], "headnorm_bwd", tr)


def _res_norm(x, y, g_post, g_pre, name, tr):
    d = x.shape[1]

    def fn(x, y, g_post, g_pre):
        x2 = x + y * _rms_r(y) * g_post
        return x2, x2 * _rms_r(x2) * g_pre
    return _rowwise(fn, [x, y], [g_post, g_pre], [(d, F32), (d, BF16)], [], name, tr)


def _res_norm_bwd(dx_res, dh, x2, y, g_pre, g_post, name, tr):
    d = x2.shape[1]

    def fn(dx_res, dh, x2, y, g_pre, g_post):
        dxa, dg_pre = _rms_bwd(x2, g_pre, dh)
        dx2 = dx_res + dxa
        dy, dg_post = _rms_bwd(y, g_post, dx2)
        return dx2, dy, dg_pre, dg_post
    return _rowwise(fn, [dx_res, dh, x2, y], [g_pre, g_post], [(d, F32), (d, BF16)], [d, d], name, tr)


def _swiglu(gu, tr):
    ff = gu.shape[1] // 2

    def fn(gu):
        g, u = gu[:, :ff], gu[:, ff:]
        return (g * _sigmoid(g) * u,)
    return _rowwise(fn, [gu], [], [(ff, BF16)], [], "swiglu", tr)[0]


def _swiglu_bwd(dact, gu, tr):
    ff = gu.shape[1] // 2

    def fn(dact, gu):
        g, u = gu[:, :ff], gu[:, ff:]
        sg = _sigmoid(g)
        dg = dact * u * sg * (1.0 + g * (1.0 - sg))
        du = dact * g * sg
        return (jnp.concatenate([dg, du], axis=1),)
    return _rowwise(fn, [dact, gu], [], [(2 * ff, BF16)], [], "swiglu_bwd", tr)[0]


def _pli_out(x3, gl, pp, g_next, tr):
    d = x3.shape[1]

    def fn(x3, gl, pp, g):
        x4 = x3 + _sigmoid(gl) * pp
        return x4, x4 * _rms_r(x4) * g
    return _rowwise(fn, [x3, gl, pp], [g_next], [(d, F32), (d, BF16)], [], "pli_out", tr)


def _loss_head(x3, gl, pp, target, tr):
    d = x3.shape[1]

    def fn(x3, gl, pp, t):
        err = x3 + _sigmoid(gl) * pp - t
        sq = err * err
        part = sq[:, :LANE]
        for k in range(1, d // LANE):
            part = part + sq[:, k * LANE:(k + 1) * LANE]
        return err * (1.0 / d), part
    return _rowwise(fn, [x3, gl, pp, target], [], [(d, F32)], [LANE], "loss_head", tr)


def _pli_bwd(dx, gl, pp, tr):
    d = dx.shape[1]

    def fn(dx, gl, pp):
        gate = _sigmoid(gl)
        return dx * gate, dx * pp * gate * (1.0 - gate)
    return _rowwise(fn, [dx, gl, pp], [], [(d, BF16), (d, BF16)], [], "pli_bwd", tr)


def _cast_layer(w, layer, name):
    _, r, c = w.shape
    tr = r
    while tr * c * 4 > (4 << 20) and tr % 32 == 0:
        tr //= 2

    def body(w_ref, o_ref):
        o_ref[...] = w_ref[...].astype(o_ref.dtype)

    return pl.pallas_call(
        body, grid=(r // tr,), in_specs=[pl.BlockSpec((None, tr, c), lambda i: (layer, i, 0))],
        out_specs=pl.BlockSpec((tr, c), lambda i: (i, 0)), out_shape=jax.ShapeDtypeStruct((r, c), BF16),
        compiler_params=_params(("parallel",)), name="cast_" + name,
    )(w)


def _sum8(recv, name):
    _, h, c = recv.shape
    tr = h
    while tr * c * 2 * N_DEV > (8 << 20) and tr % 32 == 0:
        tr //= 2

    def body(r_ref, o_ref):
        acc = r_ref[0].astype(F32)
        for i in range(1, N_DEV):
            acc = acc + r_ref[i].astype(F32)
        o_ref[...] = acc

    return pl.pallas_call(
        body, grid=(h // tr,), in_specs=[pl.BlockSpec((N_DEV, tr, c), lambda i: (0, i, 0))],
        out_specs=pl.BlockSpec((tr, c), lambda i: (i, 0)), out_shape=jax.ShapeDtypeStruct((h, c), F32),
        compiler_params=_params(("parallel",)), name="sum8_" + name,
    )(recv)


def _adamw(w, g, m, v, name):
    shape = w.shape
    if w.ndim == 3:
        w, g, m, v = (a.reshape(shape[0] * shape[1], shape[2]) for a in (w, g, m, v))
    r, c = w.shape
    tr = r
    while tr * c * 4 > (1 << 20) and tr % 16 == 0:
        tr //= 2

    def body(w_ref, g_ref, m_ref, v_ref, d_ref, m2_ref, v2_ref):
        g = g_ref[...]
        m2 = ADAM_B1 * m_ref[...] + (1.0 - ADAM_B1) * g
        v2 = ADAM_B2 * v_ref[...] + (1.0 - ADAM_B2) * (g * g)
        m_hat = m2 / (1.0 - ADAM_B1 ** ADAM_STEP)
        v_hat = v2 / (1.0 - ADAM_B2 ** ADAM_STEP)
        d_ref[...] = -ADAM_LR * (m_hat / (jnp.sqrt(v_hat) + ADAM_EPS) + ADAM_WD * w_ref[...])
        m2_ref[...] = m2
        v2_ref[...] = v2

    spec = pl.BlockSpec((tr, c), lambda i: (i, 0))
    res = pl.pallas_call(
        body, grid=(r // tr,), in_specs=[spec] * 4, out_specs=[spec] * 3,
        out_shape=[jax.ShapeDtypeStruct((r, c), F32)] * 3,
        compiler_params=_params(("parallel",)), name="adamw_" + name,
    )(w, g, m, v)
    return tuple(a.reshape(shape) for a in res)


def _mm(a, b, grid, a_spec, b_spec, o_spec, o_shape, o_dtype, dims, acc_shape, name):
    nk = grid[2]

    def body(a_ref, b_ref, o_ref, acc_ref):
        k = pl.program_id(2)

        @pl.when(k == 0)
        def _():
            acc_ref[...] = jnp.zeros_like(acc_ref)

        acc_ref[...] += lax.dot_general(a_ref[...].astype(BF16), b_ref[...].astype(BF16), dims,
                                        preferred_element_type=F32)

        @pl.when(k == nk - 1)
        def _():
            o_ref[...] = acc_ref[...].astype(o_ref.dtype)

    return pl.pallas_call(
        body, grid=grid, in_specs=[a_spec, b_spec], out_specs=o_spec,
        out_shape=jax.ShapeDtypeStruct(o_shape, o_dtype),
        scratch_shapes=[pltpu.VMEM(acc_shape, F32)],
        compiler_params=_params(("parallel", "parallel", "arbitrary")), name=name,
    )(a, b)


def _mm_fwd(a, wg, col, name, tm=1024, tn_cap=1536, tk_cap=1024):
    m, k = a.shape
    ns, r, c = wg.shape
    tm = min(tm, m)
    if col:
        n, tn, tk = ns * c, _tile(c, tn_cap), _tile(k, tk_cap)
        per = c // tn
        b_spec = pl.BlockSpec((None, tk, tn), lambda j, i, kk: (j // per, kk, j % per))
    else:
        n, tn, tk = c, _tile(c, tn_cap), _tile(r, tk_cap)
        per = r // tk
        b_spec = pl.BlockSpec((None, tk, tn), lambda j, i, kk: (kk // per, kk % per, j))
    return _mm(a, wg, (n // tn, m // tm, k // tk), pl.BlockSpec((tm, tk), lambda j, i, kk: (i, kk)), b_spec,
               pl.BlockSpec((tm, tn), lambda j, i, kk: (i, j)), (m, n), F32, NN, (tm, tn), name)


def _mm_dgrad(dc, wg, col, name, tm=1024, to_cap=1536, tc_cap=1024):
    m, n = dc.shape
    ns, r, c = wg.shape
    tm = min(tm, m)
    if col:
        kout, to, tc = r, _tile(r, to_cap), _tile(c, tc_cap)
        per = c // tc
        b_spec = pl.BlockSpec((None, to, tc), lambda j, i, kk: (kk // per, j, kk % per))
    else:
        kout, to, tc = ns * r, _tile(r, to_cap), _tile(c, tc_cap)
        per = r // to
        b_spec = pl.BlockSpec((None, to, tc), lambda j, i, kk: (j // per, j % per, kk))
    return _mm(dc, wg, (kout // to, m // tm, n // tc), pl.BlockSpec((tm, tc), lambda j, i, kk: (i, kk)), b_spec,
               pl.BlockSpec((tm, to), lambda j, i, kk: (i, j)), (m, kout), F32, NT, (tm, to), name)


def _mm_wgrad(a, dc, col, name, ti_cap=1536, tn_cap=1536, tkm=1024):
    m, k = a.shape
    n = dc.shape[1]
    tkm = min(tkm, m)
    if col:
        r, c = k, n // N_CHIPS
        ti, tn = _tile(r, ti_cap), _tile(c, tn_cap)
        per = c // tn
        o_spec = pl.BlockSpec((None, ti, tn), lambda i, j, kk: (j // per, i, j % per))
    else:
        r, c = k // N_CHIPS, n
        ti, tn = _tile(r, ti_cap), _tile(c, tn_cap)
        per = r // ti
        o_spec = pl.BlockSpec((None, ti, tn), lambda i, j, kk: (i // per, i % per, j))
    return _mm(a, dc, (k // ti, n // tn, m // tkm), pl.BlockSpec((tkm, ti), lambda i, j, kk: (kk, i)),
               pl.BlockSpec((tkm, tn), lambda i, j, kk: (kk, j)), o_spec, (N_CHIPS, r, c), BF16, TN, (ti, tn), name)


def _log_keep(z):
    return -(jnp.maximum(z, 0.0) + jnp.log(1.0 + jnp.exp(-jnp.abs(z))))


def _split_dot(x, t):
    hi = x.astype(BF16)
    lo = (x - hi.astype(F32)).astype(BF16)
    return (lax.dot_general(hi, t, NN, preferred_element_type=F32)
            + lax.dot_general(lo, t, NN, preferred_element_type=F32))


def _sb_fwd(proj, n_sb, bq):
    s = proj.shape[0]
    scale = 1.0 / math.sqrt(HEAD_DIM)

    def body(q_ref, k_ref, v_ref, o_ref, lt_ref):
        i = pl.program_id(1)
        q = q_ref[...].astype(BF16)
        row = lax.broadcasted_iota(jnp.int32, (bq, bq), 0)
        col = lax.broadcasted_iota(jnp.int32, (bq, bq), 1)
        later_in_block = (row > col).astype(BF16)
        keep = col < row

        def block(j, carry, diagonal):
            c, acc = carry
            ks = pl.multiple_of(j * bq, bq)
            kb = k_ref[pl.ds(ks, bq), :].astype(BF16)
            vb = v_ref[pl.ds(ks, bq), :].astype(BF16)
            z = lax.dot_general(q, kb, NT, preferred_element_type=F32) * scale
            lk = _log_keep(z)
            if diagonal:
                lk = jnp.where(keep, lk, 0.0)
            e = z + lk + _split_dot(lk, later_in_block) + c
            if diagonal:
                e = jnp.where(keep, e, NEG)
            a = jnp.exp(e)
            acc = acc + lax.dot_general(a.astype(BF16), vb, NN, preferred_element_type=F32)
            return c + jnp.sum(lk, axis=1, keepdims=True), acc

        carry = block(i, (jnp.zeros((bq, 1), F32), jnp.zeros((bq, HEAD_DIM), F32)), True)
        carry = lax.fori_loop(0, i, lambda t, cr: block(i - 1 - t, cr, False), carry)
        o_ref[...] = carry[1]
        lt_ref[...] = jnp.broadcast_to(carry[0], (bq, HEAD_DIM))

    blk = pl.BlockSpec((bq, HEAD_DIM), lambda h, i: (i, h))
    shp = jax.ShapeDtypeStruct((s, n_sb * HEAD_DIM), F32)
    return pl.pallas_call(
        body, grid=(n_sb, s // bq),
        in_specs=[blk,
                  pl.BlockSpec((s, HEAD_DIM), lambda h, i: (0, n_sb + h)),
                  pl.BlockSpec((s, HEAD_DIM), lambda h, i: (0, 2 * n_sb + h))],
        out_specs=[blk, blk], out_shape=[shp, shp],
        compiler_params=_params(("parallel", "parallel")), name="sb_fwd",
    )(proj, proj, proj)


def _sb_bwd(proj, lt, do, n_sb, bq):
    s = proj.shape[0]
    nq = s // bq
    scale = 1.0 / math.sqrt(HEAD_DIM)

    def body(q_ref, k_ref, v_ref, lt_ref, do_ref, dq_ref, dk_ref, dv_ref, dk_acc, dv_acc):
        i = pl.program_id(1)

        @pl.when(i == 0)
        def _():
            dk_acc[...] = jnp.zeros_like(dk_acc)
            dv_acc[...] = jnp.zeros_like(dv_acc)

        q = q_ref[...].astype(BF16)
        do_b = do_ref[...].astype(BF16)
        ltot = jnp.max(lt_ref[...], axis=1, keepdims=True)
        row = lax.broadcasted_iota(jnp.int32, (bq, bq), 0)
        col = lax.broadcasted_iota(jnp.int32, (bq, bq), 1)
        upto_in_block = (row <= col).astype(BF16)
        before_in_block = (row < col).astype(BF16)
        keep = col < row

        def block(j, carry, diagonal):
            pk, pg, dq = carry
            ks = pl.multiple_of(j * bq, bq)
            kb = k_ref[pl.ds(ks, bq), :].astype(BF16)
            vb = v_ref[pl.ds(ks, bq), :].astype(BF16)
            z = lax.dot_general(q, kb, NT, preferred_element_type=F32) * scale
            lk = _log_keep(z)
            if diagonal:
                lk = jnp.where(keep, lk, 0.0)
            e = z + lk + ((ltot - pk) - _split_dot(lk, upto_in_block))
            if diagonal:
                e = jnp.where(keep, e, NEG)
            a = jnp.exp(e)
            da = lax.dot_general(do_b, vb, NT, preferred_element_type=F32)
            g = a * da
            dv_acc[pl.ds(ks, bq), :] += lax.dot_general(a.astype(BF16), do_b, TN, preferred_element_type=F32)
            before = pg + _split_dot(g, before_in_block)
            dz = g * jnp.exp(lk) - before * jnp.exp(z + lk)
            if diagonal:
                dz = jnp.where(keep, dz, 0.0)
            dz_b = (dz * scale).astype(BF16)
            dq = dq + lax.dot_general(dz_b, kb, NN, preferred_element_type=F32)
            dk_acc[pl.ds(ks, bq), :] += lax.dot_general(dz_b, q, TN, preferred_element_type=F32)
            return (pk + jnp.sum(lk, axis=1, keepdims=True), pg + jnp.sum(g, axis=1, keepdims=True), dq)

        zero = jnp.zeros((bq, 1), F32)
        carry = lax.fori_loop(0, i, lambda j, cr: block(j, cr, False), (zero, zero, jnp.zeros((bq, HEAD_DIM), F32)))
        carry = block(i, carry, True)
        dq_ref[...] = carry[2].astype(dq_ref.dtype)

        @pl.when(i == nq - 1)
        def _():
            dk_ref[...] = dk_acc[...].astype(dk_ref.dtype)
            dv_ref[...] = dv_acc[...].astype(dv_ref.dtype)

    blk = pl.BlockSpec((bq, HEAD_DIM), lambda h, i: (i, h))
    full = pl.BlockSpec((s, HEAD_DIM), lambda h, i: (0, h))
    shp = jax.ShapeDtypeStruct((s, n_sb * HEAD_DIM), BF16)
    return pl.pallas_call(
        body, grid=(n_sb, nq),
        in_specs=[blk,
                  pl.BlockSpec((s, HEAD_DIM), lambda h, i: (0, n_sb + h)),
                  pl.BlockSpec((s, HEAD_DIM), lambda h, i: (0, 2 * n_sb + h)),
                  blk, blk],
        out_specs=[blk, full, full], out_shape=[shp, shp, shp],
        scratch_shapes=[pltpu.VMEM((s, HEAD_DIM), F32), pltpu.VMEM((s, HEAD_DIM), F32)],
        compiler_params=_params(("parallel", "arbitrary")), name="sb_bwd",
    )(proj, proj, proj, lt, do)


def _t5_bucket(dist):
    max_exact = NUM_BUCKETS // 2
    d = jnp.maximum(dist, 1).astype(F32)
    large = max_exact + (jnp.log(d / max_exact) / math.log(MAX_DISTANCE / max_exact)
                         * (NUM_BUCKETS - max_exact)).astype(jnp.int32)
    large = jnp.minimum(large, NUM_BUCKETS - 1)
    return jnp.where(dist < max_exact, dist, large)


def _dil_tables(rel_bias):
    qi = jnp.arange(HEAD_DIM, dtype=jnp.int32)[:, None]
    ki = jnp.arange(2 * HEAD_DIM, dtype=jnp.int32)[None, :]
    rel = HEAD_DIM + qi - ki
    band = (rel >= 0) & (rel <= HEAD_DIM)
    biases, buckets = [], []
    for d in DILATIONS:
        bucket = _t5_bucket(jnp.maximum(rel, 0) * d)
        onehot = (bucket[:, :, None] == jnp.arange(NUM_BUCKETS, dtype=jnp.int32)).astype(F32)
        bias = jnp.einsum("qkb,bh->hqk", onehot, rel_bias.astype(F32), precision=lax.Precision.HIGHEST)
        biases.append(jnp.where(band[None], bias, NEG))
        buckets.append(jnp.where(band, bucket, -1).astype(F32))
    return jnp.stack(biases, axis=1), jnp.stack(buckets, axis=0)


def _sub_rows(ref, start, d):
    if d == 1:
        return ref[pl.ds(pl.multiple_of(start, HEAD_DIM), HEAD_DIM), :]
    return ref[pl.ds(start, HEAD_DIM, stride=d), :]


def _sub_idx(start, d):
    if d == 1:
        return pl.ds(pl.multiple_of(start, HEAD_DIM), HEAD_DIM)
    return pl.ds(start, HEAD_DIM, stride=d)


def _dil_logits(q_ref, k_ref, bm, n, cur, prv, d, scale):
    qb = _sub_rows(q_ref, cur, d).astype(BF16)
    kk = jnp.concatenate([_sub_rows(k_ref, prv, d), _sub_rows(k_ref, cur, d)], axis=0).astype(BF16)
    sc = lax.dot_general(qb, kk, NT, preferred_element_type=F32) * scale + bm
    colk = lax.broadcasted_iota(jnp.int32, sc.shape, 1)
    sc = jnp.where((colk >= HEAD_DIM) | (n > 0), sc, NEG)
    return qb, kk, sc


def _dil_fwd(proj, bm, n_sb, n_dl):
    s = proj.shape[0]
    scale = 1.0 / math.sqrt(HEAD_DIM)
    chunk = min(s, 512)

    def body(q_ref, k_ref, v_ref, bm_ref, o_ref, l_ref, ob0, ob1, ob2, lb0, lb1, lb2):
        obs, lbs = (ob0, ob1, ob2), (lb0, lb1, lb2)
        for b, d in enumerate(DILATIONS):
            nb = s // (HEAD_DIM * d)

            def step(idx, _, b=b, d=d, nb=nb):
                r, n = idx // nb, idx % nb
                cur = n * (HEAD_DIM * d) + r
                prv = jnp.maximum(n - 1, 0) * (HEAD_DIM * d) + r
                _, _, sc = _dil_logits(q_ref, k_ref, bm_ref[b], n, cur, prv, d, scale)
                vv = jnp.concatenate([_sub_rows(v_ref, prv, d), _sub_rows(v_ref, cur, d)], axis=0).astype(BF16)
                mx = jnp.max(sc, axis=1, keepdims=True)
                pr = jnp.exp(sc - mx)
                den = jnp.sum(pr, axis=1, keepdims=True)
                o = lax.dot_general(pr.astype(BF16), vv, NN, preferred_element_type=F32) / den
                obs[b][_sub_idx(cur, d), :] = o
                lbs[b][_sub_idx(cur, d), :] = jnp.broadcast_to(mx + jnp.log(den), (HEAD_DIM, HEAD_DIM))
                return 0

            lax.fori_loop(0, s // HEAD_DIM, step, 0)

        for ci in range(s // chunk):
            sl = pl.ds(ci * chunk, chunk)
            l0, l1, l2 = lb0[sl, :], lb1[sl, :], lb2[sl, :]
            mx = jnp.maximum(jnp.maximum(l0, l1), l2)
            w0, w1, w2 = jnp.exp(l0 - mx), jnp.exp(l1 - mx), jnp.exp(l2 - mx)
            tot = w0 + w1 + w2
            o_ref[sl, :] = (w0 * ob0[sl, :] + w1 * ob1[sl, :] + w2 * ob2[sl, :]) / tot
            l_ref[sl, :] = mx + jnp.log(tot)

    base = 3 * n_sb
    full = pl.BlockSpec((s, HEAD_DIM), lambda h: (0, h))
    shp = jax.ShapeDtypeStruct((s, n_dl * HEAD_DIM), F32)
    return pl.pallas_call(
        body, grid=(n_dl,),
        in_specs=[pl.BlockSpec((s, HEAD_DIM), lambda h: (0, base + h)),
                  pl.BlockSpec((s, HEAD_DIM), lambda h: (0, base + n_dl + h)),
                  pl.BlockSpec((s, HEAD_DIM), lambda h: (0, base + 2 * n_dl + h)),
                  pl.BlockSpec((None, 3, HEAD_DIM, 2 * HEAD_DIM), lambda h: (h, 0, 0, 0))],
        out_specs=[full, full], out_shape=[shp, shp],
        scratch_shapes=[pltpu.VMEM((s, HEAD_DIM), F32)] * 6,
        compiler_params=_params(("parallel",)), name="dil_fwd",
    )(proj, proj, proj, bm)


def _dil_bwd(proj, do, o, lse, bm, n_sb, n_dl):
    s = proj.shape[0]
    scale = 1.0 / math.sqrt(HEAD_DIM)
    chunk = min(s, 512)

    def body(q_ref, k_ref, v_ref, do_ref, o_ref, l_ref, bm_ref, dq_ref, dk_ref, dv_ref, ds_ref, dq_s, dk_s, dv_s):
        dq_s[...] = jnp.zeros_like(dq_s)
        dk_s[...] = jnp.zeros_like(dk_s)
        dv_s[...] = jnp.zeros_like(dv_s)
        ds_ref[...] = jnp.zeros_like(ds_ref)
        for b, d in enumerate(DILATIONS):
            nb = s // (HEAD_DIM * d)

            def step(idx, _, b=b, d=d, nb=nb):
                r, n = idx // nb, idx % nb
                cur = n * (HEAD_DIM * d) + r
                prv = jnp.maximum(n - 1, 0) * (HEAD_DIM * d) + r
                qb, kk, sc = _dil_logits(q_ref, k_ref, bm_ref[b], n, cur, prv, d, scale)
                vv = jnp.concatenate([_sub_rows(v_ref, prv, d), _sub_rows(v_ref, cur, d)], axis=0).astype(BF16)
                do_f = _sub_rows(do_ref, cur, d)
                do_b = do_f.astype(BF16)
                delta = jnp.sum(do_f * _sub_rows(o_ref, cur, d), axis=1, keepdims=True)
                lr = _sub_rows(l_ref, cur, d)
                w = jnp.exp(sc - jnp.concatenate([lr, lr], axis=1))
                dp = lax.dot_general(do_b, vv, NT, preferred_element_type=F32)
                ds = w * (dp - delta)
                ds_ref[b] += ds
                ds_b = (ds * scale).astype(BF16)
                dv_blk = lax.dot_general(w.astype(BF16), do_b, TN, preferred_element_type=F32)
                dk_blk = lax.dot_general(ds_b, qb, TN, preferred_element_type=F32)
                ci, pi = _sub_idx(cur, d), _sub_idx(prv, d)
                dq_s[ci, :] += lax.dot_general(ds_b, kk, NN, preferred_element_type=F32)
                dk_s[ci, :] += dk_blk[HEAD_DIM:]
                dv_s[ci, :] += dv_blk[HEAD_DIM:]
                dk_s[pi, :] += dk_blk[:HEAD_DIM]
                dv_s[pi, :] += dv_blk[:HEAD_DIM]
                return 0

            lax.fori_loop(0, s // HEAD_DIM, step, 0)

        for ci in range(s // chunk):
            sl = pl.ds(ci * chunk, chunk)
            dq_ref[sl, :] = dq_s[sl, :].astype(dq_ref.dtype)
            dk_ref[sl, :] = dk_s[sl, :].astype(dk_ref.dtype)
            dv_ref[sl, :] = dv_s[sl, :].astype(dv_ref.dtype)

    base = 3 * n_sb
    full = pl.BlockSpec((s, HEAD_DIM), lambda h: (0, h))
    tab = pl.BlockSpec((None, 3, HEAD_DIM, 2 * HEAD_DIM), lambda h: (h, 0, 0, 0))
    shp = jax.ShapeDtypeStruct((s, n_dl * HEAD_DIM), BF16)
    return pl.pallas_call(
        body, grid=(n_dl,),
        in_specs=[pl.BlockSpec((s, HEAD_DIM), lambda h: (0, base + h)),
                  pl.BlockSpec((s, HEAD_DIM), lambda h: (0, base + n_dl + h)),
                  pl.BlockSpec((s, HEAD_DIM), lambda h: (0, base + 2 * n_dl + h)),
                  full, full, full, tab],
        out_specs=[full, full, full, tab],
        out_shape=[shp, shp, shp, jax.ShapeDtypeStruct((n_dl, 3, HEAD_DIM, 2 * HEAD_DIM), F32)],
        scratch_shapes=[pltpu.VMEM((s, HEAD_DIM), F32)] * 3,
        compiler_params=_params(("parallel",)), name="dil_bwd",
    )(proj, proj, proj, do, o, lse, bm)


def _rel_bias_grad(ds_all, buckets):
    depth, n_dl = ds_all.shape[:2]
    rows = -(-n_dl // 8) * 8

    def body(ds_ref, bk_ref, o_ref):
        lane = lax.broadcasted_iota(jnp.int32, (1, LANE), 1)

        def one_bucket(bkt, acc):
            fb = bkt.astype(F32)
            out = []
            for h in range(n_dl):
                val = jnp.zeros((1, 1), F32)
                for b in range(3):
                    tot = ds_ref[0, h, b]
                    for l in range(1, depth):
                        tot = tot + ds_ref[l, h, b]
                    val = val + jnp.sum(jnp.where(bk_ref[b] == fb, tot, 0.0), keepdims=True)
                out.append(jnp.where(lane == bkt, val, 0.0))
            out += [jnp.zeros((1, LANE), F32)] * (rows - n_dl)
            return acc + jnp.concatenate(out, axis=0)

        o_ref[...] = lax.fori_loop(0, NUM_BUCKETS, one_bucket, jnp.zeros((rows, LANE), F32))

    return pl.pallas_call(
        body, out_shape=jax.ShapeDtypeStruct((rows, LANE), F32),
        in_specs=[pl.BlockSpec(memory_space=pltpu.VMEM)] * 2, out_specs=pl.BlockSpec(memory_space=pltpu.VMEM),
        compiler_params=_params(), name="rel_bias_grad",
    )(ds_all, buckets)


def _place():
    return lax.axis_index("x"), lax.axis_index("y"), lax.axis_index("c")


def _flip(v, bit):
    return 1 - v if bit else v


HBM_SPEC = pl.BlockSpec(memory_space=pl.ANY)


HBM_ONLY = pl.BlockSpec(memory_space=pltpu.HBM)
SEM_SPEC = pl.BlockSpec(memory_space=pltpu.SEMAPHORE)
DATAFLOW = pltpu.SideEffectType.DATAFLOW_SIDE_EFFECTING


def _in_hbm(a):
    return pltpu.with_memory_space_constraint(a, pltpu.HBM)


def _place_local(srcs, out_shapes, src_of, dst_of, name):
    nw = len(srcs)

    def body(*refs):
        ins, outs, sem = refs[:nw], refs[nw:2 * nw], refs[2 * nw]
        x, y, c = _place()
        copies = [pltpu.make_async_copy(src_of(ins[w], x, y, c), dst_of(outs[w], x, y, c), sem.at[w]) for w in range(nw)]
        for cp in copies:
            cp.start()
        for cp in copies:
            cp.wait()

    return pl.pallas_call(
        body, out_shape=[jax.ShapeDtypeStruct(shp, a.dtype) for shp, a in zip(out_shapes, srcs)],
        in_specs=[HBM_SPEC] * nw, out_specs=[HBM_SPEC] * nw, scratch_shapes=[pltpu.SemaphoreType.DMA((nw,))],
        name=name,
    )(*srcs)


def _split_start(srcs, bufs, token, copies_of, n_peer, name):
    nw = len(srcs)

    def body(*refs):
        src, buf = refs[:nw], refs[nw:2 * nw]
        send, recv = refs[2 * nw + 1], refs[2 * nw + 2]
        for cp in copies_of(src, buf, send, recv):
            cp.start()
        refs[-1][...] = jnp.zeros_like(refs[-1])

    arrays = list(srcs) + list(bufs)
    sems = pltpu.SemaphoreType.DMA((nw * n_peer,))
    res = pl.pallas_call(
        body, name=name,
        out_shape=(sems, sems, *[pltpu.HBM(a.shape, a.dtype) for a in arrays], jax.ShapeDtypeStruct((8, LANE), F32)),
        in_specs=[HBM_ONLY] * (2 * nw) + [HBM_SPEC],
        out_specs=(SEM_SPEC, SEM_SPEC, *[HBM_ONLY] * (2 * nw), pl.BlockSpec(memory_space=pltpu.VMEM)),
        input_output_aliases={i: 2 + i for i in range(2 * nw)},
        compiler_params=pltpu.CompilerParams(has_side_effects=DATAFLOW),
    )(*[_in_hbm(a) for a in arrays], token)
    return res[0], res[1], res[2:2 + nw], res[2 + nw:2 + 2 * nw], res[-1]


def _split_wait(started, after, copies_of, name):
    send, recv, srcs, bufs, _ = started
    nw = len(srcs)

    def body(*refs):
        src, buf = refs[:nw], refs[nw:2 * nw]
        for cp in copies_of(src, buf, refs[2 * nw], refs[2 * nw + 1]):
            cp.wait_send()
            cp.wait_recv()

    arrays = list(srcs) + list(bufs)
    res = pl.pallas_call(
        body, name=name, out_shape=[pltpu.HBM(a.shape, a.dtype) for a in arrays],
        in_specs=[HBM_ONLY] * (2 * nw) + [SEM_SPEC, SEM_SPEC, HBM_SPEC], out_specs=[HBM_ONLY] * (2 * nw),
        input_output_aliases={i: i for i in range(2 * nw)},
        compiler_params=pltpu.CompilerParams(has_side_effects=DATAFLOW),
    )(*arrays, send, recv, after)
    return res[nw:]


def _gather_copies(src, buf, send, recv):
    x, y, c = _place()
    mine = 2 * x + y
    return [pltpu.make_async_remote_copy(
        src_ref=src[w], dst_ref=buf[w].at[mine], send_sem=send.at[4 * w + k], recv_sem=recv.at[4 * w + k],
        device_id=(_flip(x, k >> 1), _flip(y, k & 1), c), device_id_type=MESH)
        for w in range(len(src)) for k in (1, 2, 3)]


def _gather_waits(src, buf, send, recv):
    x, y, c = _place()
    return [pltpu.make_async_remote_copy(
        src_ref=src[w], dst_ref=buf[w].at[2 * _flip(x, k >> 1) + _flip(y, k & 1)], send_sem=send.at[4 * w + k],
        recv_sem=recv.at[4 * w + k], device_id=(x, y, c), device_id_type=MESH)
        for w in range(len(src)) for k in (1, 2, 3)]


def _scatter_copies(src, buf, send, recv):
    x, y, c = _place()
    me = 4 * x + 2 * y + c
    out = []
    for w in range(len(src)):
        h = src[w].shape[1] // 2
        for k in range(1, N_DEV):
            px, py, pc = _flip(x, k >> 2), _flip(y, (k >> 1) & 1), _flip(c, k & 1)
            out.append(pltpu.make_async_remote_copy(
                src_ref=src[w].at[2 * px + py, pl.ds(pc * h, h)], dst_ref=buf[w].at[me],
                send_sem=send.at[N_DEV * w + k], recv_sem=recv.at[N_DEV * w + k], device_id=(px, py, pc),
                device_id_type=MESH))
    return out


def _scatter_waits(src, buf, send, recv):
    x, y, c = _place()
    out = []
    for w in range(len(src)):
        h = src[w].shape[1] // 2
        for k in range(1, N_DEV):
            px, py, pc = _flip(x, k >> 2), _flip(y, (k >> 1) & 1), _flip(c, k & 1)
            out.append(pltpu.make_async_remote_copy(
                src_ref=src[w].at[2 * px + py, pl.ds(pc * h, h)], dst_ref=buf[w].at[4 * px + 2 * py + pc],
                send_sem=send.at[N_DEV * w + k], recv_sem=recv.at[N_DEV * w + k], device_id=(x, y, c),
                device_id_type=MESH))
    return out


def _join_halves(halves):
    nw = len(halves)

    def body(*refs):
        ins, outs = refs[:nw], refs[nw:2 * nw]
        local_sem, send, recv = refs[2 * nw:]
        x, y, c = _place()
        local, remote = [], []
        for w in range(nw):
            h = ins[w].shape[0]
            cp = pltpu.make_async_copy(ins[w], outs[w].at[pl.ds(c * h, h)], local_sem.at[w])
            cp.start()
            local.append(cp)
            hc = h // JOIN_CHUNKS
            for j in range(JOIN_CHUNKS):
                cp = pltpu.make_async_remote_copy(
                    src_ref=ins[w].at[pl.ds(j * hc, hc)], dst_ref=outs[w].at[pl.ds(c * h + j * hc, hc)],
                    send_sem=send.at[w, j], recv_sem=recv.at[w, j], device_id=(x, y, 1 - c), device_id_type=MESH)
                cp.start()
                remote.append(cp)
        for w in range(nw):
            h = ins[w].shape[0]
            hc = h // JOIN_CHUNKS
            for j in range(JOIN_CHUNKS):
                theirs = outs[w].at[pl.ds((1 - c) * h + j * hc, hc)]
                pltpu.make_async_remote_copy(
                    src_ref=theirs, dst_ref=theirs, send_sem=send.at[w, j], recv_sem=recv.at[w, j],
                    device_id=(x, y, c), device_id_type=MESH).wait_recv()
        for cp in remote:
            cp.wait_send()
        for cp in local:
            cp.wait()

    sems = [pltpu.SemaphoreType.DMA((nw,))] + [pltpu.SemaphoreType.DMA((nw, JOIN_CHUNKS))] * 2
    return pl.pallas_call(
        body, out_shape=[jax.ShapeDtypeStruct((2 * a.shape[0], a.shape[1]), a.dtype) for a in halves],
        in_specs=[HBM_SPEC] * nw, out_specs=[HBM_SPEC] * nw, scratch_shapes=sems,
        name="join_halves",
    )(*halves)


def _allreduce_small(v):
    rows, c = v.shape

    def body(v_ref, o_ref, buf, local_sem, send, recv):
        x, y, cc = _place()
        me = 4 * x + 2 * y + cc
        own = pltpu.make_async_copy(v_ref, buf.at[me], local_sem)
        own.start()
        sends = []
        for k in range(1, N_DEV):
            px, py, pc = _flip(x, k >> 2), _flip(y, (k >> 1) & 1), _flip(cc, k & 1)
            cp = pltpu.make_async_remote_copy(
                src_ref=v_ref, dst_ref=buf.at[me], send_sem=send.at[k], recv_sem=recv.at[k],
                device_id=(px, py, pc), device_id_type=MESH)
            cp.start()
            sends.append(cp)
        for k in range(1, N_DEV):
            px, py, pc = _flip(x, k >> 2), _flip(y, (k >> 1) & 1), _flip(cc, k & 1)
            slot = buf.at[4 * px + 2 * py + pc]
            pltpu.make_async_remote_copy(
                src_ref=slot, dst_ref=slot, send_sem=send.at[k], recv_sem=recv.at[k],
                device_id=(x, y, cc), device_id_type=MESH).wait_recv()
        for cp in sends:
            cp.wait_send()
        own.wait()
        acc = buf[0]
        for i in range(1, N_DEV):
            acc = acc + buf[i]
        o_ref[...] = acc

    return pl.pallas_call(
        body, out_shape=jax.ShapeDtypeStruct((rows, c), F32),
        in_specs=[pl.BlockSpec(memory_space=pltpu.VMEM)], out_specs=pl.BlockSpec(memory_space=pltpu.VMEM),
        scratch_shapes=[pltpu.VMEM((N_DEV, rows, c), F32), pltpu.SemaphoreType.DMA,
                        pltpu.SemaphoreType.DMA((N_DEV,)), pltpu.SemaphoreType.DMA((N_DEV,))],
        compiler_params=_params(),
        name="allreduce_small",
    )(v)


def kernel(x, p, ln_mix_pre, w_in, ln_head, w_out, ln_mix_post, rel_bias, ln_ffn_pre, w_gate_up, w_down, ln_ffn_post, ln_pli, w_pli_gate, w_pli_proj, loss_target, m_ln_mix_pre, m_w_in, m_ln_head, m_w_out, m_ln_mix_post, m_rel_bias, m_ln_ffn_pre, m_w_gate_up, m_w_down, m_ln_ffn_post, m_ln_pli, m_w_pli_gate, m_w_pli_proj, v_ln_mix_pre, v_w_in, v_ln_head, v_w_out, v_ln_mix_post, v_rel_bias, v_ln_ffn_pre, v_w_gate_up, v_w_down, v_ln_ffn_post, v_ln_pli, v_w_pli_gate, v_w_pli_proj):
    weights = dict(ln_mix_pre=ln_mix_pre, w_in=w_in, ln_head=ln_head, w_out=w_out, ln_mix_post=ln_mix_post,
                   rel_bias=rel_bias, ln_ffn_pre=ln_ffn_pre, w_gate_up=w_gate_up, w_down=w_down,
                   ln_ffn_post=ln_ffn_post, ln_pli=ln_pli, w_pli_gate=w_pli_gate, w_pli_proj=w_pli_proj)
    mom1 = dict(ln_mix_pre=m_ln_mix_pre, w_in=m_w_in, ln_head=m_ln_head, w_out=m_w_out, ln_mix_post=m_ln_mix_post,
                rel_bias=m_rel_bias, ln_ffn_pre=m_ln_ffn_pre, w_gate_up=m_w_gate_up, w_down=m_w_down,
                ln_ffn_post=m_ln_ffn_post, ln_pli=m_ln_pli, w_pli_gate=m_w_pli_gate, w_pli_proj=m_w_pli_proj)
    mom2 = dict(ln_mix_pre=v_ln_mix_pre, w_in=v_w_in, ln_head=v_ln_head, w_out=v_w_out, ln_mix_post=v_ln_mix_post,
                rel_bias=v_rel_bias, ln_ffn_pre=v_ln_ffn_pre, w_gate_up=v_w_gate_up, w_down=v_w_down,
                ln_ffn_post=v_ln_ffn_post, ln_pli=v_ln_pli, w_pli_gate=v_w_pli_gate, w_pli_proj=v_w_pli_proj)

    _, seq, d_model = x.shape
    depth = w_in.shape[0]
    n_heads = d_model // HEAD_DIM
    n_sb = n_heads // 2
    n_dl = n_heads - n_sb
    assert seq % (HEAD_DIM * DILATIONS[-1]) == 0 and d_model % (2 * HEAD_DIM) == 0
    bq = 256
    tr = 128
    tr_ff = 64

    xs = x[0]
    target = loss_target[0]
    gain = {n: [weights[n][l][None, :] for l in range(depth)] for n in SMALL}
    bias_mask, buckets = _dil_tables(rel_bias)

    zero_token = jnp.zeros((8, LANE), F32)
    token = zero_token
    gathers = []
    for l in range(depth):
        srcs = [_cast_layer(weights[n], l, n) for n in BIG]
        slots = _place_local(srcs, [(N_CHIPS,) + a.shape for a in srcs], lambda ref, x, y, c: ref,
                             lambda ref, x, y, c: ref.at[2 * x + y], "place_shards")
        gathers.append(_split_start(srcs, slots, token, _gather_copies, 4, f"gather_start_{l}"))
        token = gathers[-1][4]

    saved = []
    h1 = _norm_in(xs, gain["ln_mix_pre"][0], tr)
    xin = xs
    for l in range(depth):
        after = token if l == 0 else saved[l - 1]["x3"]
        wg = dict(zip(BIG, _split_wait(gathers[l], after, _gather_waits, f"gather_wait_{l}")))
        proj = _mm_fwd(h1, wg["w_in"], True, "mm_in", tn_cap=768, tk_cap=2048)
        o_sb, lt_sb = _sb_fwd(proj, n_sb, bq)
        o_dl, lse_dl = _dil_fwd(proj, bias_mask, n_sb, n_dl)
        on = _headnorm(o_sb, o_dl, gain["ln_head"][l], tr)
        y = _mm_fwd(on, wg["w_out"], False, "mm_out", tn_cap=1024)
        x2, h2 = _res_norm(xin, y, gain["ln_mix_post"][l], gain["ln_ffn_pre"][l], "post_attn", tr)
        gu = _mm_fwd(h2, wg["w_gate_up"], True, "mm_gate_up")
        act = _swiglu(gu, tr_ff)
        f = _mm_fwd(act, wg["w_down"], False, "mm_down", tn_cap=1024, tk_cap=1536)
        x3, h3 = _res_norm(x2, f, gain["ln_ffn_post"][l], gain["ln_pli"][l], "post_ffn", tr)
        gl = _mm_fwd(h3, wg["w_pli_gate"], False, "mm_pli_gate", tn_cap=1024)
        pl_in = p[l, 0]
        pp = _mm_fwd(pl_in, wg["w_pli_proj"], True, "mm_pli_proj")
        saved.append(dict(wg=wg, x=xin, h1=h1, proj=proj, o_sb=o_sb, lt_sb=lt_sb, o_dl=o_dl, lse_dl=lse_dl, on=on, y=y, x2=x2,
                          h2=h2, gu=gu, act=act, f=f, x3=x3, h3=h3, gl=gl, pp=pp, p=pl_in))
        if l + 1 < depth:
            xin, h1 = _pli_out(x3, gl, pp, gain["ln_mix_pre"][l + 1], tr)
        else:
            dx, loss_part = _loss_head(x3, gl, pp, target, tr)

    grad_big = {n: [None] * depth for n in BIG}
    grad_gain = {n: [None] * depth for n in SMALL}
    ds_layers = [None] * depth

    def reduce_layer(layer, started, after):
        pieces = _split_wait(started, after, _scatter_waits, f"scatter_wait_{layer}")
        halves = [_sum8(pc, n) for pc, n in zip(pieces, BIG)]
        for n, g in zip(BIG, _join_halves(halves)):
            grad_big[n][layer] = g

    pending = None
    for l in reversed(range(depth)):
        sv = saved[l]
        wg = sv["wg"]
        dpp, dgl = _pli_bwd(dx, sv["gl"], sv["pp"], tr)
        dw = {}
        dw["w_pli_proj"] = _mm_wgrad(sv["p"], dpp, True, "wg_pli_proj")
        dw["w_pli_gate"] = _mm_wgrad(sv["h3"], dgl, False, "wg_pli_gate")
        dh3 = _mm_dgrad(dgl, wg["w_pli_gate"], False, "dg_pli_gate")
        dx3, df, grad_gain["ln_pli"][l], grad_gain["ln_ffn_post"][l] = _res_norm_bwd(
            dx, dh3, sv["x3"], sv["f"], gain["ln_pli"][l], gain["ln_ffn_post"][l], "post_ffn_bwd", tr)
        dw["w_down"] = _mm_wgrad(sv["act"], df, False, "wg_down")
        dact = _mm_dgrad(df, wg["w_down"], False, "dg_down")
        dgu = _swiglu_bwd(dact, sv["gu"], tr_ff)
        dw["w_gate_up"] = _mm_wgrad(sv["h2"], dgu, True, "wg_gate_up")
        dh2 = _mm_dgrad(dgu, wg["w_gate_up"], True, "dg_gate_up", tc_cap=1536)
        dx2, dy, grad_gain["ln_ffn_pre"][l], grad_gain["ln_mix_post"][l] = _res_norm_bwd(
            dx3, dh2, sv["x2"], sv["y"], gain["ln_ffn_pre"][l], gain["ln_mix_post"][l], "post_attn_bwd", tr)
        dw["w_out"] = _mm_wgrad(sv["on"], dy, False, "wg_out")
        don = _mm_dgrad(dy, wg["w_out"], False, "dg_out")
        do_sb, do_dl, grad_gain["ln_head"][l] = _headnorm_bwd(don, sv["o_sb"], sv["o_dl"], gain["ln_head"][l], tr)
        dq_s, dk_s, dv_s = _sb_bwd(sv["proj"], sv["lt_sb"], do_sb, n_sb, bq)
        dq_d, dk_d, dv_d, ds_layers[l] = _dil_bwd(sv["proj"], do_dl, sv["o_dl"], sv["lse_dl"], bias_mask, n_sb, n_dl)
        dproj = jnp.concatenate([dq_s, dk_s, dv_s, dq_d, dk_d, dv_d], axis=1)
        dw["w_in"] = _mm_wgrad(sv["h1"], dproj, True, "wg_in")
        dh1 = _mm_dgrad(dproj, wg["w_in"], True, "dg_in", tc_cap=1536)
        dx, grad_gain["ln_mix_pre"][l] = _norm_in_bwd(dx2, dh1, sv["x"], gain["ln_mix_pre"][l], tr)
        if pending is not None:
            reduce_layer(*pending, dx)
        dws = [dw[n] for n in BIG]
        slots = _place_local(
            dws, [(N_DEV, a.shape[1] // 2, a.shape[2]) for a in dws],
            lambda ref, x, y, c: ref.at[2 * x + y, pl.ds(c * (ref.shape[1] // 2), ref.shape[1] // 2)],
            lambda ref, x, y, c: ref.at[4 * x + 2 * y + c], "place_pieces")
        pending = (l, _split_start(dws, slots, zero_token, _scatter_copies, N_DEV, f"scatter_start_{l}"))
    reduce_layer(*pending, pending[1][4])

    db = _rel_bias_grad(jnp.stack(ds_layers, axis=0), buckets)
    rb_flat = db[:n_dl, :NUM_BUCKETS].T.reshape(1, NUM_BUCKETS * n_dl)
    def widen(v):
        return jnp.pad(v, ((0, 0), (0, d_model - v.shape[1])))
    small_rows = [grad_gain[n][l] for n in SMALL for l in range(depth)] + [widen(rb_flat), widen(loss_part)]
    n_rows = len(small_rows)
    small = jnp.concatenate(small_rows + [jnp.zeros((-n_rows % 8, d_model), F32)], axis=0)
    total = _allreduce_small(small)
    grads = {}
    for i, n in enumerate(SMALL):
        grads[n] = total[i * depth:(i + 1) * depth]
    grads["rel_bias"] = total[len(SMALL) * depth, :NUM_BUCKETS * n_dl].reshape(NUM_BUCKETS, n_dl)
    loss = (0.5 / d_model) * jnp.sum(total[len(SMALL) * depth + 1, :LANE])
    for n in BIG:
        grads[n] = jnp.stack(grad_big[n], axis=0)

    delta, new_m, new_v = {}, {}, {}
    for n in WEIGHTS:
        delta[n], new_m[n], new_v[n] = _adamw(weights[n], grads[n], mom1[n], mom2[n], n)
    return (loss, dx[None], *[grads[n] for n in WEIGHTS], *[delta[n] for n in WEIGHTS],
            *[new_m[n] for n in WEIGHTS], *[new_v[n] for n in WEIGHTS])
```

```python
import functools
import math

import jax
import jax.numpy as jnp
from jax import lax
from jax.experimental import pallas as pl
from jax.experimental.pallas import tpu as pltpu

F32 = jnp.float32
BF16 = jnp.bfloat16

HEAD_DIM = 128
RMS_EPS = 1e-6
DILATIONS = (1, 4, 16)
NUM_BUCKETS = 32
MAX_DISTANCE = 2048
NEG = -1e30
N_CHIPS = 4
N_DEV = 8
JOIN_CHUNKS = 8

ADAM_LR = 0.001
ADAM_B1 = 0.9
ADAM_B2 = 0.999
ADAM_EPS = 1e-08
ADAM_WD = 0.01
ADAM_STEP = 10

V7X_VMEM_LIMIT = 48 * 1024 * 1024
LANE = 128

NN = (((1,), (0,)), ((), ()))
NT = (((1,), (1,)), ((), ()))
TN = (((0,), (0,)), ((), ()))
MESH = pl.DeviceIdType.MESH

BIG = ("w_in", "w_out", "w_gate_up", "w_down", "w_pli_gate", "w_pli_proj")
COL_SHARDED = {"w_in": True, "w_out": False, "w_gate_up": True, "w_down": False,
               "w_pli_gate": False, "w_pli_proj": True}
SMALL = ("ln_mix_pre", "ln_head", "ln_mix_post", "ln_ffn_pre", "ln_ffn_post", "ln_pli")
WEIGHTS = ("ln_mix_pre", "w_in", "ln_head", "w_out", "ln_mix_post", "rel_bias", "ln_ffn_pre",
           "w_gate_up", "w_down", "ln_ffn_post", "ln_pli", "w_pli_gate", "w_pli_proj")


def _tile(n, cap):
    t = min(n, cap) // LANE * LANE
    while t >= LANE:
        if n % t == 0:
            return t
        t -= LANE
    return n


def _params(sem=None):
    return pltpu.CompilerParams(dimension_semantics=sem, vmem_limit_bytes=V7X_VMEM_LIMIT)


def _rowwise(fn, rows, vecs, outs, sums, name, tr, after=()):
    s = rows[0].shape[0]
    nr, nv, no, ns, na = len(rows), len(vecs), len(outs), len(sums), len(after)

    def body(*refs):
        ins = [r[...] for r in refs[:nr + nv]]
        res = fn(*ins)
        out_refs = refs[nr + nv + na:nr + nv + na + no]
        sum_refs = refs[nr + nv + na + no:]
        for o_ref, val in zip(out_refs, res[:no]):
            o_ref[...] = val.astype(o_ref.dtype)
        if ns:
            @pl.when(pl.program_id(0) == 0)
            def _():
                for s_ref in sum_refs:
                    s_ref[...] = jnp.zeros_like(s_ref)
            for s_ref, val in zip(sum_refs, res[no:]):
                s_ref[...] += jnp.sum(val, axis=0, keepdims=True)

    in_specs = [pl.BlockSpec((tr, r.shape[1]), lambda i: (i, 0)) for r in rows]
    in_specs += [pl.BlockSpec(v.shape, lambda i: (0, 0)) for v in vecs]
    in_specs += [pl.BlockSpec(memory_space=pl.ANY)] * na
    out_specs = [pl.BlockSpec((tr, c), lambda i: (i, 0)) for c, _ in outs]
    out_specs += [pl.BlockSpec((1, c), lambda i: (0, 0)) for c in sums]
    out_shape = [jax.ShapeDtypeStruct((s, c), dt) for c, dt in outs]
    out_shape += [jax.ShapeDtypeStruct((1, c), F32) for c in sums]
    return pl.pallas_call(
        body, grid=(s // tr,), in_specs=in_specs, out_specs=out_specs, out_shape=out_shape,
        compiler_params=_params(("arbitrary",) if ns else ("parallel",)), name=name,
    )(*rows, *vecs, *after)


def _rms_r(x):
    return lax.rsqrt(jnp.mean(x * x, axis=-1, keepdims=True) + RMS_EPS)


def _rms_bwd(x, g, dy):
    r = _rms_r(x)
    u = dy * g
    dx = r * (u - x * (r * r) * jnp.mean(u * x, axis=-1, keepdims=True))
    return dx, dy * x * r


def _sigmoid(z):
    return 1.0 / (1.0 + jnp.exp(-z))


def _norm_in(x, g, tr):
    d = x.shape[1]
    return _rowwise(lambda x, g: (x * _rms_r(x) * g,), [x], [g], [(d, BF16)], [], "norm_in", tr)[0]


def _norm_in_bwd(dx_res, dh, x, g, tr):
    d = x.shape[1]

    def fn(dx_res, dh, x, g):
        dx, dg = _rms_bwd(x, g, dh)
        return dx_res + dx, dg
    return _rowwise(fn, [dx_res, dh, x], [g], [(d, F32)], [d], "norm_in_bwd", tr)


def _headnorm(o_sb, o_dl, g, tr):
    d = g.shape[1]

    def fn(o_sb, o_dl, g):
        o = jnp.concatenate([o_sb, o_dl], axis=1)
        parts = []
        for h in range(d // HEAD_DIM):
            sl = slice(h * HEAD_DIM, (h + 1) * HEAD_DIM)
            oh = o[:, sl]
            parts.append(oh * _rms_r(oh) * g[:, sl])
        return (jnp.concatenate(parts, axis=1),)
    return _rowwise(fn, [o_sb, o_dl], [g], [(d, BF16)], [], "headnorm", tr)[0]


def _headnorm_bwd(don, o_sb, o_dl, g, tr):
    d = g.shape[1]
    n_sb = o_sb.shape[1]

    def fn(don, o_sb, o_dl, g):
        o = jnp.concatenate([o_sb, o_dl], axis=1)
        dos, dgs = [], []
        for h in range(d // HEAD_DIM):
            sl = slice(h * HEAD_DIM, (h + 1) * HEAD_DIM)
            dx, dg = _rms_bwd(o[:, sl], g[:, sl], don[:, sl])
            dos.append(dx)
            dgs.append(dg)
        do = jnp.concatenate(dos, axis=1)
        return do[:, :n_sb], do[:, n_sb:], jnp.concatenate(dgs, axis=1)
    return _rowwise(fn, [don, o_sb, o_dl], [g], [(n_sb, F32), (d - n_sb, F32)], [d], "headnorm_bwd", tr)


def _res_norm(x, y, g_post, g_pre, name, tr):
    d = x.shape[1]

    def fn(x, y, g_post, g_pre):
        x2 = x + y * _rms_r(y) * g_post
        return x2, x2 * _rms_r(x2) * g_pre
    return _rowwise(fn, [x, y], [g_post, g_pre], [(d, F32), (d, BF16)], [], name, tr)


def _res_norm_bwd(dx_res, dh, x2, y, g_pre, g_post, name, tr):
    d = x2.shape[1]

    def fn(dx_res, dh, x2, y, g_pre, g_post):
        dxa, dg_pre = _rms_bwd(x2, g_pre, dh)
        dx2 = dx_res + dxa
        dy, dg_post = _rms_bwd(y, g_post, dx2)
        return dx2, dy, dg_pre, dg_post
    return _rowwise(fn, [dx_res, dh, x2, y], [g_pre, g_post], [(d, F32), (d, BF16)], [d, d], name, tr)


def _swiglu(gu, tr):
    ff = gu.shape[1] // 2

    def fn(gu):
        g, u = gu[:, :ff], gu[:, ff:]
        return (g * _sigmoid(g) * u,)
    return _rowwise(fn, [gu], [], [(ff, BF16)], [], "swiglu", tr)[0]


def _swiglu_bwd(dact, gu, tr):
    ff = gu.shape[1] // 2

    def fn(dact, gu):
        g, u = gu[:, :ff], gu[:, ff:]
        sg = _sigmoid(g)
        dg = dact * u * sg * (1.0 + g * (1.0 - sg))
        du = dact * g * sg
        return (jnp.concatenate([dg, du], axis=1),)
    return _rowwise(fn, [dact, gu], [], [(2 * ff, BF16)], [], "swiglu_bwd", tr)[0]


def _pli_out(x3, gl, pp, g_next, tr):
    d = x3.shape[1]

    def fn(x3, gl, pp, g):
        x4 = x3 + _sigmoid(gl) * pp
        return x4, x4 * _rms_r(x4) * g
    return _rowwise(fn, [x3, gl, pp], [g_next], [(d, F32), (d, BF16)], [], "pli_out", tr)


def _loss_head(x3, gl, pp, target, tr):
    d = x3.shape[1]

    def fn(x3, gl, pp, t):
        err = x3 + _sigmoid(gl) * pp - t
        sq = err * err
        part = sq[:, :LANE]
        for k in range(1, d // LANE):
            part = part + sq[:, k * LANE:(k + 1) * LANE]
        return err * (1.0 / d), part
    return _rowwise(fn, [x3, gl, pp, target], [], [(d, F32)], [LANE], "loss_head", tr)


def _pli_bwd(dx, gl, pp, tr, after=()):
    d = dx.shape[1]

    def fn(dx, gl, pp):
        gate = _sigmoid(gl)
        return dx * gate, dx * pp * gate * (1.0 - gate)
    return _rowwise(fn, [dx, gl, pp], [], [(d, BF16), (d, BF16)], [], "pli_bwd", tr, after)


def _my_chip():
    return 2 * lax.axis_index("x") + lax.axis_index("y")


def _cast_layer(w, layer, name):
    _, r, c = w.shape
    tr = r
    while tr * c * 4 > (4 << 20) and tr % 32 == 0:
        tr //= 2

    def body(w_ref, o_ref):
        o_ref[...] = w_ref[...].astype(o_ref.dtype)

    return pl.pallas_call(
        body, grid=(r // tr,),
        in_specs=[pl.BlockSpec((None, tr, c), lambda i: (layer, i, 0))],
        out_specs=pl.BlockSpec((None, tr, c), lambda i: (_my_chip(), i, 0)),
        out_shape=jax.ShapeDtypeStruct((N_CHIPS, r, c), BF16),
        compiler_params=_params(("parallel",)), name="cast_" + name,
    )(w)


def _reduce_piece(dw, recv, name):
    _, r, c = dw.shape
    h = r // 2
    tr = h
    while tr * c * 2 * N_DEV > (8 << 20) and tr % 32 == 0:
        tr //= 2
    per = h // tr

    def body(d_ref, r_ref, o_ref):
        acc = d_ref[...].astype(F32)
        for i in range(N_DEV - 1):
            acc = acc + r_ref[i].astype(F32)
        o_ref[...] = acc

    return pl.pallas_call(
        body, grid=(per,),
        in_specs=[pl.BlockSpec((None, tr, c), lambda i: (_my_chip(), lax.axis_index("c") * per + i, 0)),
                  pl.BlockSpec((N_DEV - 1, tr, c), lambda i: (0, i, 0))],
        out_specs=pl.BlockSpec((tr, c), lambda i: (lax.axis_index("c") * per + i, 0)),
        out_shape=jax.ShapeDtypeStruct((r, c), F32),
        compiler_params=_params(("parallel",)), name="reduce_" + name,
    )(dw, recv)


def _adamw(w, g, m, v, name):
    shape = w.shape
    if w.ndim == 3:
        w, g, m, v = (a.reshape(shape[0] * shape[1], shape[2]) for a in (w, g, m, v))
    r, c = w.shape
    tr = r
    while tr * c * 4 > (1 << 20) and tr % 16 == 0:
        tr //= 2

    def body(w_ref, g_ref, m_ref, v_ref, d_ref, m2_ref, v2_ref):
        g = g_ref[...]
        m2 = ADAM_B1 * m_ref[...] + (1.0 - ADAM_B1) * g
        v2 = ADAM_B2 * v_ref[...] + (1.0 - ADAM_B2) * (g * g)
        m_hat = m2 / (1.0 - ADAM_B1 ** ADAM_STEP)
        v_hat = v2 / (1.0 - ADAM_B2 ** ADAM_STEP)
        d_ref[...] = -ADAM_LR * (m_hat / (jnp.sqrt(v_hat) + ADAM_EPS) + ADAM_WD * w_ref[...])
        m2_ref[...] = m2
        v2_ref[...] = v2

    spec = pl.BlockSpec((tr, c), lambda i: (i, 0))
    res = pl.pallas_call(
        body, grid=(r // tr,), in_specs=[spec] * 4, out_specs=[spec] * 3,
        out_shape=[jax.ShapeDtypeStruct((r, c), F32)] * 3,
        compiler_params=_params(("parallel",)), name="adamw_" + name,
    )(w, g, m, v)
    return tuple(a.reshape(shape) for a in res)


def _mm(a, b, grid, a_spec, b_spec, o_spec, o_shape, o_dtype, dims, acc_shape, name):
    nk = grid[2]

    def body(a_ref, b_ref, o_ref, acc_ref):
        k = pl.program_id(2)

        @pl.when(k == 0)
        def _():
            acc_ref[...] = jnp.zeros_like(acc_ref)

        acc_ref[...] += lax.dot_general(a_ref[...].astype(BF16), b_ref[...].astype(BF16), dims,
                                        preferred_element_type=F32)

        @pl.when(k == nk - 1)
        def _():
            o_ref[...] = acc_ref[...].astype(o_ref.dtype)

    return pl.pallas_call(
        body, grid=grid, in_specs=[a_spec, b_spec], out_specs=o_spec,
        out_shape=jax.ShapeDtypeStruct(o_shape, o_dtype),
        scratch_shapes=[pltpu.VMEM(acc_shape, F32)],
        compiler_params=_params(("parallel", "parallel", "arbitrary")), name=name,
    )(a, b)


def _mm_fwd(a, wg, col, name, tm=1024, tn_cap=1536, tk_cap=1024):
    m, k = a.shape
    ns, r, c = wg.shape
    tm = min(tm, m)
    if col:
        n, tn, tk = ns * c, _tile(c, tn_cap), _tile(k, tk_cap)
        per = c // tn
        b_spec = pl.BlockSpec((None, tk, tn), lambda j, i, kk: (j // per, kk, j % per))
    else:
        n, tn, tk = c, _tile(c, tn_cap), _tile(r, tk_cap)
        per = r // tk
        b_spec = pl.BlockSpec((None, tk, tn), lambda j, i, kk: (kk // per, kk % per, j))
    return _mm(a, wg, (n // tn, m // tm, k // tk), pl.BlockSpec((tm, tk), lambda j, i, kk: (i, kk)), b_spec,
               pl.BlockSpec((tm, tn), lambda j, i, kk: (i, j)), (m, n), F32, NN, (tm, tn), name)


def _mm_dgrad(dc, wg, col, name, tm=1024, to_cap=1536, tc_cap=1024):
    m, n = dc.shape
    ns, r, c = wg.shape
    tm = min(tm, m)
    if col:
        kout, to, tc = r, _tile(r, to_cap), _tile(c, tc_cap)
        per = c // tc
        b_spec = pl.BlockSpec((None, to, tc), lambda j, i, kk: (kk // per, j, kk % per))
    else:
        kout, to, tc = ns * r, _tile(r, to_cap), _tile(c, tc_cap)
        per = r // to
        b_spec = pl.BlockSpec((None, to, tc), lambda j, i, kk: (j // per, j % per, kk))
    return _mm(dc, wg, (kout // to, m // tm, n // tc), pl.BlockSpec((tm, tc), lambda j, i, kk: (i, kk)), b_spec,
               pl.BlockSpec((tm, to), lambda j, i, kk: (i, j)), (m, kout), F32, NT, (tm, to), name)


def _mm_wgrad(a, dc, col, name, ti_cap=1536, tn_cap=1536, tkm=1024):
    m, k = a.shape
    n = dc.shape[1]
    tkm = min(tkm, m)
    if col:
        r, c = k, n // N_CHIPS
        ti, tn = _tile(r, ti_cap), _tile(c, tn_cap)
        per = c // tn
        o_spec = pl.BlockSpec((None, ti, tn), lambda i, j, kk: (j // per, i, j % per))
    else:
        r, c = k // N_CHIPS, n
        ti, tn = _tile(r, ti_cap), _tile(c, tn_cap)
        per = r // ti
        o_spec = pl.BlockSpec((None, ti, tn), lambda i, j, kk: (i // per, i % per, j))
    return _mm(a, dc, (k // ti, n // tn, m // tkm), pl.BlockSpec((tkm, ti), lambda i, j, kk: (kk, i)),
               pl.BlockSpec((tkm, tn), lambda i, j, kk: (kk, j)), o_spec, (N_CHIPS, r, c), BF16, TN, (ti, tn), name)


def _log_keep(z):
    return -(jnp.maximum(z, 0.0) + jnp.log(1.0 + jnp.exp(-jnp.abs(z))))


def _split_dot(x, t):
    hi = x.astype(BF16)
    lo = (x - hi.astype(F32)).astype(BF16)
    return (lax.dot_general(hi, t, NN, preferred_element_type=F32)
            + lax.dot_general(lo, t, NN, preferred_element_type=F32))


def _sb_fwd(proj, n_sb, bq):
    s = proj.shape[0]
    scale = 1.0 / math.sqrt(HEAD_DIM)

    def body(q_ref, k_ref, v_ref, o_ref, lt_ref):
        i = pl.program_id(1)
        q = q_ref[...].astype(BF16)
        row = lax.broadcasted_iota(jnp.int32, (bq, bq), 0)
        col = lax.broadcasted_iota(jnp.int32, (bq, bq), 1)
        later_in_block = (row > col).astype(BF16)
        keep = col < row

        def block(j, carry, diagonal):
            c, acc = carry
            ks = pl.multiple_of(j * bq, bq)
            kb = k_ref[pl.ds(ks, bq), :].astype(BF16)
            vb = v_ref[pl.ds(ks, bq), :].astype(BF16)
            z = lax.dot_general(q, kb, NT, preferred_element_type=F32) * scale
            lk = _log_keep(z)
            if diagonal:
                lk = jnp.where(keep, lk, 0.0)
            e = z + lk + _split_dot(lk, later_in_block) + c
            if diagonal:
                e = jnp.where(keep, e, NEG)
            a = jnp.exp(e)
            acc = acc + lax.dot_general(a.astype(BF16), vb, NN, preferred_element_type=F32)
            return c + jnp.sum(lk, axis=1, keepdims=True), acc

        carry = block(i, (jnp.zeros((bq, 1), F32), jnp.zeros((bq, HEAD_DIM), F32)), True)
        carry = lax.fori_loop(0, i, lambda t, cr: block(i - 1 - t, cr, False), carry)
        o_ref[...] = carry[1]
        lt_ref[...] = jnp.broadcast_to(carry[0], (bq, HEAD_DIM))

    blk = pl.BlockSpec((bq, HEAD_DIM), lambda h, i: (i, h))
    shp = jax.ShapeDtypeStruct((s, n_sb * HEAD_DIM), F32)
    return pl.pallas_call(
        body, grid=(n_sb, s // bq),
        in_specs=[blk,
                  pl.BlockSpec((s, HEAD_DIM), lambda h, i: (0, n_sb + h)),
                  pl.BlockSpec((s, HEAD_DIM), lambda h, i: (0, 2 * n_sb + h))],
        out_specs=[blk, blk], out_shape=[shp, shp],
        compiler_params=_params(("parallel", "parallel")), name="sb_fwd",
    )(proj, proj, proj)


def _sb_bwd(proj, lt, do, n_sb, bq):
    s = proj.shape[0]
    nq = s // bq
    scale = 1.0 / math.sqrt(HEAD_DIM)

    def body(q_ref, k_ref, v_ref, lt_ref, do_ref, dq_ref, dk_ref, dv_ref, dk_acc, dv_acc):
        i = pl.program_id(1)

        @pl.when(i == 0)
        def _():
            dk_acc[...] = jnp.zeros_like(dk_acc)
            dv_acc[...] = jnp.zeros_like(dv_acc)

        q = q_ref[...].astype(BF16)
        do_b = do_ref[...].astype(BF16)
        ltot = jnp.max(lt_ref[...], axis=1, keepdims=True)
        row = lax.broadcasted_iota(jnp.int32, (bq, bq), 0)
        col = lax.broadcasted_iota(jnp.int32, (bq, bq), 1)
        upto_in_block = (row <= col).astype(BF16)
        before_in_block = (row < col).astype(BF16)
        keep = col < row

        def block(j, carry, diagonal):
            pk, pg, dq = carry
            ks = pl.multiple_of(j * bq, bq)
            kb = k_ref[pl.ds(ks, bq), :].astype(BF16)
            vb = v_ref[pl.ds(ks, bq), :].astype(BF16)
            z = lax.dot_general(q, kb, NT, preferred_element_type=F32) * scale
            lk = _log_keep(z)
            if diagonal:
                lk = jnp.where(keep, lk, 0.0)
            e = z + lk + ((ltot - pk) - _split_dot(lk, upto_in_block))
            if diagonal:
                e = jnp.where(keep, e, NEG)
            a = jnp.exp(e)
            da = lax.dot_general(do_b, vb, NT, preferred_element_type=F32)
            g = a * da
            dv_acc[pl.ds(ks, bq), :] += lax.dot_general(a.astype(BF16), do_b, TN, preferred_element_type=F32)
            before = pg + _split_dot(g, before_in_block)
            dz = g * jnp.exp(lk) - before * jnp.exp(z + lk)
            if diagonal:
                dz = jnp.where(keep, dz, 0.0)
            dz_b = (dz * scale).astype(BF16)
            dq = dq + lax.dot_general(dz_b, kb, NN, preferred_element_type=F32)
            dk_acc[pl.ds(ks, bq), :] += lax.dot_general(dz_b, q, TN, preferred_element_type=F32)
            return (pk + jnp.sum(lk, axis=1, keepdims=True), pg + jnp.sum(g, axis=1, keepdims=True), dq)

        zero = jnp.zeros((bq, 1), F32)
        carry = lax.fori_loop(0, i, lambda j, cr: block(j, cr, False), (zero, zero, jnp.zeros((bq, HEAD_DIM), F32)))
        carry = block(i, carry, True)
        dq_ref[...] = carry[2].astype(dq_ref.dtype)

        @pl.when(i == nq - 1)
        def _():
            dk_ref[...] = dk_acc[...].astype(dk_ref.dtype)
            dv_ref[...] = dv_acc[...].astype(dv_ref.dtype)

    blk = pl.BlockSpec((bq, HEAD_DIM), lambda h, i: (i, h))
    full = pl.BlockSpec((s, HEAD_DIM), lambda h, i: (0, h))
    shp = jax.ShapeDtypeStruct((s, n_sb * HEAD_DIM), BF16)
    return pl.pallas_call(
        body, grid=(n_sb, nq),
        in_specs=[blk,
                  pl.BlockSpec((s, HEAD_DIM), lambda h, i: (0, n_sb + h)),
                  pl.BlockSpec((s, HEAD_DIM), lambda h, i: (0, 2 * n_sb + h)),
                  blk, blk],
        out_specs=[blk, full, full], out_shape=[shp, shp, shp],
        scratch_shapes=[pltpu.VMEM((s, HEAD_DIM), F32), pltpu.VMEM((s, HEAD_DIM), F32)],
        compiler_params=_params(("parallel", "arbitrary")), name="sb_bwd",
    )(proj, proj, proj, lt, do)


def _t5_bucket(dist):
    max_exact = NUM_BUCKETS // 2
    d = jnp.maximum(dist, 1).astype(F32)
    large = max_exact + (jnp.log(d / max_exact) / math.log(MAX_DISTANCE / max_exact)
                         * (NUM_BUCKETS - max_exact)).astype(jnp.int32)
    large = jnp.minimum(large, NUM_BUCKETS - 1)
    return jnp.where(dist < max_exact, dist, large)


def _dil_tables(rel_bias):
    qi = jnp.arange(HEAD_DIM, dtype=jnp.int32)[:, None]
    ki = jnp.arange(2 * HEAD_DIM, dtype=jnp.int32)[None, :]
    rel = HEAD_DIM + qi - ki
    band = (rel >= 0) & (rel <= HEAD_DIM)
    biases, buckets = [], []
    for d in DILATIONS:
        bucket = _t5_bucket(jnp.maximum(rel, 0) * d)
        onehot = (bucket[:, :, None] == jnp.arange(NUM_BUCKETS, dtype=jnp.int32)).astype(F32)
        bias = jnp.einsum("qkb,bh->hqk", onehot, rel_bias.astype(F32), precision=lax.Precision.HIGHEST)
        biases.append(jnp.where(band[None], bias, NEG))
        buckets.append(jnp.where(band, bucket, -1).astype(F32))
    return jnp.stack(biases, axis=1), jnp.stack(buckets, axis=0)


def _sub_rows(ref, start, d):
    if d == 1:
        return ref[pl.ds(pl.multiple_of(start, HEAD_DIM), HEAD_DIM), :]
    return ref[pl.ds(start, HEAD_DIM, stride=d), :]


def _sub_idx(start, d):
    if d == 1:
        return pl.ds(pl.multiple_of(start, HEAD_DIM), HEAD_DIM)
    return pl.ds(start, HEAD_DIM, stride=d)


def _dil_logits(q_ref, k_ref, bm, n, cur, prv, d, scale):
    qb = _sub_rows(q_ref, cur, d).astype(BF16)
    kk = jnp.concatenate([_sub_rows(k_ref, prv, d), _sub_rows(k_ref, cur, d)], axis=0).astype(BF16)
    sc = lax.dot_general(qb, kk, NT, preferred_element_type=F32) * scale + bm
    colk = lax.broadcasted_iota(jnp.int32, sc.shape, 1)
    sc = jnp.where((colk >= HEAD_DIM) | (n > 0), sc, NEG)
    return qb, kk, sc


def _dil_fwd(proj, bm, n_sb, n_dl):
    s = proj.shape[0]
    scale = 1.0 / math.sqrt(HEAD_DIM)
    chunk = min(s, 512)

    def body(q_ref, k_ref, v_ref, bm_ref, o_ref, l_ref, ob0, ob1, ob2, lb0, lb1, lb2):
        obs, lbs = (ob0, ob1, ob2), (lb0, lb1, lb2)
        for b, d in enumerate(DILATIONS):
            nb = s // (HEAD_DIM * d)

            def step(idx, _, b=b, d=d, nb=nb):
                r, n = idx // nb, idx % nb
                cur = n * (HEAD_DIM * d) + r
                prv = jnp.maximum(n - 1, 0) * (HEAD_DIM * d) + r
                _, _, sc = _dil_logits(q_ref, k_ref, bm_ref[b], n, cur, prv, d, scale)
                vv = jnp.concatenate([_sub_rows(v_ref, prv, d), _sub_rows(v_ref, cur, d)], axis=0).astype(BF16)
                mx = jnp.max(sc, axis=1, keepdims=True)
                pr = jnp.exp(sc - mx)
                den = jnp.sum(pr, axis=1, keepdims=True)
                o = lax.dot_general(pr.astype(BF16), vv, NN, preferred_element_type=F32) / den
                obs[b][_sub_idx(cur, d), :] = o
                lbs[b][_sub_idx(cur, d), :] = jnp.broadcast_to(mx + jnp.log(den), (HEAD_DIM, HEAD_DIM))
                return 0

            lax.fori_loop(0, s // HEAD_DIM, step, 0)

        for ci in range(s // chunk):
            sl = pl.ds(ci * chunk, chunk)
            l0, l1, l2 = lb0[sl, :], lb1[sl, :], lb2[sl, :]
            mx = jnp.maximum(jnp.maximum(l0, l1), l2)
            w0, w1, w2 = jnp.exp(l0 - mx), jnp.exp(l1 - mx), jnp.exp(l2 - mx)
            tot = w0 + w1 + w2
            o_ref[sl, :] = (w0 * ob0[sl, :] + w1 * ob1[sl, :] + w2 * ob2[sl, :]) / tot
            l_ref[sl, :] = mx + jnp.log(tot)

    base = 3 * n_sb
    full = pl.BlockSpec((s, HEAD_DIM), lambda h: (0, h))
    shp = jax.ShapeDtypeStruct((s, n_dl * HEAD_DIM), F32)
    return pl.pallas_call(
        body, grid=(n_dl,),
        in_specs=[pl.BlockSpec((s, HEAD_DIM), lambda h: (0, base + h)),
                  pl.BlockSpec((s, HEAD_DIM), lambda h: (0, base + n_dl + h)),
                  pl.BlockSpec((s, HEAD_DIM), lambda h: (0, base + 2 * n_dl + h)),
                  pl.BlockSpec((None, 3, HEAD_DIM, 2 * HEAD_DIM), lambda h: (h, 0, 0, 0))],
        out_specs=[full, full], out_shape=[shp, shp],
        scratch_shapes=[pltpu.VMEM((s, HEAD_DIM), F32)] * 6,
        compiler_params=_params(("parallel",)), name="dil_fwd",
    )(proj, proj, proj, bm)


def _dil_bwd(proj, do, o, lse, bm, n_sb, n_dl):
    s = proj.shape[0]
    scale = 1.0 / math.sqrt(HEAD_DIM)
    chunk = min(s, 512)

    def body(q_ref, k_ref, v_ref, do_ref, o_ref, l_ref, bm_ref, dq_ref, dk_ref, dv_ref, ds_ref, dq_s, dk_s, dv_s):
        dq_s[...] = jnp.zeros_like(dq_s)
        dk_s[...] = jnp.zeros_like(dk_s)
        dv_s[...] = jnp.zeros_like(dv_s)
        ds_ref[...] = jnp.zeros_like(ds_ref)
        for b, d in enumerate(DILATIONS):
            nb = s // (HEAD_DIM * d)

            def step(idx, _, b=b, d=d, nb=nb):
                r, n = idx // nb, idx % nb
                cur = n * (HEAD_DIM * d) + r
                prv = jnp.maximum(n - 1, 0) * (HEAD_DIM * d) + r
                qb, kk, sc = _dil_logits(q_ref, k_ref, bm_ref[b], n, cur, prv, d, scale)
                vv = jnp.concatenate([_sub_rows(v_ref, prv, d), _sub_rows(v_ref, cur, d)], axis=0).astype(BF16)
                do_f = _sub_rows(do_ref, cur, d)
                do_b = do_f.astype(BF16)
                delta = jnp.sum(do_f * _sub_rows(o_ref, cur, d), axis=1, keepdims=True)
                lr = _sub_rows(l_ref, cur, d)
                w = jnp.exp(sc - jnp.concatenate([lr, lr], axis=1))
                dp = lax.dot_general(do_b, vv, NT, preferred_element_type=F32)
                ds = w * (dp - delta)
                ds_ref[b] += ds
                ds_b = (ds * scale).astype(BF16)
                dv_blk = lax.dot_general(w.astype(BF16), do_b, TN, preferred_element_type=F32)
                dk_blk = lax.dot_general(ds_b, qb, TN, preferred_element_type=F32)
                ci, pi = _sub_idx(cur, d), _sub_idx(prv, d)
                dq_s[ci, :] += lax.dot_general(ds_b, kk, NN, preferred_element_type=F32)
                dk_s[ci, :] += dk_blk[HEAD_DIM:]
                dv_s[ci, :] += dv_blk[HEAD_DIM:]
                dk_s[pi, :] += dk_blk[:HEAD_DIM]
                dv_s[pi, :] += dv_blk[:HEAD_DIM]
                return 0

            lax.fori_loop(0, s // HEAD_DIM, step, 0)

        for ci in range(s // chunk):
            sl = pl.ds(ci * chunk, chunk)
            dq_ref[sl, :] = dq_s[sl, :].astype(dq_ref.dtype)
            dk_ref[sl, :] = dk_s[sl, :].astype(dk_ref.dtype)
            dv_ref[sl, :] = dv_s[sl, :].astype(dv_ref.dtype)

    base = 3 * n_sb
    full = pl.BlockSpec((s, HEAD_DIM), lambda h: (0, h))
    tab = pl.BlockSpec((None, 3, HEAD_DIM, 2 * HEAD_DIM), lambda h: (h, 0, 0, 0))
    shp = jax.ShapeDtypeStruct((s, n_dl * HEAD_DIM), BF16)
    return pl.pallas_call(
        body, grid=(n_dl,),
        in_specs=[pl.BlockSpec((s, HEAD_DIM), lambda h: (0, base + h)),
                  pl.BlockSpec((s, HEAD_DIM), lambda h: (0, base + n_dl + h)),
                  pl.BlockSpec((s, HEAD_DIM), lambda h: (0, base + 2 * n_dl + h)),
                  full, full, full, tab],
        out_specs=[full, full, full, tab],
        out_shape=[shp, shp, shp, jax.ShapeDtypeStruct((n_dl, 3, HEAD_DIM, 2 * HEAD_DIM), F32)],
        scratch_shapes=[pltpu.VMEM((s, HEAD_DIM), F32)] * 3,
        compiler_params=_params(("parallel",)), name="dil_bwd",
    )(proj, proj, proj, do, o, lse, bm)


def _rel_bias_grad(ds_all, buckets):
    depth, n_dl = ds_all.shape[:2]
    rows = -(-n_dl // 8) * 8

    def body(ds_ref, bk_ref, o_ref):
        lane = lax.broadcasted_iota(jnp.int32, (1, LANE), 1)

        def one_bucket(bkt, acc):
            fb = bkt.astype(F32)
            out = []
            for h in range(n_dl):
                val = jnp.zeros((1, 1), F32)
                for b in range(3):
                    tot = ds_ref[0, h, b]
                    for l in range(1, depth):
                        tot = tot + ds_ref[l, h, b]
                    val = val + jnp.sum(jnp.where(bk_ref[b] == fb, tot, 0.0), keepdims=True)
                out.append(jnp.where(lane == bkt, val, 0.0))
            out += [jnp.zeros((1, LANE), F32)] * (rows - n_dl)
            return acc + jnp.concatenate(out, axis=0)

        o_ref[...] = lax.fori_loop(0, NUM_BUCKETS, one_bucket, jnp.zeros((rows, LANE), F32))

    return pl.pallas_call(
        body, out_shape=jax.ShapeDtypeStruct((rows, LANE), F32),
        in_specs=[pl.BlockSpec(memory_space=pltpu.VMEM)] * 2, out_specs=pl.BlockSpec(memory_space=pltpu.VMEM),
        compiler_params=_params(), name="rel_bias_grad",
    )(ds_all, buckets)


def _place():
    return lax.axis_index("x"), lax.axis_index("y"), lax.axis_index("c")


def _flip(v, bit):
    return 1 - v if bit else v


HBM_SPEC = pl.BlockSpec(memory_space=pl.ANY)


HBM_ONLY = pl.BlockSpec(memory_space=pltpu.HBM)
SEM_SPEC = pl.BlockSpec(memory_space=pltpu.SEMAPHORE)
DATAFLOW = pltpu.SideEffectType.DATAFLOW_SIDE_EFFECTING


def _in_hbm(a):
    return pltpu.with_memory_space_constraint(a, pltpu.HBM)


def _split_start(arrays, token, copies_of, n_sem, name):
    na = len(arrays)

    def body(*refs):
        for cp in copies_of(refs[:na], refs[na + 1], refs[na + 2]):
            cp.start()
        refs[-1][...] = jnp.zeros_like(refs[-1])

    sems = pltpu.SemaphoreType.DMA((n_sem,))
    res = pl.pallas_call(
        body, name=name,
        out_shape=(sems, sems, *[pltpu.HBM(a.shape, a.dtype) for a in arrays], jax.ShapeDtypeStruct((8, LANE), F32)),
        in_specs=[HBM_ONLY] * na + [HBM_SPEC],
        out_specs=(SEM_SPEC, SEM_SPEC, *[HBM_ONLY] * na, pl.BlockSpec(memory_space=pltpu.VMEM)),
        input_output_aliases={i: 2 + i for i in range(na)},
        compiler_params=pltpu.CompilerParams(has_side_effects=DATAFLOW),
    )(*[_in_hbm(a) for a in arrays], token)
    return res[0], res[1], res[2:2 + na], res[-1]


def _split_wait(started, after, copies_of, name):
    send, recv, arrays, _ = started
    na = len(arrays)

    def body(*refs):
        for cp in copies_of(refs[:na], refs[na], refs[na + 1]):
            cp.wait_send()
            cp.wait_recv()

    return pl.pallas_call(
        body, name=name, out_shape=[pltpu.HBM(a.shape, a.dtype) for a in arrays],
        in_specs=[HBM_ONLY] * na + [SEM_SPEC, SEM_SPEC, HBM_SPEC], out_specs=[HBM_ONLY] * na,
        input_output_aliases={i: i for i in range(na)},
        compiler_params=pltpu.CompilerParams(has_side_effects=DATAFLOW),
    )(*arrays, send, recv, after)


def _gather_copies(buf, send, recv):
    x, y, c = _place()
    mine = 2 * x + y
    return [pltpu.make_async_remote_copy(
        src_ref=buf[w].at[mine], dst_ref=buf[w].at[mine], send_sem=send.at[4 * w + k], recv_sem=recv.at[4 * w + k],
        device_id=(_flip(x, k >> 1), _flip(y, k & 1), c), device_id_type=MESH)
        for w in range(len(buf)) for k in (1, 2, 3)]


def _scatter_copies(refs, send, recv):
    nw = len(refs) // 2
    src, buf = refs[:nw], refs[nw:]
    x, y, c = _place()
    out = []
    for w in range(nw):
        h = src[w].shape[1] // 2
        for k in range(1, N_DEV):
            px, py, pc = _flip(x, k >> 2), _flip(y, (k >> 1) & 1), _flip(c, k & 1)
            out.append(pltpu.make_async_remote_copy(
                src_ref=src[w].at[2 * px + py, pl.ds(pc * h, h)], dst_ref=buf[w].at[k - 1],
                send_sem=send.at[N_DEV * w + k], recv_sem=recv.at[N_DEV * w + k], device_id=(px, py, pc),
                device_id_type=MESH))
    return out


def _join_halves(grads):
    nw = len(grads)

    def body(*refs):
        ins, outs = refs[:nw], refs[nw:2 * nw]
        send, recv = refs[2 * nw:]
        x, y, c = _place()
        remote = []
        for w in range(nw):
            h = ins[w].shape[0] // 2
            hc = h // JOIN_CHUNKS
            for j in range(JOIN_CHUNKS):
                rows = pl.ds(c * h + j * hc, hc)
                cp = pltpu.make_async_remote_copy(
                    src_ref=ins[w].at[rows], dst_ref=outs[w].at[rows],
                    send_sem=send.at[w, j], recv_sem=recv.at[w, j], device_id=(x, y, 1 - c), device_id_type=MESH)
                cp.start()
                remote.append(cp)
        for w in range(nw):
            h = ins[w].shape[0] // 2
            hc = h // JOIN_CHUNKS
            for j in range(JOIN_CHUNKS):
                theirs = outs[w].at[pl.ds((1 - c) * h + j * hc, hc)]
                pltpu.make_async_remote_copy(
                    src_ref=theirs, dst_ref=theirs, send_sem=send.at[w, j], recv_sem=recv.at[w, j],
                    device_id=(x, y, c), device_id_type=MESH).wait_recv()
        for cp in remote:
            cp.wait_send()

    sems = [pltpu.SemaphoreType.DMA((nw, JOIN_CHUNKS))] * 2
    return pl.pallas_call(
        body, out_shape=[jax.ShapeDtypeStruct(a.shape, a.dtype) for a in grads],
        in_specs=[HBM_SPEC] * nw, out_specs=[HBM_SPEC] * nw, scratch_shapes=sems,
        input_output_aliases={i: i for i in range(nw)}, name="join_halves",
    )(*grads)


def _allreduce_small(v):
    rows, c = v.shape

    def body(v_ref, o_ref, buf, local_sem, send, recv):
        x, y, cc = _place()
        me = 4 * x + 2 * y + cc
        own = pltpu.make_async_copy(v_ref, buf.at[me], local_sem)
        own.start()
        sends = []
        for k in range(1, N_DEV):
            px, py, pc = _flip(x, k >> 2), _flip(y, (k >> 1) & 1), _flip(cc, k & 1)
            cp = pltpu.make_async_remote_copy(
                src_ref=v_ref, dst_ref=buf.at[me], send_sem=send.at[k], recv_sem=recv.at[k],
                device_id=(px, py, pc), device_id_type=MESH)
            cp.start()
            sends.append(cp)
        for k in range(1, N_DEV):
            px, py, pc = _flip(x, k >> 2), _flip(y, (k >> 1) & 1), _flip(cc, k & 1)
            slot = buf.at[4 * px + 2 * py + pc]
            pltpu.make_async_remote_copy(
                src_ref=slot, dst_ref=slot, send_sem=send.at[k], recv_sem=recv.at[k],
                device_id=(x, y, cc), device_id_type=MESH).wait_recv()
        for cp in sends:
            cp.wait_send()
        own.wait()
        acc = buf[0]
        for i in range(1, N_DEV):
            acc = acc + buf[i]
        o_ref[...] = acc

    return pl.pallas_call(
        body, out_shape=jax.ShapeDtypeStruct((rows, c), F32),
        in_specs=[pl.BlockSpec(memory_space=pltpu.VMEM)], out_specs=pl.BlockSpec(memory_space=pltpu.VMEM),
        scratch_shapes=[pltpu.VMEM((N_DEV, rows, c), F32), pltpu.SemaphoreType.DMA,
                        pltpu.SemaphoreType.DMA((N_DEV,)), pltpu.SemaphoreType.DMA((N_DEV,))],
        compiler_params=_params(),
        name="allreduce_small",
    )(v)


def kernel(x, p, ln_mix_pre, w_in, ln_head, w_out, ln_mix_post, rel_bias, ln_ffn_pre, w_gate_up, w_down, ln_ffn_post, ln_pli, w_pli_gate, w_pli_proj, loss_target, m_ln_mix_pre, m_w_in, m_ln_head, m_w_out, m_ln_mix_post, m_rel_bias, m_ln_ffn_pre, m_w_gate_up, m_w_down, m_ln_ffn_post, m_ln_pli, m_w_pli_gate, m_w_pli_proj, v_ln_mix_pre, v_w_in, v_ln_head, v_w_out, v_ln_mix_post, v_rel_bias, v_ln_ffn_pre, v_w_gate_up, v_w_down, v_ln_ffn_post, v_ln_pli, v_w_pli_gate, v_w_pli_proj):
    weights = dict(ln_mix_pre=ln_mix_pre, w_in=w_in, ln_head=ln_head, w_out=w_out, ln_mix_post=ln_mix_post,
                   rel_bias=rel_bias, ln_ffn_pre=ln_ffn_pre, w_gate_up=w_gate_up, w_down=w_down,
                   ln_ffn_post=ln_ffn_post, ln_pli=ln_pli, w_pli_gate=w_pli_gate, w_pli_proj=w_pli_proj)
    mom1 = dict(ln_mix_pre=m_ln_mix_pre, w_in=m_w_in, ln_head=m_ln_head, w_out=m_w_out, ln_mix_post=m_ln_mix_post,
                rel_bias=m_rel_bias, ln_ffn_pre=m_ln_ffn_pre, w_gate_up=m_w_gate_up, w_down=m_w_down,
                ln_ffn_post=m_ln_ffn_post, ln_pli=m_ln_pli, w_pli_gate=m_w_pli_gate, w_pli_proj=m_w_pli_proj)
    mom2 = dict(ln_mix_pre=v_ln_mix_pre, w_in=v_w_in, ln_head=v_ln_head, w_out=v_w_out, ln_mix_post=v_ln_mix_post,
                rel_bias=v_rel_bias, ln_ffn_pre=v_ln_ffn_pre, w_gate_up=v_w_gate_up, w_down=v_w_down,
                ln_ffn_post=v_ln_ffn_post, ln_pli=v_ln_pli, w_pli_gate=v_w_pli_gate, w_pli_proj=v_w_pli_proj)

    _, seq, d_model = x.shape
    depth = w_in.shape[0]
    n_heads = d_model // HEAD_DIM
    n_sb = n_heads // 2
    n_dl = n_heads - n_sb
    assert seq % (HEAD_DIM * DILATIONS[-1]) == 0 and d_model % (2 * HEAD_DIM) == 0
    bq = 256
    tr = 128
    tr_ff = 64

    xs = x[0]
    target = loss_target[0]
    gain = {n: [weights[n][l][None, :] for l in range(depth)] for n in SMALL}
    bias_mask, buckets = _dil_tables(rel_bias)

    zero_token = jnp.zeros((8, LANE), F32)
    token = zero_token
    gathers = []
    for l in range(depth):
        slots = [_cast_layer(weights[n], l, n) for n in BIG]
        gathers.append(_split_start(slots, token, _gather_copies, 4 * len(BIG), f"gather_start_{l}"))
        token = gathers[-1][3]

    saved = []
    h1 = _norm_in(xs, gain["ln_mix_pre"][0], tr)
    xin = xs
    for l in range(depth):
        after = token if l == 0 else saved[l - 1]["x3"]
        wg = dict(zip(BIG, _split_wait(gathers[l], after, _gather_copies, f"gather_wait_{l}")))
        proj = _mm_fwd(h1, wg["w_in"], True, "mm_in", tn_cap=768, tk_cap=2048)
        o_sb, lt_sb = _sb_fwd(proj, n_sb, bq)
        o_dl, lse_dl = _dil_fwd(proj, bias_mask, n_sb, n_dl)
        on = _headnorm(o_sb, o_dl, gain["ln_head"][l], tr)
        y = _mm_fwd(on, wg["w_out"], False, "mm_out", tn_cap=1024)
        x2, h2 = _res_norm(xin, y, gain["ln_mix_post"][l], gain["ln_ffn_pre"][l], "post_attn", tr)
        gu = _mm_fwd(h2, wg["w_gate_up"], True, "mm_gate_up")
        act = _swiglu(gu, tr_ff)
        f = _mm_fwd(act, wg["w_down"], False, "mm_down", tn_cap=1024, tk_cap=1536)
        x3, h3 = _res_norm(x2, f, gain["ln_ffn_post"][l], gain["ln_pli"][l], "post_ffn", tr)
        gl = _mm_fwd(h3, wg["w_pli_gate"], False, "mm_pli_gate", tn_cap=1024)
        pl_in = p[l, 0]
        pp = _mm_fwd(pl_in, wg["w_pli_proj"], True, "mm_pli_proj")
        saved.append(dict(wg=wg, x=xin, h1=h1, proj=proj, o_sb=o_sb, lt_sb=lt_sb, o_dl=o_dl, lse_dl=lse_dl, on=on, y=y, x2=x2,
                          h2=h2, gu=gu, act=act, f=f, x3=x3, h3=h3, gl=gl, pp=pp, p=pl_in))
        if l + 1 < depth:
            xin, h1 = _pli_out(x3, gl, pp, gain["ln_mix_pre"][l + 1], tr)
        else:
            dx, loss_part = _loss_head(x3, gl, pp, target, tr)

    grad_big = {n: [None] * depth for n in BIG}
    grad_gain = {n: [None] * depth for n in SMALL}
    ds_layers = [None] * depth

    def reduce_layer(layer, started, after):
        arrays = _split_wait(started, after, _scatter_copies, f"scatter_wait_{layer}")
        nw = len(BIG)
        halves = [_reduce_piece(arrays[w], arrays[nw + w], BIG[w]) for w in range(nw)]
        for n, g in zip(BIG, _join_halves(halves)):
            grad_big[n][layer] = g

    pending = None
    for l in reversed(range(depth)):
        sv = saved[l]
        wg = sv["wg"]
        dpp, dgl = _pli_bwd(dx, sv["gl"], sv["pp"], tr, () if pending is None else (pending[1][3],))
        dw = {}
        dw["w_pli_proj"] = _mm_wgrad(sv["p"], dpp, True, "wg_pli_proj")
        dw["w_pli_gate"] = _mm_wgrad(sv["h3"], dgl, False, "wg_pli_gate")
        dh3 = _mm_dgrad(dgl, wg["w_pli_gate"], False, "dg_pli_gate")
        dx3, df, grad_gain["ln_pli"][l], grad_gain["ln_ffn_post"][l] = _res_norm_bwd(
            dx, dh3, sv["x3"], sv["f"], gain["ln_pli"][l], gain["ln_ffn_post"][l], "post_ffn_bwd", tr)
        dw["w_down"] = _mm_wgrad(sv["act"], df, False, "wg_down")
        dact = _mm_dgrad(df, wg["w_down"], False, "dg_down")
        dgu = _swiglu_bwd(dact, sv["gu"], tr_ff)
        dw["w_gate_up"] = _mm_wgrad(sv["h2"], dgu, True, "wg_gate_up")
        dh2 = _mm_dgrad(dgu, wg["w_gate_up"], True, "dg_gate_up", tc_cap=1536)
        dx2, dy, grad_gain["ln_ffn_pre"][l], grad_gain["ln_mix_post"][l] = _res_norm_bwd(
            dx3, dh2, sv["x2"], sv["y"], gain["ln_ffn_pre"][l], gain["ln_mix_post"][l], "post_attn_bwd", tr)
        dw["w_out"] = _mm_wgrad(sv["on"], dy, False, "wg_out")
        don = _mm_dgrad(dy, wg["w_out"], False, "dg_out")
        do_sb, do_dl, grad_gain["ln_head"][l] = _headnorm_bwd(don, sv["o_sb"], sv["o_dl"], gain["ln_head"][l], tr)
        dq_s, dk_s, dv_s = _sb_bwd(sv["proj"], sv["lt_sb"], do_sb, n_sb, bq)
        dq_d, dk_d, dv_d, ds_layers[l] = _dil_bwd(sv["proj"], do_dl, sv["o_dl"], sv["lse_dl"], bias_mask, n_sb, n_dl)
        dproj = jnp.concatenate([dq_s, dk_s, dv_s, dq_d, dk_d, dv_d], axis=1)
        dw["w_in"] = _mm_wgrad(sv["h1"], dproj, True, "wg_in")
        dh1 = _mm_dgrad(dproj, wg["w_in"], True, "dg_in", tc_cap=1536)
        dx, grad_gain["ln_mix_pre"][l] = _norm_in_bwd(dx2, dh1, sv["x"], gain["ln_mix_pre"][l], tr)
        if pending is not None:
            reduce_layer(*pending, dx)
        dws = [dw[n] for n in BIG]
        landing = [lax.empty((N_DEV - 1, a.shape[1] // 2, a.shape[2]), a.dtype) for a in dws]
        pending = (l, _split_start(dws + landing, zero_token, _scatter_copies, N_DEV * len(BIG), f"scatter_start_{l}"))
    reduce_layer(*pending, pending[1][3])

    db = _rel_bias_grad(jnp.stack(ds_layers, axis=0), buckets)
    rb_flat = db[:n_dl, :NUM_BUCKETS].T.reshape(1, NUM_BUCKETS * n_dl)
    def widen(v):
        return jnp.pad(v, ((0, 0), (0, d_model - v.shape[1])))
    small_rows = [grad_gain[n][l] for n in SMALL for l in range(depth)] + [widen(rb_flat), widen(loss_part)]
    n_rows = len(small_rows)
    small = jnp.concatenate(small_rows + [jnp.zeros((-n_rows % 8, d_model), F32)], axis=0)
    total = _allreduce_small(small)
    grads = {}
    for i, n in enumerate(SMALL):
        grads[n] = total[i * depth:(i + 1) * depth]
    grads["rel_bias"] = total[len(SMALL) * depth, :NUM_BUCKETS * n_dl].reshape(NUM_BUCKETS, n_dl)
    loss = (0.5 / d_model) * jnp.sum(total[len(SMALL) * depth + 1, :LANE])
    for n in BIG:
        grads[n] = jnp.stack(grad_big[n], axis=0)

    delta, new_m, new_v = {}, {}, {}
    for n in WEIGHTS:
        delta[n], new_m[n], new_v[n] = _adamw(weights[n], grads[n], mom1[n], mom2[n], n)
    return (loss, dx[None], *[grads[n] for n in WEIGHTS], *[delta[n] for n in WEIGHTS],
            *[new_m[n] for n in WEIGHTS], *[new_v[n] for n in WEIGHTS])
```

```python
import functools
import math

import jax
import jax.numpy as jnp
from jax import lax
from jax.experimental import pallas as pl
from jax.experimental.pallas import tpu as pltpu

F32 = jnp.float32
BF16 = jnp.bfloat16

HEAD_DIM = 128
RMS_EPS = 1e-6
DILATIONS = (1, 4, 16)
NUM_BUCKETS = 32
MAX_DISTANCE = 2048
NEG = -1e30
N_CHIPS = 4
N_DEV = 8
JOIN_CHUNKS = 8
DIL_UNROLL = 8

ADAM_LR = 0.001
ADAM_B1 = 0.9
ADAM_B2 = 0.999
ADAM_EPS = 1e-08
ADAM_WD = 0.01
ADAM_STEP = 10

V7X_VMEM_LIMIT = 48 * 1024 * 1024
LANE = 128

NN = (((1,), (0,)), ((), ()))
NT = (((1,), (1,)), ((), ()))
TN = (((0,), (0,)), ((), ()))
MESH = pl.DeviceIdType.MESH

BIG = ("w_in", "w_out", "w_gate_up", "w_down", "w_pli_gate", "w_pli_proj")
FIRST_USED = BIG[:1]
REST = BIG[1:]
COL_SHARDED = {"w_in": True, "w_out": False, "w_gate_up": True, "w_down": False,
               "w_pli_gate": False, "w_pli_proj": True}
SMALL = ("ln_mix_pre", "ln_head", "ln_mix_post", "ln_ffn_pre", "ln_ffn_post", "ln_pli")
WEIGHTS = ("ln_mix_pre", "w_in", "ln_head", "w_out", "ln_mix_post", "rel_bias", "ln_ffn_pre",
           "w_gate_up", "w_down", "ln_ffn_post", "ln_pli", "w_pli_gate", "w_pli_proj")


def _tile(n, cap):
    t = min(n, cap) // LANE * LANE
    while t >= LANE:
        if n % t == 0:
            return t
        t -= LANE
    return n


def _params(sem=None):
    return pltpu.CompilerParams(dimension_semantics=sem, vmem_limit_bytes=V7X_VMEM_LIMIT)


def _rowwise(fn, rows, vecs, outs, sums, name, tr, after=()):
    s = rows[0].shape[0]
    nr, nv, no, ns, na = len(rows), len(vecs), len(outs), len(sums), len(after)

    def body(*refs):
        ins = [r[...] for r in refs[:nr + nv]]
        res = fn(*ins)
        out_refs = refs[nr + nv + na:nr + nv + na + no]
        sum_refs = refs[nr + nv + na + no:]
        for o_ref, val in zip(out_refs, res[:no]):
            o_ref[...] = val.astype(o_ref.dtype)
        if ns:
            @pl.when(pl.program_id(0) == 0)
            def _():
                for s_ref in sum_refs:
                    s_ref[...] = jnp.zeros_like(s_ref)
            for s_ref, val in zip(sum_refs, res[no:]):
                s_ref[...] += jnp.sum(val, axis=0, keepdims=True)

    in_specs = [pl.BlockSpec((tr, r.shape[1]), lambda i: (i, 0)) for r in rows]
    in_specs += [pl.BlockSpec(v.shape, lambda i: (0, 0)) for v in vecs]
    in_specs += [pl.BlockSpec(memory_space=pl.ANY)] * na
    out_specs = [pl.BlockSpec((tr, c), lambda i: (i, 0)) for c, _ in outs]
    out_specs += [pl.BlockSpec((1, c), lambda i: (0, 0)) for c in sums]
    out_shape = [jax.ShapeDtypeStruct((s, c), dt) for c, dt in outs]
    out_shape += [jax.ShapeDtypeStruct((1, c), F32) for c in sums]
    return pl.pallas_call(
        body, grid=(s // tr,), in_specs=in_specs, out_specs=out_specs, out_shape=out_shape,
        compiler_params=_params(("arbitrary",) if ns else ("parallel",)), name=name,
    )(*rows, *vecs, *after)


def _rms_r(x):
    return lax.rsqrt(jnp.mean(x * x, axis=-1, keepdims=True) + RMS_EPS)


def _rms_bwd(x, g, dy):
    r = _rms_r(x)
    u = dy * g
    dx = r * (u - x * (r * r) * jnp.mean(u * x, axis=-1, keepdims=True))
    return dx, dy * x * r


def _sigmoid(z):
    return 1.0 / (1.0 + jnp.exp(-z))


def _norm_in(x, g, tr):
    d = x.shape[1]
    return _rowwise(lambda x, g: (x * _rms_r(x) * g,), [x], [g], [(d, BF16)], [], "norm_in", tr)[0]


def _norm_in_bwd(dx_res, dh, x, g, tr):
    d = x.shape[1]

    def fn(dx_res, dh, x, g):
        dx, dg = _rms_bwd(x, g, dh)
        return dx_res + dx, dg
    return _rowwise(fn, [dx_res, dh, x], [g], [(d, F32)], [d], "norm_in_bwd", tr)


def _headnorm(o_sb, o_dl, g, tr):
    d = g.shape[1]

    def fn(o_sb, o_dl, g):
        o = jnp.concatenate([o_sb, o_dl], axis=1)
        parts = []
        for h in range(d // HEAD_DIM):
            sl = slice(h * HEAD_DIM, (h + 1) * HEAD_DIM)
            oh = o[:, sl]
            parts.append(oh * _rms_r(oh) * g[:, sl])
        return (jnp.concatenate(parts, axis=1),)
    return _rowwise(fn, [o_sb, o_dl], [g], [(d, BF16)], [], "headnorm", tr)[0]


def _headnorm_bwd(don, o_sb, o_dl, g, tr, after=()):
    d = g.shape[1]
    n_sb = o_sb.shape[1]

    def fn(don, o_sb, o_dl, g):
        o = jnp.concatenate([o_sb, o_dl], axis=1)
        dos, dgs = [], []
        for h in range(d // HEAD_DIM):
            sl = slice(h * HEAD_DIM, (h + 1) * HEAD_DIM)
            dx, dg = _rms_bwd(o[:, sl], g[:, sl], don[:, sl])
            dos.append(dx)
            dgs.append(dg)
        do = jnp.concatenate(dos, axis=1)
        return do[:, :n_sb], do[:, n_sb:], jnp.concatenate(dgs, axis=1)
    return _rowwise(fn, [don, o_sb, o_dl], [g], [(n_sb, F32), (d - n_sb, F32)], [d], "headnorm_bwd", tr, after)


def _res_norm(x, y, g_post, g_pre, name, tr):
    d = x.shape[1]

    def fn(x, y, g_post, g_pre):
        x2 = x + y * _rms_r(y) * g_post
        return x2, x2 * _rms_r(x2) * g_pre
    return _rowwise(fn, [x, y], [g_post, g_pre], [(d, F32), (d, BF16)], [], name, tr)


def _res_norm_bwd(dx_res, dh, x2, y, g_pre, g_post, name, tr):
    d = x2.shape[1]

    def fn(dx_res, dh, x2, y, g_pre, g_post):
        dxa, dg_pre = _rms_bwd(x2, g_pre, dh)
        dx2 = dx_res + dxa
        dy, dg_post = _rms_bwd(y, g_post, dx2)
        return dx2, dy, dg_pre, dg_post
    return _rowwise(fn, [dx_res, dh, x2, y], [g_pre, g_post], [(d, F32), (d, BF16)], [d, d], name, tr)


def _swiglu(gu, tr):
    ff = gu.shape[1] // 2

    def fn(gu):
        g, u = gu[:, :ff], gu[:, ff:]
        return (g * _sigmoid(g) * u,)
    return _rowwise(fn, [gu], [], [(ff, BF16)], [], "swiglu", tr)[0]


def _swiglu_bwd(dact, gu, tr):
    ff = gu.shape[1] // 2

    def fn(dact, gu):
        g, u = gu[:, :ff], gu[:, ff:]
        sg = _sigmoid(g)
        dg = dact * u * sg * (1.0 + g * (1.0 - sg))
        du = dact * g * sg
        return (jnp.concatenate([dg, du], axis=1),)
    return _rowwise(fn, [dact, gu], [], [(2 * ff, BF16)], [], "swiglu_bwd", tr)[0]


def _pli_out(x3, gl, pp, g_next, tr):
    d = x3.shape[1]

    def fn(x3, gl, pp, g):
        x4 = x3 + _sigmoid(gl) * pp
        return x4, x4 * _rms_r(x4) * g
    return _rowwise(fn, [x3, gl, pp], [g_next], [(d, F32), (d, BF16)], [], "pli_out", tr)


def _loss_head(x3, gl, pp, target, tr):
    d = x3.shape[1]

    def fn(x3, gl, pp, t):
        err = x3 + _sigmoid(gl) * pp - t
        sq = err * err
        part = sq[:, :LANE]
        for k in range(1, d // LANE):
            part = part + sq[:, k * LANE:(k + 1) * LANE]
        return err * (1.0 / d), part
    return _rowwise(fn, [x3, gl, pp, target], [], [(d, F32)], [LANE], "loss_head", tr)


def _pli_bwd(dx, gl, pp, tr, after=()):
    d = dx.shape[1]

    def fn(dx, gl, pp):
        gate = _sigmoid(gl)
        return dx * gate, dx * pp * gate * (1.0 - gate)
    return _rowwise(fn, [dx, gl, pp], [], [(d, BF16), (d, BF16)], [], "pli_bwd", tr, after)


def _my_chip():
    return 2 * lax.axis_index("x") + lax.axis_index("y")


def _cast_layer(w, layer, name):
    _, r, c = w.shape
    tr = r
    while tr * c * 4 > (4 << 20) and tr % 32 == 0:
        tr //= 2

    def body(w_ref, o_ref):
        o_ref[...] = w_ref[...].astype(o_ref.dtype)

    return pl.pallas_call(
        body, grid=(r // tr,),
        in_specs=[pl.BlockSpec((None, tr, c), lambda i: (layer, i, 0))],
        out_specs=pl.BlockSpec((None, tr, c), lambda i: (_my_chip(), i, 0)),
        out_shape=jax.ShapeDtypeStruct((N_CHIPS, r, c), BF16),
        compiler_params=_params(("parallel",)), name="cast_" + name,
    )(w)


def _reduce_piece(dw, recv, name):
    _, r, c = dw.shape
    h = r // 2
    tr = h
    while tr * c * 2 * N_DEV > (8 << 20) and tr % 32 == 0:
        tr //= 2
    per = h // tr

    def body(d_ref, r_ref, o_ref):
        acc = d_ref[...].astype(F32)
        for i in range(N_DEV - 1):
            acc = acc + r_ref[i].astype(F32)
        o_ref[...] = acc

    return pl.pallas_call(
        body, grid=(per,),
        in_specs=[pl.BlockSpec((None, tr, c), lambda i: (_my_chip(), lax.axis_index("c") * per + i, 0)),
                  pl.BlockSpec((N_DEV - 1, tr, c), lambda i: (0, i, 0))],
        out_specs=pl.BlockSpec((tr, c), lambda i: (lax.axis_index("c") * per + i, 0)),
        out_shape=jax.ShapeDtypeStruct((r, c), F32),
        compiler_params=_params(("parallel",)), name="reduce_" + name,
    )(dw, recv)


def _adamw(w, g, m, v, name):
    shape = w.shape
    if w.ndim == 3:
        w, g, m, v = (a.reshape(shape[0] * shape[1], shape[2]) for a in (w, g, m, v))
    r, c = w.shape
    tr = r
    while tr * c * 4 > (1 << 20) and tr % 16 == 0:
        tr //= 2

    def body(w_ref, g_ref, m_ref, v_ref, d_ref, m2_ref, v2_ref):
        g = g_ref[...]
        m2 = ADAM_B1 * m_ref[...] + (1.0 - ADAM_B1) * g
        v2 = ADAM_B2 * v_ref[...] + (1.0 - ADAM_B2) * (g * g)
        m_hat = m2 / (1.0 - ADAM_B1 ** ADAM_STEP)
        v_hat = v2 / (1.0 - ADAM_B2 ** ADAM_STEP)
        d_ref[...] = -ADAM_LR * (m_hat / (jnp.sqrt(v_hat) + ADAM_EPS) + ADAM_WD * w_ref[...])
        m2_ref[...] = m2
        v2_ref[...] = v2

    spec = pl.BlockSpec((tr, c), lambda i: (i, 0))
    res = pl.pallas_call(
        body, grid=(r // tr,), in_specs=[spec] * 4, out_specs=[spec] * 3,
        out_shape=[jax.ShapeDtypeStruct((r, c), F32)] * 3,
        compiler_params=_params(("parallel",)), name="adamw_" + name,
    )(w, g, m, v)
    return tuple(a.reshape(shape) for a in res)


def _mm(a, b, grid, a_spec, b_spec, o_spec, o_shape, o_dtype, dims, acc_shape, name):
    nk = grid[2]

    def body(a_ref, b_ref, o_ref, acc_ref):
        k = pl.program_id(2)

        @pl.when(k == 0)
        def _():
            acc_ref[...] = jnp.zeros_like(acc_ref)

        acc_ref[...] += lax.dot_general(a_ref[...].astype(BF16), b_ref[...].astype(BF16), dims,
                                        preferred_element_type=F32)

        @pl.when(k == nk - 1)
        def _():
            o_ref[...] = acc_ref[...].astype(o_ref.dtype)

    return pl.pallas_call(
        body, grid=grid, in_specs=[a_spec, b_spec], out_specs=o_spec,
        out_shape=jax.ShapeDtypeStruct(o_shape, o_dtype),
        scratch_shapes=[pltpu.VMEM(acc_shape, F32)],
        compiler_params=_params(("parallel", "parallel", "arbitrary")), name=name,
    )(a, b)


def _mm_fwd(a, wg, col, name, tm=1024, tn_cap=1536, tk_cap=1024):
    m, k = a.shape
    ns, r, c = wg.shape
    tm = min(tm, m)
    if col:
        n, tn, tk = ns * c, _tile(c, tn_cap), _tile(k, tk_cap)
        per = c // tn
        b_spec = pl.BlockSpec((None, tk, tn), lambda j, i, kk: (j // per, kk, j % per))
    else:
        n, tn, tk = c, _tile(c, tn_cap), _tile(r, tk_cap)
        per = r // tk
        b_spec = pl.BlockSpec((None, tk, tn), lambda j, i, kk: (kk // per, kk % per, j))
    return _mm(a, wg, (n // tn, m // tm, k // tk), pl.BlockSpec((tm, tk), lambda j, i, kk: (i, kk)), b_spec,
               pl.BlockSpec((tm, tn), lambda j, i, kk: (i, j)), (m, n), F32, NN, (tm, tn), name)


def _mm_dgrad(dc, wg, col, name, tm=1024, to_cap=1536, tc_cap=1024):
    m, n = dc.shape
    ns, r, c = wg.shape
    tm = min(tm, m)
    if col:
        kout, to, tc = r, _tile(r, to_cap), _tile(c, tc_cap)
        per = c // tc
        b_spec = pl.BlockSpec((None, to, tc), lambda j, i, kk: (kk // per, j, kk % per))
    else:
        kout, to, tc = ns * r, _tile(r, to_cap), _tile(c, tc_cap)
        per = r // to
        b_spec = pl.BlockSpec((None, to, tc), lambda j, i, kk: (j // per, j % per, kk))
    return _mm(dc, wg, (kout // to, m // tm, n // tc), pl.BlockSpec((tm, tc), lambda j, i, kk: (i, kk)), b_spec,
               pl.BlockSpec((tm, to), lambda j, i, kk: (i, j)), (m, kout), F32, NT, (tm, to), name)


def _mm_wgrad(a, dc, col, name, ti_cap=1536, tn_cap=1536, tkm=1024):
    m, k = a.shape
    n = dc.shape[1]
    tkm = min(tkm, m)
    if col:
        r, c = k, n // N_CHIPS
        ti, tn = _tile(r, ti_cap), _tile(c, tn_cap)
        per = c // tn
        o_spec = pl.BlockSpec((None, ti, tn), lambda i, j, kk: (j // per, i, j % per))
    else:
        r, c = k // N_CHIPS, n
        ti, tn = _tile(r, ti_cap), _tile(c, tn_cap)
        per = r // ti
        o_spec = pl.BlockSpec((None, ti, tn), lambda i, j, kk: (i // per, i % per, j))
    return _mm(a, dc, (k // ti, n // tn, m // tkm), pl.BlockSpec((tkm, ti), lambda i, j, kk: (kk, i)),
               pl.BlockSpec((tkm, tn), lambda i, j, kk: (kk, j)), o_spec, (N_CHIPS, r, c), BF16, TN, (ti, tn), name)


def _log_keep(z):
    return -(jnp.maximum(z, 0.0) + jnp.log(1.0 + jnp.exp(-jnp.abs(z))))


def _split_dot(x, t):
    hi = x.astype(BF16)
    lo = (x - hi.astype(F32)).astype(BF16)
    return (lax.dot_general(hi, t, NN, preferred_element_type=F32)
            + lax.dot_general(lo, t, NN, preferred_element_type=F32))


def _sb_fwd(proj, n_sb, bq):
    s = proj.shape[0]
    scale = 1.0 / math.sqrt(HEAD_DIM)

    def body(q_ref, k_ref, v_ref, o_ref, lt_ref):
        i = pl.program_id(1)
        q = q_ref[...].astype(BF16)
        row = lax.broadcasted_iota(jnp.int32, (bq, bq), 0)
        col = lax.broadcasted_iota(jnp.int32, (bq, bq), 1)
        later_in_block = (row > col).astype(BF16)
        keep = col < row

        def block(j, carry, diagonal):
            c, acc = carry
            ks = pl.multiple_of(j * bq, bq)
            kb = k_ref[pl.ds(ks, bq), :].astype(BF16)
            vb = v_ref[pl.ds(ks, bq), :].astype(BF16)
            z = lax.dot_general(q, kb, NT, preferred_element_type=F32) * scale
            lk = _log_keep(z)
            if diagonal:
                lk = jnp.where(keep, lk, 0.0)
            e = z + lk + _split_dot(lk, later_in_block) + c
            if diagonal:
                e = jnp.where(keep, e, NEG)
            a = jnp.exp(e)
            acc = acc + lax.dot_general(a.astype(BF16), vb, NN, preferred_element_type=F32)
            return c + jnp.sum(lk, axis=1, keepdims=True), acc

        carry = block(i, (jnp.zeros((bq, 1), F32), jnp.zeros((bq, HEAD_DIM), F32)), True)
        carry = lax.fori_loop(
            0, i // 2, lambda t, cr: block(i - 2 - 2 * t, block(i - 1 - 2 * t, cr, False), False), carry)
        carry = lax.fori_loop(0, i % 2, lambda t, cr: block(0, cr, False), carry)
        o_ref[...] = carry[1]
        lt_ref[...] = jnp.broadcast_to(carry[0], (bq, HEAD_DIM))

    blk = pl.BlockSpec((bq, HEAD_DIM), lambda h, i: (i, h))
    shp = jax.ShapeDtypeStruct((s, n_sb * HEAD_DIM), F32)
    return pl.pallas_call(
        body, grid=(n_sb, s // bq),
        in_specs=[blk,
                  pl.BlockSpec((s, HEAD_DIM), lambda h, i: (0, n_sb + h)),
                  pl.BlockSpec((s, HEAD_DIM), lambda h, i: (0, 2 * n_sb + h))],
        out_specs=[blk, blk], out_shape=[shp, shp],
        compiler_params=_params(("parallel", "parallel")), name="sb_fwd",
    )(proj, proj, proj)


def _sb_bwd(proj, lt, do, n_sb, bq):
    s = proj.shape[0]
    nq = s // bq
    scale = 1.0 / math.sqrt(HEAD_DIM)

    def body(q_ref, k_ref, v_ref, lt_ref, do_ref, dq_ref, dk_ref, dv_ref, dk_acc, dv_acc):
        i = pl.program_id(1)

        @pl.when(i == 0)
        def _():
            dk_acc[...] = jnp.zeros_like(dk_acc)
            dv_acc[...] = jnp.zeros_like(dv_acc)

        q = q_ref[...].astype(BF16)
        do_b = do_ref[...].astype(BF16)
        ltot = jnp.max(lt_ref[...], axis=1, keepdims=True)
        row = lax.broadcasted_iota(jnp.int32, (bq, bq), 0)
        col = lax.broadcasted_iota(jnp.int32, (bq, bq), 1)
        upto_in_block = (row <= col).astype(BF16)
        before_in_block = (row < col).astype(BF16)
        keep = col < row

        def block(j, carry, diagonal):
            pk, pg, dq = carry
            ks = pl.multiple_of(j * bq, bq)
            kb = k_ref[pl.ds(ks, bq), :].astype(BF16)
            vb = v_ref[pl.ds(ks, bq), :].astype(BF16)
            z = lax.dot_general(q, kb, NT, preferred_element_type=F32) * scale
            lk = _log_keep(z)
            if diagonal:
                lk = jnp.where(keep, lk, 0.0)
            e = z + lk + ((ltot - pk) - _split_dot(lk, upto_in_block))
            if diagonal:
                e = jnp.where(keep, e, NEG)
            a = jnp.exp(e)
            da = lax.dot_general(do_b, vb, NT, preferred_element_type=F32)
            g = a * da
            dv_acc[pl.ds(ks, bq), :] += lax.dot_general(a.astype(BF16), do_b, TN, preferred_element_type=F32)
            before = pg + _split_dot(g, before_in_block)
            dz = g * jnp.exp(lk) - before * jnp.exp(z + lk)
            if diagonal:
                dz = jnp.where(keep, dz, 0.0)
            dz_b = (dz * scale).astype(BF16)
            dq = dq + lax.dot_general(dz_b, kb, NN, preferred_element_type=F32)
            dk_acc[pl.ds(ks, bq), :] += lax.dot_general(dz_b, q, TN, preferred_element_type=F32)
            return (pk + jnp.sum(lk, axis=1, keepdims=True), pg + jnp.sum(g, axis=1, keepdims=True), dq)

        zero = jnp.zeros((bq, 1), F32)
        carry = lax.fori_loop(0, i // 2, lambda t, cr: block(2 * t + 1, block(2 * t, cr, False), False),
                              (zero, zero, jnp.zeros((bq, HEAD_DIM), F32)))
        carry = lax.fori_loop(0, i % 2, lambda t, cr: block(i - 1, cr, False), carry)
        carry = block(i, carry, True)
        dq_ref[...] = carry[2].astype(dq_ref.dtype)

        @pl.when(i == nq - 1)
        def _():
            dk_ref[...] = dk_acc[...].astype(dk_ref.dtype)
            dv_ref[...] = dv_acc[...].astype(dv_ref.dtype)

    blk = pl.BlockSpec((bq, HEAD_DIM), lambda h, i: (i, h))
    full = pl.BlockSpec((s, HEAD_DIM), lambda h, i: (0, h))
    shp = jax.ShapeDtypeStruct((s, n_sb * HEAD_DIM), BF16)
    return pl.pallas_call(
        body, grid=(n_sb, nq),
        in_specs=[blk,
                  pl.BlockSpec((s, HEAD_DIM), lambda h, i: (0, n_sb + h)),
                  pl.BlockSpec((s, HEAD_DIM), lambda h, i: (0, 2 * n_sb + h)),
                  blk, blk],
        out_specs=[blk, full, full], out_shape=[shp, shp, shp],
        scratch_shapes=[pltpu.VMEM((s, HEAD_DIM), F32), pltpu.VMEM((s, HEAD_DIM), F32)],
        compiler_params=_params(("parallel", "arbitrary")), name="sb_bwd",
    )(proj, proj, proj, lt, do)


def _t5_bucket(dist):
    max_exact = NUM_BUCKETS // 2
    d = jnp.maximum(dist, 1).astype(F32)
    large = max_exact + (jnp.log(d / max_exact) / math.log(MAX_DISTANCE / max_exact)
                         * (NUM_BUCKETS - max_exact)).astype(jnp.int32)
    large = jnp.minimum(large, NUM_BUCKETS - 1)
    return jnp.where(dist < max_exact, dist, large)


def _dil_tables(rel_bias):
    qi = jnp.arange(HEAD_DIM, dtype=jnp.int32)[:, None]
    ki = jnp.arange(2 * HEAD_DIM, dtype=jnp.int32)[None, :]
    rel = HEAD_DIM + qi - ki
    band = (rel >= 0) & (rel <= HEAD_DIM)
    biases, buckets = [], []
    for d in DILATIONS:
        bucket = _t5_bucket(jnp.maximum(rel, 0) * d)
        onehot = (bucket[:, :, None] == jnp.arange(NUM_BUCKETS, dtype=jnp.int32)).astype(F32)
        bias = jnp.einsum("qkb,bh->hqk", onehot, rel_bias.astype(F32), precision=lax.Precision.HIGHEST)
        biases.append(jnp.where(band[None], bias, NEG))
        buckets.append(jnp.where(band, bucket, -1).astype(F32))
    return jnp.stack(biases, axis=1), jnp.stack(buckets, axis=0)


def _sub_rows(ref, start, d):
    if d == 1:
        return ref[pl.ds(pl.multiple_of(start, HEAD_DIM), HEAD_DIM), :]
    return ref[pl.ds(start, HEAD_DIM, stride=d), :]


def _sub_idx(start, d):
    if d == 1:
        return pl.ds(pl.multiple_of(start, HEAD_DIM), HEAD_DIM)
    return pl.ds(start, HEAD_DIM, stride=d)


def _dil_logits(q_ref, k_ref, bm, n, cur, prv, d, scale):
    qb = _sub_rows(q_ref, cur, d).astype(BF16)
    kk = jnp.concatenate([_sub_rows(k_ref, prv, d), _sub_rows(k_ref, cur, d)], axis=0).astype(BF16)
    sc = lax.dot_general(qb, kk, NT, preferred_element_type=F32) * scale + bm
    colk = lax.broadcasted_iota(jnp.int32, sc.shape, 1)
    sc = jnp.where((colk >= HEAD_DIM) | (n > 0), sc, NEG)
    return qb, kk, sc


def _dil_fwd(proj, bm, n_sb, n_dl):
    s = proj.shape[0]
    scale = 1.0 / math.sqrt(HEAD_DIM)
    chunk = min(s, 512)

    def body(q_ref, k_ref, v_ref, bm_ref, o_ref, l_ref, ob0, ob1, ob2, lb0, lb1, lb2):
        obs, lbs = (ob0, ob1, ob2), (lb0, lb1, lb2)
        for b, d in enumerate(DILATIONS):
            nb = s // (HEAD_DIM * d)

            def step(idx, _, b=b, d=d, nb=nb):
                r, n = idx // nb, idx % nb
                cur = n * (HEAD_DIM * d) + r
                prv = jnp.maximum(n - 1, 0) * (HEAD_DIM * d) + r
                _, _, sc = _dil_logits(q_ref, k_ref, bm_ref[b], n, cur, prv, d, scale)
                vv = jnp.concatenate([_sub_rows(v_ref, prv, d), _sub_rows(v_ref, cur, d)], axis=0).astype(BF16)
                mx = jnp.max(sc, axis=1, keepdims=True)
                pr = jnp.exp(sc - mx)
                den = jnp.sum(pr, axis=1, keepdims=True)
                o = lax.dot_general(pr.astype(BF16), vv, NN, preferred_element_type=F32) / den
                obs[b][_sub_idx(cur, d), :] = o
                lbs[b][_sub_idx(cur, d), :] = jnp.broadcast_to(mx + jnp.log(den), (HEAD_DIM, HEAD_DIM))
                return 0

            def group(g, _, step=step):
                for u in range(DIL_UNROLL):
                    step(g * DIL_UNROLL + u, 0)
                return 0

            lax.fori_loop(0, s // HEAD_DIM // DIL_UNROLL, group, 0)

        for ci in range(s // chunk):
            sl = pl.ds(ci * chunk, chunk)
            l0, l1, l2 = lb0[sl, :], lb1[sl, :], lb2[sl, :]
            mx = jnp.maximum(jnp.maximum(l0, l1), l2)
            w0, w1, w2 = jnp.exp(l0 - mx), jnp.exp(l1 - mx), jnp.exp(l2 - mx)
            tot = w0 + w1 + w2
            o_ref[sl, :] = (w0 * ob0[sl, :] + w1 * ob1[sl, :] + w2 * ob2[sl, :]) / tot
            l_ref[sl, :] = mx + jnp.log(tot)

    base = 3 * n_sb
    full = pl.BlockSpec((s, HEAD_DIM), lambda h: (0, h))
    shp = jax.ShapeDtypeStruct((s, n_dl * HEAD_DIM), F32)
    return pl.pallas_call(
        body, grid=(n_dl,),
        in_specs=[pl.BlockSpec((s, HEAD_DIM), lambda h: (0, base + h)),
                  pl.BlockSpec((s, HEAD_DIM), lambda h: (0, base + n_dl + h)),
                  pl.BlockSpec((s, HEAD_DIM), lambda h: (0, base + 2 * n_dl + h)),
                  pl.BlockSpec((None, 3, HEAD_DIM, 2 * HEAD_DIM), lambda h: (h, 0, 0, 0))],
        out_specs=[full, full], out_shape=[shp, shp],
        scratch_shapes=[pltpu.VMEM((s, HEAD_DIM), F32)] * 6,
        compiler_params=_params(("parallel",)), name="dil_fwd",
    )(proj, proj, proj, bm)


def _dil_bwd(proj, do, o, lse, bm, n_sb, n_dl):
    s = proj.shape[0]
    scale = 1.0 / math.sqrt(HEAD_DIM)
    chunk = min(s, 512)

    def body(q_ref, k_ref, v_ref, do_ref, o_ref, l_ref, bm_ref, dq_ref, dk_ref, dv_ref, ds_ref, dq_s, dk_s, dv_s):
        dq_s[...] = jnp.zeros_like(dq_s)
        dk_s[...] = jnp.zeros_like(dk_s)
        dv_s[...] = jnp.zeros_like(dv_s)
        ds_ref[...] = jnp.zeros_like(ds_ref)
        for b, d in enumerate(DILATIONS):
            nb = s // (HEAD_DIM * d)

            def step(idx, _, b=b, d=d, nb=nb):
                r, n = idx // nb, idx % nb
                cur = n * (HEAD_DIM * d) + r
                prv = jnp.maximum(n - 1, 0) * (HEAD_DIM * d) + r
                qb, kk, sc = _dil_logits(q_ref, k_ref, bm_ref[b], n, cur, prv, d, scale)
                vv = jnp.concatenate([_sub_rows(v_ref, prv, d), _sub_rows(v_ref, cur, d)], axis=0).astype(BF16)
                do_f = _sub_rows(do_ref, cur, d)
                do_b = do_f.astype(BF16)
                delta = jnp.sum(do_f * _sub_rows(o_ref, cur, d), axis=1, keepdims=True)
                lr = _sub_rows(l_ref, cur, d)
                w = jnp.exp(sc - jnp.concatenate([lr, lr], axis=1))
                dp = lax.dot_general(do_b, vv, NT, preferred_element_type=F32)
                ds = w * (dp - delta)
                ds_ref[b] += ds
                ds_b = (ds * scale).astype(BF16)
                dv_blk = lax.dot_general(w.astype(BF16), do_b, TN, preferred_element_type=F32)
                dk_blk = lax.dot_general(ds_b, qb, TN, preferred_element_type=F32)
                ci, pi = _sub_idx(cur, d), _sub_idx(prv, d)
                dq_s[ci, :] += lax.dot_general(ds_b, kk, NN, preferred_element_type=F32)
                dk_s[ci, :] += dk_blk[HEAD_DIM:]
                dv_s[ci, :] += dv_blk[HEAD_DIM:]
                dk_s[pi, :] += dk_blk[:HEAD_DIM]
                dv_s[pi, :] += dv_blk[:HEAD_DIM]
                return 0

            def group(g, _, step=step):
                for u in range(DIL_UNROLL):
                    step(g * DIL_UNROLL + u, 0)
                return 0

            lax.fori_loop(0, s // HEAD_DIM // DIL_UNROLL, group, 0)

        for ci in range(s // chunk):
            sl = pl.ds(ci * chunk, chunk)
            dq_ref[sl, :] = dq_s[sl, :].astype(dq_ref.dtype)
            dk_ref[sl, :] = dk_s[sl, :].astype(dk_ref.dtype)
            dv_ref[sl, :] = dv_s[sl, :].astype(dv_ref.dtype)

    base = 3 * n_sb
    full = pl.BlockSpec((s, HEAD_DIM), lambda h: (0, h))
    tab = pl.BlockSpec((None, 3, HEAD_DIM, 2 * HEAD_DIM), lambda h: (h, 0, 0, 0))
    shp = jax.ShapeDtypeStruct((s, n_dl * HEAD_DIM), BF16)
    return pl.pallas_call(
        body, grid=(n_dl,),
        in_specs=[pl.BlockSpec((s, HEAD_DIM), lambda h: (0, base + h)),
                  pl.BlockSpec((s, HEAD_DIM), lambda h: (0, base + n_dl + h)),
                  pl.BlockSpec((s, HEAD_DIM), lambda h: (0, base + 2 * n_dl + h)),
                  full, full, full, tab],
        out_specs=[full, full, full, tab],
        out_shape=[shp, shp, shp, jax.ShapeDtypeStruct((n_dl, 3, HEAD_DIM, 2 * HEAD_DIM), F32)],
        scratch_shapes=[pltpu.VMEM((s, HEAD_DIM), F32)] * 3,
        compiler_params=_params(("parallel",)), name="dil_bwd",
    )(proj, proj, proj, do, o, lse, bm)


def _rel_bias_grad(ds_all, buckets):
    depth, n_dl = ds_all.shape[:2]
    rows = -(-n_dl // 8) * 8

    def body(ds_ref, bk_ref, o_ref):
        lane = lax.broadcasted_iota(jnp.int32, (1, LANE), 1)

        def one_bucket(bkt, acc):
            fb = bkt.astype(F32)
            out = []
            for h in range(n_dl):
                val = jnp.zeros((1, 1), F32)
                for b in range(3):
                    tot = ds_ref[0, h, b]
                    for l in range(1, depth):
                        tot = tot + ds_ref[l, h, b]
                    val = val + jnp.sum(jnp.where(bk_ref[b] == fb, tot, 0.0), keepdims=True)
                out.append(jnp.where(lane == bkt, val, 0.0))
            out += [jnp.zeros((1, LANE), F32)] * (rows - n_dl)
            return acc + jnp.concatenate(out, axis=0)

        o_ref[...] = lax.fori_loop(0, NUM_BUCKETS, one_bucket, jnp.zeros((rows, LANE), F32))

    return pl.pallas_call(
        body, out_shape=jax.ShapeDtypeStruct((rows, LANE), F32),
        in_specs=[pl.BlockSpec(memory_space=pltpu.VMEM)] * 2, out_specs=pl.BlockSpec(memory_space=pltpu.VMEM),
        compiler_params=_params(), name="rel_bias_grad",
    )(ds_all, buckets)


def _place():
    return lax.axis_index("x"), lax.axis_index("y"), lax.axis_index("c")


def _flip(v, bit):
    return 1 - v if bit else v


HBM_SPEC = pl.BlockSpec(memory_space=pl.ANY)


HBM_ONLY = pl.BlockSpec(memory_space=pltpu.HBM)
SEM_SPEC = pl.BlockSpec(memory_space=pltpu.SEMAPHORE)
DATAFLOW = pltpu.SideEffectType.DATAFLOW_SIDE_EFFECTING


def _in_hbm(a):
    return pltpu.with_memory_space_constraint(a, pltpu.HBM)


def _split_start(arrays, token, copies_of, n_sem, name):
    na = len(arrays)

    def body(*refs):
        for cp in copies_of(refs[:na], refs[na + 1], refs[na + 2]):
            cp.start()
        refs[-1][...] = jnp.zeros_like(refs[-1])

    sems = pltpu.SemaphoreType.DMA((n_sem,))
    res = pl.pallas_call(
        body, name=name,
        out_shape=(sems, sems, *[pltpu.HBM(a.shape, a.dtype) for a in arrays], jax.ShapeDtypeStruct((8, LANE), F32)),
        in_specs=[HBM_ONLY] * na + [HBM_SPEC],
        out_specs=(SEM_SPEC, SEM_SPEC, *[HBM_ONLY] * na, pl.BlockSpec(memory_space=pltpu.VMEM)),
        input_output_aliases={i: 2 + i for i in range(na)},
        compiler_params=pltpu.CompilerParams(has_side_effects=DATAFLOW),
    )(*[_in_hbm(a) for a in arrays], token)
    return res[0], res[1], res[2:2 + na], res[-1]


def _split_wait(started, after, copies_of, name):
    send, recv, arrays, _ = started
    na = len(arrays)

    def body(*refs):
        for cp in copies_of(refs[:na], refs[na], refs[na + 1]):
            cp.wait_send()
            cp.wait_recv()

    return pl.pallas_call(
        body, name=name, out_shape=[pltpu.HBM(a.shape, a.dtype) for a in arrays],
        in_specs=[HBM_ONLY] * na + [SEM_SPEC, SEM_SPEC, HBM_SPEC], out_specs=[HBM_ONLY] * na,
        input_output_aliases={i: i for i in range(na)},
        compiler_params=pltpu.CompilerParams(has_side_effects=DATAFLOW),
    )(*arrays, send, recv, after)


def _gather_copies(buf, send, recv):
    x, y, c = _place()
    mine = 2 * x + y
    return [pltpu.make_async_remote_copy(
        src_ref=buf[w].at[mine], dst_ref=buf[w].at[mine], send_sem=send.at[4 * w + k], recv_sem=recv.at[4 * w + k],
        device_id=(_flip(x, k >> 1), _flip(y, k & 1), c), device_id_type=MESH)
        for w in range(len(buf)) for k in (1, 2, 3)]


def _scatter_copies(refs, send, recv):
    nw = len(refs) // 2
    src, buf = refs[:nw], refs[nw:]
    x, y, c = _place()
    out = []
    for w in range(nw):
        h = src[w].shape[1] // 2
        for k in range(1, N_DEV):
            px, py, pc = _flip(x, k >> 2), _flip(y, (k >> 1) & 1), _flip(c, k & 1)
            out.append(pltpu.make_async_remote_copy(
                src_ref=src[w].at[2 * px + py, pl.ds(pc * h, h)], dst_ref=buf[w].at[k - 1],
                send_sem=send.at[N_DEV * w + k], recv_sem=recv.at[N_DEV * w + k], device_id=(px, py, pc),
                device_id_type=MESH))
    return out


def _join_halves(grads):
    nw = len(grads)

    def body(*refs):
        ins, outs = refs[:nw], refs[nw:2 * nw]
        send, recv = refs[2 * nw:]
        x, y, c = _place()
        remote = []
        for w in range(nw):
            h = ins[w].shape[0] // 2
            hc = h // JOIN_CHUNKS
            for j in range(JOIN_CHUNKS):
                rows = pl.ds(c * h + j * hc, hc)
                cp = pltpu.make_async_remote_copy(
                    src_ref=ins[w].at[rows], dst_ref=outs[w].at[rows],
                    send_sem=send.at[w, j], recv_sem=recv.at[w, j], device_id=(x, y, 1 - c), device_id_type=MESH)
                cp.start()
                remote.append(cp)
        for w in range(nw):
            h = ins[w].shape[0] // 2
            hc = h // JOIN_CHUNKS
            for j in range(JOIN_CHUNKS):
                theirs = outs[w].at[pl.ds((1 - c) * h + j * hc, hc)]
                pltpu.make_async_remote_copy(
                    src_ref=theirs, dst_ref=theirs, send_sem=send.at[w, j], recv_sem=recv.at[w, j],
                    device_id=(x, y, c), device_id_type=MESH).wait_recv()
        for cp in remote:
            cp.wait_send()

    sems = [pltpu.SemaphoreType.DMA((nw, JOIN_CHUNKS))] * 2
    return pl.pallas_call(
        body, out_shape=[jax.ShapeDtypeStruct(a.shape, a.dtype) for a in grads],
        in_specs=[HBM_SPEC] * nw, out_specs=[HBM_SPEC] * nw, scratch_shapes=sems,
        input_output_aliases={i: i for i in range(nw)}, name="join_halves",
    )(*grads)


def _allreduce_small(v):
    rows, c = v.shape

    def body(v_ref, o_ref, buf, local_sem, send, recv):
        x, y, cc = _place()
        me = 4 * x + 2 * y + cc
        own = pltpu.make_async_copy(v_ref, buf.at[me], local_sem)
        own.start()
        sends = []
        for k in range(1, N_DEV):
            px, py, pc = _flip(x, k >> 2), _flip(y, (k >> 1) & 1), _flip(cc, k & 1)
            cp = pltpu.make_async_remote_copy(
                src_ref=v_ref, dst_ref=buf.at[me], send_sem=send.at[k], recv_sem=recv.at[k],
                device_id=(px, py, pc), device_id_type=MESH)
            cp.start()
            sends.append(cp)
        for k in range(1, N_DEV):
            px, py, pc = _flip(x, k >> 2), _flip(y, (k >> 1) & 1), _flip(cc, k & 1)
            slot = buf.at[4 * px + 2 * py + pc]
            pltpu.make_async_remote_copy(
                src_ref=slot, dst_ref=slot, send_sem=send.at[k], recv_sem=recv.at[k],
                device_id=(x, y, cc), device_id_type=MESH).wait_recv()
        for cp in sends:
            cp.wait_send()
        own.wait()
        acc = buf[0]
        for i in range(1, N_DEV):
            acc = acc + buf[i]
        o_ref[...] = acc

    return pl.pallas_call(
        body, out_shape=jax.ShapeDtypeStruct((rows, c), F32),
        in_specs=[pl.BlockSpec(memory_space=pltpu.VMEM)], out_specs=pl.BlockSpec(memory_space=pltpu.VMEM),
        scratch_shapes=[pltpu.VMEM((N_DEV, rows, c), F32), pltpu.SemaphoreType.DMA,
                        pltpu.SemaphoreType.DMA((N_DEV,)), pltpu.SemaphoreType.DMA((N_DEV,))],
        compiler_params=_params(),
        name="allreduce_small",
    )(v)


def kernel(x, p, ln_mix_pre, w_in, ln_head, w_out, ln_mix_post, rel_bias, ln_ffn_pre, w_gate_up, w_down, ln_ffn_post, ln_pli, w_pli_gate, w_pli_proj, loss_target, m_ln_mix_pre, m_w_in, m_ln_head, m_w_out, m_ln_mix_post, m_rel_bias, m_ln_ffn_pre, m_w_gate_up, m_w_down, m_ln_ffn_post, m_ln_pli, m_w_pli_gate, m_w_pli_proj, v_ln_mix_pre, v_w_in, v_ln_head, v_w_out, v_ln_mix_post, v_rel_bias, v_ln_ffn_pre, v_w_gate_up, v_w_down, v_ln_ffn_post, v_ln_pli, v_w_pli_gate, v_w_pli_proj):
    weights = dict(ln_mix_pre=ln_mix_pre, w_in=w_in, ln_head=ln_head, w_out=w_out, ln_mix_post=ln_mix_post,
                   rel_bias=rel_bias, ln_ffn_pre=ln_ffn_pre, w_gate_up=w_gate_up, w_down=w_down,
                   ln_ffn_post=ln_ffn_post, ln_pli=ln_pli, w_pli_gate=w_pli_gate, w_pli_proj=w_pli_proj)
    mom1 = dict(ln_mix_pre=m_ln_mix_pre, w_in=m_w_in, ln_head=m_ln_head, w_out=m_w_out, ln_mix_post=m_ln_mix_post,
                rel_bias=m_rel_bias, ln_ffn_pre=m_ln_ffn_pre, w_gate_up=m_w_gate_up, w_down=m_w_down,
                ln_ffn_post=m_ln_ffn_post, ln_pli=m_ln_pli, w_pli_gate=m_w_pli_gate, w_pli_proj=m_w_pli_proj)
    mom2 = dict(ln_mix_pre=v_ln_mix_pre, w_in=v_w_in, ln_head=v_ln_head, w_out=v_w_out, ln_mix_post=v_ln_mix_post,
                rel_bias=v_rel_bias, ln_ffn_pre=v_ln_ffn_pre, w_gate_up=v_w_gate_up, w_down=v_w_down,
                ln_ffn_post=v_ln_ffn_post, ln_pli=v_ln_pli, w_pli_gate=v_w_pli_gate, w_pli_proj=v_w_pli_proj)

    _, seq, d_model = x.shape
    depth = w_in.shape[0]
    n_heads = d_model // HEAD_DIM
    n_sb = n_heads // 2
    n_dl = n_heads - n_sb
    assert seq % (HEAD_DIM * DILATIONS[-1]) == 0 and d_model % (2 * HEAD_DIM) == 0
    bq = 256
    tr = 128
    tr_ff = 64

    xs = x[0]
    target = loss_target[0]
    gain = {n: [weights[n][l][None, :] for l in range(depth)] for n in SMALL}
    bias_mask, buckets = _dil_tables(rel_bias)

    zero_token = jnp.zeros((8, LANE), F32)
    token = zero_token
    gathers = []
    for l in range(depth):
        parts = []
        for names in ((FIRST_USED, REST) if l == 0 else (BIG,)):
            slots = [_cast_layer(weights[n], l, n) for n in names]
            tag = f"{l}" if names is BIG else f"{l}_{names[0]}"
            parts.append((names, tag, _split_start(slots, token, _gather_copies, 4 * len(names), f"gather_start_{tag}")))
            token = parts[-1][2][3]
        gathers.append(parts)

    saved = []
    h1 = _norm_in(xs, gain["ln_mix_pre"][0], tr)
    xin = xs
    for l in range(depth):
        wg = {}
        for i, (names, tag, started) in enumerate(gathers[l]):
            after = (token if l == 0 else saved[l - 1]["x3"]) if i == 0 else proj
            wg.update(zip(names, _split_wait(started, after, _gather_copies, f"gather_wait_{tag}")))
            if i == 0:
                proj = _mm_fwd(h1, wg["w_in"], True, "mm_in", tn_cap=768, tk_cap=2048)
        o_sb, lt_sb = _sb_fwd(proj, n_sb, bq)
        o_dl, lse_dl = _dil_fwd(proj, bias_mask, n_sb, n_dl)
        on = _headnorm(o_sb, o_dl, gain["ln_head"][l], tr)
        y = _mm_fwd(on, wg["w_out"], False, "mm_out", tn_cap=1024)
        x2, h2 = _res_norm(xin, y, gain["ln_mix_post"][l], gain["ln_ffn_pre"][l], "post_attn", tr)
        gu = _mm_fwd(h2, wg["w_gate_up"], True, "mm_gate_up")
        act = _swiglu(gu, tr_ff)
        f = _mm_fwd(act, wg["w_down"], False, "mm_down", tn_cap=1024, tk_cap=1536)
        x3, h3 = _res_norm(x2, f, gain["ln_ffn_post"][l], gain["ln_pli"][l], "post_ffn", tr)
        gl = _mm_fwd(h3, wg["w_pli_gate"], False, "mm_pli_gate", tn_cap=1024)
        pl_in = p[l, 0]
        pp = _mm_fwd(pl_in, wg["w_pli_proj"], True, "mm_pli_proj")
        saved.append(dict(wg=wg, x=xin, h1=h1, proj=proj, o_sb=o_sb, lt_sb=lt_sb, o_dl=o_dl, lse_dl=lse_dl, on=on, y=y, x2=x2,
                          h2=h2, gu=gu, act=act, f=f, x3=x3, h3=h3, gl=gl, pp=pp, p=pl_in))
        if l + 1 < depth:
            xin, h1 = _pli_out(x3, gl, pp, gain["ln_mix_pre"][l + 1], tr)
        else:
            dx, loss_part = _loss_head(x3, gl, pp, target, tr)

    grad_big = {n: [None] * depth for n in BIG}
    grad_gain = {n: [None] * depth for n in SMALL}
    ds_layers = [None] * depth

    def start_scatter(layer, names, dw):
        dws = [dw[n] for n in names]
        landing = [lax.empty((N_DEV - 1, a.shape[1] // 2, a.shape[2]), a.dtype) for a in dws]
        tag = f"{layer}" if names is BIG else f"{layer}_{names[0]}"
        return layer, names, tag, _split_start(dws + landing, zero_token, _scatter_copies, N_DEV * len(names),
                                               f"scatter_start_{tag}")

    def reduce_layer(layer, names, tag, started, after):
        arrays = _split_wait(started, after, _scatter_copies, f"scatter_wait_{tag}")
        nw = len(names)
        halves = [_reduce_piece(arrays[w], arrays[nw + w], names[w]) for w in range(nw)]
        joined = _join_halves(halves)
        for n, g in zip(names, joined):
            grad_big[n][layer] = g
        return joined[0]

    pending = None
    for l in reversed(range(depth)):
        sv = saved[l]
        wg = sv["wg"]
        dpp, dgl = _pli_bwd(dx, sv["gl"], sv["pp"], tr, () if pending is None else (pending[3][3],))
        dw = {}
        dw["w_pli_proj"] = _mm_wgrad(sv["p"], dpp, True, "wg_pli_proj")
        dw["w_pli_gate"] = _mm_wgrad(sv["h3"], dgl, False, "wg_pli_gate")
        dh3 = _mm_dgrad(dgl, wg["w_pli_gate"], False, "dg_pli_gate")
        dx3, df, grad_gain["ln_pli"][l], grad_gain["ln_ffn_post"][l] = _res_norm_bwd(
            dx, dh3, sv["x3"], sv["f"], gain["ln_pli"][l], gain["ln_ffn_post"][l], "post_ffn_bwd", tr)
        dw["w_down"] = _mm_wgrad(sv["act"], df, False, "wg_down")
        dact = _mm_dgrad(df, wg["w_down"], False, "dg_down")
        dgu = _swiglu_bwd(dact, sv["gu"], tr_ff)
        dw["w_gate_up"] = _mm_wgrad(sv["h2"], dgu, True, "wg_gate_up")
        dh2 = _mm_dgrad(dgu, wg["w_gate_up"], True, "dg_gate_up", tc_cap=1536)
        dx2, dy, grad_gain["ln_ffn_pre"][l], grad_gain["ln_mix_post"][l] = _res_norm_bwd(
            dx3, dh2, sv["x2"], sv["y"], gain["ln_ffn_pre"][l], gain["ln_mix_post"][l], "post_attn_bwd", tr)
        dw["w_out"] = _mm_wgrad(sv["on"], dy, False, "wg_out")
        don = _mm_dgrad(dy, wg["w_out"], False, "dg_out")
        early = start_scatter(l, REST, dw) if l == 0 else None
        do_sb, do_dl, grad_gain["ln_head"][l] = _headnorm_bwd(
            don, sv["o_sb"], sv["o_dl"], gain["ln_head"][l], tr, () if early is None else (early[3][3],))
        dq_s, dk_s, dv_s = _sb_bwd(sv["proj"], sv["lt_sb"], do_sb, n_sb, bq)
        dq_d, dk_d, dv_d, ds_layers[l] = _dil_bwd(sv["proj"], do_dl, sv["o_dl"], sv["lse_dl"], bias_mask, n_sb, n_dl)
        dproj = jnp.concatenate([dq_s, dk_s, dv_s, dq_d, dk_d, dv_d], axis=1)
        dw["w_in"] = _mm_wgrad(sv["h1"], dproj, True, "wg_in")
        dh1 = _mm_dgrad(dproj, wg["w_in"], True, "dg_in", tc_cap=1536)
        dx, grad_gain["ln_mix_pre"][l] = _norm_in_bwd(dx2, dh1, sv["x"], gain["ln_mix_pre"][l], tr)
        late = start_scatter(l, BIG if l > 0 else FIRST_USED, dw)
        if pending is not None:
            reduce_layer(*pending, dx if l > 0 else late[3][3])
        pending = late
    done = reduce_layer(*early, pending[3][3])
    reduce_layer(*pending, done)

    db = _rel_bias_grad(jnp.stack(ds_layers, axis=0), buckets)
    rb_flat = db[:n_dl, :NUM_BUCKETS].T.reshape(1, NUM_BUCKETS * n_dl)
    def widen(v):
        return jnp.pad(v, ((0, 0), (0, d_model - v.shape[1])))
    small_rows = [grad_gain[n][l] for n in SMALL for l in range(depth)] + [widen(rb_flat), widen(loss_part)]
    n_rows = len(small_rows)
    small = jnp.concatenate(small_rows + [jnp.zeros((-n_rows % 8, d_model), F32)], axis=0)
    total = _allreduce_small(small)
    grads = {}
    for i, n in enumerate(SMALL):
        grads[n] = total[i * depth:(i + 1) * depth]
    grads["rel_bias"] = total[len(SMALL) * depth, :NUM_BUCKETS * n_dl].reshape(NUM_BUCKETS, n_dl)
    loss = (0.5 / d_model) * jnp.sum(total[len(SMALL) * depth + 1, :LANE])
    for n in BIG:
        grads[n] = jnp.stack(grad_big[n], axis=0)

    delta, new_m, new_v = {}, {}, {}
    for n in WEIGHTS:
        delta[n], new_m[n], new_v[n] = _adamw(weights[n], grads[n], mom1[n], mom2[n], n)
    return (loss, dx[None], *[grads[n] for n in WEIGHTS], *[delta[n] for n in WEIGHTS],
            *[new_m[n] for n in WEIGHTS], *[new_v[n] for n in WEIGHTS])
```

```python
import functools
import math

import jax
import jax.numpy as jnp
from jax import lax
from jax.experimental import pallas as pl
from jax.experimental.pallas import tpu as pltpu

F32 = jnp.float32
BF16 = jnp.bfloat16

HEAD_DIM = 128
RMS_EPS = 1e-6
DILATIONS = (1, 4, 16)
NUM_BUCKETS = 32
MAX_DISTANCE = 2048
NEG = -1e30
N_CHIPS = 4
N_DEV = 8
JOIN_CHUNKS = 8
DIL_UNROLL = 8

ADAM_LR = 0.001
ADAM_B1 = 0.9
ADAM_B2 = 0.999
ADAM_EPS = 1e-08
ADAM_WD = 0.01
ADAM_STEP = 10

V7X_VMEM_LIMIT = 48 * 1024 * 1024
LANE = 128

NN = (((1,), (0,)), ((), ()))
NT = (((1,), (1,)), ((), ()))
TN = (((0,), (0,)), ((), ()))
MESH = pl.DeviceIdType.MESH

BIG = ("w_in", "w_out", "w_gate_up", "w_down", "w_pli_gate", "w_pli_proj")
FIRST_USED = BIG[:1]
REST = BIG[1:]
COL_SHARDED = {"w_in": True, "w_out": False, "w_gate_up": True, "w_down": False,
               "w_pli_gate": False, "w_pli_proj": True}
SMALL = ("ln_mix_pre", "ln_head", "ln_mix_post", "ln_ffn_pre", "ln_ffn_post", "ln_pli")
WEIGHTS = ("ln_mix_pre", "w_in", "ln_head", "w_out", "ln_mix_post", "rel_bias", "ln_ffn_pre",
           "w_gate_up", "w_down", "ln_ffn_post", "ln_pli", "w_pli_gate", "w_pli_proj")


def _tile(n, cap):
    t = min(n, cap) // LANE * LANE
    while t >= LANE:
        if n % t == 0:
            return t
        t -= LANE
    return n


def _params(sem=None):
    return pltpu.CompilerParams(dimension_semantics=sem, vmem_limit_bytes=V7X_VMEM_LIMIT)


def _rowwise(fn, rows, vecs, outs, sums, name, tr, after=()):
    s = rows[0].shape[0]
    nr, nv, no, ns, na = len(rows), len(vecs), len(outs), len(sums), len(after)

    def body(*refs):
        ins = [r[...] for r in refs[:nr + nv]]
        res = fn(*ins)
        out_refs = refs[nr + nv + na:nr + nv + na + no]
        sum_refs = refs[nr + nv + na + no:]
        for o_ref, val in zip(out_refs, res[:no]):
            o_ref[...] = val.astype(o_ref.dtype)
        if ns:
            @pl.when(pl.program_id(0) == 0)
            def _():
                for s_ref in sum_refs:
                    s_ref[...] = jnp.zeros_like(s_ref)
            for s_ref, val in zip(sum_refs, res[no:]):
                s_ref[...] += jnp.sum(val, axis=0, keepdims=True)

    in_specs = [pl.BlockSpec((tr, r.shape[1]), lambda i: (i, 0)) for r in rows]
    in_specs += [pl.BlockSpec(v.shape, lambda i: (0, 0)) for v in vecs]
    in_specs += [pl.BlockSpec(memory_space=pl.ANY)] * na
    out_specs = [pl.BlockSpec((tr, c), lambda i: (i, 0)) for c, _ in outs]
    out_specs += [pl.BlockSpec((1, c), lambda i: (0, 0)) for c in sums]
    out_shape = [jax.ShapeDtypeStruct((s, c), dt) for c, dt in outs]
    out_shape += [jax.ShapeDtypeStruct((1, c), F32) for c in sums]
    return pl.pallas_call(
        body, grid=(s // tr,), in_specs=in_specs, out_specs=out_specs, out_shape=out_shape,
        compiler_params=_params(("arbitrary",) if ns else ("parallel",)), name=name,
    )(*rows, *vecs, *after)


def _rms_r(x):
    return lax.rsqrt(jnp.mean(x * x, axis=-1, keepdims=True) + RMS_EPS)


def _rms_bwd(x, g, dy):
    r = _rms_r(x)
    u = dy * g
    dx = r * (u - x * (r * r) * jnp.mean(u * x, axis=-1, keepdims=True))
    return dx, dy * x * r


def _sigmoid(z):
    return 1.0 / (1.0 + jnp.exp(-z))


def _norm_in(x, g, tr):
    d = x.shape[1]
    return _rowwise(lambda x, g: (x * _rms_r(x) * g,), [x], [g], [(d, BF16)], [], "norm_in", tr)[0]


def _norm_in_bwd(dx_res, dh, x, g, tr):
    d = x.shape[1]

    def fn(dx_res, dh, x, g):
        dx, dg = _rms_bwd(x, g, dh)
        return dx_res + dx, dg
    return _rowwise(fn, [dx_res, dh, x], [g], [(d, F32)], [d], "norm_in_bwd", tr)


def _headnorm(o_sb, o_dl, g, tr):
    d = g.shape[1]

    def fn(o_sb, o_dl, g):
        o = jnp.concatenate([o_sb, o_dl], axis=1)
        parts = []
        for h in range(d // HEAD_DIM):
            sl = slice(h * HEAD_DIM, (h + 1) * HEAD_DIM)
            oh = o[:, sl]
            parts.append(oh * _rms_r(oh) * g[:, sl])
        return (jnp.concatenate(parts, axis=1),)
    return _rowwise(fn, [o_sb, o_dl], [g], [(d, BF16)], [], "headnorm", tr)[0]


def _headnorm_bwd(don, o_sb, o_dl, g, tr, after=()):
    d = g.shape[1]
    n_sb = o_sb.shape[1]

    def fn(don, o_sb, o_dl, g):
        o = jnp.concatenate([o_sb, o_dl], axis=1)
        dos, dgs = [], []
        for h in range(d // HEAD_DIM):
            sl = slice(h * HEAD_DIM, (h + 1) * HEAD_DIM)
            dx, dg = _rms_bwd(o[:, sl], g[:, sl], don[:, sl])
            dos.append(dx)
            dgs.append(dg)
        do = jnp.concatenate(dos, axis=1)
        return do[:, :n_sb], do[:, n_sb:], jnp.concatenate(dgs, axis=1)
    return _rowwise(fn, [don, o_sb, o_dl], [g], [(n_sb, F32), (d - n_sb, F32)], [d], "headnorm_bwd", tr, after)


def _res_norm(x, y, g_post, g_pre, name, tr):
    d = x.shape[1]

    def fn(x, y, g_post, g_pre):
        x2 = x + y * _rms_r(y) * g_post
        return x2, x2 * _rms_r(x2) * g_pre
    return _rowwise(fn, [x, y], [g_post, g_pre], [(d, F32), (d, BF16)], [], name, tr)


def _res_norm_bwd(dx_res, dh, x2, y, g_pre, g_post, name, tr):
    d = x2.shape[1]

    def fn(dx_res, dh, x2, y, g_pre, g_post):
        dxa, dg_pre = _rms_bwd(x2, g_pre, dh)
        dx2 = dx_res + dxa
        dy, dg_post = _rms_bwd(y, g_post, dx2)
        return dx2, dy, dg_pre, dg_post
    return _rowwise(fn, [dx_res, dh, x2, y], [g_pre, g_post], [(d, F32), (d, BF16)], [d, d], name, tr)


def _swiglu(gu, tr):
    ff = gu.shape[1] // 2

    def fn(gu):
        g, u = gu[:, :ff], gu[:, ff:]
        return (g * _sigmoid(g) * u,)
    return _rowwise(fn, [gu], [], [(ff, BF16)], [], "swiglu", tr)[0]


def _swiglu_bwd(dact, gu, tr):
    ff = gu.shape[1] // 2

    def fn(dact, gu):
        g, u = gu[:, :ff], gu[:, ff:]
        sg = _sigmoid(g)
        dg = dact * u * sg * (1.0 + g * (1.0 - sg))
        du = dact * g * sg
        return (jnp.concatenate([dg, du], axis=1),)
    return _rowwise(fn, [dact, gu], [], [(2 * ff, BF16)], [], "swiglu_bwd", tr)[0]


def _pli_out(x3, gl, pp, g_next, tr):
    d = x3.shape[1]

    def fn(x3, gl, pp, g):
        x4 = x3 + _sigmoid(gl) * pp
        return x4, x4 * _rms_r(x4) * g
    return _rowwise(fn, [x3, gl, pp], [g_next], [(d, F32), (d, BF16)], [], "pli_out", tr)


def _loss_head(x3, gl, pp, target, tr):
    d = x3.shape[1]

    def fn(x3, gl, pp, t):
        err = x3 + _sigmoid(gl) * pp - t
        sq = err * err
        part = sq[:, :LANE]
        for k in range(1, d // LANE):
            part = part + sq[:, k * LANE:(k + 1) * LANE]
        return err * (1.0 / d), part
    return _rowwise(fn, [x3, gl, pp, target], [], [(d, F32)], [LANE], "loss_head", tr)


def _pli_bwd(dx, gl, pp, tr, after=()):
    d = dx.shape[1]

    def fn(dx, gl, pp):
        gate = _sigmoid(gl)
        return dx * gate, dx * pp * gate * (1.0 - gate)
    return _rowwise(fn, [dx, gl, pp], [], [(d, BF16), (d, BF16)], [], "pli_bwd", tr, after)


def _my_chip():
    return 2 * lax.axis_index("x") + lax.axis_index("y")


def _cast_layer(w, layer, name):
    _, r, c = w.shape
    tr = r
    while tr * c * 4 > (4 << 20) and tr % 32 == 0:
        tr //= 2

    def body(w_ref, o_ref):
        o_ref[...] = w_ref[...].astype(o_ref.dtype)

    return pl.pallas_call(
        body, grid=(r // tr,),
        in_specs=[pl.BlockSpec((None, tr, c), lambda i: (layer, i, 0))],
        out_specs=pl.BlockSpec((None, tr, c), lambda i: (_my_chip(), i, 0)),
        out_shape=jax.ShapeDtypeStruct((N_CHIPS, r, c), BF16),
        compiler_params=_params(("parallel",)), name="cast_" + name,
    )(w)


def _reduce_piece(dw, recv, name):
    _, r, c = dw.shape
    h = r // 2
    tr = h
    while tr * c * 2 * N_DEV > (8 << 20) and tr % 32 == 0:
        tr //= 2
    per = h // tr

    def body(d_ref, r_ref, o_ref):
        acc = d_ref[...].astype(F32)
        for i in range(N_DEV - 1):
            acc = acc + r_ref[i].astype(F32)
        o_ref[...] = acc

    return pl.pallas_call(
        body, grid=(per,),
        in_specs=[pl.BlockSpec((None, tr, c), lambda i: (_my_chip(), lax.axis_index("c") * per + i, 0)),
                  pl.BlockSpec((N_DEV - 1, tr, c), lambda i: (0, i, 0))],
        out_specs=pl.BlockSpec((tr, c), lambda i: (lax.axis_index("c") * per + i, 0)),
        out_shape=jax.ShapeDtypeStruct((r, c), F32),
        compiler_params=_params(("parallel",)), name="reduce_" + name,
    )(dw, recv)


def _adamw(w, g, m, v, name):
    shape = w.shape
    if w.ndim == 3:
        w, g, m, v = (a.reshape(shape[0] * shape[1], shape[2]) for a in (w, g, m, v))
    r, c = w.shape
    tr = r
    while tr * c * 4 > (1 << 20) and tr % 16 == 0:
        tr //= 2

    def body(w_ref, g_ref, m_ref, v_ref, d_ref, m2_ref, v2_ref):
        g = g_ref[...]
        m2 = ADAM_B1 * m_ref[...] + (1.0 - ADAM_B1) * g
        v2 = ADAM_B2 * v_ref[...] + (1.0 - ADAM_B2) * (g * g)
        m_hat = m2 / (1.0 - ADAM_B1 ** ADAM_STEP)
        v_hat = v2 / (1.0 - ADAM_B2 ** ADAM_STEP)
        d_ref[...] = -ADAM_LR * (m_hat / (jnp.sqrt(v_hat) + ADAM_EPS) + ADAM_WD * w_ref[...])
        m2_ref[...] = m2
        v2_ref[...] = v2

    spec = pl.BlockSpec((tr, c), lambda i: (i, 0))
    res = pl.pallas_call(
        body, grid=(r // tr,), in_specs=[spec] * 4, out_specs=[spec] * 3,
        out_shape=[jax.ShapeDtypeStruct((r, c), F32)] * 3,
        compiler_params=_params(("parallel",)), name="adamw_" + name,
    )(w, g, m, v)
    return tuple(a.reshape(shape) for a in res)


def _mm(a, b, grid, a_spec, b_spec, o_spec, o_shape, o_dtype, dims, acc_shape, name):
    nk = grid[2]

    def body(a_ref, b_ref, o_ref, acc_ref):
        k = pl.program_id(2)

        @pl.when(k == 0)
        def _():
            acc_ref[...] = jnp.zeros_like(acc_ref)

        acc_ref[...] += lax.dot_general(a_ref[...].astype(BF16), b_ref[...].astype(BF16), dims,
                                        preferred_element_type=F32)

        @pl.when(k == nk - 1)
        def _():
            o_ref[...] = acc_ref[...].astype(o_ref.dtype)

    return pl.pallas_call(
        body, grid=grid, in_specs=[a_spec, b_spec], out_specs=o_spec,
        out_shape=jax.ShapeDtypeStruct(o_shape, o_dtype),
        scratch_shapes=[pltpu.VMEM(acc_shape, F32)],
        compiler_params=_params(("parallel", "parallel", "arbitrary")), name=name,
    )(a, b)


def _mm_fwd(a, wg, col, name, tm=1024, tn_cap=1536, tk_cap=1024):
    m, k = a.shape
    ns, r, c = wg.shape
    tm = min(tm, m)
    if col:
        n, tn, tk = ns * c, _tile(c, tn_cap), _tile(k, tk_cap)
        per = c // tn
        b_spec = pl.BlockSpec((None, tk, tn), lambda j, i, kk: (j // per, kk, j % per))
    else:
        n, tn, tk = c, _tile(c, tn_cap), _tile(r, tk_cap)
        per = r // tk
        b_spec = pl.BlockSpec((None, tk, tn), lambda j, i, kk: (kk // per, kk % per, j))
    return _mm(a, wg, (n // tn, m // tm, k // tk), pl.BlockSpec((tm, tk), lambda j, i, kk: (i, kk)), b_spec,
               pl.BlockSpec((tm, tn), lambda j, i, kk: (i, j)), (m, n), F32, NN, (tm, tn), name)


def _mm_dgrad(dc, wg, col, name, tm=1024, to_cap=1536, tc_cap=1024):
    m, n = dc.shape
    ns, r, c = wg.shape
    tm = min(tm, m)
    if col:
        kout, to, tc = r, _tile(r, to_cap), _tile(c, tc_cap)
        per = c // tc
        b_spec = pl.BlockSpec((None, to, tc), lambda j, i, kk: (kk // per, j, kk % per))
    else:
        kout, to, tc = ns * r, _tile(r, to_cap), _tile(c, tc_cap)
        per = r // to
        b_spec = pl.BlockSpec((None, to, tc), lambda j, i, kk: (j // per, j % per, kk))
    return _mm(dc, wg, (kout // to, m // tm, n // tc), pl.BlockSpec((tm, tc), lambda j, i, kk: (i, kk)), b_spec,
               pl.BlockSpec((tm, to), lambda j, i, kk: (i, j)), (m, kout), F32, NT, (tm, to), name)


def _mm_wgrad(a, dc, col, name, ti_cap=1536, tn_cap=1536, tkm=1024):
    m, k = a.shape
    n = dc.shape[1]
    tkm = min(tkm, m)
    if col:
        r, c = k, n // N_CHIPS
        ti, tn = _tile(r, ti_cap), _tile(c, tn_cap)
        per = c // tn
        o_spec = pl.BlockSpec((None, ti, tn), lambda i, j, kk: (j // per, i, j % per))
    else:
        r, c = k // N_CHIPS, n
        ti, tn = _tile(r, ti_cap), _tile(c, tn_cap)
        per = r // ti
        o_spec = pl.BlockSpec((None, ti, tn), lambda i, j, kk: (i // per, i % per, j))
    return _mm(a, dc, (k // ti, n // tn, m // tkm), pl.BlockSpec((tkm, ti), lambda i, j, kk: (kk, i)),
               pl.BlockSpec((tkm, tn), lambda i, j, kk: (kk, j)), o_spec, (N_CHIPS, r, c), BF16, TN, (ti, tn), name)


def _log_keep(z):
    return -(jnp.maximum(z, 0.0) + jnp.log(1.0 + jnp.exp(-jnp.abs(z))))


def _split_dot(x, t):
    hi = x.astype(BF16)
    lo = (x - hi.astype(F32)).astype(BF16)
    return (lax.dot_general(hi, t, NN, preferred_element_type=F32)
            + lax.dot_general(lo, t, NN, preferred_element_type=F32))


def _sb_fwd(proj, n_sb, bq):
    s = proj.shape[0]
    scale = 1.0 / math.sqrt(HEAD_DIM)

    def body(q_ref, k_ref, v_ref, o_ref, lt_ref):
        i = pl.program_id(1)
        q = q_ref[...].astype(BF16)
        row = lax.broadcasted_iota(jnp.int32, (bq, bq), 0)
        col = lax.broadcasted_iota(jnp.int32, (bq, bq), 1)
        later_in_block = (row > col).astype(BF16)
        keep = col < row

        def blocks(js, carry, diagonal=False):
            c, acc = carry
            sl = [pl.ds(pl.multiple_of(j * bq, bq), bq) for j in js]
            kbs = [k_ref[s_, :].astype(BF16) for s_ in sl]
            zs = [lax.dot_general(q, kb, NT, preferred_element_type=F32) * scale for kb in kbs]
            lks = [jnp.where(keep, _log_keep(z), 0.0) if diagonal else _log_keep(z) for z in zs]
            within = [_split_dot(lk, later_in_block) for lk in lks]
            es = []
            for z, lk, w in zip(zs, lks, within):
                e = z + lk + w + c
                es.append(jnp.where(keep, e, NEG) if diagonal else e)
                c = c + jnp.sum(lk, axis=1, keepdims=True)
            for e, s_ in zip(es, sl):
                acc = acc + lax.dot_general(jnp.exp(e).astype(BF16), v_ref[s_, :].astype(BF16), NN,
                                            preferred_element_type=F32)
            return c, acc

        carry = blocks([i], (jnp.zeros((bq, 1), F32), jnp.zeros((bq, HEAD_DIM), F32)), True)
        carry = lax.fori_loop(0, i // 4, lambda t, cr: blocks([i - 1 - 4 * t - u for u in range(4)], cr), carry)
        carry = lax.fori_loop(0, (i % 4) // 2, lambda t, cr: blocks([i % 2 + 1, i % 2], cr), carry)
        carry = lax.fori_loop(0, i % 2, lambda t, cr: blocks([0], cr), carry)
        o_ref[...] = carry[1]
        lt_ref[...] = jnp.broadcast_to(carry[0], (bq, HEAD_DIM))

    blk = pl.BlockSpec((bq, HEAD_DIM), lambda h, i: (i, h))
    shp = jax.ShapeDtypeStruct((s, n_sb * HEAD_DIM), F32)
    return pl.pallas_call(
        body, grid=(n_sb, s // bq),
        in_specs=[blk,
                  pl.BlockSpec((s, HEAD_DIM), lambda h, i: (0, n_sb + h)),
                  pl.BlockSpec((s, HEAD_DIM), lambda h, i: (0, 2 * n_sb + h))],
        out_specs=[blk, blk], out_shape=[shp, shp],
        compiler_params=_params(("parallel", "parallel")), name="sb_fwd",
    )(proj, proj, proj)


def _sb_bwd(proj, lt, do, n_sb, bq):
    s = proj.shape[0]
    nq = s // bq
    scale = 1.0 / math.sqrt(HEAD_DIM)

    def body(q_ref, k_ref, v_ref, lt_ref, do_ref, dq_ref, dk_ref, dv_ref, dk_acc, dv_acc):
        i = pl.program_id(1)

        @pl.when(i == 0)
        def _():
            dk_acc[...] = jnp.zeros_like(dk_acc)
            dv_acc[...] = jnp.zeros_like(dv_acc)

        q = q_ref[...].astype(BF16)
        do_b = do_ref[...].astype(BF16)
        ltot = jnp.max(lt_ref[...], axis=1, keepdims=True)
        row = lax.broadcasted_iota(jnp.int32, (bq, bq), 0)
        col = lax.broadcasted_iota(jnp.int32, (bq, bq), 1)
        upto_in_block = (row <= col).astype(BF16)
        before_in_block = (row < col).astype(BF16)
        keep = col < row

        def blocks(js, carry, diagonal=False):
            pk, pg, dq = carry
            sl = [pl.ds(pl.multiple_of(j * bq, bq), bq) for j in js]
            kbs = [k_ref[s_, :].astype(BF16) for s_ in sl]
            zs = [lax.dot_general(q, kb, NT, preferred_element_type=F32) * scale for kb in kbs]
            das = [lax.dot_general(do_b, v_ref[s_, :].astype(BF16), NT, preferred_element_type=F32) for s_ in sl]
            lks = [jnp.where(keep, _log_keep(z), 0.0) if diagonal else _log_keep(z) for z in zs]
            upto = [_split_dot(lk, upto_in_block) for lk in lks]
            gs, abs_ = [], []
            for z, lk, u, da in zip(zs, lks, upto, das):
                e = z + lk + ((ltot - pk) - u)
                a = jnp.exp(jnp.where(keep, e, NEG) if diagonal else e)
                gs.append(a * da)
                abs_.append(a.astype(BF16))
                pk = pk + jnp.sum(lk, axis=1, keepdims=True)
            for a_b, s_ in zip(abs_, sl):
                dv_acc[s_, :] += lax.dot_general(a_b, do_b, TN, preferred_element_type=F32)
            before = [_split_dot(g, before_in_block) for g in gs]
            dzs = []
            for z, lk, g, bf in zip(zs, lks, gs, before):
                dz = g * jnp.exp(lk) - (pg + bf) * jnp.exp(z + lk)
                dzs.append(((jnp.where(keep, dz, 0.0) if diagonal else dz) * scale).astype(BF16))
                pg = pg + jnp.sum(g, axis=1, keepdims=True)
            for dz_b, kb in zip(dzs, kbs):
                dq = dq + lax.dot_general(dz_b, kb, NN, preferred_element_type=F32)
            for dz_b, s_ in zip(dzs, sl):
                dk_acc[s_, :] += lax.dot_general(dz_b, q, TN, preferred_element_type=F32)
            return pk, pg, dq

        zero = jnp.zeros((bq, 1), F32)
        carry = lax.fori_loop(0, i // 4, lambda t, cr: blocks([4 * t + u for u in range(4)], cr),
                              (zero, zero, jnp.zeros((bq, HEAD_DIM), F32)))
        carry = lax.fori_loop(0, (i % 4) // 2, lambda t, cr: blocks([i // 4 * 4, i // 4 * 4 + 1], cr), carry)
        carry = lax.fori_loop(0, i % 2, lambda t, cr: blocks([i - 1], cr), carry)
        carry = blocks([i], carry, True)
        dq_ref[...] = carry[2].astype(dq_ref.dtype)

        @pl.when(i == nq - 1)
        def _():
            dk_ref[...] = dk_acc[...].astype(dk_ref.dtype)
            dv_ref[...] = dv_acc[...].astype(dv_ref.dtype)

    blk = pl.BlockSpec((bq, HEAD_DIM), lambda h, i: (i, h))
    full = pl.BlockSpec((s, HEAD_DIM), lambda h, i: (0, h))
    shp = jax.ShapeDtypeStruct((s, n_sb * HEAD_DIM), BF16)
    return pl.pallas_call(
        body, grid=(n_sb, nq),
        in_specs=[blk,
                  pl.BlockSpec((s, HEAD_DIM), lambda h, i: (0, n_sb + h)),
                  pl.BlockSpec((s, HEAD_DIM), lambda h, i: (0, 2 * n_sb + h)),
                  blk, blk],
        out_specs=[blk, full, full], out_shape=[shp, shp, shp],
        scratch_shapes=[pltpu.VMEM((s, HEAD_DIM), F32), pltpu.VMEM((s, HEAD_DIM), F32)],
        compiler_params=_params(("parallel", "arbitrary")), name="sb_bwd",
    )(proj, proj, proj, lt, do)


def _t5_bucket(dist):
    max_exact = NUM_BUCKETS // 2
    d = jnp.maximum(dist, 1).astype(F32)
    large = max_exact + (jnp.log(d / max_exact) / math.log(MAX_DISTANCE / max_exact)
                         * (NUM_BUCKETS - max_exact)).astype(jnp.int32)
    large = jnp.minimum(large, NUM_BUCKETS - 1)
    return jnp.where(dist < max_exact, dist, large)


def _dil_tables(rel_bias):
    qi = jnp.arange(HEAD_DIM, dtype=jnp.int32)[:, None]
    ki = jnp.arange(2 * HEAD_DIM, dtype=jnp.int32)[None, :]
    rel = HEAD_DIM + qi - ki
    band = (rel >= 0) & (rel <= HEAD_DIM)
    biases, buckets = [], []
    for d in DILATIONS:
        bucket = _t5_bucket(jnp.maximum(rel, 0) * d)
        onehot = (bucket[:, :, None] == jnp.arange(NUM_BUCKETS, dtype=jnp.int32)).astype(F32)
        bias = jnp.einsum("qkb,bh->hqk", onehot, rel_bias.astype(F32), precision=lax.Precision.HIGHEST)
        biases.append(jnp.where(band[None], bias, NEG))
        buckets.append(jnp.where(band, bucket, -1).astype(F32))
    return jnp.stack(biases, axis=1), jnp.stack(buckets, axis=0)


def _sub_rows(ref, start, d):
    if d == 1:
        return ref[pl.ds(pl.multiple_of(start, HEAD_DIM), HEAD_DIM), :]
    return ref[pl.ds(start, HEAD_DIM, stride=d), :]


def _sub_idx(start, d):
    if d == 1:
        return pl.ds(pl.multiple_of(start, HEAD_DIM), HEAD_DIM)
    return pl.ds(start, HEAD_DIM, stride=d)


def _dil_logits(q_ref, k_ref, bm, n, cur, prv, d, scale):
    qb = _sub_rows(q_ref, cur, d).astype(BF16)
    kk = jnp.concatenate([_sub_rows(k_ref, prv, d), _sub_rows(k_ref, cur, d)], axis=0).astype(BF16)
    sc = lax.dot_general(qb, kk, NT, preferred_element_type=F32) * scale + bm
    colk = lax.broadcasted_iota(jnp.int32, sc.shape, 1)
    sc = jnp.where((colk >= HEAD_DIM) | (n > 0), sc, NEG)
    return qb, kk, sc


def _dil_fwd(proj, bm, n_sb, n_dl):
    s = proj.shape[0]
    scale = 1.0 / math.sqrt(HEAD_DIM)
    chunk = min(s, 512)

    def body(q_ref, k_ref, v_ref, bm_ref, o_ref, l_ref, ob0, ob1, ob2, lb0, lb1, lb2):
        obs, lbs = (ob0, ob1, ob2), (lb0, lb1, lb2)
        for b, d in enumerate(DILATIONS):
            nb = s // (HEAD_DIM * d)

            def step(idx, _, b=b, d=d, nb=nb):
                r, n = idx // nb, idx % nb
                cur = n * (HEAD_DIM * d) + r
                prv = jnp.maximum(n - 1, 0) * (HEAD_DIM * d) + r
                _, _, sc = _dil_logits(q_ref, k_ref, bm_ref[b], n, cur, prv, d, scale)
                vv = jnp.concatenate([_sub_rows(v_ref, prv, d), _sub_rows(v_ref, cur, d)], axis=0).astype(BF16)
                mx = jnp.max(sc, axis=1, keepdims=True)
                pr = jnp.exp(sc - mx)
                den = jnp.sum(pr, axis=1, keepdims=True)
                o = lax.dot_general(pr.astype(BF16), vv, NN, preferred_element_type=F32) / den
                obs[b][_sub_idx(cur, d), :] = o
                lbs[b][_sub_idx(cur, d), :] = jnp.broadcast_to(mx + jnp.log(den), (HEAD_DIM, HEAD_DIM))
                return 0

            def group(g, _, step=step):
                for u in range(DIL_UNROLL):
                    step(g * DIL_UNROLL + u, 0)
                return 0

            lax.fori_loop(0, s // HEAD_DIM // DIL_UNROLL, group, 0)

        for ci in range(s // chunk):
            sl = pl.ds(ci * chunk, chunk)
            l0, l1, l2 = lb0[sl, :], lb1[sl, :], lb2[sl, :]
            mx = jnp.maximum(jnp.maximum(l0, l1), l2)
            w0, w1, w2 = jnp.exp(l0 - mx), jnp.exp(l1 - mx), jnp.exp(l2 - mx)
            tot = w0 + w1 + w2
            o_ref[sl, :] = (w0 * ob0[sl, :] + w1 * ob1[sl, :] + w2 * ob2[sl, :]) / tot
            l_ref[sl, :] = mx + jnp.log(tot)

    base = 3 * n_sb
    full = pl.BlockSpec((s, HEAD_DIM), lambda h: (0, h))
    shp = jax.ShapeDtypeStruct((s, n_dl * HEAD_DIM), F32)
    return pl.pallas_call(
        body, grid=(n_dl,),
        in_specs=[pl.BlockSpec((s, HEAD_DIM), lambda h: (0, base + h)),
                  pl.BlockSpec((s, HEAD_DIM), lambda h: (0, base + n_dl + h)),
                  pl.BlockSpec((s, HEAD_DIM), lambda h: (0, base + 2 * n_dl + h)),
                  pl.BlockSpec((None, 3, HEAD_DIM, 2 * HEAD_DIM), lambda h: (h, 0, 0, 0))],
        out_specs=[full, full], out_shape=[shp, shp],
        scratch_shapes=[pltpu.VMEM((s, HEAD_DIM), F32)] * 6,
        compiler_params=_params(("parallel",)), name="dil_fwd",
    )(proj, proj, proj, bm)


def _dil_bwd(proj, do, o, lse, bm, n_sb, n_dl):
    s = proj.shape[0]
    scale = 1.0 / math.sqrt(HEAD_DIM)
    chunk = min(s, 512)

    def body(q_ref, k_ref, v_ref, do_ref, o_ref, l_ref, bm_ref, dq_ref, dk_ref, dv_ref, ds_ref, dq_s, dk_s, dv_s):
        dq_s[...] = jnp.zeros_like(dq_s)
        dk_s[...] = jnp.zeros_like(dk_s)
        dv_s[...] = jnp.zeros_like(dv_s)
        ds_ref[...] = jnp.zeros_like(ds_ref)
        for b, d in enumerate(DILATIONS):
            nb = s // (HEAD_DIM * d)

            def step(idx, _, b=b, d=d, nb=nb):
                r, n = idx // nb, idx % nb
                cur = n * (HEAD_DIM * d) + r
                prv = jnp.maximum(n - 1, 0) * (HEAD_DIM * d) + r
                qb, kk, sc = _dil_logits(q_ref, k_ref, bm_ref[b], n, cur, prv, d, scale)
                vv = jnp.concatenate([_sub_rows(v_ref, prv, d), _sub_rows(v_ref, cur, d)], axis=0).astype(BF16)
                do_f = _sub_rows(do_ref, cur, d)
                do_b = do_f.astype(BF16)
                delta = jnp.sum(do_f * _sub_rows(o_ref, cur, d), axis=1, keepdims=True)
                lr = _sub_rows(l_ref, cur, d)
                w = jnp.exp(sc - jnp.concatenate([lr, lr], axis=1))
                dp = lax.dot_general(do_b, vv, NT, preferred_element_type=F32)
                ds = w * (dp - delta)
                ds_ref[b] += ds
                ds_b = (ds * scale).astype(BF16)
                dv_blk = lax.dot_general(w.astype(BF16), do_b, TN, preferred_element_type=F32)
                dk_blk = lax.dot_general(ds_b, qb, TN, preferred_element_type=F32)
                ci, pi = _sub_idx(cur, d), _sub_idx(prv, d)
                dq_s[ci, :] += lax.dot_general(ds_b, kk, NN, preferred_element_type=F32)
                dk_s[ci, :] += dk_blk[HEAD_DIM:]
                dv_s[ci, :] += dv_blk[HEAD_DIM:]
                dk_s[pi, :] += dk_blk[:HEAD_DIM]
                dv_s[pi, :] += dv_blk[:HEAD_DIM]
                return 0

            def group(g, _, step=step):
                for u in range(DIL_UNROLL):
                    step(g * DIL_UNROLL + u, 0)
                return 0

            lax.fori_loop(0, s // HEAD_DIM // DIL_UNROLL, group, 0)

        for ci in range(s // chunk):
            sl = pl.ds(ci * chunk, chunk)
            dq_ref[sl, :] = dq_s[sl, :].astype(dq_ref.dtype)
            dk_ref[sl, :] = dk_s[sl, :].astype(dk_ref.dtype)
            dv_ref[sl, :] = dv_s[sl, :].astype(dv_ref.dtype)

    base = 3 * n_sb
    full = pl.BlockSpec((s, HEAD_DIM), lambda h: (0, h))
    tab = pl.BlockSpec((None, 3, HEAD_DIM, 2 * HEAD_DIM), lambda h: (h, 0, 0, 0))
    shp = jax.ShapeDtypeStruct((s, n_dl * HEAD_DIM), BF16)
    return pl.pallas_call(
        body, grid=(n_dl,),
        in_specs=[pl.BlockSpec((s, HEAD_DIM), lambda h: (0, base + h)),
                  pl.BlockSpec((s, HEAD_DIM), lambda h: (0, base + n_dl + h)),
                  pl.BlockSpec((s, HEAD_DIM), lambda h: (0, base + 2 * n_dl + h)),
                  full, full, full, tab],
        out_specs=[full, full, full, tab],
        out_shape=[shp, shp, shp, jax.ShapeDtypeStruct((n_dl, 3, HEAD_DIM, 2 * HEAD_DIM), F32)],
        scratch_shapes=[pltpu.VMEM((s, HEAD_DIM), F32)] * 3,
        compiler_params=_params(("parallel",)), name="dil_bwd",
    )(proj, proj, proj, do, o, lse, bm)


def _rel_bias_grad(ds_all, buckets):
    depth, n_dl = ds_all.shape[:2]
    rows = -(-n_dl // 8) * 8

    def body(ds_ref, bk_ref, o_ref):
        lane = lax.broadcasted_iota(jnp.int32, (1, LANE), 1)

        def one_bucket(bkt, acc):
            fb = bkt.astype(F32)
            out = []
            for h in range(n_dl):
                val = jnp.zeros((1, 1), F32)
                for b in range(3):
                    tot = ds_ref[0, h, b]
                    for l in range(1, depth):
                        tot = tot + ds_ref[l, h, b]
                    val = val + jnp.sum(jnp.where(bk_ref[b] == fb, tot, 0.0), keepdims=True)
                out.append(jnp.where(lane == bkt, val, 0.0))
            out += [jnp.zeros((1, LANE), F32)] * (rows - n_dl)
            return acc + jnp.concatenate(out, axis=0)

        o_ref[...] = lax.fori_loop(0, NUM_BUCKETS, one_bucket, jnp.zeros((rows, LANE), F32))

    return pl.pallas_call(
        body, out_shape=jax.ShapeDtypeStruct((rows, LANE), F32),
        in_specs=[pl.BlockSpec(memory_space=pltpu.VMEM)] * 2, out_specs=pl.BlockSpec(memory_space=pltpu.VMEM),
        compiler_params=_params(), name="rel_bias_grad",
    )(ds_all, buckets)


def _place():
    return lax.axis_index("x"), lax.axis_index("y"), lax.axis_index("c")


def _flip(v, bit):
    return 1 - v if bit else v


HBM_SPEC = pl.BlockSpec(memory_space=pl.ANY)


HBM_ONLY = pl.BlockSpec(memory_space=pltpu.HBM)
SEM_SPEC = pl.BlockSpec(memory_space=pltpu.SEMAPHORE)
DATAFLOW = pltpu.SideEffectType.DATAFLOW_SIDE_EFFECTING


def _in_hbm(a):
    return pltpu.with_memory_space_constraint(a, pltpu.HBM)


def _split_start(arrays, token, copies_of, n_sem, name):
    na = len(arrays)

    def body(*refs):
        for cp in copies_of(refs[:na], refs[na + 1], refs[na + 2]):
            cp.start()
        refs[-1][...] = jnp.zeros_like(refs[-1])

    sems = pltpu.SemaphoreType.DMA((n_sem,))
    res = pl.pallas_call(
        body, name=name,
        out_shape=(sems, sems, *[pltpu.HBM(a.shape, a.dtype) for a in arrays], jax.ShapeDtypeStruct((8, LANE), F32)),
        in_specs=[HBM_ONLY] * na + [HBM_SPEC],
        out_specs=(SEM_SPEC, SEM_SPEC, *[HBM_ONLY] * na, pl.BlockSpec(memory_space=pltpu.VMEM)),
        input_output_aliases={i: 2 + i for i in range(na)},
        compiler_params=pltpu.CompilerParams(has_side_effects=DATAFLOW),
    )(*[_in_hbm(a) for a in arrays], token)
    return res[0], res[1], res[2:2 + na], res[-1]


def _split_wait(started, after, copies_of, name):
    send, recv, arrays, _ = started
    na = len(arrays)

    def body(*refs):
        for cp in copies_of(refs[:na], refs[na], refs[na + 1]):
            cp.wait_send()
            cp.wait_recv()

    return pl.pallas_call(
        body, name=name, out_shape=[pltpu.HBM(a.shape, a.dtype) for a in arrays],
        in_specs=[HBM_ONLY] * na + [SEM_SPEC, SEM_SPEC, HBM_SPEC], out_specs=[HBM_ONLY] * na,
        input_output_aliases={i: i for i in range(na)},
        compiler_params=pltpu.CompilerParams(has_side_effects=DATAFLOW),
    )(*arrays, send, recv, after)


def _gather_copies(buf, send, recv):
    x, y, c = _place()
    mine = 2 * x + y
    return [pltpu.make_async_remote_copy(
        src_ref=buf[w].at[mine], dst_ref=buf[w].at[mine], send_sem=send.at[4 * w + k], recv_sem=recv.at[4 * w + k],
        device_id=(_flip(x, k >> 1), _flip(y, k & 1), c), device_id_type=MESH)
        for w in range(len(buf)) for k in (1, 2, 3)]


def _scatter_copies(refs, send, recv):
    nw = len(refs) // 2
    src, buf = refs[:nw], refs[nw:]
    x, y, c = _place()
    out = []
    for w in range(nw):
        h = src[w].shape[1] // 2
        for k in range(1, N_DEV):
            px, py, pc = _flip(x, k >> 2), _flip(y, (k >> 1) & 1), _flip(c, k & 1)
            out.append(pltpu.make_async_remote_copy(
                src_ref=src[w].at[2 * px + py, pl.ds(pc * h, h)], dst_ref=buf[w].at[k - 1],
                send_sem=send.at[N_DEV * w + k], recv_sem=recv.at[N_DEV * w + k], device_id=(px, py, pc),
                device_id_type=MESH))
    return out


def _join_halves(grads):
    nw = len(grads)

    def body(*refs):
        ins, outs = refs[:nw], refs[nw:2 * nw]
        send, recv = refs[2 * nw:]
        x, y, c = _place()
        remote = []
        for w in range(nw):
            h = ins[w].shape[0] // 2
            hc = h // JOIN_CHUNKS
            for j in range(JOIN_CHUNKS):
                rows = pl.ds(c * h + j * hc, hc)
                cp = pltpu.make_async_remote_copy(
                    src_ref=ins[w].at[rows], dst_ref=outs[w].at[rows],
                    send_sem=send.at[w, j], recv_sem=recv.at[w, j], device_id=(x, y, 1 - c), device_id_type=MESH)
                cp.start()
                remote.append(cp)
        for w in range(nw):
            h = ins[w].shape[0] // 2
            hc = h // JOIN_CHUNKS
            for j in range(JOIN_CHUNKS):
                theirs = outs[w].at[pl.ds((1 - c) * h + j * hc, hc)]
                pltpu.make_async_remote_copy(
                    src_ref=theirs, dst_ref=theirs, send_sem=send.at[w, j], recv_sem=recv.at[w, j],
                    device_id=(x, y, c), device_id_type=MESH).wait_recv()
        for cp in remote:
            cp.wait_send()

    sems = [pltpu.SemaphoreType.DMA((nw, JOIN_CHUNKS))] * 2
    return pl.pallas_call(
        body, out_shape=[jax.ShapeDtypeStruct(a.shape, a.dtype) for a in grads],
        in_specs=[HBM_SPEC] * nw, out_specs=[HBM_SPEC] * nw, scratch_shapes=sems,
        input_output_aliases={i: i for i in range(nw)}, name="join_halves",
    )(*grads)


def _allreduce_small(v):
    rows, c = v.shape

    def body(v_ref, o_ref, buf, local_sem, send, recv):
        x, y, cc = _place()
        me = 4 * x + 2 * y + cc
        own = pltpu.make_async_copy(v_ref, buf.at[me], local_sem)
        own.start()
        sends = []
        for k in range(1, N_DEV):
            px, py, pc = _flip(x, k >> 2), _flip(y, (k >> 1) & 1), _flip(cc, k & 1)
            cp = pltpu.make_async_remote_copy(
                src_ref=v_ref, dst_ref=buf.at[me], send_sem=send.at[k], recv_sem=recv.at[k],
                device_id=(px, py, pc), device_id_type=MESH)
            cp.start()
            sends.append(cp)
        for k in range(1, N_DEV):
            px, py, pc = _flip(x, k >> 2), _flip(y, (k >> 1) & 1), _flip(cc, k & 1)
            slot = buf.at[4 * px + 2 * py + pc]
            pltpu.make_async_remote_copy(
                src_ref=slot, dst_ref=slot, send_sem=send.at[k], recv_sem=recv.at[k],
                device_id=(x, y, cc), device_id_type=MESH).wait_recv()
        for cp in sends:
            cp.wait_send()
        own.wait()
        acc = buf[0]
        for i in range(1, N_DEV):
            acc = acc + buf[i]
        o_ref[...] = acc

    return pl.pallas_call(
        body, out_shape=jax.ShapeDtypeStruct((rows, c), F32),
        in_specs=[pl.BlockSpec(memory_space=pltpu.VMEM)], out_specs=pl.BlockSpec(memory_space=pltpu.VMEM),
        scratch_shapes=[pltpu.VMEM((N_DEV, rows, c), F32), pltpu.SemaphoreType.DMA,
                        pltpu.SemaphoreType.DMA((N_DEV,)), pltpu.SemaphoreType.DMA((N_DEV,))],
        compiler_params=_params(),
        name="allreduce_small",
    )(v)


def kernel(x, p, ln_mix_pre, w_in, ln_head, w_out, ln_mix_post, rel_bias, ln_ffn_pre, w_gate_up, w_down, ln_ffn_post, ln_pli, w_pli_gate, w_pli_proj, loss_target, m_ln_mix_pre, m_w_in, m_ln_head, m_w_out, m_ln_mix_post, m_rel_bias, m_ln_ffn_pre, m_w_gate_up, m_w_down, m_ln_ffn_post, m_ln_pli, m_w_pli_gate, m_w_pli_proj, v_ln_mix_pre, v_w_in, v_ln_head, v_w_out, v_ln_mix_post, v_rel_bias, v_ln_ffn_pre, v_w_gate_up, v_w_down, v_ln_ffn_post, v_ln_pli, v_w_pli_gate, v_w_pli_proj):
    weights = dict(ln_mix_pre=ln_mix_pre, w_in=w_in, ln_head=ln_head, w_out=w_out, ln_mix_post=ln_mix_post,
                   rel_bias=rel_bias, ln_ffn_pre=ln_ffn_pre, w_gate_up=w_gate_up, w_down=w_down,
                   ln_ffn_post=ln_ffn_post, ln_pli=ln_pli, w_pli_gate=w_pli_gate, w_pli_proj=w_pli_proj)
    mom1 = dict(ln_mix_pre=m_ln_mix_pre, w_in=m_w_in, ln_head=m_ln_head, w_out=m_w_out, ln_mix_post=m_ln_mix_post,
                rel_bias=m_rel_bias, ln_ffn_pre=m_ln_ffn_pre, w_gate_up=m_w_gate_up, w_down=m_w_down,
                ln_ffn_post=m_ln_ffn_post, ln_pli=m_ln_pli, w_pli_gate=m_w_pli_gate, w_pli_proj=m_w_pli_proj)
    mom2 = dict(ln_mix_pre=v_ln_mix_pre, w_in=v_w_in, ln_head=v_ln_head, w_out=v_w_out, ln_mix_post=v_ln_mix_post,
                rel_bias=v_rel_bias, ln_ffn_pre=v_ln_ffn_pre, w_gate_up=v_w_gate_up, w_down=v_w_down,
                ln_ffn_post=v_ln_ffn_post, ln_pli=v_ln_pli, w_pli_gate=v_w_pli_gate, w_pli_proj=v_w_pli_proj)

    _, seq, d_model = x.shape
    depth = w_in.shape[0]
    n_heads = d_model // HEAD_DIM
    n_sb = n_heads // 2
    n_dl = n_heads - n_sb
    assert seq % (HEAD_DIM * DILATIONS[-1]) == 0 and d_model % (2 * HEAD_DIM) == 0
    bq = 256
    tr = 128
    tr_ff = 64

    xs = x[0]
    target = loss_target[0]
    gain = {n: [weights[n][l][None, :] for l in range(depth)] for n in SMALL}
    bias_mask, buckets = _dil_tables(rel_bias)

    zero_token = jnp.zeros((8, LANE), F32)
    token = zero_token
    gathers = []
    for l in range(depth):
        parts = []
        for names in ((FIRST_USED, REST) if l == 0 else (BIG,)):
            slots = [_cast_layer(weights[n], l, n) for n in names]
            tag = f"{l}" if names is BIG else f"{l}_{names[0]}"
            parts.append((names, tag, _split_start(slots, token, _gather_copies, 4 * len(names), f"gather_start_{tag}")))
            token = parts[-1][2][3]
        gathers.append(parts)

    saved = []
    h1 = _norm_in(xs, gain["ln_mix_pre"][0], tr)
    xin = xs
    for l in range(depth):
        wg = {}
        for i, (names, tag, started) in enumerate(gathers[l]):
            after = (token if l == 0 else saved[l - 1]["x3"]) if i == 0 else proj
            wg.update(zip(names, _split_wait(started, after, _gather_copies, f"gather_wait_{tag}")))
            if i == 0:
                proj = _mm_fwd(h1, wg["w_in"], True, "mm_in", tn_cap=768, tk_cap=2048)
        o_sb, lt_sb = _sb_fwd(proj, n_sb, bq)
        o_dl, lse_dl = _dil_fwd(proj, bias_mask, n_sb, n_dl)
        on = _headnorm(o_sb, o_dl, gain["ln_head"][l], tr)
        y = _mm_fwd(on, wg["w_out"], False, "mm_out", tn_cap=1024)
        x2, h2 = _res_norm(xin, y, gain["ln_mix_post"][l], gain["ln_ffn_pre"][l], "post_attn", tr)
        gu = _mm_fwd(h2, wg["w_gate_up"], True, "mm_gate_up")
        act = _swiglu(gu, tr_ff)
        f = _mm_fwd(act, wg["w_down"], False, "mm_down", tn_cap=1024, tk_cap=1536)
        x3, h3 = _res_norm(x2, f, gain["ln_ffn_post"][l], gain["ln_pli"][l], "post_ffn", tr)
        gl = _mm_fwd(h3, wg["w_pli_gate"], False, "mm_pli_gate", tn_cap=1024)
        pl_in = p[l, 0]
        pp = _mm_fwd(pl_in, wg["w_pli_proj"], True, "mm_pli_proj")
        saved.append(dict(wg=wg, x=xin, h1=h1, proj=proj, o_sb=o_sb, lt_sb=lt_sb, o_dl=o_dl, lse_dl=lse_dl, on=on, y=y, x2=x2,
                          h2=h2, gu=gu, act=act, f=f, x3=x3, h3=h3, gl=gl, pp=pp, p=pl_in))
        if l + 1 < depth:
            xin, h1 = _pli_out(x3, gl, pp, gain["ln_mix_pre"][l + 1], tr)
        else:
            dx, loss_part = _loss_head(x3, gl, pp, target, tr)

    grad_big = {n: [None] * depth for n in BIG}
    grad_gain = {n: [None] * depth for n in SMALL}
    ds_layers = [None] * depth

    def start_scatter(layer, names, dw):
        dws = [dw[n] for n in names]
        landing = [lax.empty((N_DEV - 1, a.shape[1] // 2, a.shape[2]), a.dtype) for a in dws]
        tag = f"{layer}" if names is BIG else f"{layer}_{names[0]}"
        return layer, names, tag, _split_start(dws + landing, zero_token, _scatter_copies, N_DEV * len(names),
                                               f"scatter_start_{tag}")

    def reduce_layer(layer, names, tag, started, after):
        arrays = _split_wait(started, after, _scatter_copies, f"scatter_wait_{tag}")
        nw = len(names)
        halves = [_reduce_piece(arrays[w], arrays[nw + w], names[w]) for w in range(nw)]
        joined = _join_halves(halves)
        for n, g in zip(names, joined):
            grad_big[n][layer] = g
        return joined[0]

    pending = None
    for l in reversed(range(depth)):
        sv = saved[l]
        wg = sv["wg"]
        dpp, dgl = _pli_bwd(dx, sv["gl"], sv["pp"], tr, () if pending is None else (pending[3][3],))
        dw = {}
        dw["w_pli_proj"] = _mm_wgrad(sv["p"], dpp, True, "wg_pli_proj")
        dw["w_pli_gate"] = _mm_wgrad(sv["h3"], dgl, False, "wg_pli_gate")
        dh3 = _mm_dgrad(dgl, wg["w_pli_gate"], False, "dg_pli_gate")
        dx3, df, grad_gain["ln_pli"][l], grad_gain["ln_ffn_post"][l] = _res_norm_bwd(
            dx, dh3, sv["x3"], sv["f"], gain["ln_pli"][l], gain["ln_ffn_post"][l], "post_ffn_bwd", tr)
        dw["w_down"] = _mm_wgrad(sv["act"], df, False, "wg_down")
        dact = _mm_dgrad(df, wg["w_down"], False, "dg_down")
        dgu = _swiglu_bwd(dact, sv["gu"], tr_ff)
        dw["w_gate_up"] = _mm_wgrad(sv["h2"], dgu, True, "wg_gate_up")
        dh2 = _mm_dgrad(dgu, wg["w_gate_up"], True, "dg_gate_up", tc_cap=1536)
        dx2, dy, grad_gain["ln_ffn_pre"][l], grad_gain["ln_mix_post"][l] = _res_norm_bwd(
            dx3, dh2, sv["x2"], sv["y"], gain["ln_ffn_pre"][l], gain["ln_mix_post"][l], "post_attn_bwd", tr)
        dw["w_out"] = _mm_wgrad(sv["on"], dy, False, "wg_out")
        don = _mm_dgrad(dy, wg["w_out"], False, "dg_out")
        early = start_scatter(l, REST, dw) if l == 0 else None
        do_sb, do_dl, grad_gain["ln_head"][l] = _headnorm_bwd(
            don, sv["o_sb"], sv["o_dl"], gain["ln_head"][l], tr, () if early is None else (early[3][3],))
        dq_s, dk_s, dv_s = _sb_bwd(sv["proj"], sv["lt_sb"], do_sb, n_sb, bq)
        dq_d, dk_d, dv_d, ds_layers[l] = _dil_bwd(sv["proj"], do_dl, sv["o_dl"], sv["lse_dl"], bias_mask, n_sb, n_dl)
        dproj = jnp.concatenate([dq_s, dk_s, dv_s, dq_d, dk_d, dv_d], axis=1)
        dw["w_in"] = _mm_wgrad(sv["h1"], dproj, True, "wg_in")
        dh1 = _mm_dgrad(dproj, wg["w_in"], True, "dg_in", tc_cap=1536)
        dx, grad_gain["ln_mix_pre"][l] = _norm_in_bwd(dx2, dh1, sv["x"], gain["ln_mix_pre"][l], tr)
        late = start_scatter(l, BIG if l > 0 else FIRST_USED, dw)
        if pending is not None:
            reduce_layer(*pending, dx if l > 0 else late[3][3])
        pending = late
    done = reduce_layer(*early, pending[3][3])
    reduce_layer(*pending, done)

    db = _rel_bias_grad(jnp.stack(ds_layers, axis=0), buckets)
    rb_flat = db[:n_dl, :NUM_BUCKETS].T.reshape(1, NUM_BUCKETS * n_dl)
    def widen(v):
        return jnp.pad(v, ((0, 0), (0, d_model - v.shape[1])))
    small_rows = [grad_gain[n][l] for n in SMALL for l in range(depth)] + [widen(rb_flat), widen(loss_part)]
    n_rows = len(small_rows)
    small = jnp.concatenate(small_rows + [jnp.zeros((-n_rows % 8, d_model), F32)], axis=0)
    total = _allreduce_small(small)
    grads = {}
    for i, n in enumerate(SMALL):
        grads[n] = total[i * depth:(i + 1) * depth]
    grads["rel_bias"] = total[len(SMALL) * depth, :NUM_BUCKETS * n_dl].reshape(NUM_BUCKETS, n_dl)
    loss = (0.5 / d_model) * jnp.sum(total[len(SMALL) * depth + 1, :LANE])
    for n in BIG:
        grads[n] = jnp.stack(grad_big[n], axis=0)

    delta, new_m, new_v = {}, {}, {}
    for n in WEIGHTS:
        delta[n], new_m[n], new_v[n] = _adamw(weights[n], grads[n], mom1[n], mom2[n], n)
    return (loss, dx[None], *[grads[n] for n in WEIGHTS], *[delta[n] for n in WEIGHTS],
            *[new_m[n] for n in WEIGHTS], *[new_v[n] for n in WEIGHTS])
```

```python
import functools
import math

import jax
import jax.numpy as jnp
from jax import lax
from jax.experimental import pallas as pl
from jax.experimental.pallas import tpu as pltpu

F32 = jnp.float32
BF16 = jnp.bfloat16

HEAD_DIM = 128
RMS_EPS = 1e-6
DILATIONS = (1, 4, 16)
NUM_BUCKETS = 32
MAX_DISTANCE = 2048
NEG = -1e30
N_CHIPS = 4
N_DEV = 8
JOIN_CHUNKS = 8
DIL_UNROLL = 8

ADAM_LR = 0.001
ADAM_B1 = 0.9
ADAM_B2 = 0.999
ADAM_EPS = 1e-08
ADAM_WD = 0.01
ADAM_STEP = 10

V7X_VMEM_LIMIT = 48 * 1024 * 1024
LANE = 128

NN = (((1,), (0,)), ((), ()))
NT = (((1,), (1,)), ((), ()))
TN = (((0,), (0,)), ((), ()))
MESH = pl.DeviceIdType.MESH

BIG = ("w_in", "w_out", "w_gate_up", "w_down", "w_pli_gate", "w_pli_proj")
FIRST_USED = BIG[:1]
REST = BIG[1:]
COL_SHARDED = {"w_in": True, "w_out": False, "w_gate_up": True, "w_down": False,
               "w_pli_gate": False, "w_pli_proj": True}
SMALL = ("ln_mix_pre", "ln_head", "ln_mix_post", "ln_ffn_pre", "ln_ffn_post", "ln_pli")
WEIGHTS = ("ln_mix_pre", "w_in", "ln_head", "w_out", "ln_mix_post", "rel_bias", "ln_ffn_pre",
           "w_gate_up", "w_down", "ln_ffn_post", "ln_pli", "w_pli_gate", "w_pli_proj")


def _tile(n, cap):
    t = min(n, cap) // LANE * LANE
    while t >= LANE:
        if n % t == 0:
            return t
        t -= LANE
    return n


def _params(sem=None):
    return pltpu.CompilerParams(dimension_semantics=sem, vmem_limit_bytes=V7X_VMEM_LIMIT)


def _rowwise(fn, rows, vecs, outs, sums, name, tr, after=()):
    s = rows[0].shape[0]
    nr, nv, no, ns, na = len(rows), len(vecs), len(outs), len(sums), len(after)

    def body(*refs):
        ins = [r[...] for r in refs[:nr + nv]]
        res = fn(*ins)
        out_refs = refs[nr + nv + na:nr + nv + na + no]
        sum_refs = refs[nr + nv + na + no:]
        for o_ref, val in zip(out_refs, res[:no]):
            o_ref[...] = val.astype(o_ref.dtype)
        if ns:
            @pl.when(pl.program_id(0) == 0)
            def _():
                for s_ref in sum_refs:
                    s_ref[...] = jnp.zeros_like(s_ref)
            for s_ref, val in zip(sum_refs, res[no:]):
                s_ref[...] += jnp.sum(val, axis=0, keepdims=True)

    in_specs = [pl.BlockSpec((tr, r.shape[1]), lambda i: (i, 0)) for r in rows]
    in_specs += [pl.BlockSpec(v.shape, lambda i: (0, 0)) for v in vecs]
    in_specs += [pl.BlockSpec(memory_space=pl.ANY)] * na
    out_specs = [pl.BlockSpec((tr, c), lambda i: (i, 0)) for c, _ in outs]
    out_specs += [pl.BlockSpec((1, c), lambda i: (0, 0)) for c in sums]
    out_shape = [jax.ShapeDtypeStruct((s, c), dt) for c, dt in outs]
    out_shape += [jax.ShapeDtypeStruct((1, c), F32) for c in sums]
    return pl.pallas_call(
        body, grid=(s // tr,), in_specs=in_specs, out_specs=out_specs, out_shape=out_shape,
        compiler_params=_params(("arbitrary",) if ns else ("parallel",)), name=name,
    )(*rows, *vecs, *after)


def _rms_r(x):
    return lax.rsqrt(jnp.mean(x * x, axis=-1, keepdims=True) + RMS_EPS)


def _rms_bwd(x, g, dy):
    r = _rms_r(x)
    u = dy * g
    dx = r * (u - x * (r * r) * jnp.mean(u * x, axis=-1, keepdims=True))
    return dx, dy * x * r


def _sigmoid(z):
    return 1.0 / (1.0 + jnp.exp(-z))


def _norm_in(x, g, tr):
    d = x.shape[1]
    return _rowwise(lambda x, g: (x * _rms_r(x) * g,), [x], [g], [(d, BF16)], [], "norm_in", tr)[0]


def _norm_in_bwd(dx_res, dh, x, g, tr):
    d = x.shape[1]

    def fn(dx_res, dh, x, g):
        dx, dg = _rms_bwd(x, g, dh)
        return dx_res + dx, dg
    return _rowwise(fn, [dx_res, dh, x], [g], [(d, F32)], [d], "norm_in_bwd", tr)


def _headnorm(o_sb, o_dl, g, tr):
    d = g.shape[1]

    def fn(o_sb, o_dl, g):
        o = jnp.concatenate([o_sb, o_dl], axis=1)
        parts = []
        for h in range(d // HEAD_DIM):
            sl = slice(h * HEAD_DIM, (h + 1) * HEAD_DIM)
            oh = o[:, sl]
            parts.append(oh * _rms_r(oh) * g[:, sl])
        return (jnp.concatenate(parts, axis=1),)
    return _rowwise(fn, [o_sb, o_dl], [g], [(d, BF16)], [], "headnorm", tr)[0]


def _headnorm_bwd(don, o_sb, o_dl, g, tr, after=()):
    d = g.shape[1]
    n_sb = o_sb.shape[1]

    def fn(don, o_sb, o_dl, g):
        o = jnp.concatenate([o_sb, o_dl], axis=1)
        dos, dgs = [], []
        for h in range(d // HEAD_DIM):
            sl = slice(h * HEAD_DIM, (h + 1) * HEAD_DIM)
            dx, dg = _rms_bwd(o[:, sl], g[:, sl], don[:, sl])
            dos.append(dx)
            dgs.append(dg)
        do = jnp.concatenate(dos, axis=1)
        return do[:, :n_sb], do[:, n_sb:], jnp.concatenate(dgs, axis=1)
    return _rowwise(fn, [don, o_sb, o_dl], [g], [(n_sb, F32), (d - n_sb, F32)], [d], "headnorm_bwd", tr, after)


def _res_norm(x, y, g_post, g_pre, name, tr, after=()):
    d = x.shape[1]

    def fn(x, y, g_post, g_pre):
        x2 = x + y * _rms_r(y) * g_post
        return x2, x2 * _rms_r(x2) * g_pre
    return _rowwise(fn, [x, y], [g_post, g_pre], [(d, F32), (d, BF16)], [], name, tr, after)


def _res_norm_bwd(dx_res, dh, x2, y, g_pre, g_post, name, tr):
    d = x2.shape[1]

    def fn(dx_res, dh, x2, y, g_pre, g_post):
        dxa, dg_pre = _rms_bwd(x2, g_pre, dh)
        dx2 = dx_res + dxa
        dy, dg_post = _rms_bwd(y, g_post, dx2)
        return dx2, dy, dg_pre, dg_post
    return _rowwise(fn, [dx_res, dh, x2, y], [g_pre, g_post], [(d, F32), (d, BF16)], [d, d], name, tr)


def _swiglu(gu, tr):
    ff = gu.shape[1] // 2

    def fn(gu):
        g, u = gu[:, :ff], gu[:, ff:]
        return (g * _sigmoid(g) * u,)
    return _rowwise(fn, [gu], [], [(ff, BF16)], [], "swiglu", tr)[0]


def _swiglu_bwd(dact, gu, tr):
    ff = gu.shape[1] // 2

    def fn(dact, gu):
        g, u = gu[:, :ff], gu[:, ff:]
        sg = _sigmoid(g)
        dg = dact * u * sg * (1.0 + g * (1.0 - sg))
        du = dact * g * sg
        return (jnp.concatenate([dg, du], axis=1),)
    return _rowwise(fn, [dact, gu], [], [(2 * ff, BF16)], [], "swiglu_bwd", tr)[0]


def _pli_out(x3, gl, pp, g_next, tr):
    d = x3.shape[1]

    def fn(x3, gl, pp, g):
        x4 = x3 + _sigmoid(gl) * pp
        return x4, x4 * _rms_r(x4) * g
    return _rowwise(fn, [x3, gl, pp], [g_next], [(d, F32), (d, BF16)], [], "pli_out", tr)


def _loss_head(x3, gl, pp, target, tr):
    d = x3.shape[1]

    def fn(x3, gl, pp, t):
        err = x3 + _sigmoid(gl) * pp - t
        sq = err * err
        part = sq[:, :LANE]
        for k in range(1, d // LANE):
            part = part + sq[:, k * LANE:(k + 1) * LANE]
        return err * (1.0 / d), part
    return _rowwise(fn, [x3, gl, pp, target], [], [(d, F32)], [LANE], "loss_head", tr)


def _pli_bwd(dx, gl, pp, tr, after=()):
    d = dx.shape[1]

    def fn(dx, gl, pp):
        gate = _sigmoid(gl)
        return dx * gate, dx * pp * gate * (1.0 - gate)
    return _rowwise(fn, [dx, gl, pp], [], [(d, BF16), (d, BF16)], [], "pli_bwd", tr, after)


def _my_chip():
    return 2 * lax.axis_index("x") + lax.axis_index("y")


def _cast_layer(w, layer, name):
    _, r, c = w.shape
    tr = r
    while tr * c * 4 > (4 << 20) and tr % 32 == 0:
        tr //= 2

    def body(w_ref, o_ref):
        o_ref[...] = w_ref[...].astype(o_ref.dtype)

    return pl.pallas_call(
        body, grid=(r // tr,),
        in_specs=[pl.BlockSpec((None, tr, c), lambda i: (layer, i, 0))],
        out_specs=pl.BlockSpec((None, tr, c), lambda i: (_my_chip(), i, 0)),
        out_shape=jax.ShapeDtypeStruct((N_CHIPS, r, c), BF16),
        compiler_params=_params(("parallel",)), name="cast_" + name,
    )(w)


def _reduce_piece(dw, recv, name):
    _, r, c = dw.shape
    h = r // 2
    tr = h
    while tr * c * 2 * N_DEV > (8 << 20) and tr % 32 == 0:
        tr //= 2
    per = h // tr

    def body(d_ref, r_ref, o_ref):
        acc = d_ref[...].astype(F32)
        for i in range(N_DEV - 1):
            acc = acc + r_ref[i].astype(F32)
        o_ref[...] = acc

    return pl.pallas_call(
        body, grid=(per,),
        in_specs=[pl.BlockSpec((None, tr, c), lambda i: (_my_chip(), lax.axis_index("c") * per + i, 0)),
                  pl.BlockSpec((N_DEV - 1, tr, c), lambda i: (0, i, 0))],
        out_specs=pl.BlockSpec((tr, c), lambda i: (lax.axis_index("c") * per + i, 0)),
        out_shape=jax.ShapeDtypeStruct((r, c), F32),
        compiler_params=_params(("parallel",)), name="reduce_" + name,
    )(dw, recv)


def _adamw(w, g, m, v, name):
    shape = w.shape
    if w.ndim == 3:
        w, g, m, v = (a.reshape(shape[0] * shape[1], shape[2]) for a in (w, g, m, v))
    r, c = w.shape
    tr = r
    while tr * c * 4 > (1 << 20) and tr % 16 == 0:
        tr //= 2

    def body(w_ref, g_ref, m_ref, v_ref, d_ref, m2_ref, v2_ref):
        g = g_ref[...]
        m2 = ADAM_B1 * m_ref[...] + (1.0 - ADAM_B1) * g
        v2 = ADAM_B2 * v_ref[...] + (1.0 - ADAM_B2) * (g * g)
        m_hat = m2 / (1.0 - ADAM_B1 ** ADAM_STEP)
        v_hat = v2 / (1.0 - ADAM_B2 ** ADAM_STEP)
        d_ref[...] = -ADAM_LR * (m_hat / (jnp.sqrt(v_hat) + ADAM_EPS) + ADAM_WD * w_ref[...])
        m2_ref[...] = m2
        v2_ref[...] = v2

    spec = pl.BlockSpec((tr, c), lambda i: (i, 0))
    res = pl.pallas_call(
        body, grid=(r // tr,), in_specs=[spec] * 4, out_specs=[spec] * 3,
        out_shape=[jax.ShapeDtypeStruct((r, c), F32)] * 3,
        compiler_params=_params(("parallel",)), name="adamw_" + name,
    )(w, g, m, v)
    return tuple(a.reshape(shape) for a in res)


def _mm(a, b, grid, a_spec, b_spec, o_spec, o_shape, o_dtype, dims, acc_shape, name):
    nk = grid[2]

    def whole(a_ref, b_ref, o_ref):
        o_ref[...] = lax.dot_general(a_ref[...].astype(BF16), b_ref[...].astype(BF16), dims,
                                     preferred_element_type=F32).astype(o_ref.dtype)

    def body(a_ref, b_ref, o_ref, acc_ref):
        k = pl.program_id(2)

        @pl.when(k == 0)
        def _():
            acc_ref[...] = jnp.zeros_like(acc_ref)

        acc_ref[...] += lax.dot_general(a_ref[...].astype(BF16), b_ref[...].astype(BF16), dims,
                                        preferred_element_type=F32)

        @pl.when(k == nk - 1)
        def _():
            o_ref[...] = acc_ref[...].astype(o_ref.dtype)

    return pl.pallas_call(
        whole if nk == 1 else body, grid=grid, in_specs=[a_spec, b_spec], out_specs=o_spec,
        out_shape=jax.ShapeDtypeStruct(o_shape, o_dtype),
        scratch_shapes=[] if nk == 1 else [pltpu.VMEM(acc_shape, F32)],
        compiler_params=_params(("parallel", "parallel", "arbitrary")), name=name,
    )(a, b)


def _mm_fwd(a, wg, col, name, tm=1024, tn_cap=1536, tk_cap=1024):
    m, k = a.shape
    ns, r, c = wg.shape
    tm = min(tm, m)
    if col:
        n, tn, tk = ns * c, _tile(c, tn_cap), _tile(k, tk_cap)
        per = c // tn
        b_spec = pl.BlockSpec((None, tk, tn), lambda j, i, kk: (j // per, kk, j % per))
    else:
        n, tn, tk = c, _tile(c, tn_cap), _tile(r, tk_cap)
        per = r // tk
        b_spec = pl.BlockSpec((None, tk, tn), lambda j, i, kk: (kk // per, kk % per, j))
    return _mm(a, wg, (n // tn, m // tm, k // tk), pl.BlockSpec((tm, tk), lambda j, i, kk: (i, kk)), b_spec,
               pl.BlockSpec((tm, tn), lambda j, i, kk: (i, j)), (m, n), F32, NN, (tm, tn), name)


def _mm_dgrad(dc, wg, col, name, tm=1024, to_cap=1536, tc_cap=2048):
    m, n = dc.shape
    ns, r, c = wg.shape
    tm = min(tm, m)
    if col:
        kout, to, tc = r, _tile(r, to_cap), _tile(c, tc_cap)
        per = c // tc
        b_spec = pl.BlockSpec((None, to, tc), lambda j, i, kk: (kk // per, j, kk % per))
    else:
        kout, to, tc = ns * r, _tile(r, to_cap), _tile(c, tc_cap)
        per = r // to
        b_spec = pl.BlockSpec((None, to, tc), lambda j, i, kk: (j // per, j % per, kk))
    return _mm(dc, wg, (kout // to, m // tm, n // tc), pl.BlockSpec((tm, tc), lambda j, i, kk: (i, kk)), b_spec,
               pl.BlockSpec((tm, to), lambda j, i, kk: (i, j)), (m, kout), F32, NT, (tm, to), name)


def _mm_wgrad(a, dc, col, name, ti_cap=1536, tn_cap=1536, tkm=2048):
    m, k = a.shape
    n = dc.shape[1]
    tkm = min(tkm, m)
    if col:
        r, c = k, n // N_CHIPS
        ti, tn = _tile(r, ti_cap), _tile(c, tn_cap)
        per = c // tn
        o_spec = pl.BlockSpec((None, ti, tn), lambda i, j, kk: (j // per, i, j % per))
    else:
        r, c = k // N_CHIPS, n
        ti, tn = _tile(r, ti_cap), _tile(c, tn_cap)
        per = r // ti
        o_spec = pl.BlockSpec((None, ti, tn), lambda i, j, kk: (i // per, i % per, j))
    return _mm(a, dc, (k // ti, n // tn, m // tkm), pl.BlockSpec((tkm, ti), lambda i, j, kk: (kk, i)),
               pl.BlockSpec((tkm, tn), lambda i, j, kk: (kk, j)), o_spec, (N_CHIPS, r, c), BF16, TN, (ti, tn), name)


def _log_keep(z):
    return -(jnp.maximum(z, 0.0) + jnp.log(1.0 + jnp.exp(-jnp.abs(z))))


def _split_dot(x, t):
    hi = x.astype(BF16)
    lo = (x - hi.astype(F32)).astype(BF16)
    return (lax.dot_general(hi, t, NN, preferred_element_type=F32)
            + lax.dot_general(lo, t, NN, preferred_element_type=F32))


def _sb_fwd(proj, n_sb, bq):
    s = proj.shape[0]
    scale = 1.0 / math.sqrt(HEAD_DIM)

    def body(q_ref, k_ref, v_ref, o_ref, lt_ref):
        i = pl.program_id(1)
        q = q_ref[...].astype(BF16)
        row = lax.broadcasted_iota(jnp.int32, (bq, bq), 0)
        col = lax.broadcasted_iota(jnp.int32, (bq, bq), 1)
        later_in_block = (row > col).astype(BF16)
        keep = col < row

        def blocks(js, carry, diagonal=False):
            c, acc = carry
            sl = [pl.ds(pl.multiple_of(j * bq, bq), bq) for j in js]
            kbs = [k_ref[s_, :].astype(BF16) for s_ in sl]
            zs = [lax.dot_general(q, kb, NT, preferred_element_type=F32) * scale for kb in kbs]
            lks = [jnp.where(keep, _log_keep(z), 0.0) if diagonal else _log_keep(z) for z in zs]
            within = [_split_dot(lk, later_in_block) for lk in lks]
            es = []
            for z, lk, w in zip(zs, lks, within):
                e = z + lk + w + c
                es.append(jnp.where(keep, e, NEG) if diagonal else e)
                c = c + jnp.sum(lk, axis=1, keepdims=True)
            for e, s_ in zip(es, sl):
                acc = acc + lax.dot_general(jnp.exp(e).astype(BF16), v_ref[s_, :].astype(BF16), NN,
                                            preferred_element_type=F32)
            return c, acc

        carry = blocks([i], (jnp.zeros((bq, 1), F32), jnp.zeros((bq, HEAD_DIM), F32)), True)
        carry = lax.fori_loop(0, i // 4, lambda t, cr: blocks([i - 1 - 4 * t - u for u in range(4)], cr), carry)
        carry = lax.fori_loop(0, (i % 4) // 2, lambda t, cr: blocks([i % 2 + 1, i % 2], cr), carry)
        carry = lax.fori_loop(0, i % 2, lambda t, cr: blocks([0], cr), carry)
        o_ref[...] = carry[1]
        lt_ref[...] = jnp.broadcast_to(carry[0], (bq, HEAD_DIM))

    blk = pl.BlockSpec((bq, HEAD_DIM), lambda h, i: (i, h))
    shp = jax.ShapeDtypeStruct((s, n_sb * HEAD_DIM), F32)
    return pl.pallas_call(
        body, grid=(n_sb, s // bq),
        in_specs=[blk,
                  pl.BlockSpec((s, HEAD_DIM), lambda h, i: (0, n_sb + h)),
                  pl.BlockSpec((s, HEAD_DIM), lambda h, i: (0, 2 * n_sb + h))],
        out_specs=[blk, blk], out_shape=[shp, shp],
        compiler_params=_params(("parallel", "parallel")), name="sb_fwd",
    )(proj, proj, proj)


def _sb_bwd(proj, lt, do, n_sb, bq):
    s = proj.shape[0]
    nq = s // bq
    scale = 1.0 / math.sqrt(HEAD_DIM)

    def body(q_ref, k_ref, v_ref, lt_ref, do_ref, dq_ref, dk_ref, dv_ref, dk_acc, dv_acc):
        i = pl.program_id(1)

        @pl.when(i == 0)
        def _():
            dk_acc[...] = jnp.zeros_like(dk_acc)
            dv_acc[...] = jnp.zeros_like(dv_acc)

        q = q_ref[...].astype(BF16)
        do_b = do_ref[...].astype(BF16)
        ltot = jnp.max(lt_ref[...], axis=1, keepdims=True)
        row = lax.broadcasted_iota(jnp.int32, (bq, bq), 0)
        col = lax.broadcasted_iota(jnp.int32, (bq, bq), 1)
        upto_in_block = (row <= col).astype(BF16)
        before_in_block = (row < col).astype(BF16)
        keep = col < row

        def blocks(js, carry, diagonal=False):
            pk, pg, dq = carry
            sl = [pl.ds(pl.multiple_of(j * bq, bq), bq) for j in js]
            kbs = [k_ref[s_, :].astype(BF16) for s_ in sl]
            zs = [lax.dot_general(q, kb, NT, preferred_element_type=F32) * scale for kb in kbs]
            das = [lax.dot_general(do_b, v_ref[s_, :].astype(BF16), NT, preferred_element_type=F32) for s_ in sl]
            lks = [jnp.where(keep, _log_keep(z), 0.0) if diagonal else _log_keep(z) for z in zs]
            upto = [_split_dot(lk, upto_in_block) for lk in lks]
            gs, abs_ = [], []
            for z, lk, u, da in zip(zs, lks, upto, das):
                e = z + lk + ((ltot - pk) - u)
                a = jnp.exp(jnp.where(keep, e, NEG) if diagonal else e)
                gs.append(a * da)
                abs_.append(a.astype(BF16))
                pk = pk + jnp.sum(lk, axis=1, keepdims=True)
            for a_b, s_ in zip(abs_, sl):
                dv_acc[s_, :] += lax.dot_general(a_b, do_b, TN, preferred_element_type=F32)
            before = [_split_dot(g, before_in_block) for g in gs]
            dzs = []
            for z, lk, g, bf in zip(zs, lks, gs, before):
                dz = g * jnp.exp(lk) - (pg + bf) * jnp.exp(z + lk)
                dzs.append(((jnp.where(keep, dz, 0.0) if diagonal else dz) * scale).astype(BF16))
                pg = pg + jnp.sum(g, axis=1, keepdims=True)
            for dz_b, kb in zip(dzs, kbs):
                dq = dq + lax.dot_general(dz_b, kb, NN, preferred_element_type=F32)
            for dz_b, s_ in zip(dzs, sl):
                dk_acc[s_, :] += lax.dot_general(dz_b, q, TN, preferred_element_type=F32)
            return pk, pg, dq

        zero = jnp.zeros((bq, 1), F32)
        carry = lax.fori_loop(0, i // 4, lambda t, cr: blocks([4 * t + u for u in range(4)], cr),
                              (zero, zero, jnp.zeros((bq, HEAD_DIM), F32)))
        carry = lax.fori_loop(0, (i % 4) // 2, lambda t, cr: blocks([i // 4 * 4, i // 4 * 4 + 1], cr), carry)
        carry = lax.fori_loop(0, i % 2, lambda t, cr: blocks([i - 1], cr), carry)
        carry = blocks([i], carry, True)
        dq_ref[...] = carry[2].astype(dq_ref.dtype)

        @pl.when(i == nq - 1)
        def _():
            dk_ref[...] = dk_acc[...].astype(dk_ref.dtype)
            dv_ref[...] = dv_acc[...].astype(dv_ref.dtype)

    blk = pl.BlockSpec((bq, HEAD_DIM), lambda h, i: (i, h))
    full = pl.BlockSpec((s, HEAD_DIM), lambda h, i: (0, h))
    shp = jax.ShapeDtypeStruct((s, n_sb * HEAD_DIM), BF16)
    return pl.pallas_call(
        body, grid=(n_sb, nq),
        in_specs=[blk,
                  pl.BlockSpec((s, HEAD_DIM), lambda h, i: (0, n_sb + h)),
                  pl.BlockSpec((s, HEAD_DIM), lambda h, i: (0, 2 * n_sb + h)),
                  blk, blk],
        out_specs=[blk, full, full], out_shape=[shp, shp, shp],
        scratch_shapes=[pltpu.VMEM((s, HEAD_DIM), F32), pltpu.VMEM((s, HEAD_DIM), F32)],
        compiler_params=_params(("parallel", "arbitrary")), name="sb_bwd",
    )(proj, proj, proj, lt, do)


def _t5_bucket(dist):
    max_exact = NUM_BUCKETS // 2
    d = jnp.maximum(dist, 1).astype(F32)
    large = max_exact + (jnp.log(d / max_exact) / math.log(MAX_DISTANCE / max_exact)
                         * (NUM_BUCKETS - max_exact)).astype(jnp.int32)
    large = jnp.minimum(large, NUM_BUCKETS - 1)
    return jnp.where(dist < max_exact, dist, large)


def _dil_tables(rel_bias):
    qi = jnp.arange(HEAD_DIM, dtype=jnp.int32)[:, None]
    ki = jnp.arange(2 * HEAD_DIM, dtype=jnp.int32)[None, :]
    rel = HEAD_DIM + qi - ki
    band = (rel >= 0) & (rel <= HEAD_DIM)
    biases, buckets = [], []
    for d in DILATIONS:
        bucket = _t5_bucket(jnp.maximum(rel, 0) * d)
        onehot = (bucket[:, :, None] == jnp.arange(NUM_BUCKETS, dtype=jnp.int32)).astype(F32)
        bias = jnp.einsum("qkb,bh->hqk", onehot, rel_bias.astype(F32), precision=lax.Precision.HIGHEST)
        biases.append(jnp.where(band[None], bias, NEG))
        buckets.append(jnp.where(band, bucket, -1).astype(F32))
    return jnp.stack(biases, axis=1), jnp.stack(buckets, axis=0)


def _sub_rows(ref, start, d):
    if d == 1:
        return ref[pl.ds(pl.multiple_of(start, HEAD_DIM), HEAD_DIM), :]
    return ref[pl.ds(start, HEAD_DIM, stride=d), :]


def _sub_idx(start, d):
    if d == 1:
        return pl.ds(pl.multiple_of(start, HEAD_DIM), HEAD_DIM)
    return pl.ds(start, HEAD_DIM, stride=d)


def _dil_logits(q_ref, k_ref, bm, n, cur, prv, d, scale):
    qb = _sub_rows(q_ref, cur, d).astype(BF16)
    kk = jnp.concatenate([_sub_rows(k_ref, prv, d), _sub_rows(k_ref, cur, d)], axis=0).astype(BF16)
    sc = lax.dot_general(qb, kk, NT, preferred_element_type=F32) * scale + bm
    colk = lax.broadcasted_iota(jnp.int32, sc.shape, 1)
    sc = jnp.where((colk >= HEAD_DIM) | (n > 0), sc, NEG)
    return qb, kk, sc


def _dil_fwd(proj, bm, n_sb, n_dl, after=()):
    s = proj.shape[0]
    scale = 1.0 / math.sqrt(HEAD_DIM)
    chunk = min(s, 512)
    na = len(after)

    def body(q_ref, k_ref, v_ref, bm_ref, *rest):
        o_ref, l_ref, ob0, ob1, ob2, lb0, lb1, lb2 = rest[na:]
        obs, lbs = (ob0, ob1, ob2), (lb0, lb1, lb2)
        for b, d in enumerate(DILATIONS):
            nb = s // (HEAD_DIM * d)

            def step(idx, _, b=b, d=d, nb=nb):
                r, n = idx // nb, idx % nb
                cur = n * (HEAD_DIM * d) + r
                prv = jnp.maximum(n - 1, 0) * (HEAD_DIM * d) + r
                _, _, sc = _dil_logits(q_ref, k_ref, bm_ref[b], n, cur, prv, d, scale)
                vv = jnp.concatenate([_sub_rows(v_ref, prv, d), _sub_rows(v_ref, cur, d)], axis=0).astype(BF16)
                mx = jnp.max(sc, axis=1, keepdims=True)
                pr = jnp.exp(sc - mx)
                den = jnp.sum(pr, axis=1, keepdims=True)
                o = lax.dot_general(pr.astype(BF16), vv, NN, preferred_element_type=F32) / den
                obs[b][_sub_idx(cur, d), :] = o
                lbs[b][_sub_idx(cur, d), :] = jnp.broadcast_to(mx + jnp.log(den), (HEAD_DIM, HEAD_DIM))
                return 0

            def group(g, _, step=step):
                for u in range(DIL_UNROLL):
                    step(g * DIL_UNROLL + u, 0)
                return 0

            lax.fori_loop(0, s // HEAD_DIM // DIL_UNROLL, group, 0)

        for ci in range(s // chunk):
            sl = pl.ds(ci * chunk, chunk)
            l0, l1, l2 = lb0[sl, :], lb1[sl, :], lb2[sl, :]
            mx = jnp.maximum(jnp.maximum(l0, l1), l2)
            w0, w1, w2 = jnp.exp(l0 - mx), jnp.exp(l1 - mx), jnp.exp(l2 - mx)
            tot = w0 + w1 + w2
            o_ref[sl, :] = (w0 * ob0[sl, :] + w1 * ob1[sl, :] + w2 * ob2[sl, :]) / tot
            l_ref[sl, :] = mx + jnp.log(tot)

    base = 3 * n_sb
    full = pl.BlockSpec((s, HEAD_DIM), lambda h: (0, h))
    shp = jax.ShapeDtypeStruct((s, n_dl * HEAD_DIM), F32)
    return pl.pallas_call(
        body, grid=(n_dl,),
        in_specs=[pl.BlockSpec((s, HEAD_DIM), lambda h: (0, base + h)),
                  pl.BlockSpec((s, HEAD_DIM), lambda h: (0, base + n_dl + h)),
                  pl.BlockSpec((s, HEAD_DIM), lambda h: (0, base + 2 * n_dl + h)),
                  pl.BlockSpec((None, 3, HEAD_DIM, 2 * HEAD_DIM), lambda h: (h, 0, 0, 0))]
        + [pl.BlockSpec(memory_space=pl.ANY)] * na,
        out_specs=[full, full], out_shape=[shp, shp],
        scratch_shapes=[pltpu.VMEM((s, HEAD_DIM), F32)] * 6,
        compiler_params=_params(("parallel",)), name="dil_fwd",
    )(proj, proj, proj, bm, *after)


def _dil_bwd(proj, do, o, lse, bm, n_sb, n_dl):
    s = proj.shape[0]
    scale = 1.0 / math.sqrt(HEAD_DIM)
    chunk = min(s, 512)

    def body(q_ref, k_ref, v_ref, do_ref, o_ref, l_ref, bm_ref, dq_ref, dk_ref, dv_ref, ds_ref, dq_s, dk_s, dv_s):
        dq_s[...] = jnp.zeros_like(dq_s)
        dk_s[...] = jnp.zeros_like(dk_s)
        dv_s[...] = jnp.zeros_like(dv_s)
        ds_ref[...] = jnp.zeros_like(ds_ref)
        for b, d in enumerate(DILATIONS):
            nb = s // (HEAD_DIM * d)

            def step(idx, _, b=b, d=d, nb=nb):
                r, n = idx // nb, idx % nb
                cur = n * (HEAD_DIM * d) + r
                prv = jnp.maximum(n - 1, 0) * (HEAD_DIM * d) + r
                qb, kk, sc = _dil_logits(q_ref, k_ref, bm_ref[b], n, cur, prv, d, scale)
                vv = jnp.concatenate([_sub_rows(v_ref, prv, d), _sub_rows(v_ref, cur, d)], axis=0).astype(BF16)
                do_f = _sub_rows(do_ref, cur, d)
                do_b = do_f.astype(BF16)
                delta = jnp.sum(do_f * _sub_rows(o_ref, cur, d), axis=1, keepdims=True)
                lr = _sub_rows(l_ref, cur, d)
                w = jnp.exp(sc - jnp.concatenate([lr, lr], axis=1))
                dp = lax.dot_general(do_b, vv, NT, preferred_element_type=F32)
                ds = w * (dp - delta)
                ds_ref[b] += ds
                ds_b = (ds * scale).astype(BF16)
                dv_blk = lax.dot_general(w.astype(BF16), do_b, TN, preferred_element_type=F32)
                dk_blk = lax.dot_general(ds_b, qb, TN, preferred_element_type=F32)
                ci, pi = _sub_idx(cur, d), _sub_idx(prv, d)
                dq_s[ci, :] += lax.dot_general(ds_b, kk, NN, preferred_element_type=F32)
                dk_s[ci, :] += dk_blk[HEAD_DIM:]
                dv_s[ci, :] += dv_blk[HEAD_DIM:]
                dk_s[pi, :] += dk_blk[:HEAD_DIM]
                dv_s[pi, :] += dv_blk[:HEAD_DIM]
                return 0

            def group(g, _, step=step):
                for u in range(DIL_UNROLL):
                    step(g * DIL_UNROLL + u, 0)
                return 0

            lax.fori_loop(0, s // HEAD_DIM // DIL_UNROLL, group, 0)

        for ci in range(s // chunk):
            sl = pl.ds(ci * chunk, chunk)
            dq_ref[sl, :] = dq_s[sl, :].astype(dq_ref.dtype)
            dk_ref[sl, :] = dk_s[sl, :].astype(dk_ref.dtype)
            dv_ref[sl, :] = dv_s[sl, :].astype(dv_ref.dtype)

    base = 3 * n_sb
    full = pl.BlockSpec((s, HEAD_DIM), lambda h: (0, h))
    tab = pl.BlockSpec((None, 3, HEAD_DIM, 2 * HEAD_DIM), lambda h: (h, 0, 0, 0))
    shp = jax.ShapeDtypeStruct((s, n_dl * HEAD_DIM), BF16)
    return pl.pallas_call(
        body, grid=(n_dl,),
        in_specs=[pl.BlockSpec((s, HEAD_DIM), lambda h: (0, base + h)),
                  pl.BlockSpec((s, HEAD_DIM), lambda h: (0, base + n_dl + h)),
                  pl.BlockSpec((s, HEAD_DIM), lambda h: (0, base + 2 * n_dl + h)),
                  full, full, full, tab],
        out_specs=[full, full, full, tab],
        out_shape=[shp, shp, shp, jax.ShapeDtypeStruct((n_dl, 3, HEAD_DIM, 2 * HEAD_DIM), F32)],
        scratch_shapes=[pltpu.VMEM((s, HEAD_DIM), F32)] * 3,
        compiler_params=_params(("parallel",)), name="dil_bwd",
    )(proj, proj, proj, do, o, lse, bm)


def _rel_bias_grad(ds_all, buckets):
    depth, n_dl = ds_all.shape[:2]
    rows = -(-n_dl // 8) * 8

    def body(ds_ref, bk_ref, o_ref):
        lane = lax.broadcasted_iota(jnp.int32, (1, LANE), 1)

        def one_bucket(bkt, acc):
            fb = bkt.astype(F32)
            out = []
            for h in range(n_dl):
                val = jnp.zeros((1, 1), F32)
                for b in range(3):
                    tot = ds_ref[0, h, b]
                    for l in range(1, depth):
                        tot = tot + ds_ref[l, h, b]
                    val = val + jnp.sum(jnp.where(bk_ref[b] == fb, tot, 0.0), keepdims=True)
                out.append(jnp.where(lane == bkt, val, 0.0))
            out += [jnp.zeros((1, LANE), F32)] * (rows - n_dl)
            return acc + jnp.concatenate(out, axis=0)

        o_ref[...] = lax.fori_loop(0, NUM_BUCKETS, one_bucket, jnp.zeros((rows, LANE), F32))

    return pl.pallas_call(
        body, out_shape=jax.ShapeDtypeStruct((rows, LANE), F32),
        in_specs=[pl.BlockSpec(memory_space=pltpu.VMEM)] * 2, out_specs=pl.BlockSpec(memory_space=pltpu.VMEM),
        compiler_params=_params(), name="rel_bias_grad",
    )(ds_all, buckets)


def _place():
    return lax.axis_index("x"), lax.axis_index("y"), lax.axis_index("c")


def _flip(v, bit):
    return 1 - v if bit else v


HBM_SPEC = pl.BlockSpec(memory_space=pl.ANY)


HBM_ONLY = pl.BlockSpec(memory_space=pltpu.HBM)
SEM_SPEC = pl.BlockSpec(memory_space=pltpu.SEMAPHORE)
DATAFLOW = pltpu.SideEffectType.DATAFLOW_SIDE_EFFECTING


def _in_hbm(a):
    return pltpu.with_memory_space_constraint(a, pltpu.HBM)


def _split_start(arrays, token, copies_of, n_sem, name):
    na = len(arrays)

    def body(*refs):
        for cp in copies_of(refs[:na], refs[na + 1], refs[na + 2]):
            cp.start()
        refs[-1][...] = jnp.zeros_like(refs[-1])

    sems = pltpu.SemaphoreType.DMA((n_sem,))
    res = pl.pallas_call(
        body, name=name,
        out_shape=(sems, sems, *[pltpu.HBM(a.shape, a.dtype) for a in arrays], jax.ShapeDtypeStruct((8, LANE), F32)),
        in_specs=[HBM_ONLY] * na + [HBM_SPEC],
        out_specs=(SEM_SPEC, SEM_SPEC, *[HBM_ONLY] * na, pl.BlockSpec(memory_space=pltpu.VMEM)),
        input_output_aliases={i: 2 + i for i in range(na)},
        compiler_params=pltpu.CompilerParams(has_side_effects=DATAFLOW),
    )(*[_in_hbm(a) for a in arrays], token)
    return res[0], res[1], res[2:2 + na], res[-1]


def _split_wait(started, after, copies_of, name):
    send, recv, arrays, _ = started
    na = len(arrays)

    def body(*refs):
        for cp in copies_of(refs[:na], refs[na], refs[na + 1]):
            cp.wait_send()
            cp.wait_recv()

    return pl.pallas_call(
        body, name=name, out_shape=[pltpu.HBM(a.shape, a.dtype) for a in arrays],
        in_specs=[HBM_ONLY] * na + [SEM_SPEC, SEM_SPEC, HBM_SPEC], out_specs=[HBM_ONLY] * na,
        input_output_aliases={i: i for i in range(na)},
        compiler_params=pltpu.CompilerParams(has_side_effects=DATAFLOW),
    )(*arrays, send, recv, after)


def _gather_ici_copies(buf, send, recv):
    x, y, c = _place()
    out = []
    for w in range(len(buf)):
        h = buf[w].shape[1] // 2
        mine = buf[w].at[2 * x + y, pl.ds(c * h, h)]
        for k in (1, 2, 3):
            out.append(pltpu.make_async_remote_copy(
                src_ref=mine, dst_ref=mine, send_sem=send.at[4 * w + k], recv_sem=recv.at[4 * w + k],
                device_id=(_flip(x, k >> 1), _flip(y, k & 1), c), device_id_type=MESH))
    return out


def _gather_d2d_copies(buf, send, recv):
    x, y, c = _place()
    out = []
    for w in range(len(buf)):
        h = buf[w].shape[1] // 2
        for k in (1, 2, 3):
            got = buf[w].at[2 * _flip(x, k >> 1) + _flip(y, k & 1), pl.ds(c * h, h)]
            out.append(pltpu.make_async_remote_copy(
                src_ref=got, dst_ref=got, send_sem=send.at[4 * w + k], recv_sem=recv.at[4 * w + k],
                device_id=(x, y, 1 - c), device_id_type=MESH))
    return out


def _scatter_copies(refs, send, recv):
    nw = len(refs) // 2
    src, buf = refs[:nw], refs[nw:]
    x, y, c = _place()
    out = []
    for w in range(nw):
        h = src[w].shape[1] // 2
        for k in range(1, N_DEV):
            px, py, pc = _flip(x, k >> 2), _flip(y, (k >> 1) & 1), _flip(c, k & 1)
            out.append(pltpu.make_async_remote_copy(
                src_ref=src[w].at[2 * px + py, pl.ds(pc * h, h)], dst_ref=buf[w].at[k - 1],
                send_sem=send.at[N_DEV * w + k], recv_sem=recv.at[N_DEV * w + k], device_id=(px, py, pc),
                device_id_type=MESH))
    return out


def _join_halves(grads):
    nw = len(grads)

    def body(*refs):
        ins, outs = refs[:nw], refs[nw:2 * nw]
        send, recv = refs[2 * nw:]
        x, y, c = _place()
        remote = []
        for w in range(nw):
            h = ins[w].shape[0] // 2
            hc = h // JOIN_CHUNKS
            for j in range(JOIN_CHUNKS):
                rows = pl.ds(c * h + j * hc, hc)
                cp = pltpu.make_async_remote_copy(
                    src_ref=ins[w].at[rows], dst_ref=outs[w].at[rows],
                    send_sem=send.at[w, j], recv_sem=recv.at[w, j], device_id=(x, y, 1 - c), device_id_type=MESH)
                cp.start()
                remote.append(cp)
        for w in range(nw):
            h = ins[w].shape[0] // 2
            hc = h // JOIN_CHUNKS
            for j in range(JOIN_CHUNKS):
                theirs = outs[w].at[pl.ds((1 - c) * h + j * hc, hc)]
                pltpu.make_async_remote_copy(
                    src_ref=theirs, dst_ref=theirs, send_sem=send.at[w, j], recv_sem=recv.at[w, j],
                    device_id=(x, y, c), device_id_type=MESH).wait_recv()
        for cp in remote:
            cp.wait_send()

    sems = [pltpu.SemaphoreType.DMA((nw, JOIN_CHUNKS))] * 2
    return pl.pallas_call(
        body, out_shape=[jax.ShapeDtypeStruct(a.shape, a.dtype) for a in grads],
        in_specs=[HBM_SPEC] * nw, out_specs=[HBM_SPEC] * nw, scratch_shapes=sems,
        input_output_aliases={i: i for i in range(nw)}, name="join_halves",
    )(*grads)


def _allreduce_small(v):
    rows, c = v.shape

    def body(v_ref, o_ref, buf, local_sem, send, recv):
        x, y, cc = _place()
        me = 4 * x + 2 * y + cc
        own = pltpu.make_async_copy(v_ref, buf.at[me], local_sem)
        own.start()
        sends = []
        for k in range(1, N_DEV):
            px, py, pc = _flip(x, k >> 2), _flip(y, (k >> 1) & 1), _flip(cc, k & 1)
            cp = pltpu.make_async_remote_copy(
                src_ref=v_ref, dst_ref=buf.at[me], send_sem=send.at[k], recv_sem=recv.at[k],
                device_id=(px, py, pc), device_id_type=MESH)
            cp.start()
            sends.append(cp)
        for k in range(1, N_DEV):
            px, py, pc = _flip(x, k >> 2), _flip(y, (k >> 1) & 1), _flip(cc, k & 1)
            slot = buf.at[4 * px + 2 * py + pc]
            pltpu.make_async_remote_copy(
                src_ref=slot, dst_ref=slot, send_sem=send.at[k], recv_sem=recv.at[k],
                device_id=(x, y, cc), device_id_type=MESH).wait_recv()
        for cp in sends:
            cp.wait_send()
        own.wait()
        acc = buf[0]
        for i in range(1, N_DEV):
            acc = acc + buf[i]
        o_ref[...] = acc

    return pl.pallas_call(
        body, out_shape=jax.ShapeDtypeStruct((rows, c), F32),
        in_specs=[pl.BlockSpec(memory_space=pltpu.VMEM)], out_specs=pl.BlockSpec(memory_space=pltpu.VMEM),
        scratch_shapes=[pltpu.VMEM((N_DEV, rows, c), F32), pltpu.SemaphoreType.DMA,
                        pltpu.SemaphoreType.DMA((N_DEV,)), pltpu.SemaphoreType.DMA((N_DEV,))],
        compiler_params=_params(),
        name="allreduce_small",
    )(v)


def kernel(x, p, ln_mix_pre, w_in, ln_head, w_out, ln_mix_post, rel_bias, ln_ffn_pre, w_gate_up, w_down, ln_ffn_post, ln_pli, w_pli_gate, w_pli_proj, loss_target, m_ln_mix_pre, m_w_in, m_ln_head, m_w_out, m_ln_mix_post, m_rel_bias, m_ln_ffn_pre, m_w_gate_up, m_w_down, m_ln_ffn_post, m_ln_pli, m_w_pli_gate, m_w_pli_proj, v_ln_mix_pre, v_w_in, v_ln_head, v_w_out, v_ln_mix_post, v_rel_bias, v_ln_ffn_pre, v_w_gate_up, v_w_down, v_ln_ffn_post, v_ln_pli, v_w_pli_gate, v_w_pli_proj):
    weights = dict(ln_mix_pre=ln_mix_pre, w_in=w_in, ln_head=ln_head, w_out=w_out, ln_mix_post=ln_mix_post,
                   rel_bias=rel_bias, ln_ffn_pre=ln_ffn_pre, w_gate_up=w_gate_up, w_down=w_down,
                   ln_ffn_post=ln_ffn_post, ln_pli=ln_pli, w_pli_gate=w_pli_gate, w_pli_proj=w_pli_proj)
    mom1 = dict(ln_mix_pre=m_ln_mix_pre, w_in=m_w_in, ln_head=m_ln_head, w_out=m_w_out, ln_mix_post=m_ln_mix_post,
                rel_bias=m_rel_bias, ln_ffn_pre=m_ln_ffn_pre, w_gate_up=m_w_gate_up, w_down=m_w_down,
                ln_ffn_post=m_ln_ffn_post, ln_pli=m_ln_pli, w_pli_gate=m_w_pli_gate, w_pli_proj=m_w_pli_proj)
    mom2 = dict(ln_mix_pre=v_ln_mix_pre, w_in=v_w_in, ln_head=v_ln_head, w_out=v_w_out, ln_mix_post=v_ln_mix_post,
                rel_bias=v_rel_bias, ln_ffn_pre=v_ln_ffn_pre, w_gate_up=v_w_gate_up, w_down=v_w_down,
                ln_ffn_post=v_ln_ffn_post, ln_pli=v_ln_pli, w_pli_gate=v_w_pli_gate, w_pli_proj=v_w_pli_proj)

    _, seq, d_model = x.shape
    depth = w_in.shape[0]
    n_heads = d_model // HEAD_DIM
    n_sb = n_heads // 2
    n_dl = n_heads - n_sb
    assert seq % (HEAD_DIM * DILATIONS[-1]) == 0 and d_model % (2 * HEAD_DIM) == 0
    bq = 256
    tr = 128
    tr_ff = 64

    xs = x[0]
    target = loss_target[0]
    gain = {n: [weights[n][l][None, :] for l in range(depth)] for n in SMALL}
    bias_mask, buckets = _dil_tables(rel_bias)

    zero_token = jnp.zeros((8, LANE), F32)
    token = zero_token
    gathers = []
    for l in range(depth):
        parts = []
        for names in ((FIRST_USED, REST) if l == 0 else (BIG,)):
            slots = [_cast_layer(weights[n], l, n) for n in names]
            tag = f"{l}" if names is BIG else f"{l}_{names[0]}"
            parts.append((names, tag, _split_start(slots, token, _gather_ici_copies, 4 * len(names),
                                                   f"gather_start_{tag}")))
            token = parts[-1][2][3]
        gathers.append(parts)

    def gather_pass_on(part, after):
        names, tag, started = part
        arrays = _split_wait(started, after, _gather_ici_copies, f"gather_wait_{tag}")
        started = _split_start(arrays, zero_token, _gather_d2d_copies, 4 * len(names), f"gather_pass_{tag}")
        return names, tag, started, started[3]

    def gather_done(passed, after):
        names, tag, started, _ = passed
        return dict(zip(names, _split_wait(started, after, _gather_d2d_copies, f"gather_done_{tag}")))

    saved = []
    h1 = _norm_in(xs, gain["ln_mix_pre"][0], tr)
    xin = xs
    passed = gather_pass_on(gathers[0][0], token)
    wg = gather_done(passed, passed[3])
    for l in range(depth):
        proj = _mm_fwd(h1, wg["w_in"], True, "mm_in", tn_cap=768, tk_cap=2048)
        o_sb, lt_sb = _sb_fwd(proj, n_sb, bq)
        if l == 0:
            passed = gather_pass_on(gathers[0][1], o_sb)
            o_dl, lse_dl = _dil_fwd(proj, bias_mask, n_sb, n_dl, (passed[3],))
            wg.update(gather_done(passed, o_dl))
        else:
            o_dl, lse_dl = _dil_fwd(proj, bias_mask, n_sb, n_dl)
        on = _headnorm(o_sb, o_dl, gain["ln_head"][l], tr)
        y = _mm_fwd(on, wg["w_out"], False, "mm_out", tn_cap=1024)
        x2, h2 = _res_norm(xin, y, gain["ln_mix_post"][l], gain["ln_ffn_pre"][l], "post_attn", tr)
        gu = _mm_fwd(h2, wg["w_gate_up"], True, "mm_gate_up", tk_cap=2048)
        act = _swiglu(gu, tr_ff)
        f = _mm_fwd(act, wg["w_down"], False, "mm_down", tn_cap=1024, tk_cap=1536)
        passed = gather_pass_on(gathers[l + 1][0], f) if l + 1 < depth else None
        x3, h3 = _res_norm(x2, f, gain["ln_ffn_post"][l], gain["ln_pli"][l], "post_ffn", tr,
                           () if passed is None else (passed[3],))
        gl = _mm_fwd(h3, wg["w_pli_gate"], False, "mm_pli_gate", tn_cap=1024)
        pl_in = p[l, 0]
        pp = _mm_fwd(pl_in, wg["w_pli_proj"], True, "mm_pli_proj")
        saved.append(dict(wg=wg, x=xin, h1=h1, proj=proj, o_sb=o_sb, lt_sb=lt_sb, o_dl=o_dl, lse_dl=lse_dl, on=on, y=y, x2=x2,
                          h2=h2, gu=gu, act=act, f=f, x3=x3, h3=h3, gl=gl, pp=pp, p=pl_in))
        if l + 1 < depth:
            xin, h1 = _pli_out(x3, gl, pp, gain["ln_mix_pre"][l + 1], tr)
            wg = gather_done(passed, pp)
        else:
            dx, loss_part = _loss_head(x3, gl, pp, target, tr)

    grad_big = {n: [None] * depth for n in BIG}
    grad_gain = {n: [None] * depth for n in SMALL}
    ds_layers = [None] * depth

    def start_scatter(layer, names, dw):
        dws = [dw[n] for n in names]
        landing = [lax.empty((N_DEV - 1, a.shape[1] // 2, a.shape[2]), a.dtype) for a in dws]
        tag = f"{layer}" if names is BIG else f"{layer}_{names[0]}"
        return layer, names, tag, _split_start(dws + landing, zero_token, _scatter_copies, N_DEV * len(names),
                                               f"scatter_start_{tag}")

    def reduce_layer(layer, names, tag, started, after):
        arrays = _split_wait(started, after, _scatter_copies, f"scatter_wait_{tag}")
        nw = len(names)
        halves = [_reduce_piece(arrays[w], arrays[nw + w], names[w]) for w in range(nw)]
        joined = _join_halves(halves)
        for n, g in zip(names, joined):
            grad_big[n][layer] = g
        return joined[0]

    pending = None
    for l in reversed(range(depth)):
        sv = saved[l]
        wg = sv["wg"]
        dpp, dgl = _pli_bwd(dx, sv["gl"], sv["pp"], tr, () if pending is None else (pending[3][3],))
        dw = {}
        dw["w_pli_proj"] = _mm_wgrad(sv["p"], dpp, True, "wg_pli_proj")
        dw["w_pli_gate"] = _mm_wgrad(sv["h3"], dgl, False, "wg_pli_gate")
        dh3 = _mm_dgrad(dgl, wg["w_pli_gate"], False, "dg_pli_gate")
        dx3, df, grad_gain["ln_pli"][l], grad_gain["ln_ffn_post"][l] = _res_norm_bwd(
            dx, dh3, sv["x3"], sv["f"], gain["ln_pli"][l], gain["ln_ffn_post"][l], "post_ffn_bwd", tr)
        dw["w_down"] = _mm_wgrad(sv["act"], df, False, "wg_down")
        dact = _mm_dgrad(df, wg["w_down"], False, "dg_down")
        dgu = _swiglu_bwd(dact, sv["gu"], tr_ff)
        dw["w_gate_up"] = _mm_wgrad(sv["h2"], dgu, True, "wg_gate_up")
        dh2 = _mm_dgrad(dgu, wg["w_gate_up"], True, "dg_gate_up", tc_cap=2816)
        dx2, dy, grad_gain["ln_ffn_pre"][l], grad_gain["ln_mix_post"][l] = _res_norm_bwd(
            dx3, dh2, sv["x2"], sv["y"], gain["ln_ffn_pre"][l], gain["ln_mix_post"][l], "post_attn_bwd", tr)
        dw["w_out"] = _mm_wgrad(sv["on"], dy, False, "wg_out")
        don = _mm_dgrad(dy, wg["w_out"], False, "dg_out")
        early = start_scatter(l, REST, dw) if l == 0 else None
        do_sb, do_dl, grad_gain["ln_head"][l] = _headnorm_bwd(
            don, sv["o_sb"], sv["o_dl"], gain["ln_head"][l], tr, () if early is None else (early[3][3],))
        dq_s, dk_s, dv_s = _sb_bwd(sv["proj"], sv["lt_sb"], do_sb, n_sb, bq)
        dq_d, dk_d, dv_d, ds_layers[l] = _dil_bwd(sv["proj"], do_dl, sv["o_dl"], sv["lse_dl"], bias_mask, n_sb, n_dl)
        dproj = jnp.concatenate([dq_s, dk_s, dv_s, dq_d, dk_d, dv_d], axis=1)
        dw["w_in"] = _mm_wgrad(sv["h1"], dproj, True, "wg_in")
        dh1 = _mm_dgrad(dproj, wg["w_in"], True, "dg_in", tc_cap=1536)
        dx, grad_gain["ln_mix_pre"][l] = _norm_in_bwd(dx2, dh1, sv["x"], gain["ln_mix_pre"][l], tr)
        late = start_scatter(l, BIG if l > 0 else FIRST_USED, dw)
        if pending is not None:
            reduce_layer(*pending, dx if l > 0 else late[3][3])
        pending = late
    done = reduce_layer(*early, pending[3][3])
    reduce_layer(*pending, done)

    db = _rel_bias_grad(jnp.stack(ds_layers, axis=0), buckets)
    rb_flat = db[:n_dl, :NUM_BUCKETS].T.reshape(1, NUM_BUCKETS * n_dl)
    def widen(v):
        return jnp.pad(v, ((0, 0), (0, d_model - v.shape[1])))
    small_rows = [grad_gain[n][l] for n in SMALL for l in range(depth)] + [widen(rb_flat), widen(loss_part)]
    n_rows = len(small_rows)
    small = jnp.concatenate(small_rows + [jnp.zeros((-n_rows % 8, d_model), F32)], axis=0)
    total = _allreduce_small(small)
    grads = {}
    for i, n in enumerate(SMALL):
        grads[n] = total[i * depth:(i + 1) * depth]
    grads["rel_bias"] = total[len(SMALL) * depth, :NUM_BUCKETS * n_dl].reshape(NUM_BUCKETS, n_dl)
    loss = (0.5 / d_model) * jnp.sum(total[len(SMALL) * depth + 1, :LANE])
    for n in BIG:
        grads[n] = jnp.stack(grad_big[n], axis=0)

    delta, new_m, new_v = {}, {}, {}
    for n in WEIGHTS:
        delta[n], new_m[n], new_v[n] = _adamw(weights[n], grads[n], mom1[n], mom2[n], n)
    return (loss, dx[None], *[grads[n] for n in WEIGHTS], *[delta[n] for n in WEIGHTS],
            *[new_m[n] for n in WEIGHTS], *[new_v[n] for n in WEIGHTS])
```

```python
import functools
import math

import jax
import jax.numpy as jnp
from jax import lax
from jax.experimental import pallas as pl
from jax.experimental.pallas import tpu as pltpu

F32 = jnp.float32
BF16 = jnp.bfloat16

HEAD_DIM = 128
RMS_EPS = 1e-6
DILATIONS = (1, 4, 16)
NUM_BUCKETS = 32
MAX_DISTANCE = 2048
NEG = -1e30
N_CHIPS = 4
N_DEV = 8
JOIN_CHUNKS = 8
DIL_UNROLL = 8

ADAM_LR = 0.001
ADAM_B1 = 0.9
ADAM_B2 = 0.999
ADAM_EPS = 1e-08
ADAM_WD = 0.01
ADAM_STEP = 10

V7X_VMEM_LIMIT = 48 * 1024 * 1024
LANE = 128

NN = (((1,), (0,)), ((), ()))
NT = (((1,), (1,)), ((), ()))
TN = (((0,), (0,)), ((), ()))
MESH = pl.DeviceIdType.MESH

BIG = ("w_in", "w_out", "w_gate_up", "w_down", "w_pli_gate", "w_pli_proj")
FIRST_USED = BIG[:1]
REST = BIG[1:]
COL_SHARDED = {"w_in": True, "w_out": False, "w_gate_up": True, "w_down": False,
               "w_pli_gate": False, "w_pli_proj": True}
SMALL = ("ln_mix_pre", "ln_head", "ln_mix_post", "ln_ffn_pre", "ln_ffn_post", "ln_pli")
WEIGHTS = ("ln_mix_pre", "w_in", "ln_head", "w_out", "ln_mix_post", "rel_bias", "ln_ffn_pre",
           "w_gate_up", "w_down", "ln_ffn_post", "ln_pli", "w_pli_gate", "w_pli_proj")


def _tile(n, cap):
    t = min(n, cap) // LANE * LANE
    while t >= LANE:
        if n % t == 0:
            return t
        t -= LANE
    return n


def _params(sem=None):
    return pltpu.CompilerParams(dimension_semantics=sem, vmem_limit_bytes=V7X_VMEM_LIMIT)


def _rowwise(fn, rows, vecs, outs, sums, name, tr, after=()):
    s = rows[0].shape[0]
    nr, nv, no, ns, na = len(rows), len(vecs), len(outs), len(sums), len(after)

    def body(*refs):
        ins = [r[...] for r in refs[:nr + nv]]
        res = fn(*ins)
        out_refs = refs[nr + nv + na:nr + nv + na + no]
        sum_refs = refs[nr + nv + na + no:]
        for o_ref, val in zip(out_refs, res[:no]):
            o_ref[...] = val.astype(o_ref.dtype)
        if ns:
            @pl.when(pl.program_id(0) == 0)
            def _():
                for s_ref in sum_refs:
                    s_ref[...] = jnp.zeros_like(s_ref)
            for s_ref, val in zip(sum_refs, res[no:]):
                s_ref[...] += jnp.sum(val, axis=0, keepdims=True)

    in_specs = [pl.BlockSpec((tr, r.shape[1]), lambda i: (i, 0)) for r in rows]
    in_specs += [pl.BlockSpec(v.shape, lambda i: (0, 0)) for v in vecs]
    in_specs += [pl.BlockSpec(memory_space=pl.ANY)] * na
    out_specs = [pl.BlockSpec((tr, c), lambda i: (i, 0)) for c, _ in outs]
    out_specs += [pl.BlockSpec((1, c), lambda i: (0, 0)) for c in sums]
    out_shape = [jax.ShapeDtypeStruct((s, c), dt) for c, dt in outs]
    out_shape += [jax.ShapeDtypeStruct((1, c), F32) for c in sums]
    return pl.pallas_call(
        body, grid=(s // tr,), in_specs=in_specs, out_specs=out_specs, out_shape=out_shape,
        compiler_params=_params(("arbitrary",) if ns else ("parallel",)), name=name,
    )(*rows, *vecs, *after)


def _rms_r(x):
    return lax.rsqrt(jnp.mean(x * x, axis=-1, keepdims=True) + RMS_EPS)


def _rms_bwd(x, g, dy):
    r = _rms_r(x)
    u = dy * g
    dx = r * (u - x * (r * r) * jnp.mean(u * x, axis=-1, keepdims=True))
    return dx, dy * x * r


def _sigmoid(z):
    return 1.0 / (1.0 + jnp.exp(-z))


def _norm_in(x, g, tr):
    d = x.shape[1]
    return _rowwise(lambda x, g: (x * _rms_r(x) * g,), [x], [g], [(d, BF16)], [], "norm_in", tr)[0]


def _norm_in_bwd(dx_res, dh, x, g, tr):
    d = x.shape[1]

    def fn(dx_res, dh, x, g):
        dx, dg = _rms_bwd(x, g, dh)
        return dx_res + dx, dg
    return _rowwise(fn, [dx_res, dh, x], [g], [(d, F32)], [d], "norm_in_bwd", tr)


def _headnorm(o_sb, o_dl, g, tr):
    d = g.shape[1]

    def fn(o_sb, o_dl, g):
        o = jnp.concatenate([o_sb, o_dl], axis=1)
        parts = []
        for h in range(d // HEAD_DIM):
            sl = slice(h * HEAD_DIM, (h + 1) * HEAD_DIM)
            oh = o[:, sl]
            parts.append(oh * _rms_r(oh) * g[:, sl])
        return (jnp.concatenate(parts, axis=1),)
    return _rowwise(fn, [o_sb, o_dl], [g], [(d, BF16)], [], "headnorm", tr)[0]


def _headnorm_bwd(don, o_sb, o_dl, g, tr, after=()):
    d = g.shape[1]
    n_sb = o_sb.shape[1]

    def fn(don, o_sb, o_dl, g):
        o = jnp.concatenate([o_sb, o_dl], axis=1)
        dos, dgs = [], []
        for h in range(d // HEAD_DIM):
            sl = slice(h * HEAD_DIM, (h + 1) * HEAD_DIM)
            dx, dg = _rms_bwd(o[:, sl], g[:, sl], don[:, sl])
            dos.append(dx)
            dgs.append(dg)
        do = jnp.concatenate(dos, axis=1)
        return do[:, :n_sb], do[:, n_sb:], jnp.concatenate(dgs, axis=1)
    return _rowwise(fn, [don, o_sb, o_dl], [g], [(n_sb, F32), (d - n_sb, F32)], [d], "headnorm_bwd", tr, after)


def _res_norm(x, y, g_post, g_pre, name, tr, after=()):
    d = x.shape[1]

    def fn(x, y, g_post, g_pre):
        x2 = x + y * _rms_r(y) * g_post
        return x2, x2 * _rms_r(x2) * g_pre
    return _rowwise(fn, [x, y], [g_post, g_pre], [(d, F32), (d, BF16)], [], name, tr, after)


def _res_norm_bwd(dx_res, dh, x2, y, g_pre, g_post, name, tr):
    d = x2.shape[1]

    def fn(dx_res, dh, x2, y, g_pre, g_post):
        dxa, dg_pre = _rms_bwd(x2, g_pre, dh)
        dx2 = dx_res + dxa
        dy, dg_post = _rms_bwd(y, g_post, dx2)
        return dx2, dy, dg_pre, dg_post
    return _rowwise(fn, [dx_res, dh, x2, y], [g_pre, g_post], [(d, F32), (d, BF16)], [d, d], name, tr)


def _swiglu(gu, tr):
    ff = gu.shape[1] // 2

    def fn(gu):
        g, u = gu[:, :ff], gu[:, ff:]
        return (g * _sigmoid(g) * u,)
    return _rowwise(fn, [gu], [], [(ff, BF16)], [], "swiglu", tr)[0]


def _swiglu_bwd(dact, gu, tr):
    ff = gu.shape[1] // 2

    def fn(dact, gu):
        g, u = gu[:, :ff], gu[:, ff:]
        sg = _sigmoid(g)
        dg = dact * u * sg * (1.0 + g * (1.0 - sg))
        du = dact * g * sg
        return (jnp.concatenate([dg, du], axis=1),)
    return _rowwise(fn, [dact, gu], [], [(2 * ff, BF16)], [], "swiglu_bwd", tr)[0]


def _pli_out(x3, gl, pp, g_next, tr):
    d = x3.shape[1]

    def fn(x3, gl, pp, g):
        x4 = x3 + _sigmoid(gl) * pp
        return x4, x4 * _rms_r(x4) * g
    return _rowwise(fn, [x3, gl, pp], [g_next], [(d, F32), (d, BF16)], [], "pli_out", tr)


def _loss_head(x3, gl, pp, target, tr):
    d = x3.shape[1]

    def fn(x3, gl, pp, t):
        err = x3 + _sigmoid(gl) * pp - t
        sq = err * err
        part = sq[:, :LANE]
        for k in range(1, d // LANE):
            part = part + sq[:, k * LANE:(k + 1) * LANE]
        return err * (1.0 / d), part
    return _rowwise(fn, [x3, gl, pp, target], [], [(d, F32)], [LANE], "loss_head", tr)


def _pli_bwd(dx, gl, pp, tr, after=()):
    d = dx.shape[1]

    def fn(dx, gl, pp):
        gate = _sigmoid(gl)
        return dx * gate, dx * pp * gate * (1.0 - gate)
    return _rowwise(fn, [dx, gl, pp], [], [(d, BF16), (d, BF16)], [], "pli_bwd", tr, after)


def _my_chip():
    return 2 * lax.axis_index("x") + lax.axis_index("y")


def _cast_layer(w, layer, name):
    _, r, c = w.shape
    tr = r
    while tr * c * 4 > (4 << 20) and tr % 32 == 0:
        tr //= 2

    def body(w_ref, o_ref):
        o_ref[...] = w_ref[...].astype(o_ref.dtype)

    return pl.pallas_call(
        body, grid=(r // tr,),
        in_specs=[pl.BlockSpec((None, tr, c), lambda i: (layer, i, 0))],
        out_specs=pl.BlockSpec((None, tr, c), lambda i: (_my_chip(), i, 0)),
        out_shape=jax.ShapeDtypeStruct((N_CHIPS, r, c), BF16),
        compiler_params=_params(("parallel",)), name="cast_" + name,
    )(w)


def _reduce_piece(dw, recv, name):
    _, r, c = dw.shape
    h = r // 2
    tr = h
    while tr * c * 2 * N_DEV > (8 << 20) and tr % 32 == 0:
        tr //= 2
    per = h // tr

    def body(d_ref, r_ref, o_ref):
        acc = d_ref[...].astype(F32)
        for i in range(N_DEV - 1):
            acc = acc + r_ref[i].astype(F32)
        o_ref[...] = acc

    return pl.pallas_call(
        body, grid=(per,),
        in_specs=[pl.BlockSpec((None, tr, c), lambda i: (_my_chip(), lax.axis_index("c") * per + i, 0)),
                  pl.BlockSpec((N_DEV - 1, tr, c), lambda i: (0, i, 0))],
        out_specs=pl.BlockSpec((tr, c), lambda i: (lax.axis_index("c") * per + i, 0)),
        out_shape=jax.ShapeDtypeStruct((r, c), F32),
        compiler_params=_params(("parallel",)), name="reduce_" + name,
    )(dw, recv)


def _adamw(w, g, m, v, name):
    shape = w.shape
    if w.ndim == 3:
        w, g, m, v = (a.reshape(shape[0] * shape[1], shape[2]) for a in (w, g, m, v))
    r, c = w.shape
    tr = r
    while tr * c * 4 > (1 << 20) and tr % 16 == 0:
        tr //= 2

    def body(w_ref, g_ref, m_ref, v_ref, d_ref, m2_ref, v2_ref):
        g = g_ref[...]
        m2 = ADAM_B1 * m_ref[...] + (1.0 - ADAM_B1) * g
        v2 = ADAM_B2 * v_ref[...] + (1.0 - ADAM_B2) * (g * g)
        m_hat = m2 / (1.0 - ADAM_B1 ** ADAM_STEP)
        v_hat = v2 / (1.0 - ADAM_B2 ** ADAM_STEP)
        d_ref[...] = -ADAM_LR * (m_hat / (jnp.sqrt(v_hat) + ADAM_EPS) + ADAM_WD * w_ref[...])
        m2_ref[...] = m2
        v2_ref[...] = v2

    spec = pl.BlockSpec((tr, c), lambda i: (i, 0))
    res = pl.pallas_call(
        body, grid=(r // tr,), in_specs=[spec] * 4, out_specs=[spec] * 3,
        out_shape=[jax.ShapeDtypeStruct((r, c), F32)] * 3,
        compiler_params=_params(("parallel",)), name="adamw_" + name,
    )(w, g, m, v)
    return tuple(a.reshape(shape) for a in res)


def _mm(a, b, grid, a_spec, b_spec, o_spec, o_shape, o_dtype, dims, acc_shape, name):
    nk = grid[2]

    def whole(a_ref, b_ref, o_ref):
        o_ref[...] = lax.dot_general(a_ref[...].astype(BF16), b_ref[...].astype(BF16), dims,
                                     preferred_element_type=F32).astype(o_ref.dtype)

    def body(a_ref, b_ref, o_ref, acc_ref):
        k = pl.program_id(2)

        @pl.when(k == 0)
        def _():
            acc_ref[...] = jnp.zeros_like(acc_ref)

        acc_ref[...] += lax.dot_general(a_ref[...].astype(BF16), b_ref[...].astype(BF16), dims,
                                        preferred_element_type=F32)

        @pl.when(k == nk - 1)
        def _():
            o_ref[...] = acc_ref[...].astype(o_ref.dtype)

    return pl.pallas_call(
        whole if nk == 1 else body, grid=grid, in_specs=[a_spec, b_spec], out_specs=o_spec,
        out_shape=jax.ShapeDtypeStruct(o_shape, o_dtype),
        scratch_shapes=[] if nk == 1 else [pltpu.VMEM(acc_shape, F32)],
        compiler_params=_params(("parallel", "parallel", "arbitrary")), name=name,
    )(a, b)


def _mm_fwd(a, wg, col, name, tm=1024, tn_cap=1536, tk_cap=1024):
    m, k = a.shape
    ns, r, c = wg.shape
    tm = min(tm, m)
    if col:
        n, tn, tk = ns * c, _tile(c, tn_cap), _tile(k, tk_cap)
        per = c // tn
        b_spec = pl.BlockSpec((None, tk, tn), lambda j, i, kk: (j // per, kk, j % per))
    else:
        n, tn, tk = c, _tile(c, tn_cap), _tile(r, tk_cap)
        per = r // tk
        b_spec = pl.BlockSpec((None, tk, tn), lambda j, i, kk: (kk // per, kk % per, j))
    return _mm(a, wg, (n // tn, m // tm, k // tk), pl.BlockSpec((tm, tk), lambda j, i, kk: (i, kk)), b_spec,
               pl.BlockSpec((tm, tn), lambda j, i, kk: (i, j)), (m, n), F32, NN, (tm, tn), name)


def _mm_dgrad(dc, wg, col, name, tm=1024, to_cap=1536, tc_cap=2048):
    m, n = dc.shape
    ns, r, c = wg.shape
    tm = min(tm, m)
    if col:
        kout, to, tc = r, _tile(r, to_cap), _tile(c, tc_cap)
        per = c // tc
        b_spec = pl.BlockSpec((None, to, tc), lambda j, i, kk: (kk // per, j, kk % per))
    else:
        kout, to, tc = ns * r, _tile(r, to_cap), _tile(c, tc_cap)
        per = r // to
        b_spec = pl.BlockSpec((None, to, tc), lambda j, i, kk: (j // per, j % per, kk))
    return _mm(dc, wg, (kout // to, m // tm, n // tc), pl.BlockSpec((tm, tc), lambda j, i, kk: (i, kk)), b_spec,
               pl.BlockSpec((tm, to), lambda j, i, kk: (i, j)), (m, kout), F32, NT, (tm, to), name)


def _mm_wgrad(a, dc, col, name, ti_cap=1536, tn_cap=1536, tkm=2048):
    m, k = a.shape
    n = dc.shape[1]
    tkm = min(tkm, m)
    if col:
        r, c = k, n // N_CHIPS
        ti, tn = _tile(r, ti_cap), _tile(c, tn_cap)
        per = c // tn
        o_spec = pl.BlockSpec((None, ti, tn), lambda i, j, kk: (j // per, i, j % per))
    else:
        r, c = k // N_CHIPS, n
        ti, tn = _tile(r, ti_cap), _tile(c, tn_cap)
        per = r // ti
        o_spec = pl.BlockSpec((None, ti, tn), lambda i, j, kk: (i // per, i % per, j))
    return _mm(a, dc, (k // ti, n // tn, m // tkm), pl.BlockSpec((tkm, ti), lambda i, j, kk: (kk, i)),
               pl.BlockSpec((tkm, tn), lambda i, j, kk: (kk, j)), o_spec, (N_CHIPS, r, c), BF16, TN, (ti, tn), name)


def _log_keep(z):
    return -(jnp.maximum(z, 0.0) + jnp.log(1.0 + jnp.exp(-jnp.abs(z))))


def _split_dot(x, t):
    hi = x.astype(BF16)
    lo = (x - hi.astype(F32)).astype(BF16)
    return (lax.dot_general(hi, t, NN, preferred_element_type=F32)
            + lax.dot_general(lo, t, NN, preferred_element_type=F32))


def _sb_fwd(proj, n_sb, bq):
    s = proj.shape[0]
    scale = 1.0 / math.sqrt(HEAD_DIM)

    def body(q_ref, k_ref, v_ref, o_ref, lt_ref):
        i = pl.program_id(1)
        q = q_ref[...].astype(BF16)
        row = lax.broadcasted_iota(jnp.int32, (bq, bq), 0)
        col = lax.broadcasted_iota(jnp.int32, (bq, bq), 1)
        later_in_block = (row > col).astype(BF16)
        keep = col < row

        def blocks(js, carry, diagonal=False):
            c, acc = carry
            sl = [pl.ds(pl.multiple_of(j * bq, bq), bq) for j in js]
            masked = [diagonal and n == 0 for n in range(len(js))]
            kbs = [k_ref[s_, :].astype(BF16) for s_ in sl]
            zs = [lax.dot_general(q, kb, NT, preferred_element_type=F32) * scale for kb in kbs]
            lks = [jnp.where(keep, _log_keep(z), 0.0) if m else _log_keep(z) for z, m in zip(zs, masked)]
            within = [_split_dot(lk, later_in_block) for lk in lks]
            es = []
            for z, lk, w, m in zip(zs, lks, within, masked):
                e = z + lk + w + c
                es.append(jnp.where(keep, e, NEG) if m else e)
                c = c + jnp.sum(lk, axis=1, keepdims=True)
            for e, s_ in zip(es, sl):
                acc = acc + lax.dot_general(jnp.exp(e).astype(BF16), v_ref[s_, :].astype(BF16), NN,
                                            preferred_element_type=F32)
            return c, acc

        first = [functools.partial(blocks, [i - u for u in range(g + 1)], diagonal=True) for g in range(4)]
        carry = lax.switch(i % 4, first, (jnp.zeros((bq, 1), F32), jnp.zeros((bq, HEAD_DIM), F32)))
        top = i - i % 4 - 1
        carry = lax.fori_loop(0, i // 4, lambda t, cr: blocks([top - 4 * t - u for u in range(4)], cr), carry)
        o_ref[...] = carry[1]
        lt_ref[...] = jnp.broadcast_to(carry[0], (bq, HEAD_DIM))

    blk = pl.BlockSpec((bq, HEAD_DIM), lambda h, i: (i, h))
    shp = jax.ShapeDtypeStruct((s, n_sb * HEAD_DIM), F32)
    return pl.pallas_call(
        body, grid=(n_sb, s // bq),
        in_specs=[blk,
                  pl.BlockSpec((s, HEAD_DIM), lambda h, i: (0, n_sb + h)),
                  pl.BlockSpec((s, HEAD_DIM), lambda h, i: (0, 2 * n_sb + h))],
        out_specs=[blk, blk], out_shape=[shp, shp],
        compiler_params=_params(("parallel", "parallel")), name="sb_fwd",
    )(proj, proj, proj)


def _sb_bwd(proj, lt, do, n_sb, bq):
    s = proj.shape[0]
    nq = s // bq
    scale = 1.0 / math.sqrt(HEAD_DIM)

    def body(q_ref, k_ref, v_ref, lt_ref, do_ref, dq_ref, dk_ref, dv_ref, dk_acc, dv_acc):
        i = pl.program_id(1)

        @pl.when(i == 0)
        def _():
            dk_acc[...] = jnp.zeros_like(dk_acc)
            dv_acc[...] = jnp.zeros_like(dv_acc)

        q = q_ref[...].astype(BF16)
        do_b = do_ref[...].astype(BF16)
        ltot = jnp.max(lt_ref[...], axis=1, keepdims=True)
        row = lax.broadcasted_iota(jnp.int32, (bq, bq), 0)
        col = lax.broadcasted_iota(jnp.int32, (bq, bq), 1)
        upto_in_block = (row <= col).astype(BF16)
        before_in_block = (row < col).astype(BF16)
        keep = col < row

        def blocks(js, carry, diagonal=False):
            pk, pg, dq = carry
            sl = [pl.ds(pl.multiple_of(j * bq, bq), bq) for j in js]
            masked = [diagonal and n == len(js) - 1 for n in range(len(js))]
            kbs = [k_ref[s_, :].astype(BF16) for s_ in sl]
            zs = [lax.dot_general(q, kb, NT, preferred_element_type=F32) * scale for kb in kbs]
            das = [lax.dot_general(do_b, v_ref[s_, :].astype(BF16), NT, preferred_element_type=F32) for s_ in sl]
            lks = [jnp.where(keep, _log_keep(z), 0.0) if m else _log_keep(z) for z, m in zip(zs, masked)]
            upto = [_split_dot(lk, upto_in_block) for lk in lks]
            gs, abs_ = [], []
            for z, lk, u, da, m in zip(zs, lks, upto, das, masked):
                e = z + lk + ((ltot - pk) - u)
                a = jnp.exp(jnp.where(keep, e, NEG) if m else e)
                gs.append(a * da)
                abs_.append(a.astype(BF16))
                pk = pk + jnp.sum(lk, axis=1, keepdims=True)
            for a_b, s_ in zip(abs_, sl):
                dv_acc[s_, :] += lax.dot_general(a_b, do_b, TN, preferred_element_type=F32)
            before = [_split_dot(g, before_in_block) for g in gs]
            dzs = []
            for z, lk, g, bf, m in zip(zs, lks, gs, before, masked):
                keep_p = jnp.exp(lk)
                dz = g * keep_p - (pg + bf) * (1.0 - keep_p)
                dzs.append(((jnp.where(keep, dz, 0.0) if m else dz) * scale).astype(BF16))
                pg = pg + jnp.sum(g, axis=1, keepdims=True)
            for dz_b, kb in zip(dzs, kbs):
                dq = dq + lax.dot_general(dz_b, kb, NN, preferred_element_type=F32)
            for dz_b, s_ in zip(dzs, sl):
                dk_acc[s_, :] += lax.dot_general(dz_b, q, TN, preferred_element_type=F32)
            return pk, pg, dq

        zero = jnp.zeros((bq, 1), F32)
        carry = lax.fori_loop(0, i // 4, lambda t, cr: blocks([4 * t + u for u in range(4)], cr),
                              (zero, zero, jnp.zeros((bq, HEAD_DIM), F32)))
        last = [functools.partial(blocks, [i - g + u for u in range(g + 1)], diagonal=True) for g in range(4)]
        carry = lax.switch(i % 4, last, carry)
        dq_ref[...] = carry[2].astype(dq_ref.dtype)

        @pl.when(i == nq - 1)
        def _():
            dk_ref[...] = dk_acc[...].astype(dk_ref.dtype)
            dv_ref[...] = dv_acc[...].astype(dv_ref.dtype)

    blk = pl.BlockSpec((bq, HEAD_DIM), lambda h, i: (i, h))
    full = pl.BlockSpec((s, HEAD_DIM), lambda h, i: (0, h))
    shp = jax.ShapeDtypeStruct((s, n_sb * HEAD_DIM), BF16)
    return pl.pallas_call(
        body, grid=(n_sb, nq),
        in_specs=[blk,
                  pl.BlockSpec((s, HEAD_DIM), lambda h, i: (0, n_sb + h)),
                  pl.BlockSpec((s, HEAD_DIM), lambda h, i: (0, 2 * n_sb + h)),
                  blk, blk],
        out_specs=[blk, full, full], out_shape=[shp, shp, shp],
        scratch_shapes=[pltpu.VMEM((s, HEAD_DIM), F32), pltpu.VMEM((s, HEAD_DIM), F32)],
        compiler_params=_params(("parallel", "arbitrary")), name="sb_bwd",
    )(proj, proj, proj, lt, do)


def _t5_bucket(dist):
    max_exact = NUM_BUCKETS // 2
    d = jnp.maximum(dist, 1).astype(F32)
    large = max_exact + (jnp.log(d / max_exact) / math.log(MAX_DISTANCE / max_exact)
                         * (NUM_BUCKETS - max_exact)).astype(jnp.int32)
    large = jnp.minimum(large, NUM_BUCKETS - 1)
    return jnp.where(dist < max_exact, dist, large)


def _dil_tables(rel_bias):
    qi = jnp.arange(HEAD_DIM, dtype=jnp.int32)[:, None]
    ki = jnp.arange(2 * HEAD_DIM, dtype=jnp.int32)[None, :]
    rel = HEAD_DIM + qi - ki
    band = (rel >= 0) & (rel <= HEAD_DIM)
    biases, buckets = [], []
    for d in DILATIONS:
        bucket = _t5_bucket(jnp.maximum(rel, 0) * d)
        onehot = (bucket[:, :, None] == jnp.arange(NUM_BUCKETS, dtype=jnp.int32)).astype(F32)
        bias = jnp.einsum("qkb,bh->hqk", onehot, rel_bias.astype(F32), precision=lax.Precision.HIGHEST)
        biases.append(jnp.where(band[None], bias, NEG))
        buckets.append(jnp.where(band, bucket, -1).astype(F32))
    return jnp.stack(biases, axis=1), jnp.stack(buckets, axis=0)


def _sub_rows(ref, start, d):
    if d == 1:
        return ref[pl.ds(pl.multiple_of(start, HEAD_DIM), HEAD_DIM), :]
    return ref[pl.ds(start, HEAD_DIM, stride=d), :]


def _sub_idx(start, d):
    if d == 1:
        return pl.ds(pl.multiple_of(start, HEAD_DIM), HEAD_DIM)
    return pl.ds(start, HEAD_DIM, stride=d)


def _dil_logits(q_ref, k_ref, bm, n, cur, prv, d, scale):
    qb = _sub_rows(q_ref, cur, d).astype(BF16)
    kk = jnp.concatenate([_sub_rows(k_ref, prv, d), _sub_rows(k_ref, cur, d)], axis=0).astype(BF16)
    sc = lax.dot_general(qb, kk, NT, preferred_element_type=F32) * scale + bm
    colk = lax.broadcasted_iota(jnp.int32, sc.shape, 1)
    sc = jnp.where((colk >= HEAD_DIM) | (n > 0), sc, NEG)
    return qb, kk, sc


def _dil_fwd(proj, bm, n_sb, n_dl, after=()):
    s = proj.shape[0]
    scale = 1.0 / math.sqrt(HEAD_DIM)
    chunk = min(s, 512)
    na = len(after)

    def body(q_ref, k_ref, v_ref, bm_ref, *rest):
        o_ref, l_ref, ob0, ob1, ob2, lb0, lb1, lb2 = rest[na:]
        obs, lbs = (ob0, ob1, ob2), (lb0, lb1, lb2)
        for b, d in enumerate(DILATIONS):
            nb = s // (HEAD_DIM * d)

            def step(idx, _, b=b, d=d, nb=nb):
                r, n = idx // nb, idx % nb
                cur = n * (HEAD_DIM * d) + r
                prv = jnp.maximum(n - 1, 0) * (HEAD_DIM * d) + r
                _, _, sc = _dil_logits(q_ref, k_ref, bm_ref[b], n, cur, prv, d, scale)
                vv = jnp.concatenate([_sub_rows(v_ref, prv, d), _sub_rows(v_ref, cur, d)], axis=0).astype(BF16)
                mx = jnp.max(sc, axis=1, keepdims=True)
                pr = jnp.exp(sc - mx)
                den = jnp.sum(pr, axis=1, keepdims=True)
                o = lax.dot_general(pr.astype(BF16), vv, NN, preferred_element_type=F32) / den
                obs[b][_sub_idx(cur, d), :] = o
                lbs[b][_sub_idx(cur, d), :] = jnp.broadcast_to(mx + jnp.log(den), (HEAD_DIM, HEAD_DIM))
                return 0

            def group(g, _, step=step):
                for u in range(DIL_UNROLL):
                    step(g * DIL_UNROLL + u, 0)
                return 0

            lax.fori_loop(0, s // HEAD_DIM // DIL_UNROLL, group, 0)

        for ci in range(s // chunk):
            sl = pl.ds(ci * chunk, chunk)
            l0, l1, l2 = lb0[sl, :], lb1[sl, :], lb2[sl, :]
            mx = jnp.maximum(jnp.maximum(l0, l1), l2)
            w0, w1, w2 = jnp.exp(l0 - mx), jnp.exp(l1 - mx), jnp.exp(l2 - mx)
            tot = w0 + w1 + w2
            o_ref[sl, :] = (w0 * ob0[sl, :] + w1 * ob1[sl, :] + w2 * ob2[sl, :]) / tot
            l_ref[sl, :] = mx + jnp.log(tot)

    base = 3 * n_sb
    full = pl.BlockSpec((s, HEAD_DIM), lambda h: (0, h))
    shp = jax.ShapeDtypeStruct((s, n_dl * HEAD_DIM), F32)
    return pl.pallas_call(
        body, grid=(n_dl,),
        in_specs=[pl.BlockSpec((s, HEAD_DIM), lambda h: (0, base + h)),
                  pl.BlockSpec((s, HEAD_DIM), lambda h: (0, base + n_dl + h)),
                  pl.BlockSpec((s, HEAD_DIM), lambda h: (0, base + 2 * n_dl + h)),
                  pl.BlockSpec((None, 3, HEAD_DIM, 2 * HEAD_DIM), lambda h: (h, 0, 0, 0))]
        + [pl.BlockSpec(memory_space=pl.ANY)] * na,
        out_specs=[full, full], out_shape=[shp, shp],
        scratch_shapes=[pltpu.VMEM((s, HEAD_DIM), F32)] * 6,
        compiler_params=_params(("parallel",)), name="dil_fwd",
    )(proj, proj, proj, bm, *after)


def _dil_bwd(proj, do, o, lse, bm, n_sb, n_dl):
    s = proj.shape[0]
    scale = 1.0 / math.sqrt(HEAD_DIM)
    chunk = min(s, 512)

    def body(q_ref, k_ref, v_ref, do_ref, o_ref, l_ref, bm_ref, dq_ref, dk_ref, dv_ref, ds_ref, dq_s, dk_s, dv_s):
        dq_s[...] = jnp.zeros_like(dq_s)
        dk_s[...] = jnp.zeros_like(dk_s)
        dv_s[...] = jnp.zeros_like(dv_s)
        ds_ref[...] = jnp.zeros_like(ds_ref)
        for b, d in enumerate(DILATIONS):
            nb = s // (HEAD_DIM * d)

            def step(idx, _, b=b, d=d, nb=nb):
                r, n = idx // nb, idx % nb
                cur = n * (HEAD_DIM * d) + r
                prv = jnp.maximum(n - 1, 0) * (HEAD_DIM * d) + r
                qb, kk, sc = _dil_logits(q_ref, k_ref, bm_ref[b], n, cur, prv, d, scale)
                vv = jnp.concatenate([_sub_rows(v_ref, prv, d), _sub_rows(v_ref, cur, d)], axis=0).astype(BF16)
                do_f = _sub_rows(do_ref, cur, d)
                do_b = do_f.astype(BF16)
                delta = jnp.sum(do_f * _sub_rows(o_ref, cur, d), axis=1, keepdims=True)
                lr = _sub_rows(l_ref, cur, d)
                w = jnp.exp(sc - jnp.concatenate([lr, lr], axis=1))
                dp = lax.dot_general(do_b, vv, NT, preferred_element_type=F32)
                ds = w * (dp - delta)
                ds_ref[b] += ds
                ds_b = (ds * scale).astype(BF16)
                dv_blk = lax.dot_general(w.astype(BF16), do_b, TN, preferred_element_type=F32)
                dk_blk = lax.dot_general(ds_b, qb, TN, preferred_element_type=F32)
                ci, pi = _sub_idx(cur, d), _sub_idx(prv, d)
                dq_s[ci, :] += lax.dot_general(ds_b, kk, NN, preferred_element_type=F32)
                dk_s[ci, :] += dk_blk[HEAD_DIM:]
                dv_s[ci, :] += dv_blk[HEAD_DIM:]
                dk_s[pi, :] += dk_blk[:HEAD_DIM]
                dv_s[pi, :] += dv_blk[:HEAD_DIM]
                return 0

            def group(g, _, step=step):
                for u in range(DIL_UNROLL):
                    step(g * DIL_UNROLL + u, 0)
                return 0

            lax.fori_loop(0, s // HEAD_DIM // DIL_UNROLL, group, 0)

        for ci in range(s // chunk):
            sl = pl.ds(ci * chunk, chunk)
            dq_ref[sl, :] = dq_s[sl, :].astype(dq_ref.dtype)
            dk_ref[sl, :] = dk_s[sl, :].astype(dk_ref.dtype)
            dv_ref[sl, :] = dv_s[sl, :].astype(dv_ref.dtype)

    base = 3 * n_sb
    full = pl.BlockSpec((s, HEAD_DIM), lambda h: (0, h))
    tab = pl.BlockSpec((None, 3, HEAD_DIM, 2 * HEAD_DIM), lambda h: (h, 0, 0, 0))
    shp = jax.ShapeDtypeStruct((s, n_dl * HEAD_DIM), BF16)
    return pl.pallas_call(
        body, grid=(n_dl,),
        in_specs=[pl.BlockSpec((s, HEAD_DIM), lambda h: (0, base + h)),
                  pl.BlockSpec((s, HEAD_DIM), lambda h: (0, base + n_dl + h)),
                  pl.BlockSpec((s, HEAD_DIM), lambda h: (0, base + 2 * n_dl + h)),
                  full, full, full, tab],
        out_specs=[full, full, full, tab],
        out_shape=[shp, shp, shp, jax.ShapeDtypeStruct((n_dl, 3, HEAD_DIM, 2 * HEAD_DIM), F32)],
        scratch_shapes=[pltpu.VMEM((s, HEAD_DIM), F32)] * 3,
        compiler_params=_params(("parallel",)), name="dil_bwd",
    )(proj, proj, proj, do, o, lse, bm)


def _rel_bias_grad(ds_all, buckets):
    depth, n_dl = ds_all.shape[:2]
    rows = -(-n_dl // 8) * 8

    def body(ds_ref, bk_ref, o_ref):
        lane = lax.broadcasted_iota(jnp.int32, (1, LANE), 1)

        def one_bucket(bkt, acc):
            fb = bkt.astype(F32)
            out = []
            for h in range(n_dl):
                val = jnp.zeros((1, 1), F32)
                for b in range(3):
                    tot = ds_ref[0, h, b]
                    for l in range(1, depth):
                        tot = tot + ds_ref[l, h, b]
                    val = val + jnp.sum(jnp.where(bk_ref[b] == fb, tot, 0.0), keepdims=True)
                out.append(jnp.where(lane == bkt, val, 0.0))
            out += [jnp.zeros((1, LANE), F32)] * (rows - n_dl)
            return acc + jnp.concatenate(out, axis=0)

        o_ref[...] = lax.fori_loop(0, NUM_BUCKETS, one_bucket, jnp.zeros((rows, LANE), F32))

    return pl.pallas_call(
        body, out_shape=jax.ShapeDtypeStruct((rows, LANE), F32),
        in_specs=[pl.BlockSpec(memory_space=pltpu.VMEM)] * 2, out_specs=pl.BlockSpec(memory_space=pltpu.VMEM),
        compiler_params=_params(), name="rel_bias_grad",
    )(ds_all, buckets)


def _place():
    return lax.axis_index("x"), lax.axis_index("y"), lax.axis_index("c")


def _flip(v, bit):
    return 1 - v if bit else v


HBM_SPEC = pl.BlockSpec(memory_space=pl.ANY)


HBM_ONLY = pl.BlockSpec(memory_space=pltpu.HBM)
SEM_SPEC = pl.BlockSpec(memory_space=pltpu.SEMAPHORE)
DATAFLOW = pltpu.SideEffectType.DATAFLOW_SIDE_EFFECTING


def _in_hbm(a):
    return pltpu.with_memory_space_constraint(a, pltpu.HBM)


def _split_start(arrays, token, copies_of, n_sem, name):
    na = len(arrays)

    def body(*refs):
        for cp in copies_of(refs[:na], refs[na + 1], refs[na + 2]):
            cp.start()
        refs[-1][...] = jnp.zeros_like(refs[-1])

    sems = pltpu.SemaphoreType.DMA((n_sem,))
    res = pl.pallas_call(
        body, name=name,
        out_shape=(sems, sems, *[pltpu.HBM(a.shape, a.dtype) for a in arrays], jax.ShapeDtypeStruct((8, LANE), F32)),
        in_specs=[HBM_ONLY] * na + [HBM_SPEC],
        out_specs=(SEM_SPEC, SEM_SPEC, *[HBM_ONLY] * na, pl.BlockSpec(memory_space=pltpu.VMEM)),
        input_output_aliases={i: 2 + i for i in range(na)},
        compiler_params=pltpu.CompilerParams(has_side_effects=DATAFLOW),
    )(*[_in_hbm(a) for a in arrays], token)
    return res[0], res[1], res[2:2 + na], res[-1]


def _split_wait(started, after, copies_of, name):
    send, recv, arrays, _ = started
    na = len(arrays)

    def body(*refs):
        for cp in copies_of(refs[:na], refs[na], refs[na + 1]):
            cp.wait_send()
            cp.wait_recv()

    return pl.pallas_call(
        body, name=name, out_shape=[pltpu.HBM(a.shape, a.dtype) for a in arrays],
        in_specs=[HBM_ONLY] * na + [SEM_SPEC, SEM_SPEC, HBM_SPEC], out_specs=[HBM_ONLY] * na,
        input_output_aliases={i: i for i in range(na)},
        compiler_params=pltpu.CompilerParams(has_side_effects=DATAFLOW),
    )(*arrays, send, recv, after)


def _gather_ici_copies(buf, send, recv):
    x, y, c = _place()
    out = []
    for w in range(len(buf)):
        h = buf[w].shape[1] // 2
        mine = buf[w].at[2 * x + y, pl.ds(c * h, h)]
        for k in (1, 2, 3):
            out.append(pltpu.make_async_remote_copy(
                src_ref=mine, dst_ref=mine, send_sem=send.at[4 * w + k], recv_sem=recv.at[4 * w + k],
                device_id=(_flip(x, k >> 1), _flip(y, k & 1), c), device_id_type=MESH))
    return out


def _gather_d2d_copies(buf, send, recv):
    x, y, c = _place()
    out = []
    for w in range(len(buf)):
        h = buf[w].shape[1] // 2
        for k in (1, 2, 3):
            got = buf[w].at[2 * _flip(x, k >> 1) + _flip(y, k & 1), pl.ds(c * h, h)]
            out.append(pltpu.make_async_remote_copy(
                src_ref=got, dst_ref=got, send_sem=send.at[4 * w + k], recv_sem=recv.at[4 * w + k],
                device_id=(x, y, 1 - c), device_id_type=MESH))
    return out


def _scatter_copies(refs, send, recv):
    nw = len(refs) // 2
    src, buf = refs[:nw], refs[nw:]
    x, y, c = _place()
    out = []
    for w in range(nw):
        h = src[w].shape[1] // 2
        for k in range(1, N_DEV):
            px, py, pc = _flip(x, k >> 2), _flip(y, (k >> 1) & 1), _flip(c, k & 1)
            out.append(pltpu.make_async_remote_copy(
                src_ref=src[w].at[2 * px + py, pl.ds(pc * h, h)], dst_ref=buf[w].at[k - 1],
                send_sem=send.at[N_DEV * w + k], recv_sem=recv.at[N_DEV * w + k], device_id=(px, py, pc),
                device_id_type=MESH))
    return out


def _join_halves(grads):
    nw = len(grads)

    def body(*refs):
        ins, outs = refs[:nw], refs[nw:2 * nw]
        send, recv = refs[2 * nw:]
        x, y, c = _place()
        remote = []
        for w in range(nw):
            h = ins[w].shape[0] // 2
            hc = h // JOIN_CHUNKS
            for j in range(JOIN_CHUNKS):
                rows = pl.ds(c * h + j * hc, hc)
                cp = pltpu.make_async_remote_copy(
                    src_ref=ins[w].at[rows], dst_ref=outs[w].at[rows],
                    send_sem=send.at[w, j], recv_sem=recv.at[w, j], device_id=(x, y, 1 - c), device_id_type=MESH)
                cp.start()
                remote.append(cp)
        for w in range(nw):
            h = ins[w].shape[0] // 2
            hc = h // JOIN_CHUNKS
            for j in range(JOIN_CHUNKS):
                theirs = outs[w].at[pl.ds((1 - c) * h + j * hc, hc)]
                pltpu.make_async_remote_copy(
                    src_ref=theirs, dst_ref=theirs, send_sem=send.at[w, j], recv_sem=recv.at[w, j],
                    device_id=(x, y, c), device_id_type=MESH).wait_recv()
        for cp in remote:
            cp.wait_send()

    sems = [pltpu.SemaphoreType.DMA((nw, JOIN_CHUNKS))] * 2
    return pl.pallas_call(
        body, out_shape=[jax.ShapeDtypeStruct(a.shape, a.dtype) for a in grads],
        in_specs=[HBM_SPEC] * nw, out_specs=[HBM_SPEC] * nw, scratch_shapes=sems,
        input_output_aliases={i: i for i in range(nw)}, name="join_halves",
    )(*grads)


def _allreduce_small(v):
    rows, c = v.shape

    def body(v_ref, o_ref, buf, local_sem, send, recv):
        x, y, cc = _place()
        me = 4 * x + 2 * y + cc
        own = pltpu.make_async_copy(v_ref, buf.at[me], local_sem)
        own.start()
        sends = []
        for k in range(1, N_DEV):
            px, py, pc = _flip(x, k >> 2), _flip(y, (k >> 1) & 1), _flip(cc, k & 1)
            cp = pltpu.make_async_remote_copy(
                src_ref=v_ref, dst_ref=buf.at[me], send_sem=send.at[k], recv_sem=recv.at[k],
                device_id=(px, py, pc), device_id_type=MESH)
            cp.start()
            sends.append(cp)
        for k in range(1, N_DEV):
            px, py, pc = _flip(x, k >> 2), _flip(y, (k >> 1) & 1), _flip(cc, k & 1)
            slot = buf.at[4 * px + 2 * py + pc]
            pltpu.make_async_remote_copy(
                src_ref=slot, dst_ref=slot, send_sem=send.at[k], recv_sem=recv.at[k],
                device_id=(x, y, cc), device_id_type=MESH).wait_recv()
        for cp in sends:
            cp.wait_send()
        own.wait()
        acc = buf[0]
        for i in range(1, N_DEV):
            acc = acc + buf[i]
        o_ref[...] = acc

    return pl.pallas_call(
        body, out_shape=jax.ShapeDtypeStruct((rows, c), F32),
        in_specs=[pl.BlockSpec(memory_space=pltpu.VMEM)], out_specs=pl.BlockSpec(memory_space=pltpu.VMEM),
        scratch_shapes=[pltpu.VMEM((N_DEV, rows, c), F32), pltpu.SemaphoreType.DMA,
                        pltpu.SemaphoreType.DMA((N_DEV,)), pltpu.SemaphoreType.DMA((N_DEV,))],
        compiler_params=_params(),
        name="allreduce_small",
    )(v)


def kernel(x, p, ln_mix_pre, w_in, ln_head, w_out, ln_mix_post, rel_bias, ln_ffn_pre, w_gate_up, w_down, ln_ffn_post, ln_pli, w_pli_gate, w_pli_proj, loss_target, m_ln_mix_pre, m_w_in, m_ln_head, m_w_out, m_ln_mix_post, m_rel_bias, m_ln_ffn_pre, m_w_gate_up, m_w_down, m_ln_ffn_post, m_ln_pli, m_w_pli_gate, m_w_pli_proj, v_ln_mix_pre, v_w_in, v_ln_head, v_w_out, v_ln_mix_post, v_rel_bias, v_ln_ffn_pre, v_w_gate_up, v_w_down, v_ln_ffn_post, v_ln_pli, v_w_pli_gate, v_w_pli_proj):
    weights = dict(ln_mix_pre=ln_mix_pre, w_in=w_in, ln_head=ln_head, w_out=w_out, ln_mix_post=ln_mix_post,
                   rel_bias=rel_bias, ln_ffn_pre=ln_ffn_pre, w_gate_up=w_gate_up, w_down=w_down,
                   ln_ffn_post=ln_ffn_post, ln_pli=ln_pli, w_pli_gate=w_pli_gate, w_pli_proj=w_pli_proj)
    mom1 = dict(ln_mix_pre=m_ln_mix_pre, w_in=m_w_in, ln_head=m_ln_head, w_out=m_w_out, ln_mix_post=m_ln_mix_post,
                rel_bias=m_rel_bias, ln_ffn_pre=m_ln_ffn_pre, w_gate_up=m_w_gate_up, w_down=m_w_down,
                ln_ffn_post=m_ln_ffn_post, ln_pli=m_ln_pli, w_pli_gate=m_w_pli_gate, w_pli_proj=m_w_pli_proj)
    mom2 = dict(ln_mix_pre=v_ln_mix_pre, w_in=v_w_in, ln_head=v_ln_head, w_out=v_w_out, ln_mix_post=v_ln_mix_post,
                rel_bias=v_rel_bias, ln_ffn_pre=v_ln_ffn_pre, w_gate_up=v_w_gate_up, w_down=v_w_down,
                ln_ffn_post=v_ln_ffn_post, ln_pli=v_ln_pli, w_pli_gate=v_w_pli_gate, w_pli_proj=v_w_pli_proj)

    _, seq, d_model = x.shape
    depth = w_in.shape[0]
    n_heads = d_model // HEAD_DIM
    n_sb = n_heads // 2
    n_dl = n_heads - n_sb
    assert seq % (HEAD_DIM * DILATIONS[-1]) == 0 and d_model % (2 * HEAD_DIM) == 0
    bq = 256
    tr = 128
    tr_ff = 64

    xs = x[0]
    target = loss_target[0]
    gain = {n: [weights[n][l][None, :] for l in range(depth)] for n in SMALL}
    bias_mask, buckets = _dil_tables(rel_bias)

    zero_token = jnp.zeros((8, LANE), F32)
    token = zero_token
    gathers = []
    for l in range(depth):
        parts = []
        for names in ((FIRST_USED, REST) if l == 0 else (BIG,)):
            slots = [_cast_layer(weights[n], l, n) for n in names]
            tag = f"{l}" if names is BIG else f"{l}_{names[0]}"
            parts.append((names, tag, _split_start(slots, token, _gather_ici_copies, 4 * len(names),
                                                   f"gather_start_{tag}")))
            token = parts[-1][2][3]
        gathers.append(parts)

    def gather_pass_on(part, after):
        names, tag, started = part
        arrays = _split_wait(started, after, _gather_ici_copies, f"gather_wait_{tag}")
        started = _split_start(arrays, zero_token, _gather_d2d_copies, 4 * len(names), f"gather_pass_{tag}")
        return names, tag, started, started[3]

    def gather_done(passed, after):
        names, tag, started, _ = passed
        return dict(zip(names, _split_wait(started, after, _gather_d2d_copies, f"gather_done_{tag}")))

    saved = []
    h1 = _norm_in(xs, gain["ln_mix_pre"][0], tr)
    xin = xs
    passed = gather_pass_on(gathers[0][0], token)
    wg = gather_done(passed, passed[3])
    for l in range(depth):
        proj = _mm_fwd(h1, wg["w_in"], True, "mm_in", tn_cap=768, tk_cap=2048)
        o_sb, lt_sb = _sb_fwd(proj, n_sb, bq)
        if l == 0:
            passed = gather_pass_on(gathers[0][1], o_sb)
            o_dl, lse_dl = _dil_fwd(proj, bias_mask, n_sb, n_dl, (passed[3],))
            wg.update(gather_done(passed, o_dl))
        else:
            o_dl, lse_dl = _dil_fwd(proj, bias_mask, n_sb, n_dl)
        on = _headnorm(o_sb, o_dl, gain["ln_head"][l], tr)
        y = _mm_fwd(on, wg["w_out"], False, "mm_out", tn_cap=1024)
        x2, h2 = _res_norm(xin, y, gain["ln_mix_post"][l], gain["ln_ffn_pre"][l], "post_attn", tr)
        gu = _mm_fwd(h2, wg["w_gate_up"], True, "mm_gate_up", tk_cap=2048)
        act = _swiglu(gu, tr_ff)
        f = _mm_fwd(act, wg["w_down"], False, "mm_down", tn_cap=1024, tk_cap=1536)
        passed = gather_pass_on(gathers[l + 1][0], f) if l + 1 < depth else None
        x3, h3 = _res_norm(x2, f, gain["ln_ffn_post"][l], gain["ln_pli"][l], "post_ffn", tr,
                           () if passed is None else (passed[3],))
        gl = _mm_fwd(h3, wg["w_pli_gate"], False, "mm_pli_gate", tn_cap=1024)
        pl_in = p[l, 0]
        pp = _mm_fwd(pl_in, wg["w_pli_proj"], True, "mm_pli_proj")
        saved.append(dict(wg=wg, x=xin, h1=h1, proj=proj, o_sb=o_sb, lt_sb=lt_sb, o_dl=o_dl, lse_dl=lse_dl, on=on, y=y, x2=x2,
                          h2=h2, gu=gu, act=act, f=f, x3=x3, h3=h3, gl=gl, pp=pp, p=pl_in))
        if l + 1 < depth:
            xin, h1 = _pli_out(x3, gl, pp, gain["ln_mix_pre"][l + 1], tr)
            wg = gather_done(passed, pp)
        else:
            dx, loss_part = _loss_head(x3, gl, pp, target, tr)

    grad_big = {n: [None] * depth for n in BIG}
    grad_gain = {n: [None] * depth for n in SMALL}
    ds_layers = [None] * depth

    def start_scatter(layer, names, dw):
        dws = [dw[n] for n in names]
        landing = [lax.empty((N_DEV - 1, a.shape[1] // 2, a.shape[2]), a.dtype) for a in dws]
        tag = f"{layer}" if names is BIG else f"{layer}_{names[0]}"
        return layer, names, tag, _split_start(dws + landing, zero_token, _scatter_copies, N_DEV * len(names),
                                               f"scatter_start_{tag}")

    def reduce_layer(layer, names, tag, started, after):
        arrays = _split_wait(started, after, _scatter_copies, f"scatter_wait_{tag}")
        nw = len(names)
        halves = [_reduce_piece(arrays[w], arrays[nw + w], names[w]) for w in range(nw)]
        joined = _join_halves(halves)
        for n, g in zip(names, joined):
            grad_big[n][layer] = g
        return joined[0]

    pending = None
    for l in reversed(range(depth)):
        sv = saved[l]
        wg = sv["wg"]
        dpp, dgl = _pli_bwd(dx, sv["gl"], sv["pp"], tr, () if pending is None else (pending[3][3],))
        dw = {}
        dw["w_pli_proj"] = _mm_wgrad(sv["p"], dpp, True, "wg_pli_proj")
        dw["w_pli_gate"] = _mm_wgrad(sv["h3"], dgl, False, "wg_pli_gate")
        dh3 = _mm_dgrad(dgl, wg["w_pli_gate"], False, "dg_pli_gate")
        dx3, df, grad_gain["ln_pli"][l], grad_gain["ln_ffn_post"][l] = _res_norm_bwd(
            dx, dh3, sv["x3"], sv["f"], gain["ln_pli"][l], gain["ln_ffn_post"][l], "post_ffn_bwd", tr)
        dw["w_down"] = _mm_wgrad(sv["act"], df, False, "wg_down")
        dact = _mm_dgrad(df, wg["w_down"], False, "dg_down")
        dgu = _swiglu_bwd(dact, sv["gu"], tr_ff)
        dw["w_gate_up"] = _mm_wgrad(sv["h2"], dgu, True, "wg_gate_up")
        dh2 = _mm_dgrad(dgu, wg["w_gate_up"], True, "dg_gate_up", tc_cap=2816)
        dx2, dy, grad_gain["ln_ffn_pre"][l], grad_gain["ln_mix_post"][l] = _res_norm_bwd(
            dx3, dh2, sv["x2"], sv["y"], gain["ln_ffn_pre"][l], gain["ln_mix_post"][l], "post_attn_bwd", tr)
        dw["w_out"] = _mm_wgrad(sv["on"], dy, False, "wg_out")
        don = _mm_dgrad(dy, wg["w_out"], False, "dg_out")
        early = start_scatter(l, REST, dw) if l == 0 else None
        do_sb, do_dl, grad_gain["ln_head"][l] = _headnorm_bwd(
            don, sv["o_sb"], sv["o_dl"], gain["ln_head"][l], tr, () if early is None else (early[3][3],))
        dq_s, dk_s, dv_s = _sb_bwd(sv["proj"], sv["lt_sb"], do_sb, n_sb, bq)
        dq_d, dk_d, dv_d, ds_layers[l] = _dil_bwd(sv["proj"], do_dl, sv["o_dl"], sv["lse_dl"], bias_mask, n_sb, n_dl)
        dproj = jnp.concatenate([dq_s, dk_s, dv_s, dq_d, dk_d, dv_d], axis=1)
        dw["w_in"] = _mm_wgrad(sv["h1"], dproj, True, "wg_in")
        dh1 = _mm_dgrad(dproj, wg["w_in"], True, "dg_in", tc_cap=1536)
        dx, grad_gain["ln_mix_pre"][l] = _norm_in_bwd(dx2, dh1, sv["x"], gain["ln_mix_pre"][l], tr)
        late = start_scatter(l, BIG if l > 0 else FIRST_USED, dw)
        if pending is not None:
            reduce_layer(*pending, dx if l > 0 else late[3][3])
        pending = late
    done = reduce_layer(*early, pending[3][3])
    reduce_layer(*pending, done)

    db = _rel_bias_grad(jnp.stack(ds_layers, axis=0), buckets)
    rb_flat = db[:n_dl, :NUM_BUCKETS].T.reshape(1, NUM_BUCKETS * n_dl)
    def widen(v):
        return jnp.pad(v, ((0, 0), (0, d_model - v.shape[1])))
    small_rows = [grad_gain[n][l] for n in SMALL for l in range(depth)] + [widen(rb_flat), widen(loss_part)]
    n_rows = len(small_rows)
    small = jnp.concatenate(small_rows + [jnp.zeros((-n_rows % 8, d_model), F32)], axis=0)
    total = _allreduce_small(small)
    grads = {}
    for i, n in enumerate(SMALL):
        grads[n] = total[i * depth:(i + 1) * depth]
    grads["rel_bias"] = total[len(SMALL) * depth, :NUM_BUCKETS * n_dl].reshape(NUM_BUCKETS, n_dl)
    loss = (0.5 / d_model) * jnp.sum(total[len(SMALL) * depth + 1, :LANE])
    for n in BIG:
        grads[n] = jnp.stack(grad_big[n], axis=0)

    delta, new_m, new_v = {}, {}, {}
    for n in WEIGHTS:
        delta[n], new_m[n], new_v[n] = _adamw(weights[n], grads[n], mom1[n], mom2[n], n)
    return (loss, dx[None], *[grads[n] for n in WEIGHTS], *[delta[n] for n in WEIGHTS],
            *[new_m[n] for n in WEIGHTS], *[new_v[n] for n in WEIGHTS])
```

```python
import functools
import math

import jax
import jax.numpy as jnp
from jax import lax
from jax.experimental import pallas as pl
from jax.experimental.pallas import tpu as pltpu

F32 = jnp.float32
BF16 = jnp.bfloat16

HEAD_DIM = 128
RMS_EPS = 1e-6
DILATIONS = (1, 4, 16)
NUM_BUCKETS = 32
MAX_DISTANCE = 2048
NEG = -1e30
N_CHIPS = 4
N_DEV = 8
JOIN_CHUNKS = 8
DIL_UNROLL = 8

ADAM_LR = 0.001
ADAM_B1 = 0.9
ADAM_B2 = 0.999
ADAM_EPS = 1e-08
ADAM_WD = 0.01
ADAM_STEP = 10

V7X_VMEM_LIMIT = 48 * 1024 * 1024
LANE = 128

NN = (((1,), (0,)), ((), ()))
NT = (((1,), (1,)), ((), ()))
TN = (((0,), (0,)), ((), ()))
MESH = pl.DeviceIdType.MESH

BIG = ("w_in", "w_out", "w_gate_up", "w_down", "w_pli_gate", "w_pli_proj")
FIRST_USED = BIG[:1]
REST = BIG[1:]
COL_SHARDED = {"w_in": True, "w_out": False, "w_gate_up": True, "w_down": False,
               "w_pli_gate": False, "w_pli_proj": True}
SMALL = ("ln_mix_pre", "ln_head", "ln_mix_post", "ln_ffn_pre", "ln_ffn_post", "ln_pli")
WEIGHTS = ("ln_mix_pre", "w_in", "ln_head", "w_out", "ln_mix_post", "rel_bias", "ln_ffn_pre",
           "w_gate_up", "w_down", "ln_ffn_post", "ln_pli", "w_pli_gate", "w_pli_proj")


def _tile(n, cap):
    t = min(n, cap) // LANE * LANE
    while t >= LANE:
        if n % t == 0:
            return t
        t -= LANE
    return n


def _params(sem=None):
    return pltpu.CompilerParams(dimension_semantics=sem, vmem_limit_bytes=V7X_VMEM_LIMIT)


def _rowwise(fn, rows, vecs, outs, sums, name, tr, after=()):
    s = rows[0].shape[0]
    nr, nv, no, ns, na = len(rows), len(vecs), len(outs), len(sums), len(after)

    def body(*refs):
        ins = [r[...] for r in refs[:nr + nv]]
        res = fn(*ins)
        out_refs = refs[nr + nv + na:nr + nv + na + no]
        sum_refs = refs[nr + nv + na + no:]
        for o_ref, val in zip(out_refs, res[:no]):
            o_ref[...] = val.astype(o_ref.dtype)
        if ns:
            @pl.when(pl.program_id(0) == 0)
            def _():
                for s_ref in sum_refs:
                    s_ref[...] = jnp.zeros_like(s_ref)
            for s_ref, val in zip(sum_refs, res[no:]):
                s_ref[...] += jnp.sum(val, axis=0, keepdims=True)

    in_specs = [pl.BlockSpec((tr, r.shape[1]), lambda i: (i, 0)) for r in rows]
    in_specs += [pl.BlockSpec(v.shape, lambda i: (0, 0)) for v in vecs]
    in_specs += [pl.BlockSpec(memory_space=pl.ANY)] * na
    out_specs = [pl.BlockSpec((tr, c), lambda i: (i, 0)) for c, _ in outs]
    out_specs += [pl.BlockSpec((1, c), lambda i: (0, 0)) for c in sums]
    out_shape = [jax.ShapeDtypeStruct((s, c), dt) for c, dt in outs]
    out_shape += [jax.ShapeDtypeStruct((1, c), F32) for c in sums]
    return pl.pallas_call(
        body, grid=(s // tr,), in_specs=in_specs, out_specs=out_specs, out_shape=out_shape,
        compiler_params=_params(("arbitrary",) if ns else ("parallel",)), name=name,
    )(*rows, *vecs, *after)


def _rms_r(x):
    return lax.rsqrt(jnp.mean(x * x, axis=-1, keepdims=True) + RMS_EPS)


def _rms_bwd(x, g, dy):
    r = _rms_r(x)
    u = dy * g
    dx = r * (u - x * (r * r) * jnp.mean(u * x, axis=-1, keepdims=True))
    return dx, dy * x * r


def _sigmoid(z):
    return 1.0 / (1.0 + jnp.exp(-z))


def _norm_in(x, g, tr):
    d = x.shape[1]
    return _rowwise(lambda x, g: (x * _rms_r(x) * g,), [x], [g], [(d, BF16)], [], "norm_in", tr)[0]


def _norm_in_bwd(dx_res, dh, x, g, tr):
    d = x.shape[1]

    def fn(dx_res, dh, x, g):
        dx, dg = _rms_bwd(x, g, dh)
        return dx_res + dx, dg
    return _rowwise(fn, [dx_res, dh, x], [g], [(d, F32)], [d], "norm_in_bwd", tr)


def _headnorm(o_sb, o_dl, g, tr):
    d = g.shape[1]

    def fn(o_sb, o_dl, g):
        o = jnp.concatenate([o_sb, o_dl], axis=1)
        parts = []
        for h in range(d // HEAD_DIM):
            sl = slice(h * HEAD_DIM, (h + 1) * HEAD_DIM)
            oh = o[:, sl]
            parts.append(oh * _rms_r(oh) * g[:, sl])
        return (jnp.concatenate(parts, axis=1),)
    return _rowwise(fn, [o_sb, o_dl], [g], [(d, BF16)], [], "headnorm", tr)[0]


def _headnorm_bwd(don, o_sb, o_dl, g, tr, after=()):
    d = g.shape[1]
    n_sb = o_sb.shape[1]

    def fn(don, o_sb, o_dl, g):
        o = jnp.concatenate([o_sb, o_dl], axis=1)
        dos, dgs = [], []
        for h in range(d // HEAD_DIM):
            sl = slice(h * HEAD_DIM, (h + 1) * HEAD_DIM)
            dx, dg = _rms_bwd(o[:, sl], g[:, sl], don[:, sl])
            dos.append(dx)
            dgs.append(dg)
        do = jnp.concatenate(dos, axis=1)
        return do[:, :n_sb], do[:, n_sb:], jnp.concatenate(dgs, axis=1)
    return _rowwise(fn, [don, o_sb, o_dl], [g], [(n_sb, F32), (d - n_sb, F32)], [d], "headnorm_bwd", tr, after)


def _res_norm(x, y, g_post, g_pre, name, tr, after=()):
    d = x.shape[1]

    def fn(x, y, g_post, g_pre):
        x2 = x + y * _rms_r(y) * g_post
        return x2, x2 * _rms_r(x2) * g_pre
    return _rowwise(fn, [x, y], [g_post, g_pre], [(d, F32), (d, BF16)], [], name, tr, after)


def _res_norm_bwd(dx_res, dh, x2, y, g_pre, g_post, name, tr):
    d = x2.shape[1]

    def fn(dx_res, dh, x2, y, g_pre, g_post):
        dxa, dg_pre = _rms_bwd(x2, g_pre, dh)
        dx2 = dx_res + dxa
        dy, dg_post = _rms_bwd(y, g_post, dx2)
        return dx2, dy, dg_pre, dg_post
    return _rowwise(fn, [dx_res, dh, x2, y], [g_pre, g_post], [(d, F32), (d, BF16)], [d, d], name, tr)


def _swiglu(gu, tr):
    ff = gu.shape[1] // 2

    def fn(gu):
        g, u = gu[:, :ff].astype(F32), gu[:, ff:].astype(F32)
        return (g * _sigmoid(g) * u,)
    return _rowwise(fn, [gu], [], [(ff, BF16)], [], "swiglu", tr)[0]


def _swiglu_bwd(dact, gu, tr):
    ff = gu.shape[1] // 2

    def fn(dact, gu):
        g, u = gu[:, :ff].astype(F32), gu[:, ff:].astype(F32)
        sg = _sigmoid(g)
        dg = dact * u * sg * (1.0 + g * (1.0 - sg))
        du = dact * g * sg
        return (jnp.concatenate([dg, du], axis=1),)
    return _rowwise(fn, [dact, gu], [], [(2 * ff, BF16)], [], "swiglu_bwd", tr)[0]


def _pli_out(x3, gl, pp, g_next, tr):
    d = x3.shape[1]

    def fn(x3, gl, pp, g):
        x4 = x3 + _sigmoid(gl) * pp
        return x4, x4 * _rms_r(x4) * g
    return _rowwise(fn, [x3, gl, pp], [g_next], [(d, F32), (d, BF16)], [], "pli_out", tr)


def _loss_head(x3, gl, pp, target, tr):
    d = x3.shape[1]

    def fn(x3, gl, pp, t):
        err = x3 + _sigmoid(gl) * pp - t
        sq = err * err
        part = sq[:, :LANE]
        for k in range(1, d // LANE):
            part = part + sq[:, k * LANE:(k + 1) * LANE]
        return err * (1.0 / d), part
    return _rowwise(fn, [x3, gl, pp, target], [], [(d, F32)], [LANE], "loss_head", tr)


def _pli_bwd(dx, gl, pp, tr, after=()):
    d = dx.shape[1]

    def fn(dx, gl, pp):
        gate = _sigmoid(gl)
        return dx * gate, dx * pp * gate * (1.0 - gate)
    return _rowwise(fn, [dx, gl, pp], [], [(d, BF16), (d, BF16)], [], "pli_bwd", tr, after)


def _my_chip():
    return 2 * lax.axis_index("x") + lax.axis_index("y")


def _cast_layer(w, layer, name):
    _, r, c = w.shape
    tr = r
    while tr * c * 4 > (4 << 20) and tr % 32 == 0:
        tr //= 2

    def body(w_ref, o_ref):
        o_ref[...] = w_ref[...].astype(o_ref.dtype)

    return pl.pallas_call(
        body, grid=(r // tr,),
        in_specs=[pl.BlockSpec((None, tr, c), lambda i: (layer, i, 0))],
        out_specs=pl.BlockSpec((None, tr, c), lambda i: (_my_chip(), i, 0)),
        out_shape=jax.ShapeDtypeStruct((N_CHIPS, r, c), BF16),
        compiler_params=_params(("parallel",)), name="cast_" + name,
    )(w)


def _reduce_piece(dw, recv, name):
    _, r, c = dw.shape
    h = r // 2
    tr = h
    while tr * c * 2 * N_DEV > (8 << 20) and tr % 32 == 0:
        tr //= 2
    per = h // tr

    def body(d_ref, r_ref, o_ref):
        acc = d_ref[...].astype(F32)
        for i in range(N_DEV - 1):
            acc = acc + r_ref[i].astype(F32)
        o_ref[...] = acc

    return pl.pallas_call(
        body, grid=(per,),
        in_specs=[pl.BlockSpec((None, tr, c), lambda i: (_my_chip(), lax.axis_index("c") * per + i, 0)),
                  pl.BlockSpec((N_DEV - 1, tr, c), lambda i: (0, i, 0))],
        out_specs=pl.BlockSpec((tr, c), lambda i: (lax.axis_index("c") * per + i, 0)),
        out_shape=jax.ShapeDtypeStruct((r, c), F32),
        compiler_params=_params(("parallel",)), name="reduce_" + name,
    )(dw, recv)


def _adamw(w, g, m, v, name):
    shape = w.shape
    if w.ndim == 3:
        w, g, m, v = (a.reshape(shape[0] * shape[1], shape[2]) for a in (w, g, m, v))
    r, c = w.shape
    tr = r
    while tr * c * 4 > (1 << 20) and tr % 16 == 0:
        tr //= 2

    def body(w_ref, g_ref, m_ref, v_ref, d_ref, m2_ref, v2_ref):
        g = g_ref[...]
        m2 = ADAM_B1 * m_ref[...] + (1.0 - ADAM_B1) * g
        v2 = ADAM_B2 * v_ref[...] + (1.0 - ADAM_B2) * (g * g)
        m_hat = m2 / (1.0 - ADAM_B1 ** ADAM_STEP)
        v_hat = v2 / (1.0 - ADAM_B2 ** ADAM_STEP)
        d_ref[...] = -ADAM_LR * (m_hat / (jnp.sqrt(v_hat) + ADAM_EPS) + ADAM_WD * w_ref[...])
        m2_ref[...] = m2
        v2_ref[...] = v2

    spec = pl.BlockSpec((tr, c), lambda i: (i, 0))
    res = pl.pallas_call(
        body, grid=(r // tr,), in_specs=[spec] * 4, out_specs=[spec] * 3,
        out_shape=[jax.ShapeDtypeStruct((r, c), F32)] * 3,
        compiler_params=_params(("parallel",)), name="adamw_" + name,
    )(w, g, m, v)
    return tuple(a.reshape(shape) for a in res)


def _mm(a, b, grid, a_spec, b_spec, o_spec, o_shape, o_dtype, dims, acc_shape, name):
    nk = grid[2]

    def whole(a_ref, b_ref, o_ref):
        o_ref[...] = lax.dot_general(a_ref[...].astype(BF16), b_ref[...].astype(BF16), dims,
                                     preferred_element_type=F32).astype(o_ref.dtype)

    def body(a_ref, b_ref, o_ref, acc_ref):
        k = pl.program_id(2)

        @pl.when(k == 0)
        def _():
            acc_ref[...] = jnp.zeros_like(acc_ref)

        acc_ref[...] += lax.dot_general(a_ref[...].astype(BF16), b_ref[...].astype(BF16), dims,
                                        preferred_element_type=F32)

        @pl.when(k == nk - 1)
        def _():
            o_ref[...] = acc_ref[...].astype(o_ref.dtype)

    return pl.pallas_call(
        whole if nk == 1 else body, grid=grid, in_specs=[a_spec, b_spec], out_specs=o_spec,
        out_shape=jax.ShapeDtypeStruct(o_shape, o_dtype),
        scratch_shapes=[] if nk == 1 else [pltpu.VMEM(acc_shape, F32)],
        compiler_params=_params(("parallel", "parallel", "arbitrary")), name=name,
    )(a, b)


def _mm_fwd(a, wg, col, name, tm=1024, tn_cap=1536, tk_cap=1024, out_dtype=F32):
    m, k = a.shape
    ns, r, c = wg.shape
    tm = min(tm, m)
    if col:
        n, tn, tk = ns * c, _tile(c, tn_cap), _tile(k, tk_cap)
        per = c // tn
        b_spec = pl.BlockSpec((None, tk, tn), lambda j, i, kk: (j // per, kk, j % per))
    else:
        n, tn, tk = c, _tile(c, tn_cap), _tile(r, tk_cap)
        per = r // tk
        b_spec = pl.BlockSpec((None, tk, tn), lambda j, i, kk: (kk // per, kk % per, j))
    return _mm(a, wg, (n // tn, m // tm, k // tk), pl.BlockSpec((tm, tk), lambda j, i, kk: (i, kk)), b_spec,
               pl.BlockSpec((tm, tn), lambda j, i, kk: (i, j)), (m, n), out_dtype, NN, (tm, tn), name)


def _mm_dgrad(dc, wg, col, name, tm=1024, to_cap=1536, tc_cap=2048):
    m, n = dc.shape
    ns, r, c = wg.shape
    tm = min(tm, m)
    if col:
        kout, to, tc = r, _tile(r, to_cap), _tile(c, tc_cap)
        per = c // tc
        b_spec = pl.BlockSpec((None, to, tc), lambda j, i, kk: (kk // per, j, kk % per))
    else:
        kout, to, tc = ns * r, _tile(r, to_cap), _tile(c, tc_cap)
        per = r // to
        b_spec = pl.BlockSpec((None, to, tc), lambda j, i, kk: (j // per, j % per, kk))
    return _mm(dc, wg, (kout // to, m // tm, n // tc), pl.BlockSpec((tm, tc), lambda j, i, kk: (i, kk)), b_spec,
               pl.BlockSpec((tm, to), lambda j, i, kk: (i, j)), (m, kout), F32, NT, (tm, to), name)


def _mm_wgrad(a, dc, col, name, ti_cap=1536, tn_cap=1536, tkm=2048):
    m, k = a.shape
    n = dc.shape[1]
    tkm = min(tkm, m)
    if col:
        r, c = k, n // N_CHIPS
        ti, tn = _tile(r, ti_cap), _tile(c, tn_cap)
        per = c // tn
        o_spec = pl.BlockSpec((None, ti, tn), lambda i, j, kk: (j // per, i, j % per))
    else:
        r, c = k // N_CHIPS, n
        ti, tn = _tile(r, ti_cap), _tile(c, tn_cap)
        per = r // ti
        o_spec = pl.BlockSpec((None, ti, tn), lambda i, j, kk: (i // per, i % per, j))
    return _mm(a, dc, (k // ti, n // tn, m // tkm), pl.BlockSpec((tkm, ti), lambda i, j, kk: (kk, i)),
               pl.BlockSpec((tkm, tn), lambda i, j, kk: (kk, j)), o_spec, (N_CHIPS, r, c), BF16, TN, (ti, tn), name)


def _log_keep(z):
    return -(jnp.maximum(z, 0.0) + jnp.log(1.0 + jnp.exp(-jnp.abs(z))))


def _split_dot(x, t):
    hi = x.astype(BF16)
    lo = (x - hi.astype(F32)).astype(BF16)
    return (lax.dot_general(hi, t, NN, preferred_element_type=F32)
            + lax.dot_general(lo, t, NN, preferred_element_type=F32))


def _sb_fwd(proj, n_sb, bq):
    s = proj.shape[0]
    scale = 1.0 / math.sqrt(HEAD_DIM)

    def body(q_ref, k_ref, v_ref, o_ref, lt_ref):
        i = pl.program_id(1)
        q = q_ref[...].astype(BF16)
        row = lax.broadcasted_iota(jnp.int32, (bq, bq), 0)
        col = lax.broadcasted_iota(jnp.int32, (bq, bq), 1)
        later_in_block = (row > col).astype(BF16)
        keep = col < row

        def blocks(js, carry, diagonal=False):
            c, acc = carry
            sl = [pl.ds(pl.multiple_of(j * bq, bq), bq) for j in js]
            masked = [diagonal and n == 0 for n in range(len(js))]
            kbs = [k_ref[s_, :].astype(BF16) for s_ in sl]
            zs = [lax.dot_general(q, kb, NT, preferred_element_type=F32) * scale for kb in kbs]
            lks = [jnp.where(keep, _log_keep(z), 0.0) if m else _log_keep(z) for z, m in zip(zs, masked)]
            within = [_split_dot(lk, later_in_block) for lk in lks]
            es = []
            for z, lk, w, m in zip(zs, lks, within, masked):
                e = z + lk + w + c
                es.append(jnp.where(keep, e, NEG) if m else e)
                c = c + jnp.sum(lk, axis=1, keepdims=True)
            for e, s_ in zip(es, sl):
                acc = acc + lax.dot_general(jnp.exp(e).astype(BF16), v_ref[s_, :].astype(BF16), NN,
                                            preferred_element_type=F32)
            return c, acc

        first = [functools.partial(blocks, [i - u for u in range(g + 1)], diagonal=True) for g in range(4)]
        carry = lax.switch(i % 4, first, (jnp.zeros((bq, 1), F32), jnp.zeros((bq, HEAD_DIM), F32)))
        top = i - i % 4 - 1
        carry = lax.fori_loop(0, i // 4, lambda t, cr: blocks([top - 4 * t - u for u in range(4)], cr), carry)
        o_ref[...] = carry[1]
        lt_ref[...] = jnp.broadcast_to(carry[0], (bq, HEAD_DIM))

    blk = pl.BlockSpec((bq, HEAD_DIM), lambda h, i: (i, h))
    shp = jax.ShapeDtypeStruct((s, n_sb * HEAD_DIM), F32)
    return pl.pallas_call(
        body, grid=(n_sb, s // bq),
        in_specs=[blk,
                  pl.BlockSpec((s, HEAD_DIM), lambda h, i: (0, n_sb + h)),
                  pl.BlockSpec((s, HEAD_DIM), lambda h, i: (0, 2 * n_sb + h))],
        out_specs=[blk, blk], out_shape=[shp, shp],
        compiler_params=_params(("parallel", "parallel")), name="sb_fwd",
    )(proj, proj, proj)


def _sb_bwd(proj, lt, do, n_sb, bq):
    s = proj.shape[0]
    nq = s // bq
    scale = 1.0 / math.sqrt(HEAD_DIM)

    def body(q_ref, k_ref, v_ref, lt_ref, do_ref, dq_ref, dk_ref, dv_ref, dk_acc, dv_acc):
        i = pl.program_id(1)

        @pl.when(i == 0)
        def _():
            dk_acc[...] = jnp.zeros_like(dk_acc)
            dv_acc[...] = jnp.zeros_like(dv_acc)

        q = q_ref[...].astype(BF16)
        do_b = do_ref[...].astype(BF16)
        ltot = jnp.max(lt_ref[...], axis=1, keepdims=True)
        row = lax.broadcasted_iota(jnp.int32, (bq, bq), 0)
        col = lax.broadcasted_iota(jnp.int32, (bq, bq), 1)
        upto_in_block = (row <= col).astype(BF16)
        before_in_block = (row < col).astype(BF16)
        keep = col < row

        def blocks(js, carry, diagonal=False):
            pk, pg, dq = carry
            sl = [pl.ds(pl.multiple_of(j * bq, bq), bq) for j in js]
            masked = [diagonal and n == len(js) - 1 for n in range(len(js))]
            kbs = [k_ref[s_, :].astype(BF16) for s_ in sl]
            zs = [lax.dot_general(q, kb, NT, preferred_element_type=F32) * scale for kb in kbs]
            das = [lax.dot_general(do_b, v_ref[s_, :].astype(BF16), NT, preferred_element_type=F32) for s_ in sl]
            lks = [jnp.where(keep, _log_keep(z), 0.0) if m else _log_keep(z) for z, m in zip(zs, masked)]
            upto = [_split_dot(lk, upto_in_block) for lk in lks]
            gs, abs_ = [], []
            for z, lk, u, da, m in zip(zs, lks, upto, das, masked):
                e = z + lk + ((ltot - pk) - u)
                a = jnp.exp(jnp.where(keep, e, NEG) if m else e)
                gs.append(a * da)
                abs_.append(a.astype(BF16))
                pk = pk + jnp.sum(lk, axis=1, keepdims=True)
            for a_b, s_ in zip(abs_, sl):
                dv_acc[s_, :] += lax.dot_general(a_b, do_b, TN, preferred_element_type=F32)
            before = [lax.dot_general(g.astype(BF16), before_in_block, NN, preferred_element_type=F32) for g in gs]
            dzs = []
            for z, lk, g, bf, m in zip(zs, lks, gs, before, masked):
                keep_p = jnp.exp(lk)
                dz = g * keep_p - (pg + bf) * (1.0 - keep_p)
                dzs.append(((jnp.where(keep, dz, 0.0) if m else dz) * scale).astype(BF16))
                pg = pg + jnp.sum(g, axis=1, keepdims=True)
            for dz_b, kb in zip(dzs, kbs):
                dq = dq + lax.dot_general(dz_b, kb, NN, preferred_element_type=F32)
            for dz_b, s_ in zip(dzs, sl):
                dk_acc[s_, :] += lax.dot_general(dz_b, q, TN, preferred_element_type=F32)
            return pk, pg, dq

        zero = jnp.zeros((bq, 1), F32)
        carry = lax.fori_loop(0, i // 4, lambda t, cr: blocks([4 * t + u for u in range(4)], cr),
                              (zero, zero, jnp.zeros((bq, HEAD_DIM), F32)))
        last = [functools.partial(blocks, [i - g + u for u in range(g + 1)], diagonal=True) for g in range(4)]
        carry = lax.switch(i % 4, last, carry)
        dq_ref[...] = carry[2].astype(dq_ref.dtype)

        @pl.when(i == nq - 1)
        def _():
            dk_ref[...] = dk_acc[...].astype(dk_ref.dtype)
            dv_ref[...] = dv_acc[...].astype(dv_ref.dtype)

    blk = pl.BlockSpec((bq, HEAD_DIM), lambda h, i: (i, h))
    full = pl.BlockSpec((s, HEAD_DIM), lambda h, i: (0, h))
    shp = jax.ShapeDtypeStruct((s, n_sb * HEAD_DIM), BF16)
    return pl.pallas_call(
        body, grid=(n_sb, nq),
        in_specs=[blk,
                  pl.BlockSpec((s, HEAD_DIM), lambda h, i: (0, n_sb + h)),
                  pl.BlockSpec((s, HEAD_DIM), lambda h, i: (0, 2 * n_sb + h)),
                  blk, blk],
        out_specs=[blk, full, full], out_shape=[shp, shp, shp],
        scratch_shapes=[pltpu.VMEM((s, HEAD_DIM), F32), pltpu.VMEM((s, HEAD_DIM), F32)],
        compiler_params=_params(("parallel", "arbitrary")), name="sb_bwd",
    )(proj, proj, proj, lt, do)


def _t5_bucket(dist):
    max_exact = NUM_BUCKETS // 2
    d = jnp.maximum(dist, 1).astype(F32)
    large = max_exact + (jnp.log(d / max_exact) / math.log(MAX_DISTANCE / max_exact)
                         * (NUM_BUCKETS - max_exact)).astype(jnp.int32)
    large = jnp.minimum(large, NUM_BUCKETS - 1)
    return jnp.where(dist < max_exact, dist, large)


def _dil_tables(rel_bias):
    qi = jnp.arange(HEAD_DIM, dtype=jnp.int32)[:, None]
    ki = jnp.arange(2 * HEAD_DIM, dtype=jnp.int32)[None, :]
    rel = HEAD_DIM + qi - ki
    band = (rel >= 0) & (rel <= HEAD_DIM)
    biases, buckets = [], []
    for d in DILATIONS:
        bucket = _t5_bucket(jnp.maximum(rel, 0) * d)
        onehot = (bucket[:, :, None] == jnp.arange(NUM_BUCKETS, dtype=jnp.int32)).astype(F32)
        bias = jnp.einsum("qkb,bh->hqk", onehot, rel_bias.astype(F32), precision=lax.Precision.HIGHEST)
        biases.append(jnp.where(band[None], bias, NEG))
        buckets.append(jnp.where(band, bucket, -1).astype(F32))
    return jnp.stack(biases, axis=1), jnp.stack(buckets, axis=0)


def _sub_rows(ref, start, d):
    if d == 1:
        return ref[pl.ds(pl.multiple_of(start, HEAD_DIM), HEAD_DIM), :]
    return ref[pl.ds(start, HEAD_DIM, stride=d), :]


def _sub_idx(start, d):
    if d == 1:
        return pl.ds(pl.multiple_of(start, HEAD_DIM), HEAD_DIM)
    return pl.ds(start, HEAD_DIM, stride=d)


def _dil_logits(q_ref, k_ref, bm, n, cur, prv, d, scale):
    qb = _sub_rows(q_ref, cur, d).astype(BF16)
    kk = jnp.concatenate([_sub_rows(k_ref, prv, d), _sub_rows(k_ref, cur, d)], axis=0).astype(BF16)
    sc = lax.dot_general(qb, kk, NT, preferred_element_type=F32) * scale + bm
    colk = lax.broadcasted_iota(jnp.int32, sc.shape, 1)
    sc = jnp.where((colk >= HEAD_DIM) | (n > 0), sc, NEG)
    return qb, kk, sc


def _dil_fwd(proj, bm, n_sb, n_dl, after=()):
    s = proj.shape[0]
    scale = 1.0 / math.sqrt(HEAD_DIM)
    chunk = min(s, 512)
    na = len(after)

    def body(q_ref, k_ref, v_ref, bm_ref, *rest):
        o_ref, l_ref, ob0, ob1, ob2, lb0, lb1, lb2 = rest[na:]
        obs, lbs = (ob0, ob1, ob2), (lb0, lb1, lb2)
        for b, d in enumerate(DILATIONS):
            nb = s // (HEAD_DIM * d)

            def group(g, _, b=b, d=d, nb=nb):
                where = []
                for u in range(DIL_UNROLL):
                    idx = g * DIL_UNROLL + u
                    r, n = idx // nb, idx % nb
                    where.append((n, n * (HEAD_DIM * d) + r, jnp.maximum(n - 1, 0) * (HEAD_DIM * d) + r))
                bm = bm_ref[b]
                scs = [_dil_logits(q_ref, k_ref, bm, n, cur, prv, d, scale)[2] for n, cur, prv in where]
                mxs = [jnp.max(sc, axis=1, keepdims=True) for sc in scs]
                prs = [jnp.exp(sc - mx) for sc, mx in zip(scs, mxs)]
                dens = [jnp.sum(pr, axis=1, keepdims=True) for pr in prs]
                outs = [lax.dot_general(
                    pr.astype(BF16),
                    jnp.concatenate([_sub_rows(v_ref, prv, d), _sub_rows(v_ref, cur, d)], axis=0).astype(BF16),
                    NN, preferred_element_type=F32) for pr, (n, cur, prv) in zip(prs, where)]
                for o, mx, den, (n, cur, prv) in zip(outs, mxs, dens, where):
                    obs[b][_sub_idx(cur, d), :] = o / den
                    lbs[b][_sub_idx(cur, d), :] = jnp.broadcast_to(mx + jnp.log(den), (HEAD_DIM, HEAD_DIM))
                return 0

            lax.fori_loop(0, s // HEAD_DIM // DIL_UNROLL, group, 0)

        for ci in range(s // chunk):
            sl = pl.ds(ci * chunk, chunk)
            l0, l1, l2 = lb0[sl, :], lb1[sl, :], lb2[sl, :]
            mx = jnp.maximum(jnp.maximum(l0, l1), l2)
            w0, w1, w2 = jnp.exp(l0 - mx), jnp.exp(l1 - mx), jnp.exp(l2 - mx)
            tot = w0 + w1 + w2
            o_ref[sl, :] = (w0 * ob0[sl, :] + w1 * ob1[sl, :] + w2 * ob2[sl, :]) / tot
            l_ref[sl, :] = mx + jnp.log(tot)

    base = 3 * n_sb
    full = pl.BlockSpec((s, HEAD_DIM), lambda h: (0, h))
    shp = jax.ShapeDtypeStruct((s, n_dl * HEAD_DIM), F32)
    return pl.pallas_call(
        body, grid=(n_dl,),
        in_specs=[pl.BlockSpec((s, HEAD_DIM), lambda h: (0, base + h)),
                  pl.BlockSpec((s, HEAD_DIM), lambda h: (0, base + n_dl + h)),
                  pl.BlockSpec((s, HEAD_DIM), lambda h: (0, base + 2 * n_dl + h)),
                  pl.BlockSpec((None, 3, HEAD_DIM, 2 * HEAD_DIM), lambda h: (h, 0, 0, 0))]
        + [pl.BlockSpec(memory_space=pl.ANY)] * na,
        out_specs=[full, full], out_shape=[shp, shp],
        scratch_shapes=[pltpu.VMEM((s, HEAD_DIM), F32)] * 6,
        compiler_params=_params(("parallel",)), name="dil_fwd",
    )(proj, proj, proj, bm, *after)


def _dil_bwd(proj, do, o, lse, bm, n_sb, n_dl):
    s = proj.shape[0]
    scale = 1.0 / math.sqrt(HEAD_DIM)
    chunk = min(s, 512)

    def body(q_ref, k_ref, v_ref, do_ref, o_ref, l_ref, bm_ref, dq_ref, dk_ref, dv_ref, ds_ref, dq_s, dk_s, dv_s):
        dq_s[...] = jnp.zeros_like(dq_s)
        dk_s[...] = jnp.zeros_like(dk_s)
        dv_s[...] = jnp.zeros_like(dv_s)
        ds_ref[...] = jnp.zeros_like(ds_ref)
        for b, d in enumerate(DILATIONS):
            nb = s // (HEAD_DIM * d)

            def group(g, _, b=b, d=d, nb=nb):
                where = []
                for u in range(DIL_UNROLL):
                    idx = g * DIL_UNROLL + u
                    r, n = idx // nb, idx % nb
                    where.append((n, n * (HEAD_DIM * d) + r, jnp.maximum(n - 1, 0) * (HEAD_DIM * d) + r))
                bm = bm_ref[b]
                logits = [_dil_logits(q_ref, k_ref, bm, n, cur, prv, d, scale) for n, cur, prv in where]
                do_fs = [_sub_rows(do_ref, cur, d) for n, cur, prv in where]
                do_bs = [do_f.astype(BF16) for do_f in do_fs]
                dps = [lax.dot_general(
                    do_b, jnp.concatenate([_sub_rows(v_ref, prv, d), _sub_rows(v_ref, cur, d)], axis=0).astype(BF16),
                    NT, preferred_element_type=F32) for do_b, (n, cur, prv) in zip(do_bs, where)]
                ws, dss = [], []
                for (qb, kk, sc), do_f, dp, (n, cur, prv) in zip(logits, do_fs, dps, where):
                    delta = jnp.sum(do_f * _sub_rows(o_ref, cur, d), axis=1, keepdims=True)
                    lr = _sub_rows(l_ref, cur, d)
                    w = jnp.exp(sc - jnp.concatenate([lr, lr], axis=1))
                    ws.append(w)
                    dss.append(w * (dp - delta))
                ds_bs = [(ds * scale).astype(BF16) for ds in dss]
                dv_blks = [lax.dot_general(w.astype(BF16), do_b, TN, preferred_element_type=F32)
                           for w, do_b in zip(ws, do_bs)]
                dk_blks = [lax.dot_general(ds_b, qb, TN, preferred_element_type=F32)
                           for ds_b, (qb, kk, sc) in zip(ds_bs, logits)]
                dq_blks = [lax.dot_general(ds_b, kk, NN, preferred_element_type=F32)
                           for ds_b, (qb, kk, sc) in zip(ds_bs, logits)]
                total = dss[0]
                for ds in dss[1:]:
                    total = total + ds
                ds_ref[b] += total
                for dq_blk, dk_blk, dv_blk, (n, cur, prv) in zip(dq_blks, dk_blks, dv_blks, where):
                    ci, pi = _sub_idx(cur, d), _sub_idx(prv, d)
                    dq_s[ci, :] += dq_blk
                    dk_s[ci, :] += dk_blk[HEAD_DIM:]
                    dv_s[ci, :] += dv_blk[HEAD_DIM:]
                    dk_s[pi, :] += dk_blk[:HEAD_DIM]
                    dv_s[pi, :] += dv_blk[:HEAD_DIM]
                return 0

            lax.fori_loop(0, s // HEAD_DIM // DIL_UNROLL, group, 0)

        for ci in range(s // chunk):
            sl = pl.ds(ci * chunk, chunk)
            dq_ref[sl, :] = dq_s[sl, :].astype(dq_ref.dtype)
            dk_ref[sl, :] = dk_s[sl, :].astype(dk_ref.dtype)
            dv_ref[sl, :] = dv_s[sl, :].astype(dv_ref.dtype)

    base = 3 * n_sb
    full = pl.BlockSpec((s, HEAD_DIM), lambda h: (0, h))
    tab = pl.BlockSpec((None, 3, HEAD_DIM, 2 * HEAD_DIM), lambda h: (h, 0, 0, 0))
    shp = jax.ShapeDtypeStruct((s, n_dl * HEAD_DIM), BF16)
    return pl.pallas_call(
        body, grid=(n_dl,),
        in_specs=[pl.BlockSpec((s, HEAD_DIM), lambda h: (0, base + h)),
                  pl.BlockSpec((s, HEAD_DIM), lambda h: (0, base + n_dl + h)),
                  pl.BlockSpec((s, HEAD_DIM), lambda h: (0, base + 2 * n_dl + h)),
                  full, full, full, tab],
        out_specs=[full, full, full, tab],
        out_shape=[shp, shp, shp, jax.ShapeDtypeStruct((n_dl, 3, HEAD_DIM, 2 * HEAD_DIM), F32)],
        scratch_shapes=[pltpu.VMEM((s, HEAD_DIM), F32)] * 3,
        compiler_params=_params(("parallel",)), name="dil_bwd",
    )(proj, proj, proj, do, o, lse, bm)


def _rel_bias_grad(ds_all, buckets):
    depth, n_dl = ds_all.shape[:2]
    rows = -(-n_dl // 8) * 8

    def body(ds_ref, bk_ref, o_ref):
        lane = lax.broadcasted_iota(jnp.int32, (1, LANE), 1)

        def one_bucket(bkt, acc):
            fb = bkt.astype(F32)
            out = []
            for h in range(n_dl):
                val = jnp.zeros((1, 1), F32)
                for b in range(3):
                    tot = ds_ref[0, h, b]
                    for l in range(1, depth):
                        tot = tot + ds_ref[l, h, b]
                    val = val + jnp.sum(jnp.where(bk_ref[b] == fb, tot, 0.0), keepdims=True)
                out.append(jnp.where(lane == bkt, val, 0.0))
            out += [jnp.zeros((1, LANE), F32)] * (rows - n_dl)
            return acc + jnp.concatenate(out, axis=0)

        o_ref[...] = lax.fori_loop(0, NUM_BUCKETS, one_bucket, jnp.zeros((rows, LANE), F32))

    return pl.pallas_call(
        body, out_shape=jax.ShapeDtypeStruct((rows, LANE), F32),
        in_specs=[pl.BlockSpec(memory_space=pltpu.VMEM)] * 2, out_specs=pl.BlockSpec(memory_space=pltpu.VMEM),
        compiler_params=_params(), name="rel_bias_grad",
    )(ds_all, buckets)


def _place():
    return lax.axis_index("x"), lax.axis_index("y"), lax.axis_index("c")


def _flip(v, bit):
    return 1 - v if bit else v


HBM_SPEC = pl.BlockSpec(memory_space=pl.ANY)


HBM_ONLY = pl.BlockSpec(memory_space=pltpu.HBM)
SEM_SPEC = pl.BlockSpec(memory_space=pltpu.SEMAPHORE)
DATAFLOW = pltpu.SideEffectType.DATAFLOW_SIDE_EFFECTING


def _in_hbm(a):
    return pltpu.with_memory_space_constraint(a, pltpu.HBM)


def _split_start(arrays, token, copies_of, n_sem, name):
    na = len(arrays)

    def body(*refs):
        for cp in copies_of(refs[:na], refs[na + 1], refs[na + 2]):
            cp.start()
        refs[-1][...] = jnp.zeros_like(refs[-1])

    sems = pltpu.SemaphoreType.DMA((n_sem,))
    res = pl.pallas_call(
        body, name=name,
        out_shape=(sems, sems, *[pltpu.HBM(a.shape, a.dtype) for a in arrays], jax.ShapeDtypeStruct((8, LANE), F32)),
        in_specs=[HBM_ONLY] * na + [HBM_SPEC],
        out_specs=(SEM_SPEC, SEM_SPEC, *[HBM_ONLY] * na, pl.BlockSpec(memory_space=pltpu.VMEM)),
        input_output_aliases={i: 2 + i for i in range(na)},
        compiler_params=pltpu.CompilerParams(has_side_effects=DATAFLOW),
    )(*[_in_hbm(a) for a in arrays], token)
    return res[0], res[1], res[2:2 + na], res[-1]


def _split_wait(started, after, copies_of, name):
    send, recv, arrays, _ = started
    na = len(arrays)

    def body(*refs):
        for cp in copies_of(refs[:na], refs[na], refs[na + 1]):
            cp.wait_send()
            cp.wait_recv()

    return pl.pallas_call(
        body, name=name, out_shape=[pltpu.HBM(a.shape, a.dtype) for a in arrays],
        in_specs=[HBM_ONLY] * na + [SEM_SPEC, SEM_SPEC, HBM_SPEC], out_specs=[HBM_ONLY] * na,
        input_output_aliases={i: i for i in range(na)},
        compiler_params=pltpu.CompilerParams(has_side_effects=DATAFLOW),
    )(*arrays, send, recv, after)


def _gather_ici_copies(buf, send, recv):
    x, y, c = _place()
    out = []
    for w in range(len(buf)):
        h = buf[w].shape[1] // 2
        mine = buf[w].at[2 * x + y, pl.ds(c * h, h)]
        for k in (1, 2, 3):
            out.append(pltpu.make_async_remote_copy(
                src_ref=mine, dst_ref=mine, send_sem=send.at[4 * w + k], recv_sem=recv.at[4 * w + k],
                device_id=(_flip(x, k >> 1), _flip(y, k & 1), c), device_id_type=MESH))
    return out


def _gather_d2d_copies(buf, send, recv):
    x, y, c = _place()
    out = []
    for w in range(len(buf)):
        h = buf[w].shape[1] // 2
        for k in (1, 2, 3):
            got = buf[w].at[2 * _flip(x, k >> 1) + _flip(y, k & 1), pl.ds(c * h, h)]
            out.append(pltpu.make_async_remote_copy(
                src_ref=got, dst_ref=got, send_sem=send.at[4 * w + k], recv_sem=recv.at[4 * w + k],
                device_id=(x, y, 1 - c), device_id_type=MESH))
    return out


def _scatter_copies(refs, send, recv):
    nw = len(refs) // 2
    src, buf = refs[:nw], refs[nw:]
    x, y, c = _place()
    out = []
    for w in range(nw):
        h = src[w].shape[1] // 2
        for k in range(1, N_DEV):
            px, py, pc = _flip(x, k >> 2), _flip(y, (k >> 1) & 1), _flip(c, k & 1)
            out.append(pltpu.make_async_remote_copy(
                src_ref=src[w].at[2 * px + py, pl.ds(pc * h, h)], dst_ref=buf[w].at[k - 1],
                send_sem=send.at[N_DEV * w + k], recv_sem=recv.at[N_DEV * w + k], device_id=(px, py, pc),
                device_id_type=MESH))
    return out


def _join_halves(grads):
    nw = len(grads)

    def body(*refs):
        ins, outs = refs[:nw], refs[nw:2 * nw]
        send, recv = refs[2 * nw:]
        x, y, c = _place()
        remote = []
        for w in range(nw):
            h = ins[w].shape[0] // 2
            hc = h // JOIN_CHUNKS
            for j in range(JOIN_CHUNKS):
                rows = pl.ds(c * h + j * hc, hc)
                cp = pltpu.make_async_remote_copy(
                    src_ref=ins[w].at[rows], dst_ref=outs[w].at[rows],
                    send_sem=send.at[w, j], recv_sem=recv.at[w, j], device_id=(x, y, 1 - c), device_id_type=MESH)
                cp.start()
                remote.append(cp)
        for w in range(nw):
            h = ins[w].shape[0] // 2
            hc = h // JOIN_CHUNKS
            for j in range(JOIN_CHUNKS):
                theirs = outs[w].at[pl.ds((1 - c) * h + j * hc, hc)]
                pltpu.make_async_remote_copy(
                    src_ref=theirs, dst_ref=theirs, send_sem=send.at[w, j], recv_sem=recv.at[w, j],
                    device_id=(x, y, c), device_id_type=MESH).wait_recv()
        for cp in remote:
            cp.wait_send()

    sems = [pltpu.SemaphoreType.DMA((nw, JOIN_CHUNKS))] * 2
    return pl.pallas_call(
        body, out_shape=[jax.ShapeDtypeStruct(a.shape, a.dtype) for a in grads],
        in_specs=[HBM_SPEC] * nw, out_specs=[HBM_SPEC] * nw, scratch_shapes=sems,
        input_output_aliases={i: i for i in range(nw)}, name="join_halves",
    )(*grads)


def _allreduce_small(v):
    rows, c = v.shape

    def body(v_ref, o_ref, buf, local_sem, send, recv):
        x, y, cc = _place()
        me = 4 * x + 2 * y + cc
        own = pltpu.make_async_copy(v_ref, buf.at[me], local_sem)
        own.start()
        sends = []
        for k in range(1, N_DEV):
            px, py, pc = _flip(x, k >> 2), _flip(y, (k >> 1) & 1), _flip(cc, k & 1)
            cp = pltpu.make_async_remote_copy(
                src_ref=v_ref, dst_ref=buf.at[me], send_sem=send.at[k], recv_sem=recv.at[k],
                device_id=(px, py, pc), device_id_type=MESH)
            cp.start()
            sends.append(cp)
        for k in range(1, N_DEV):
            px, py, pc = _flip(x, k >> 2), _flip(y, (k >> 1) & 1), _flip(cc, k & 1)
            slot = buf.at[4 * px + 2 * py + pc]
            pltpu.make_async_remote_copy(
                src_ref=slot, dst_ref=slot, send_sem=send.at[k], recv_sem=recv.at[k],
                device_id=(x, y, cc), device_id_type=MESH).wait_recv()
        for cp in sends:
            cp.wait_send()
        own.wait()
        acc = buf[0]
        for i in range(1, N_DEV):
            acc = acc + buf[i]
        o_ref[...] = acc

    return pl.pallas_call(
        body, out_shape=jax.ShapeDtypeStruct((rows, c), F32),
        in_specs=[pl.BlockSpec(memory_space=pltpu.VMEM)], out_specs=pl.BlockSpec(memory_space=pltpu.VMEM),
        scratch_shapes=[pltpu.VMEM((N_DEV, rows, c), F32), pltpu.SemaphoreType.DMA,
                        pltpu.SemaphoreType.DMA((N_DEV,)), pltpu.SemaphoreType.DMA((N_DEV,))],
        compiler_params=_params(),
        name="allreduce_small",
    )(v)


def kernel(x, p, ln_mix_pre, w_in, ln_head, w_out, ln_mix_post, rel_bias, ln_ffn_pre, w_gate_up, w_down, ln_ffn_post, ln_pli, w_pli_gate, w_pli_proj, loss_target, m_ln_mix_pre, m_w_in, m_ln_head, m_w_out, m_ln_mix_post, m_rel_bias, m_ln_ffn_pre, m_w_gate_up, m_w_down, m_ln_ffn_post, m_ln_pli, m_w_pli_gate, m_w_pli_proj, v_ln_mix_pre, v_w_in, v_ln_head, v_w_out, v_ln_mix_post, v_rel_bias, v_ln_ffn_pre, v_w_gate_up, v_w_down, v_ln_ffn_post, v_ln_pli, v_w_pli_gate, v_w_pli_proj):
    weights = dict(ln_mix_pre=ln_mix_pre, w_in=w_in, ln_head=ln_head, w_out=w_out, ln_mix_post=ln_mix_post,
                   rel_bias=rel_bias, ln_ffn_pre=ln_ffn_pre, w_gate_up=w_gate_up, w_down=w_down,
                   ln_ffn_post=ln_ffn_post, ln_pli=ln_pli, w_pli_gate=w_pli_gate, w_pli_proj=w_pli_proj)
    mom1 = dict(ln_mix_pre=m_ln_mix_pre, w_in=m_w_in, ln_head=m_ln_head, w_out=m_w_out, ln_mix_post=m_ln_mix_post,
                rel_bias=m_rel_bias, ln_ffn_pre=m_ln_ffn_pre, w_gate_up=m_w_gate_up, w_down=m_w_down,
                ln_ffn_post=m_ln_ffn_post, ln_pli=m_ln_pli, w_pli_gate=m_w_pli_gate, w_pli_proj=m_w_pli_proj)
    mom2 = dict(ln_mix_pre=v_ln_mix_pre, w_in=v_w_in, ln_head=v_ln_head, w_out=v_w_out, ln_mix_post=v_ln_mix_post,
                rel_bias=v_rel_bias, ln_ffn_pre=v_ln_ffn_pre, w_gate_up=v_w_gate_up, w_down=v_w_down,
                ln_ffn_post=v_ln_ffn_post, ln_pli=v_ln_pli, w_pli_gate=v_w_pli_gate, w_pli_proj=v_w_pli_proj)

    _, seq, d_model = x.shape
    depth = w_in.shape[0]
    n_heads = d_model // HEAD_DIM
    n_sb = n_heads // 2
    n_dl = n_heads - n_sb
    assert seq % (HEAD_DIM * DILATIONS[-1]) == 0 and d_model % (2 * HEAD_DIM) == 0
    bq = 256
    tr = 128
    tr_ff = 64

    xs = x[0]
    target = loss_target[0]
    gain = {n: [weights[n][l][None, :] for l in range(depth)] for n in SMALL}
    bias_mask, buckets = _dil_tables(rel_bias)

    zero_token = jnp.zeros((8, LANE), F32)
    token = zero_token
    gathers = []
    for l in range(depth):
        parts = []
        for names in ((FIRST_USED, REST) if l == 0 else (BIG,)):
            slots = [_cast_layer(weights[n], l, n) for n in names]
            tag = f"{l}" if names is BIG else f"{l}_{names[0]}"
            parts.append((names, tag, _split_start(slots, token, _gather_ici_copies, 4 * len(names),
                                                   f"gather_start_{tag}")))
            token = parts[-1][2][3]
        gathers.append(parts)

    def gather_pass_on(part, after):
        names, tag, started = part
        arrays = _split_wait(started, after, _gather_ici_copies, f"gather_wait_{tag}")
        started = _split_start(arrays, zero_token, _gather_d2d_copies, 4 * len(names), f"gather_pass_{tag}")
        return names, tag, started, started[3]

    def gather_done(passed, after):
        names, tag, started, _ = passed
        return dict(zip(names, _split_wait(started, after, _gather_d2d_copies, f"gather_done_{tag}")))

    saved = []
    h1 = _norm_in(xs, gain["ln_mix_pre"][0], tr)
    xin = xs
    passed = gather_pass_on(gathers[0][0], token)
    wg = gather_done(passed, passed[3])
    for l in range(depth):
        proj = _mm_fwd(h1, wg["w_in"], True, "mm_in", tn_cap=768, tk_cap=2048)
        o_sb, lt_sb = _sb_fwd(proj, n_sb, bq)
        if l == 0:
            passed = gather_pass_on(gathers[0][1], o_sb)
            o_dl, lse_dl = _dil_fwd(proj, bias_mask, n_sb, n_dl, (passed[3],))
            wg.update(gather_done(passed, o_dl))
        else:
            o_dl, lse_dl = _dil_fwd(proj, bias_mask, n_sb, n_dl)
        on = _headnorm(o_sb, o_dl, gain["ln_head"][l], tr)
        y = _mm_fwd(on, wg["w_out"], False, "mm_out", tn_cap=1024)
        x2, h2 = _res_norm(xin, y, gain["ln_mix_post"][l], gain["ln_ffn_pre"][l], "post_attn", tr)
        gu = _mm_fwd(h2, wg["w_gate_up"], True, "mm_gate_up", tk_cap=2048, out_dtype=BF16)
        act = _swiglu(gu, tr_ff)
        f = _mm_fwd(act, wg["w_down"], False, "mm_down", tn_cap=1024, tk_cap=1536)
        passed = gather_pass_on(gathers[l + 1][0], f) if l + 1 < depth else None
        x3, h3 = _res_norm(x2, f, gain["ln_ffn_post"][l], gain["ln_pli"][l], "post_ffn", tr,
                           () if passed is None else (passed[3],))
        gl = _mm_fwd(h3, wg["w_pli_gate"], False, "mm_pli_gate", tn_cap=1024)
        pl_in = p[l, 0]
        pp = _mm_fwd(pl_in, wg["w_pli_proj"], True, "mm_pli_proj")
        saved.append(dict(wg=wg, x=xin, h1=h1, proj=proj, o_sb=o_sb, lt_sb=lt_sb, o_dl=o_dl, lse_dl=lse_dl, on=on, y=y, x2=x2,
                          h2=h2, gu=gu, act=act, f=f, x3=x3, h3=h3, gl=gl, pp=pp, p=pl_in))
        if l + 1 < depth:
            xin, h1 = _pli_out(x3, gl, pp, gain["ln_mix_pre"][l + 1], tr)
            wg = gather_done(passed, pp)
        else:
            dx, loss_part = _loss_head(x3, gl, pp, target, tr)

    grad_big = {n: [None] * depth for n in BIG}
    grad_gain = {n: [None] * depth for n in SMALL}
    ds_layers = [None] * depth

    def start_scatter(layer, names, dw):
        dws = [dw[n] for n in names]
        landing = [lax.empty((N_DEV - 1, a.shape[1] // 2, a.shape[2]), a.dtype) for a in dws]
        tag = f"{layer}" if names is BIG else f"{layer}_{names[0]}"
        return layer, names, tag, _split_start(dws + landing, zero_token, _scatter_copies, N_DEV * len(names),
                                               f"scatter_start_{tag}")

    def reduce_layer(layer, names, tag, started, after):
        arrays = _split_wait(started, after, _scatter_copies, f"scatter_wait_{tag}")
        nw = len(names)
        halves = [_reduce_piece(arrays[w], arrays[nw + w], names[w]) for w in range(nw)]
        joined = _join_halves(halves)
        for n, g in zip(names, joined):
            grad_big[n][layer] = g
        return joined[0]

    pending = None
    for l in reversed(range(depth)):
        sv = saved[l]
        wg = sv["wg"]
        dpp, dgl = _pli_bwd(dx, sv["gl"], sv["pp"], tr, () if pending is None else (pending[3][3],))
        dw = {}
        dw["w_pli_proj"] = _mm_wgrad(sv["p"], dpp, True, "wg_pli_proj")
        dw["w_pli_gate"] = _mm_wgrad(sv["h3"], dgl, False, "wg_pli_gate")
        dh3 = _mm_dgrad(dgl, wg["w_pli_gate"], False, "dg_pli_gate")
        dx3, df, grad_gain["ln_pli"][l], grad_gain["ln_ffn_post"][l] = _res_norm_bwd(
            dx, dh3, sv["x3"], sv["f"], gain["ln_pli"][l], gain["ln_ffn_post"][l], "post_ffn_bwd", tr)
        dw["w_down"] = _mm_wgrad(sv["act"], df, False, "wg_down")
        dact = _mm_dgrad(df, wg["w_down"], False, "dg_down")
        dgu = _swiglu_bwd(dact, sv["gu"], tr_ff)
        dw["w_gate_up"] = _mm_wgrad(sv["h2"], dgu, True, "wg_gate_up")
        dh2 = _mm_dgrad(dgu, wg["w_gate_up"], True, "dg_gate_up", tc_cap=2816)
        dx2, dy, grad_gain["ln_ffn_pre"][l], grad_gain["ln_mix_post"][l] = _res_norm_bwd(
            dx3, dh2, sv["x2"], sv["y"], gain["ln_ffn_pre"][l], gain["ln_mix_post"][l], "post_attn_bwd", tr)
        dw["w_out"] = _mm_wgrad(sv["on"], dy, False, "wg_out")
        don = _mm_dgrad(dy, wg["w_out"], False, "dg_out")
        early = start_scatter(l, REST, dw) if l == 0 else None
        do_sb, do_dl, grad_gain["ln_head"][l] = _headnorm_bwd(
            don, sv["o_sb"], sv["o_dl"], gain["ln_head"][l], tr, () if early is None else (early[3][3],))
        dq_s, dk_s, dv_s = _sb_bwd(sv["proj"], sv["lt_sb"], do_sb, n_sb, bq)
        dq_d, dk_d, dv_d, ds_layers[l] = _dil_bwd(sv["proj"], do_dl, sv["o_dl"], sv["lse_dl"], bias_mask, n_sb, n_dl)
        dproj = jnp.concatenate([dq_s, dk_s, dv_s, dq_d, dk_d, dv_d], axis=1)
        dw["w_in"] = _mm_wgrad(sv["h1"], dproj, True, "wg_in")
        dh1 = _mm_dgrad(dproj, wg["w_in"], True, "dg_in", tc_cap=1536)
        dx, grad_gain["ln_mix_pre"][l] = _norm_in_bwd(dx2, dh1, sv["x"], gain["ln_mix_pre"][l], tr)
        late = start_scatter(l, BIG if l > 0 else FIRST_USED, dw)
        if pending is not None:
            reduce_layer(*pending, dx if l > 0 else late[3][3])
        pending = late
    done = reduce_layer(*early, pending[3][3])
    reduce_layer(*pending, done)

    db = _rel_bias_grad(jnp.stack(ds_layers, axis=0), buckets)
    rb_flat = db[:n_dl, :NUM_BUCKETS].T.reshape(1, NUM_BUCKETS * n_dl)
    def widen(v):
        return jnp.pad(v, ((0, 0), (0, d_model - v.shape[1])))
    small_rows = [grad_gain[n][l] for n in SMALL for l in range(depth)] + [widen(rb_flat), widen(loss_part)]
    n_rows = len(small_rows)
    small = jnp.concatenate(small_rows + [jnp.zeros((-n_rows % 8, d_model), F32)], axis=0)
    total = _allreduce_small(small)
    grads = {}
    for i, n in enumerate(SMALL):
        grads[n] = total[i * depth:(i + 1) * depth]
    grads["rel_bias"] = total[len(SMALL) * depth, :NUM_BUCKETS * n_dl].reshape(NUM_BUCKETS, n_dl)
    loss = (0.5 / d_model) * jnp.sum(total[len(SMALL) * depth + 1, :LANE])
    for n in BIG:
        grads[n] = jnp.stack(grad_big[n], axis=0)

    delta, new_m, new_v = {}, {}, {}
    for n in WEIGHTS:
        delta[n], new_m[n], new_v[n] = _adamw(weights[n], grads[n], mom1[n], mom2[n], n)
    return (loss, dx[None], *[grads[n] for n in WEIGHTS], *[delta[n] for n in WEIGHTS],
            *[new_m[n] for n in WEIGHTS], *[new_v[n] for n in WEIGHTS])
```

```python
import functools
import math

import jax
import jax.numpy as jnp
from jax import lax
from jax.experimental import pallas as pl
from jax.experimental.pallas import tpu as pltpu

F32 = jnp.float32
BF16 = jnp.bfloat16

HEAD_DIM = 128
RMS_EPS = 1e-6
DILATIONS = (1, 4, 16)
NUM_BUCKETS = 32
MAX_DISTANCE = 2048
NEG = -1e30
N_CHIPS = 4
N_DEV = 8
JOIN_CHUNKS = 8
DIL_UNROLL = 8

ADAM_LR = 0.001
ADAM_B1 = 0.9
ADAM_B2 = 0.999
ADAM_EPS = 1e-08
ADAM_WD = 0.01
ADAM_STEP = 10

V7X_VMEM_LIMIT = 48 * 1024 * 1024
LANE = 128

NN = (((1,), (0,)), ((), ()))
NT = (((1,), (1,)), ((), ()))
TN = (((0,), (0,)), ((), ()))
MESH = pl.DeviceIdType.MESH

BIG = ("w_in", "w_out", "w_gate_up", "w_down", "w_pli_gate", "w_pli_proj")
FIRST_USED = BIG[:1]
REST = BIG[1:]
COL_SHARDED = {"w_in": True, "w_out": False, "w_gate_up": True, "w_down": False,
               "w_pli_gate": False, "w_pli_proj": True}
SMALL = ("ln_mix_pre", "ln_head", "ln_mix_post", "ln_ffn_pre", "ln_ffn_post", "ln_pli")
WEIGHTS = ("ln_mix_pre", "w_in", "ln_head", "w_out", "ln_mix_post", "rel_bias", "ln_ffn_pre",
           "w_gate_up", "w_down", "ln_ffn_post", "ln_pli", "w_pli_gate", "w_pli_proj")


def _tile(n, cap):
    t = min(n, cap) // LANE * LANE
    while t >= LANE:
        if n % t == 0:
            return t
        t -= LANE
    return n


def _params(sem=None):
    return pltpu.CompilerParams(dimension_semantics=sem, vmem_limit_bytes=V7X_VMEM_LIMIT)


def _rowwise(fn, rows, vecs, outs, sums, name, tr, after=()):
    s = rows[0].shape[0]
    nr, nv, no, ns, na = len(rows), len(vecs), len(outs), len(sums), len(after)

    def body(*refs):
        ins = [r[...] for r in refs[:nr + nv]]
        res = fn(*ins)
        out_refs = refs[nr + nv + na:nr + nv + na + no]
        sum_refs = refs[nr + nv + na + no:]
        for o_ref, val in zip(out_refs, res[:no]):
            o_ref[...] = val.astype(o_ref.dtype)
        if ns:
            @pl.when(pl.program_id(0) == 0)
            def _():
                for s_ref in sum_refs:
                    s_ref[...] = jnp.zeros_like(s_ref)
            for s_ref, val in zip(sum_refs, res[no:]):
                s_ref[...] += jnp.sum(val, axis=0, keepdims=True)

    in_specs = [pl.BlockSpec((tr, r.shape[1]), lambda i: (i, 0)) for r in rows]
    in_specs += [pl.BlockSpec(v.shape, lambda i: (0, 0)) for v in vecs]
    in_specs += [pl.BlockSpec(memory_space=pl.ANY)] * na
    out_specs = [pl.BlockSpec((tr, c), lambda i: (i, 0)) for c, _ in outs]
    out_specs += [pl.BlockSpec((1, c), lambda i: (0, 0)) for c in sums]
    out_shape = [jax.ShapeDtypeStruct((s, c), dt) for c, dt in outs]
    out_shape += [jax.ShapeDtypeStruct((1, c), F32) for c in sums]
    return pl.pallas_call(
        body, grid=(s // tr,), in_specs=in_specs, out_specs=out_specs, out_shape=out_shape,
        compiler_params=_params(("arbitrary",) if ns else ("parallel",)), name=name,
    )(*rows, *vecs, *after)


def _rms_r(x):
    return lax.rsqrt(jnp.mean(x * x, axis=-1, keepdims=True) + RMS_EPS)


def _rms_bwd(x, g, dy):
    r = _rms_r(x)
    u = dy * g
    dx = r * (u - x * (r * r) * jnp.mean(u * x, axis=-1, keepdims=True))
    return dx, dy * x * r


def _sigmoid(z):
    return 1.0 / (1.0 + jnp.exp(-z))


def _norm_in(x, g, tr):
    d = x.shape[1]
    return _rowwise(lambda x, g: (x * _rms_r(x) * g,), [x], [g], [(d, BF16)], [], "norm_in", tr)[0]


def _norm_in_bwd(dx_res, dh, x, g, tr):
    d = x.shape[1]

    def fn(dx_res, dh, x, g):
        dx, dg = _rms_bwd(x, g, dh)
        return dx_res + dx, dg
    return _rowwise(fn, [dx_res, dh, x], [g], [(d, F32)], [d], "norm_in_bwd", tr)


def _headnorm(o_sb, o_dl, g, tr):
    d = g.shape[1]

    def fn(o_sb, o_dl, g):
        o = jnp.concatenate([o_sb, o_dl], axis=1)
        parts = []
        for h in range(d // HEAD_DIM):
            sl = slice(h * HEAD_DIM, (h + 1) * HEAD_DIM)
            oh = o[:, sl]
            parts.append(oh * _rms_r(oh) * g[:, sl])
        return (jnp.concatenate(parts, axis=1),)
    return _rowwise(fn, [o_sb, o_dl], [g], [(d, BF16)], [], "headnorm", tr)[0]


def _headnorm_bwd(don, o_sb, o_dl, g, tr, after=()):
    d = g.shape[1]
    n_sb = o_sb.shape[1]

    def fn(don, o_sb, o_dl, g):
        o = jnp.concatenate([o_sb, o_dl], axis=1)
        dos, dgs = [], []
        for h in range(d // HEAD_DIM):
            sl = slice(h * HEAD_DIM, (h + 1) * HEAD_DIM)
            dx, dg = _rms_bwd(o[:, sl], g[:, sl], don[:, sl])
            dos.append(dx)
            dgs.append(dg)
        do = jnp.concatenate(dos, axis=1)
        return do[:, :n_sb], do[:, n_sb:], jnp.concatenate(dgs, axis=1)
    return _rowwise(fn, [don, o_sb, o_dl], [g], [(n_sb, F32), (d - n_sb, F32)], [d], "headnorm_bwd", tr, after)


def _res_norm(x, y, g_post, g_pre, name, tr, after=()):
    d = x.shape[1]

    def fn(x, y, g_post, g_pre):
        x2 = x + y * _rms_r(y) * g_post
        return x2, x2 * _rms_r(x2) * g_pre
    return _rowwise(fn, [x, y], [g_post, g_pre], [(d, F32), (d, BF16)], [], name, tr, after)


def _res_norm_bwd(dx_res, dh, x2, y, g_pre, g_post, name, tr):
    d = x2.shape[1]

    def fn(dx_res, dh, x2, y, g_pre, g_post):
        dxa, dg_pre = _rms_bwd(x2, g_pre, dh)
        dx2 = dx_res + dxa
        dy, dg_post = _rms_bwd(y, g_post, dx2)
        return dx2, dy, dg_pre, dg_post
    return _rowwise(fn, [dx_res, dh, x2, y], [g_pre, g_post], [(d, F32), (d, BF16)], [d, d], name, tr)


def _swiglu(gu, tr):
    ff = gu.shape[1] // 2

    def fn(gu):
        g, u = gu[:, :ff].astype(F32), gu[:, ff:].astype(F32)
        return (g * _sigmoid(g) * u,)
    return _rowwise(fn, [gu], [], [(ff, BF16)], [], "swiglu", tr)[0]


def _swiglu_bwd(dact, gu, tr):
    ff = gu.shape[1] // 2

    def fn(dact, gu):
        g, u = gu[:, :ff].astype(F32), gu[:, ff:].astype(F32)
        sg = _sigmoid(g)
        dg = dact * u * sg * (1.0 + g * (1.0 - sg))
        du = dact * g * sg
        return (jnp.concatenate([dg, du], axis=1),)
    return _rowwise(fn, [dact, gu], [], [(2 * ff, BF16)], [], "swiglu_bwd", tr)[0]


def _pli_out(x3, gl, pp, g_next, tr):
    d = x3.shape[1]

    def fn(x3, gl, pp, g):
        x4 = x3 + _sigmoid(gl) * pp
        return x4, x4 * _rms_r(x4) * g
    return _rowwise(fn, [x3, gl, pp], [g_next], [(d, F32), (d, BF16)], [], "pli_out", tr)


def _loss_head(x3, gl, pp, target, tr):
    d = x3.shape[1]

    def fn(x3, gl, pp, t):
        err = x3 + _sigmoid(gl) * pp - t
        sq = err * err
        part = sq[:, :LANE]
        for k in range(1, d // LANE):
            part = part + sq[:, k * LANE:(k + 1) * LANE]
        return err * (1.0 / d), part
    return _rowwise(fn, [x3, gl, pp, target], [], [(d, F32)], [LANE], "loss_head", tr)


def _pli_bwd(dx, gl, pp, tr, after=()):
    d = dx.shape[1]

    def fn(dx, gl, pp):
        gate = _sigmoid(gl)
        return dx * gate, dx * pp * gate * (1.0 - gate)
    return _rowwise(fn, [dx, gl, pp], [], [(d, BF16), (d, BF16)], [], "pli_bwd", tr, after)


def _my_chip():
    return 2 * lax.axis_index("x") + lax.axis_index("y")


def _cast_layer(w, layer, name):
    _, r, c = w.shape
    tr = r
    while tr * c * 4 > (4 << 20) and tr % 32 == 0:
        tr //= 2

    def body(w_ref, o_ref):
        o_ref[...] = w_ref[...].astype(o_ref.dtype)

    return pl.pallas_call(
        body, grid=(r // tr,),
        in_specs=[pl.BlockSpec((None, tr, c), lambda i: (layer, i, 0))],
        out_specs=pl.BlockSpec((None, tr, c), lambda i: (_my_chip(), i, 0)),
        out_shape=jax.ShapeDtypeStruct((N_CHIPS, r, c), BF16),
        compiler_params=_params(("parallel",)), name="cast_" + name,
    )(w)


def _reduce_piece(dw, recv, name, layer, depth, into):
    _, r, c = dw.shape
    h = r // 2
    tr = h
    while tr * c * 2 * N_DEV > (8 << 20) and tr % 32 == 0:
        tr //= 2
    per = h // tr

    def body(d_ref, r_ref, *rest):
        acc = d_ref[...].astype(F32)
        for i in range(N_DEV - 1):
            acc = acc + r_ref[i].astype(F32)
        rest[-1][...] = acc

    kept = [] if into is None else [into]
    return pl.pallas_call(
        body, grid=(per,),
        in_specs=[pl.BlockSpec((None, tr, c), lambda i: (_my_chip(), lax.axis_index("c") * per + i, 0)),
                  pl.BlockSpec((N_DEV - 1, tr, c), lambda i: (0, i, 0))] + [pl.BlockSpec(memory_space=pl.ANY)] * len(kept),
        out_specs=pl.BlockSpec((None, tr, c), lambda i: (layer, lax.axis_index("c") * per + i, 0)),
        out_shape=jax.ShapeDtypeStruct((depth, r, c), F32),
        input_output_aliases={2: 0} if kept else {},
        compiler_params=_params(("parallel",)), name="reduce_" + name,
    )(dw, recv, *kept)


def _adamw(w, g, m, v, name):
    shape = w.shape
    if w.ndim == 3:
        w, g, m, v = (a.reshape(shape[0] * shape[1], shape[2]) for a in (w, g, m, v))
    r, c = w.shape
    tr = r
    while tr * c * 4 > (1 << 20) and tr % 16 == 0:
        tr //= 2

    def body(w_ref, g_ref, m_ref, v_ref, d_ref, m2_ref, v2_ref, g2_ref):
        g = g_ref[...]
        m2 = ADAM_B1 * m_ref[...] + (1.0 - ADAM_B1) * g
        v2 = ADAM_B2 * v_ref[...] + (1.0 - ADAM_B2) * (g * g)
        m_hat = m2 / (1.0 - ADAM_B1 ** ADAM_STEP)
        v_hat = v2 / (1.0 - ADAM_B2 ** ADAM_STEP)
        d_ref[...] = -ADAM_LR * (m_hat / (jnp.sqrt(v_hat) + ADAM_EPS) + ADAM_WD * w_ref[...])
        m2_ref[...] = m2
        v2_ref[...] = v2
        g2_ref[...] = g

    spec = pl.BlockSpec((tr, c), lambda i: (i, 0))
    res = pl.pallas_call(
        body, grid=(r // tr,), in_specs=[spec] * 4, out_specs=[spec] * 4,
        out_shape=[jax.ShapeDtypeStruct((r, c), F32)] * 4,
        compiler_params=_params(("parallel",)), name="adamw_" + name,
    )(w, g, m, v)
    return tuple(a.reshape(shape) for a in res)


def _mm(a, b, grid, a_spec, b_spec, o_spec, o_shape, o_dtype, dims, acc_shape, name):
    nk = grid[2]

    def whole(a_ref, b_ref, o_ref):
        b = b_ref[...]
        if b.ndim == 3:
            b = b.reshape(b.shape[0] * b.shape[1], b.shape[2])
        o_ref[...] = lax.dot_general(a_ref[...].astype(BF16), b.astype(BF16), dims,
                                     preferred_element_type=F32).astype(o_ref.dtype)

    def body(a_ref, b_ref, o_ref, acc_ref):
        k = pl.program_id(2)

        @pl.when(k == 0)
        def _():
            acc_ref[...] = jnp.zeros_like(acc_ref)

        acc_ref[...] += lax.dot_general(a_ref[...].astype(BF16), b_ref[...].astype(BF16), dims,
                                        preferred_element_type=F32)

        @pl.when(k == nk - 1)
        def _():
            o_ref[...] = acc_ref[...].astype(o_ref.dtype)

    return pl.pallas_call(
        whole if nk == 1 else body, grid=grid, in_specs=[a_spec, b_spec], out_specs=o_spec,
        out_shape=jax.ShapeDtypeStruct(o_shape, o_dtype),
        scratch_shapes=[] if nk == 1 else [pltpu.VMEM(acc_shape, F32)],
        compiler_params=_params(("parallel", "parallel", "arbitrary")), name=name,
    )(a, b)


def _mm_fwd(a, wg, col, name, tm=1024, tn_cap=1536, tk_cap=1024, out_dtype=F32):
    m, k = a.shape
    ns, r, c = wg.shape
    tm = min(tm, m)
    if col:
        n, tn, tk = ns * c, _tile(c, tn_cap), _tile(k, tk_cap)
        per = c // tn
        b_spec = pl.BlockSpec((None, tk, tn), lambda j, i, kk: (j // per, kk, j % per))
    elif tk_cap >= k:
        n, tn, tk = c, _tile(c, tn_cap), k
        b_spec = pl.BlockSpec((ns, r, tn), lambda j, i, kk: (0, 0, j))
    else:
        n, tn, tk = c, _tile(c, tn_cap), _tile(r, tk_cap)
        per = r // tk
        b_spec = pl.BlockSpec((None, tk, tn), lambda j, i, kk: (kk // per, kk % per, j))
    return _mm(a, wg, (n // tn, m // tm, k // tk), pl.BlockSpec((tm, tk), lambda j, i, kk: (i, kk)), b_spec,
               pl.BlockSpec((tm, tn), lambda j, i, kk: (i, j)), (m, n), out_dtype, NN, (tm, tn), name)


def _mm_dgrad(dc, wg, col, name, tm=1024, to_cap=1536, tc_cap=2048):
    m, n = dc.shape
    ns, r, c = wg.shape
    tm = min(tm, m)
    if col:
        kout, to, tc = r, _tile(r, to_cap), _tile(c, tc_cap)
        per = c // tc
        b_spec = pl.BlockSpec((None, to, tc), lambda j, i, kk: (kk // per, j, kk % per))
    else:
        kout, to, tc = ns * r, _tile(r, to_cap), _tile(c, tc_cap)
        per = r // to
        b_spec = pl.BlockSpec((None, to, tc), lambda j, i, kk: (j // per, j % per, kk))
    return _mm(dc, wg, (kout // to, m // tm, n // tc), pl.BlockSpec((tm, tc), lambda j, i, kk: (i, kk)), b_spec,
               pl.BlockSpec((tm, to), lambda j, i, kk: (i, j)), (m, kout), F32, NT, (tm, to), name)


def _mm_wgrad(a, dc, col, name, ti_cap=1536, tn_cap=1536, tkm=2048):
    m, k = a.shape
    n = dc.shape[1]
    tkm = min(tkm, m)
    if col:
        r, c = k, n // N_CHIPS
        ti, tn = _tile(r, ti_cap), _tile(c, tn_cap)
        per = c // tn
        o_spec = pl.BlockSpec((None, ti, tn), lambda i, j, kk: (j // per, i, j % per))
    else:
        r, c = k // N_CHIPS, n
        ti, tn = _tile(r, ti_cap), _tile(c, tn_cap)
        per = r // ti
        o_spec = pl.BlockSpec((None, ti, tn), lambda i, j, kk: (i // per, i % per, j))
    return _mm(a, dc, (k // ti, n // tn, m // tkm), pl.BlockSpec((tkm, ti), lambda i, j, kk: (kk, i)),
               pl.BlockSpec((tkm, tn), lambda i, j, kk: (kk, j)), o_spec, (N_CHIPS, r, c), BF16, TN, (ti, tn), name)


def _log_keep(z):
    return -(jnp.maximum(z, 0.0) + jnp.log(1.0 + jnp.exp(-jnp.abs(z))))


def _split_dot(x, t):
    hi = x.astype(BF16)
    lo = (x - hi.astype(F32)).astype(BF16)
    return (lax.dot_general(hi, t, NN, preferred_element_type=F32)
            + lax.dot_general(lo, t, NN, preferred_element_type=F32))


def _sb_fwd(proj, n_sb, bq):
    s = proj.shape[0]
    scale = 1.0 / math.sqrt(HEAD_DIM)

    def body(q_ref, k_ref, v_ref, o_ref, lt_ref):
        i = pl.program_id(1)
        q = q_ref[...].astype(BF16)
        row = lax.broadcasted_iota(jnp.int32, (bq, bq), 0)
        col = lax.broadcasted_iota(jnp.int32, (bq, bq), 1)
        later_in_block = (row > col).astype(BF16)
        keep = col < row

        def blocks(js, carry, diagonal=False):
            c, acc = carry
            sl = [pl.ds(pl.multiple_of(j * bq, bq), bq) for j in js]
            masked = [diagonal and n == 0 for n in range(len(js))]
            kbs = [k_ref[s_, :].astype(BF16) for s_ in sl]
            zs = [lax.dot_general(q, kb, NT, preferred_element_type=F32) * scale for kb in kbs]
            lks = [jnp.where(keep, _log_keep(z), 0.0) if m else _log_keep(z) for z, m in zip(zs, masked)]
            within = [_split_dot(lk, later_in_block) for lk in lks]
            es = []
            for z, lk, w, m in zip(zs, lks, within, masked):
                e = z + lk + w + c
                es.append(jnp.where(keep, e, NEG) if m else e)
                c = c + jnp.sum(lk, axis=1, keepdims=True)
            for e, s_ in zip(es, sl):
                acc = acc + lax.dot_general(jnp.exp(e).astype(BF16), v_ref[s_, :].astype(BF16), NN,
                                            preferred_element_type=F32)
            return c, acc

        first = [functools.partial(blocks, [i - u for u in range(g + 1)], diagonal=True) for g in range(4)]
        carry = lax.switch(i % 4, first, (jnp.zeros((bq, 1), F32), jnp.zeros((bq, HEAD_DIM), F32)))
        top = i - i % 4 - 1
        carry = lax.fori_loop(0, i // 4, lambda t, cr: blocks([top - 4 * t - u for u in range(4)], cr), carry)
        o_ref[...] = carry[1]
        lt_ref[...] = jnp.broadcast_to(carry[0], (bq, HEAD_DIM))

    blk = pl.BlockSpec((bq, HEAD_DIM), lambda h, i: (i, h))
    shp = jax.ShapeDtypeStruct((s, n_sb * HEAD_DIM), F32)
    return pl.pallas_call(
        body, grid=(n_sb, s // bq),
        in_specs=[blk,
                  pl.BlockSpec((s, HEAD_DIM), lambda h, i: (0, n_sb + h)),
                  pl.BlockSpec((s, HEAD_DIM), lambda h, i: (0, 2 * n_sb + h))],
        out_specs=[blk, blk], out_shape=[shp, shp],
        compiler_params=_params(("parallel", "parallel")), name="sb_fwd",
    )(proj, proj, proj)


def _sb_bwd(proj, lt, do, n_sb, bq):
    s = proj.shape[0]
    nq = s // bq
    scale = 1.0 / math.sqrt(HEAD_DIM)

    def body(q_ref, k_ref, v_ref, lt_ref, do_ref, dq_ref, dk_ref, dv_ref, dk_acc, dv_acc):
        i = pl.program_id(1)

        @pl.when(i == 0)
        def _():
            dk_acc[...] = jnp.zeros_like(dk_acc)
            dv_acc[...] = jnp.zeros_like(dv_acc)

        q = q_ref[...].astype(BF16)
        do_b = do_ref[...].astype(BF16)
        ltot = jnp.max(lt_ref[...], axis=1, keepdims=True)
        row = lax.broadcasted_iota(jnp.int32, (bq, bq), 0)
        col = lax.broadcasted_iota(jnp.int32, (bq, bq), 1)
        upto_in_block = (row <= col).astype(BF16)
        before_in_block = (row < col).astype(BF16)
        keep = col < row

        def blocks(js, carry, diagonal=False):
            pk, pg, dq = carry
            sl = [pl.ds(pl.multiple_of(j * bq, bq), bq) for j in js]
            masked = [diagonal and n == len(js) - 1 for n in range(len(js))]
            kbs = [k_ref[s_, :].astype(BF16) for s_ in sl]
            zs = [lax.dot_general(q, kb, NT, preferred_element_type=F32) * scale for kb in kbs]
            das = [lax.dot_general(do_b, v_ref[s_, :].astype(BF16), NT, preferred_element_type=F32) for s_ in sl]
            lks = [jnp.where(keep, _log_keep(z), 0.0) if m else _log_keep(z) for z, m in zip(zs, masked)]
            upto = [_split_dot(lk, upto_in_block) for lk in lks]
            gs, abs_ = [], []
            for z, lk, u, da, m in zip(zs, lks, upto, das, masked):
                e = z + lk + ((ltot - pk) - u)
                a = jnp.exp(jnp.where(keep, e, NEG) if m else e)
                gs.append(a * da)
                abs_.append(a.astype(BF16))
                pk = pk + jnp.sum(lk, axis=1, keepdims=True)
            for a_b, s_ in zip(abs_, sl):
                dv_acc[s_, :] += lax.dot_general(a_b, do_b, TN, preferred_element_type=F32)
            before = [lax.dot_general(g.astype(BF16), before_in_block, NN, preferred_element_type=F32) for g in gs]
            dzs = []
            for z, lk, g, bf, m in zip(zs, lks, gs, before, masked):
                keep_p = jnp.exp(lk)
                dz = g * keep_p - (pg + bf) * (1.0 - keep_p)
                dzs.append(((jnp.where(keep, dz, 0.0) if m else dz) * scale).astype(BF16))
                pg = pg + jnp.sum(g, axis=1, keepdims=True)
            for dz_b, kb in zip(dzs, kbs):
                dq = dq + lax.dot_general(dz_b, kb, NN, preferred_element_type=F32)
            for dz_b, s_ in zip(dzs, sl):
                dk_acc[s_, :] += lax.dot_general(dz_b, q, TN, preferred_element_type=F32)
            return pk, pg, dq

        zero = jnp.zeros((bq, 1), F32)
        carry = lax.fori_loop(0, i // 4, lambda t, cr: blocks([4 * t + u for u in range(4)], cr),
                              (zero, zero, jnp.zeros((bq, HEAD_DIM), F32)))
        last = [functools.partial(blocks, [i - g + u for u in range(g + 1)], diagonal=True) for g in range(4)]
        carry = lax.switch(i % 4, last, carry)
        dq_ref[...] = carry[2].astype(dq_ref.dtype)

        @pl.when(i == nq - 1)
        def _():
            dk_ref[...] = dk_acc[...].astype(dk_ref.dtype)
            dv_ref[...] = dv_acc[...].astype(dv_ref.dtype)

    blk = pl.BlockSpec((bq, HEAD_DIM), lambda h, i: (i, h))
    full = pl.BlockSpec((s, HEAD_DIM), lambda h, i: (0, h))
    shp = jax.ShapeDtypeStruct((s, n_sb * HEAD_DIM), BF16)
    return pl.pallas_call(
        body, grid=(n_sb, nq),
        in_specs=[blk,
                  pl.BlockSpec((s, HEAD_DIM), lambda h, i: (0, n_sb + h)),
                  pl.BlockSpec((s, HEAD_DIM), lambda h, i: (0, 2 * n_sb + h)),
                  blk, blk],
        out_specs=[blk, full, full], out_shape=[shp, shp, shp],
        scratch_shapes=[pltpu.VMEM((s, HEAD_DIM), F32), pltpu.VMEM((s, HEAD_DIM), F32)],
        compiler_params=_params(("parallel", "arbitrary")), name="sb_bwd",
    )(proj, proj, proj, lt, do)


def _t5_bucket(dist):
    max_exact = NUM_BUCKETS // 2
    d = jnp.maximum(dist, 1).astype(F32)
    large = max_exact + (jnp.log(d / max_exact) / math.log(MAX_DISTANCE / max_exact)
                         * (NUM_BUCKETS - max_exact)).astype(jnp.int32)
    large = jnp.minimum(large, NUM_BUCKETS - 1)
    return jnp.where(dist < max_exact, dist, large)


def _dil_tables(rel_bias):
    qi = jnp.arange(HEAD_DIM, dtype=jnp.int32)[:, None]
    ki = jnp.arange(2 * HEAD_DIM, dtype=jnp.int32)[None, :]
    rel = HEAD_DIM + qi - ki
    band = (rel >= 0) & (rel <= HEAD_DIM)
    biases, buckets = [], []
    for d in DILATIONS:
        bucket = _t5_bucket(jnp.maximum(rel, 0) * d)
        onehot = (bucket[:, :, None] == jnp.arange(NUM_BUCKETS, dtype=jnp.int32)).astype(F32)
        bias = jnp.einsum("qkb,bh->hqk", onehot, rel_bias.astype(F32), precision=lax.Precision.HIGHEST)
        biases.append(jnp.where(band[None], bias, NEG))
        buckets.append(jnp.where(band, bucket, -1).astype(F32))
    return jnp.stack(biases, axis=1), jnp.stack(buckets, axis=0)


def _sub_rows(ref, start, d):
    if d == 1:
        return ref[pl.ds(pl.multiple_of(start, HEAD_DIM), HEAD_DIM), :]
    return ref[pl.ds(start, HEAD_DIM, stride=d), :]


def _sub_idx(start, d):
    if d == 1:
        return pl.ds(pl.multiple_of(start, HEAD_DIM), HEAD_DIM)
    return pl.ds(start, HEAD_DIM, stride=d)


def _dil_logits(q_ref, k_ref, bm, n, cur, prv, d, scale):
    qb = _sub_rows(q_ref, cur, d).astype(BF16)
    kk = jnp.concatenate([_sub_rows(k_ref, prv, d), _sub_rows(k_ref, cur, d)], axis=0).astype(BF16)
    sc = lax.dot_general(qb, kk, NT, preferred_element_type=F32) * scale + bm
    colk = lax.broadcasted_iota(jnp.int32, sc.shape, 1)
    sc = jnp.where((colk >= HEAD_DIM) | (n > 0), sc, NEG)
    return qb, kk, sc


def _dil_fwd(proj, bm, n_sb, n_dl, after=()):
    s = proj.shape[0]
    scale = 1.0 / math.sqrt(HEAD_DIM)
    chunk = min(s, 512)
    na = len(after)

    def body(q_ref, k_ref, v_ref, bm_ref, *rest):
        o_ref, l_ref, ob0, ob1, ob2, lb0, lb1, lb2 = rest[na:]
        obs, lbs = (ob0, ob1, ob2), (lb0, lb1, lb2)
        for b, d in enumerate(DILATIONS):
            nb = s // (HEAD_DIM * d)

            def group(g, _, b=b, d=d, nb=nb):
                where = []
                for u in range(DIL_UNROLL):
                    idx = g * DIL_UNROLL + u
                    r, n = idx // nb, idx % nb
                    where.append((n, n * (HEAD_DIM * d) + r, jnp.maximum(n - 1, 0) * (HEAD_DIM * d) + r))
                bm = bm_ref[b]
                scs = [_dil_logits(q_ref, k_ref, bm, n, cur, prv, d, scale)[2] for n, cur, prv in where]
                mxs = [jnp.max(sc, axis=1, keepdims=True) for sc in scs]
                prs = [jnp.exp(sc - mx) for sc, mx in zip(scs, mxs)]
                dens = [jnp.sum(pr, axis=1, keepdims=True) for pr in prs]
                outs = [lax.dot_general(
                    pr.astype(BF16),
                    jnp.concatenate([_sub_rows(v_ref, prv, d), _sub_rows(v_ref, cur, d)], axis=0).astype(BF16),
                    NN, preferred_element_type=F32) for pr, (n, cur, prv) in zip(prs, where)]
                for o, mx, den, (n, cur, prv) in zip(outs, mxs, dens, where):
                    obs[b][_sub_idx(cur, d), :] = o / den
                    lbs[b][_sub_idx(cur, d), :] = jnp.broadcast_to(mx + jnp.log(den), (HEAD_DIM, HEAD_DIM))
                return 0

            lax.fori_loop(0, s // HEAD_DIM // DIL_UNROLL, group, 0)

        for ci in range(s // chunk):
            sl = pl.ds(ci * chunk, chunk)
            l0, l1, l2 = lb0[sl, :], lb1[sl, :], lb2[sl, :]
            mx = jnp.maximum(jnp.maximum(l0, l1), l2)
            w0, w1, w2 = jnp.exp(l0 - mx), jnp.exp(l1 - mx), jnp.exp(l2 - mx)
            tot = w0 + w1 + w2
            o_ref[sl, :] = (w0 * ob0[sl, :] + w1 * ob1[sl, :] + w2 * ob2[sl, :]) / tot
            l_ref[sl, :] = mx + jnp.log(tot)

    base = 3 * n_sb
    full = pl.BlockSpec((s, HEAD_DIM), lambda h: (0, h))
    shp = jax.ShapeDtypeStruct((s, n_dl * HEAD_DIM), F32)
    return pl.pallas_call(
        body, grid=(n_dl,),
        in_specs=[pl.BlockSpec((s, HEAD_DIM), lambda h: (0, base + h)),
                  pl.BlockSpec((s, HEAD_DIM), lambda h: (0, base + n_dl + h)),
                  pl.BlockSpec((s, HEAD_DIM), lambda h: (0, base + 2 * n_dl + h)),
                  pl.BlockSpec((None, 3, HEAD_DIM, 2 * HEAD_DIM), lambda h: (h, 0, 0, 0))]
        + [pl.BlockSpec(memory_space=pl.ANY)] * na,
        out_specs=[full, full], out_shape=[shp, shp],
        scratch_shapes=[pltpu.VMEM((s, HEAD_DIM), F32)] * 6,
        compiler_params=_params(("parallel",)), name="dil_fwd",
    )(proj, proj, proj, bm, *after)


def _dil_bwd(proj, do, o, lse, bm, n_sb, n_dl):
    s = proj.shape[0]
    scale = 1.0 / math.sqrt(HEAD_DIM)
    chunk = min(s, 512)

    def body(q_ref, k_ref, v_ref, do_ref, o_ref, l_ref, bm_ref, dq_ref, dk_ref, dv_ref, ds_ref, dq_s, dk_s, dv_s):
        dq_s[...] = jnp.zeros_like(dq_s)
        dk_s[...] = jnp.zeros_like(dk_s)
        dv_s[...] = jnp.zeros_like(dv_s)
        ds_ref[...] = jnp.zeros_like(ds_ref)
        for b, d in enumerate(DILATIONS):
            nb = s // (HEAD_DIM * d)

            def group(g, _, b=b, d=d, nb=nb):
                where = []
                for u in range(DIL_UNROLL):
                    idx = g * DIL_UNROLL + u
                    r, n = idx // nb, idx % nb
                    where.append((n, n * (HEAD_DIM * d) + r, jnp.maximum(n - 1, 0) * (HEAD_DIM * d) + r))
                bm = bm_ref[b]
                logits = [_dil_logits(q_ref, k_ref, bm, n, cur, prv, d, scale) for n, cur, prv in where]
                do_fs = [_sub_rows(do_ref, cur, d) for n, cur, prv in where]
                do_bs = [do_f.astype(BF16) for do_f in do_fs]
                dps = [lax.dot_general(
                    do_b, jnp.concatenate([_sub_rows(v_ref, prv, d), _sub_rows(v_ref, cur, d)], axis=0).astype(BF16),
                    NT, preferred_element_type=F32) for do_b, (n, cur, prv) in zip(do_bs, where)]
                ws, dss = [], []
                for (qb, kk, sc), do_f, dp, (n, cur, prv) in zip(logits, do_fs, dps, where):
                    delta = jnp.sum(do_f * _sub_rows(o_ref, cur, d), axis=1, keepdims=True)
                    lr = _sub_rows(l_ref, cur, d)
                    w = jnp.exp(sc - jnp.concatenate([lr, lr], axis=1))
                    ws.append(w)
                    dss.append(w * (dp - delta))
                ds_bs = [(ds * scale).astype(BF16) for ds in dss]
                dv_blks = [lax.dot_general(w.astype(BF16), do_b, TN, preferred_element_type=F32)
                           for w, do_b in zip(ws, do_bs)]
                dk_blks = [lax.dot_general(ds_b, qb, TN, preferred_element_type=F32)
                           for ds_b, (qb, kk, sc) in zip(ds_bs, logits)]
                dq_blks = [lax.dot_general(ds_b, kk, NN, preferred_element_type=F32)
                           for ds_b, (qb, kk, sc) in zip(ds_bs, logits)]
                total = dss[0]
                for ds in dss[1:]:
                    total = total + ds
                ds_ref[b] += total
                for dq_blk, dk_blk, dv_blk, (n, cur, prv) in zip(dq_blks, dk_blks, dv_blks, where):
                    ci, pi = _sub_idx(cur, d), _sub_idx(prv, d)
                    dq_s[ci, :] += dq_blk
                    dk_s[ci, :] += dk_blk[HEAD_DIM:]
                    dv_s[ci, :] += dv_blk[HEAD_DIM:]
                    dk_s[pi, :] += dk_blk[:HEAD_DIM]
                    dv_s[pi, :] += dv_blk[:HEAD_DIM]
                return 0

            lax.fori_loop(0, s // HEAD_DIM // DIL_UNROLL, group, 0)

        for ci in range(s // chunk):
            sl = pl.ds(ci * chunk, chunk)
            dq_ref[sl, :] = dq_s[sl, :].astype(dq_ref.dtype)
            dk_ref[sl, :] = dk_s[sl, :].astype(dk_ref.dtype)
            dv_ref[sl, :] = dv_s[sl, :].astype(dv_ref.dtype)

    base = 3 * n_sb
    full = pl.BlockSpec((s, HEAD_DIM), lambda h: (0, h))
    tab = pl.BlockSpec((None, 3, HEAD_DIM, 2 * HEAD_DIM), lambda h: (h, 0, 0, 0))
    shp = jax.ShapeDtypeStruct((s, n_dl * HEAD_DIM), BF16)
    return pl.pallas_call(
        body, grid=(n_dl,),
        in_specs=[pl.BlockSpec((s, HEAD_DIM), lambda h: (0, base + h)),
                  pl.BlockSpec((s, HEAD_DIM), lambda h: (0, base + n_dl + h)),
                  pl.BlockSpec((s, HEAD_DIM), lambda h: (0, base + 2 * n_dl + h)),
                  full, full, full, tab],
        out_specs=[full, full, full, tab],
        out_shape=[shp, shp, shp, jax.ShapeDtypeStruct((n_dl, 3, HEAD_DIM, 2 * HEAD_DIM), F32)],
        scratch_shapes=[pltpu.VMEM((s, HEAD_DIM), F32)] * 3,
        compiler_params=_params(("parallel",)), name="dil_bwd",
    )(proj, proj, proj, do, o, lse, bm)


def _rel_bias_grad(ds_all, buckets):
    depth, n_dl = ds_all.shape[:2]
    rows = -(-n_dl // 8) * 8

    def body(ds_ref, bk_ref, o_ref):
        lane = lax.broadcasted_iota(jnp.int32, (1, LANE), 1)

        def one_bucket(bkt, acc):
            fb = bkt.astype(F32)
            out = []
            for h in range(n_dl):
                val = jnp.zeros((1, 1), F32)
                for b in range(3):
                    tot = ds_ref[0, h, b]
                    for l in range(1, depth):
                        tot = tot + ds_ref[l, h, b]
                    val = val + jnp.sum(jnp.where(bk_ref[b] == fb, tot, 0.0), keepdims=True)
                out.append(jnp.where(lane == bkt, val, 0.0))
            out += [jnp.zeros((1, LANE), F32)] * (rows - n_dl)
            return acc + jnp.concatenate(out, axis=0)

        o_ref[...] = lax.fori_loop(0, NUM_BUCKETS, one_bucket, jnp.zeros((rows, LANE), F32))

    return pl.pallas_call(
        body, out_shape=jax.ShapeDtypeStruct((rows, LANE), F32),
        in_specs=[pl.BlockSpec(memory_space=pltpu.VMEM)] * 2, out_specs=pl.BlockSpec(memory_space=pltpu.VMEM),
        compiler_params=_params(), name="rel_bias_grad",
    )(ds_all, buckets)


def _place():
    return lax.axis_index("x"), lax.axis_index("y"), lax.axis_index("c")


def _flip(v, bit):
    return 1 - v if bit else v


HBM_SPEC = pl.BlockSpec(memory_space=pl.ANY)


HBM_ONLY = pl.BlockSpec(memory_space=pltpu.HBM)
SEM_SPEC = pl.BlockSpec(memory_space=pltpu.SEMAPHORE)
DATAFLOW = pltpu.SideEffectType.DATAFLOW_SIDE_EFFECTING


def _in_hbm(a):
    return pltpu.with_memory_space_constraint(a, pltpu.HBM)


def _split_start(arrays, token, copies_of, n_sem, name):
    na = len(arrays)

    def body(*refs):
        for cp in copies_of(refs[:na], refs[na + 1], refs[na + 2]):
            cp.start()
        refs[-1][...] = jnp.zeros_like(refs[-1])

    sems = pltpu.SemaphoreType.DMA((n_sem,))
    res = pl.pallas_call(
        body, name=name,
        out_shape=(sems, sems, *[pltpu.HBM(a.shape, a.dtype) for a in arrays], jax.ShapeDtypeStruct((8, LANE), F32)),
        in_specs=[HBM_ONLY] * na + [HBM_SPEC],
        out_specs=(SEM_SPEC, SEM_SPEC, *[HBM_ONLY] * na, pl.BlockSpec(memory_space=pltpu.VMEM)),
        input_output_aliases={i: 2 + i for i in range(na)},
        compiler_params=pltpu.CompilerParams(has_side_effects=DATAFLOW),
    )(*[_in_hbm(a) for a in arrays], token)
    return res[0], res[1], res[2:2 + na], res[-1]


def _split_wait(started, after, copies_of, name):
    send, recv, arrays, _ = started
    na = len(arrays)

    def body(*refs):
        for cp in copies_of(refs[:na], refs[na], refs[na + 1]):
            cp.wait_send()
            cp.wait_recv()

    return pl.pallas_call(
        body, name=name, out_shape=[pltpu.HBM(a.shape, a.dtype) for a in arrays],
        in_specs=[HBM_ONLY] * na + [SEM_SPEC, SEM_SPEC, HBM_SPEC], out_specs=[HBM_ONLY] * na,
        input_output_aliases={i: i for i in range(na)},
        compiler_params=pltpu.CompilerParams(has_side_effects=DATAFLOW),
    )(*arrays, send, recv, after)


def _gather_ici_copies(buf, send, recv):
    x, y, c = _place()
    out = []
    for w in range(len(buf)):
        h = buf[w].shape[1] // 2
        mine = buf[w].at[2 * x + y, pl.ds(c * h, h)]
        for k in (1, 2, 3):
            out.append(pltpu.make_async_remote_copy(
                src_ref=mine, dst_ref=mine, send_sem=send.at[4 * w + k], recv_sem=recv.at[4 * w + k],
                device_id=(_flip(x, k >> 1), _flip(y, k & 1), c), device_id_type=MESH))
    return out


def _gather_d2d_copies(buf, send, recv):
    x, y, c = _place()
    out = []
    for w in range(len(buf)):
        h = buf[w].shape[1] // 2
        for k in (1, 2, 3):
            got = buf[w].at[2 * _flip(x, k >> 1) + _flip(y, k & 1), pl.ds(c * h, h)]
            out.append(pltpu.make_async_remote_copy(
                src_ref=got, dst_ref=got, send_sem=send.at[4 * w + k], recv_sem=recv.at[4 * w + k],
                device_id=(x, y, 1 - c), device_id_type=MESH))
    return out


def _scatter_copies(refs, send, recv):
    nw = len(refs) // 2
    src, buf = refs[:nw], refs[nw:]
    x, y, c = _place()
    out = []
    for w in range(nw):
        h = src[w].shape[1] // 2
        for k in range(1, N_DEV):
            px, py, pc = _flip(x, k >> 2), _flip(y, (k >> 1) & 1), _flip(c, k & 1)
            out.append(pltpu.make_async_remote_copy(
                src_ref=src[w].at[2 * px + py, pl.ds(pc * h, h)], dst_ref=buf[w].at[k - 1],
                send_sem=send.at[N_DEV * w + k], recv_sem=recv.at[N_DEV * w + k], device_id=(px, py, pc),
                device_id_type=MESH))
    return out


def _join_halves(grads, layer):
    nw = len(grads)

    def body(*refs):
        ins, outs = refs[:nw], refs[nw:2 * nw]
        send, recv = refs[2 * nw:]
        x, y, c = _place()
        remote = []
        for w in range(nw):
            h = ins[w].shape[1] // 2
            hc = h // JOIN_CHUNKS
            for j in range(JOIN_CHUNKS):
                rows = pl.ds(c * h + j * hc, hc)
                cp = pltpu.make_async_remote_copy(
                    src_ref=ins[w].at[layer, rows], dst_ref=outs[w].at[layer, rows],
                    send_sem=send.at[w, j], recv_sem=recv.at[w, j], device_id=(x, y, 1 - c), device_id_type=MESH)
                cp.start()
                remote.append(cp)
        for w in range(nw):
            h = ins[w].shape[1] // 2
            hc = h // JOIN_CHUNKS
            for j in range(JOIN_CHUNKS):
                theirs = outs[w].at[layer, pl.ds((1 - c) * h + j * hc, hc)]
                pltpu.make_async_remote_copy(
                    src_ref=theirs, dst_ref=theirs, send_sem=send.at[w, j], recv_sem=recv.at[w, j],
                    device_id=(x, y, c), device_id_type=MESH).wait_recv()
        for cp in remote:
            cp.wait_send()

    sems = [pltpu.SemaphoreType.DMA((nw, JOIN_CHUNKS))] * 2
    return pl.pallas_call(
        body, out_shape=[jax.ShapeDtypeStruct(a.shape, a.dtype) for a in grads],
        in_specs=[HBM_SPEC] * nw, out_specs=[HBM_SPEC] * nw, scratch_shapes=sems,
        input_output_aliases={i: i for i in range(nw)}, name="join_halves",
    )(*grads)


def _allreduce_small(v):
    rows, c = v.shape

    def body(v_ref, o_ref, buf, local_sem, send, recv):
        x, y, cc = _place()
        me = 4 * x + 2 * y + cc
        own = pltpu.make_async_copy(v_ref, buf.at[me], local_sem)
        own.start()
        sends = []
        for k in range(1, N_DEV):
            px, py, pc = _flip(x, k >> 2), _flip(y, (k >> 1) & 1), _flip(cc, k & 1)
            cp = pltpu.make_async_remote_copy(
                src_ref=v_ref, dst_ref=buf.at[me], send_sem=send.at[k], recv_sem=recv.at[k],
                device_id=(px, py, pc), device_id_type=MESH)
            cp.start()
            sends.append(cp)
        for k in range(1, N_DEV):
            px, py, pc = _flip(x, k >> 2), _flip(y, (k >> 1) & 1), _flip(cc, k & 1)
            slot = buf.at[4 * px + 2 * py + pc]
            pltpu.make_async_remote_copy(
                src_ref=slot, dst_ref=slot, send_sem=send.at[k], recv_sem=recv.at[k],
                device_id=(x, y, cc), device_id_type=MESH).wait_recv()
        for cp in sends:
            cp.wait_send()
        own.wait()
        acc = buf[0]
        for i in range(1, N_DEV):
            acc = acc + buf[i]
        o_ref[...] = acc

    return pl.pallas_call(
        body, out_shape=jax.ShapeDtypeStruct((rows, c), F32),
        in_specs=[pl.BlockSpec(memory_space=pltpu.VMEM)], out_specs=pl.BlockSpec(memory_space=pltpu.VMEM),
        scratch_shapes=[pltpu.VMEM((N_DEV, rows, c), F32), pltpu.SemaphoreType.DMA,
                        pltpu.SemaphoreType.DMA((N_DEV,)), pltpu.SemaphoreType.DMA((N_DEV,))],
        compiler_params=_params(),
        name="allreduce_small",
    )(v)


def kernel(x, p, ln_mix_pre, w_in, ln_head, w_out, ln_mix_post, rel_bias, ln_ffn_pre, w_gate_up, w_down, ln_ffn_post, ln_pli, w_pli_gate, w_pli_proj, loss_target, m_ln_mix_pre, m_w_in, m_ln_head, m_w_out, m_ln_mix_post, m_rel_bias, m_ln_ffn_pre, m_w_gate_up, m_w_down, m_ln_ffn_post, m_ln_pli, m_w_pli_gate, m_w_pli_proj, v_ln_mix_pre, v_w_in, v_ln_head, v_w_out, v_ln_mix_post, v_rel_bias, v_ln_ffn_pre, v_w_gate_up, v_w_down, v_ln_ffn_post, v_ln_pli, v_w_pli_gate, v_w_pli_proj):
    weights = dict(ln_mix_pre=ln_mix_pre, w_in=w_in, ln_head=ln_head, w_out=w_out, ln_mix_post=ln_mix_post,
                   rel_bias=rel_bias, ln_ffn_pre=ln_ffn_pre, w_gate_up=w_gate_up, w_down=w_down,
                   ln_ffn_post=ln_ffn_post, ln_pli=ln_pli, w_pli_gate=w_pli_gate, w_pli_proj=w_pli_proj)
    mom1 = dict(ln_mix_pre=m_ln_mix_pre, w_in=m_w_in, ln_head=m_ln_head, w_out=m_w_out, ln_mix_post=m_ln_mix_post,
                rel_bias=m_rel_bias, ln_ffn_pre=m_ln_ffn_pre, w_gate_up=m_w_gate_up, w_down=m_w_down,
                ln_ffn_post=m_ln_ffn_post, ln_pli=m_ln_pli, w_pli_gate=m_w_pli_gate, w_pli_proj=m_w_pli_proj)
    mom2 = dict(ln_mix_pre=v_ln_mix_pre, w_in=v_w_in, ln_head=v_ln_head, w_out=v_w_out, ln_mix_post=v_ln_mix_post,
                rel_bias=v_rel_bias, ln_ffn_pre=v_ln_ffn_pre, w_gate_up=v_w_gate_up, w_down=v_w_down,
                ln_ffn_post=v_ln_ffn_post, ln_pli=v_ln_pli, w_pli_gate=v_w_pli_gate, w_pli_proj=v_w_pli_proj)

    _, seq, d_model = x.shape
    depth = w_in.shape[0]
    n_heads = d_model // HEAD_DIM
    n_sb = n_heads // 2
    n_dl = n_heads - n_sb
    assert seq % (HEAD_DIM * DILATIONS[-1]) == 0 and d_model % (2 * HEAD_DIM) == 0
    bq = 256
    tr = 128
    tr_ff = 64

    xs = x[0]
    target = loss_target[0]
    gain = {n: [weights[n][l][None, :] for l in range(depth)] for n in SMALL}
    bias_mask, buckets = _dil_tables(rel_bias)

    zero_token = jnp.zeros((8, LANE), F32)
    token = zero_token
    gathers = []
    for l in range(depth):
        parts = []
        for names in ((FIRST_USED, REST) if l == 0 else (BIG,)):
            slots = [_cast_layer(weights[n], l, n) for n in names]
            tag = f"{l}" if names is BIG else f"{l}_{names[0]}"
            parts.append((names, tag, _split_start(slots, token, _gather_ici_copies, 4 * len(names),
                                                   f"gather_start_{tag}")))
            token = parts[-1][2][3]
        gathers.append(parts)

    def gather_pass_on(part, after):
        names, tag, started = part
        arrays = _split_wait(started, after, _gather_ici_copies, f"gather_wait_{tag}")
        started = _split_start(arrays, zero_token, _gather_d2d_copies, 4 * len(names), f"gather_pass_{tag}")
        return names, tag, started, started[3]

    def gather_done(passed, after):
        names, tag, started, _ = passed
        return dict(zip(names, _split_wait(started, after, _gather_d2d_copies, f"gather_done_{tag}")))

    saved = []
    h1 = _norm_in(xs, gain["ln_mix_pre"][0], tr)
    xin = xs
    passed = gather_pass_on(gathers[0][0], token)
    wg = gather_done(passed, passed[3])
    for l in range(depth):
        proj = _mm_fwd(h1, wg["w_in"], True, "mm_in", tn_cap=768, tk_cap=2048)
        o_sb, lt_sb = _sb_fwd(proj, n_sb, bq)
        if l == 0:
            passed = gather_pass_on(gathers[0][1], o_sb)
            o_dl, lse_dl = _dil_fwd(proj, bias_mask, n_sb, n_dl, (passed[3],))
            wg.update(gather_done(passed, o_dl))
        else:
            o_dl, lse_dl = _dil_fwd(proj, bias_mask, n_sb, n_dl)
        on = _headnorm(o_sb, o_dl, gain["ln_head"][l], tr)
        y = _mm_fwd(on, wg["w_out"], False, "mm_out", tn_cap=1024, tk_cap=2048)
        x2, h2 = _res_norm(xin, y, gain["ln_mix_post"][l], gain["ln_ffn_pre"][l], "post_attn", tr)
        gu = _mm_fwd(h2, wg["w_gate_up"], True, "mm_gate_up", tk_cap=2048, out_dtype=BF16)
        act = _swiglu(gu, tr_ff)
        f = _mm_fwd(act, wg["w_down"], False, "mm_down", tn_cap=1024, tk_cap=1536)
        passed = gather_pass_on(gathers[l + 1][0], f) if l + 1 < depth else None
        x3, h3 = _res_norm(x2, f, gain["ln_ffn_post"][l], gain["ln_pli"][l], "post_ffn", tr,
                           () if passed is None else (passed[3],))
        gl = _mm_fwd(h3, wg["w_pli_gate"], False, "mm_pli_gate", tn_cap=1024, tk_cap=2048)
        pl_in = p[l, 0]
        pp = _mm_fwd(pl_in, wg["w_pli_proj"], True, "mm_pli_proj")
        saved.append(dict(wg=wg, x=xin, h1=h1, proj=proj, o_sb=o_sb, lt_sb=lt_sb, o_dl=o_dl, lse_dl=lse_dl, on=on, y=y, x2=x2,
                          h2=h2, gu=gu, act=act, f=f, x3=x3, h3=h3, gl=gl, pp=pp, p=pl_in))
        if l + 1 < depth:
            xin, h1 = _pli_out(x3, gl, pp, gain["ln_mix_pre"][l + 1], tr)
            wg = gather_done(passed, pp)
        else:
            dx, loss_part = _loss_head(x3, gl, pp, target, tr)

    grad_big = {n: None for n in BIG}
    grad_gain = {n: [None] * depth for n in SMALL}
    ds_layers = [None] * depth

    def start_scatter(layer, names, dw):
        dws = [dw[n] for n in names]
        landing = [lax.empty((N_DEV - 1, a.shape[1] // 2, a.shape[2]), a.dtype) for a in dws]
        tag = f"{layer}" if names is BIG else f"{layer}_{names[0]}"
        return layer, names, tag, _split_start(dws + landing, zero_token, _scatter_copies, N_DEV * len(names),
                                               f"scatter_start_{tag}")

    def reduce_layer(layer, names, tag, started, after):
        arrays = _split_wait(started, after, _scatter_copies, f"scatter_wait_{tag}")
        nw = len(names)
        halves = [_reduce_piece(arrays[w], arrays[nw + w], names[w], layer, depth, grad_big[names[w]])
                  for w in range(nw)]
        joined = _join_halves(halves, layer)
        grad_big.update(zip(names, joined))
        return joined[0]

    pending = None
    for l in reversed(range(depth)):
        sv = saved[l]
        wg = sv["wg"]
        dpp, dgl = _pli_bwd(dx, sv["gl"], sv["pp"], tr, () if pending is None else (pending[3][3],))
        dw = {}
        dw["w_pli_proj"] = _mm_wgrad(sv["p"], dpp, True, "wg_pli_proj")
        dw["w_pli_gate"] = _mm_wgrad(sv["h3"], dgl, False, "wg_pli_gate")
        dh3 = _mm_dgrad(dgl, wg["w_pli_gate"], False, "dg_pli_gate")
        dx3, df, grad_gain["ln_pli"][l], grad_gain["ln_ffn_post"][l] = _res_norm_bwd(
            dx, dh3, sv["x3"], sv["f"], gain["ln_pli"][l], gain["ln_ffn_post"][l], "post_ffn_bwd", tr)
        dw["w_down"] = _mm_wgrad(sv["act"], df, False, "wg_down")
        dact = _mm_dgrad(df, wg["w_down"], False, "dg_down")
        dgu = _swiglu_bwd(dact, sv["gu"], tr_ff)
        dw["w_gate_up"] = _mm_wgrad(sv["h2"], dgu, True, "wg_gate_up")
        dh2 = _mm_dgrad(dgu, wg["w_gate_up"], True, "dg_gate_up", tc_cap=2816)
        dx2, dy, grad_gain["ln_ffn_pre"][l], grad_gain["ln_mix_post"][l] = _res_norm_bwd(
            dx3, dh2, sv["x2"], sv["y"], gain["ln_ffn_pre"][l], gain["ln_mix_post"][l], "post_attn_bwd", tr)
        dw["w_out"] = _mm_wgrad(sv["on"], dy, False, "wg_out")
        don = _mm_dgrad(dy, wg["w_out"], False, "dg_out")
        early = start_scatter(l, REST, dw) if l == 0 else None
        do_sb, do_dl, grad_gain["ln_head"][l] = _headnorm_bwd(
            don, sv["o_sb"], sv["o_dl"], gain["ln_head"][l], tr, () if early is None else (early[3][3],))
        dq_s, dk_s, dv_s = _sb_bwd(sv["proj"], sv["lt_sb"], do_sb, n_sb, bq)
        dq_d, dk_d, dv_d, ds_layers[l] = _dil_bwd(sv["proj"], do_dl, sv["o_dl"], sv["lse_dl"], bias_mask, n_sb, n_dl)
        dproj = jnp.concatenate([dq_s, dk_s, dv_s, dq_d, dk_d, dv_d], axis=1)
        dw["w_in"] = _mm_wgrad(sv["h1"], dproj, True, "wg_in")
        dh1 = _mm_dgrad(dproj, wg["w_in"], True, "dg_in", tc_cap=1536)
        dx, grad_gain["ln_mix_pre"][l] = _norm_in_bwd(dx2, dh1, sv["x"], gain["ln_mix_pre"][l], tr)
        late = start_scatter(l, BIG if l > 0 else FIRST_USED, dw)
        if pending is not None:
            reduce_layer(*pending, dx if l > 0 else late[3][3])
        pending = late
    done = reduce_layer(*early, pending[3][3])
    reduce_layer(*pending, done)

    db = _rel_bias_grad(jnp.stack(ds_layers, axis=0), buckets)
    rb_flat = db[:n_dl, :NUM_BUCKETS].T.reshape(1, NUM_BUCKETS * n_dl)
    def widen(v):
        return jnp.pad(v, ((0, 0), (0, d_model - v.shape[1])))
    small_rows = [grad_gain[n][l] for n in SMALL for l in range(depth)] + [widen(rb_flat), widen(loss_part)]
    n_rows = len(small_rows)
    small = jnp.concatenate(small_rows + [jnp.zeros((-n_rows % 8, d_model), F32)], axis=0)
    total = _allreduce_small(small)
    grads = {}
    for i, n in enumerate(SMALL):
        grads[n] = total[i * depth:(i + 1) * depth]
    grads["rel_bias"] = total[len(SMALL) * depth, :NUM_BUCKETS * n_dl].reshape(NUM_BUCKETS, n_dl)
    loss = (0.5 / d_model) * jnp.sum(total[len(SMALL) * depth + 1, :LANE])
    for n in BIG:
        grads[n] = grad_big[n]

    delta, new_m, new_v = {}, {}, {}
    for n in WEIGHTS:
        delta[n], new_m[n], new_v[n], grads[n] = _adamw(weights[n], grads[n], mom1[n], mom2[n], n)
    return (loss, dx[None], *[grads[n] for n in WEIGHTS], *[delta[n] for n in WEIGHTS],
            *[new_m[n] for n in WEIGHTS], *[new_v[n] for n in WEIGHTS])
```

```python
import functools
import math

import jax
import jax.numpy as jnp
from jax import lax
from jax.experimental import pallas as pl
from jax.experimental.pallas import tpu as pltpu

F32 = jnp.float32
BF16 = jnp.bfloat16

HEAD_DIM = 128
RMS_EPS = 1e-6
DILATIONS = (1, 4, 16)
NUM_BUCKETS = 32
MAX_DISTANCE = 2048
NEG = -1e30
N_CHIPS = 4
N_DEV = 8
JOIN_CHUNKS = 8
DIL_UNROLL = 8

ADAM_LR = 0.001
ADAM_B1 = 0.9
ADAM_B2 = 0.999
ADAM_EPS = 1e-08
ADAM_WD = 0.01
ADAM_STEP = 10

V7X_VMEM_LIMIT = 48 * 1024 * 1024
LANE = 128

NN = (((1,), (0,)), ((), ()))
NT = (((1,), (1,)), ((), ()))
TN = (((0,), (0,)), ((), ()))
MESH = pl.DeviceIdType.MESH

BIG = ("w_in", "w_out", "w_gate_up", "w_down", "w_pli_gate", "w_pli_proj")
FIRST_USED = BIG[:1]
REST = BIG[1:]
COL_SHARDED = {"w_in": True, "w_out": False, "w_gate_up": True, "w_down": False,
               "w_pli_gate": False, "w_pli_proj": True}
SMALL = ("ln_mix_pre", "ln_head", "ln_mix_post", "ln_ffn_pre", "ln_ffn_post", "ln_pli")
WEIGHTS = ("ln_mix_pre", "w_in", "ln_head", "w_out", "ln_mix_post", "rel_bias", "ln_ffn_pre",
           "w_gate_up", "w_down", "ln_ffn_post", "ln_pli", "w_pli_gate", "w_pli_proj")


def _tile(n, cap):
    t = min(n, cap) // LANE * LANE
    while t >= LANE:
        if n % t == 0:
            return t
        t -= LANE
    return n


def _params(sem=None):
    return pltpu.CompilerParams(dimension_semantics=sem, vmem_limit_bytes=V7X_VMEM_LIMIT)


def _rowwise(fn, rows, vecs, outs, sums, name, tr, after=()):
    s = rows[0].shape[0]
    nr, nv, no, ns, na = len(rows), len(vecs), len(outs), len(sums), len(after)

    def body(*refs):
        ins = [r[...] for r in refs[:nr + nv]]
        res = fn(*ins)
        out_refs = refs[nr + nv + na:nr + nv + na + no]
        sum_refs = refs[nr + nv + na + no:]
        for o_ref, val in zip(out_refs, res[:no]):
            o_ref[...] = val.astype(o_ref.dtype)
        if ns:
            @pl.when(pl.program_id(0) == 0)
            def _():
                for s_ref in sum_refs:
                    s_ref[...] = jnp.zeros_like(s_ref)
            for s_ref, val in zip(sum_refs, res[no:]):
                s_ref[...] += jnp.sum(val, axis=0, keepdims=True)

    in_specs = [pl.BlockSpec((tr, r.shape[1]), lambda i: (i, 0)) for r in rows]
    in_specs += [pl.BlockSpec(v.shape, lambda i: (0, 0)) for v in vecs]
    in_specs += [pl.BlockSpec(memory_space=pl.ANY)] * na
    out_specs = [pl.BlockSpec((tr, c), lambda i: (i, 0)) for c, _ in outs]
    out_specs += [pl.BlockSpec((1, c), lambda i: (0, 0)) for c in sums]
    out_shape = [jax.ShapeDtypeStruct((s, c), dt) for c, dt in outs]
    out_shape += [jax.ShapeDtypeStruct((1, c), F32) for c in sums]
    return pl.pallas_call(
        body, grid=(s // tr,), in_specs=in_specs, out_specs=out_specs, out_shape=out_shape,
        compiler_params=_params(("arbitrary",) if ns else ("parallel",)), name=name,
    )(*rows, *vecs, *after)


def _rms_r(x):
    return lax.rsqrt(jnp.mean(x * x, axis=-1, keepdims=True) + RMS_EPS)


def _rms_bwd(x, g, dy):
    r = _rms_r(x)
    u = dy * g
    dx = r * (u - x * (r * r) * jnp.mean(u * x, axis=-1, keepdims=True))
    return dx, dy * x * r


def _sigmoid(z):
    return 1.0 / (1.0 + jnp.exp(-z))


def _norm_in(x, g, tr):
    d = x.shape[1]
    return _rowwise(lambda x, g: (x * _rms_r(x) * g,), [x], [g], [(d, BF16)], [], "norm_in", tr)[0]


def _norm_in_bwd(dx_res, dh, x, g, tr):
    d = x.shape[1]

    def fn(dx_res, dh, x, g):
        dx, dg = _rms_bwd(x, g, dh)
        return dx_res + dx, dg
    return _rowwise(fn, [dx_res, dh, x], [g], [(d, F32)], [d], "norm_in_bwd", tr)


def _headnorm(o_sb, o_dl, g, tr):
    d = g.shape[1]

    def fn(o_sb, o_dl, g):
        o = jnp.concatenate([o_sb, o_dl], axis=1)
        parts = []
        for h in range(d // HEAD_DIM):
            sl = slice(h * HEAD_DIM, (h + 1) * HEAD_DIM)
            oh = o[:, sl]
            parts.append(oh * _rms_r(oh) * g[:, sl])
        return (jnp.concatenate(parts, axis=1),)
    return _rowwise(fn, [o_sb, o_dl], [g], [(d, BF16)], [], "headnorm", tr)[0]


def _headnorm_bwd(don, o_sb, o_dl, g, tr, after=()):
    d = g.shape[1]
    n_sb = o_sb.shape[1]

    def fn(don, o_sb, o_dl, g):
        o = jnp.concatenate([o_sb, o_dl], axis=1)
        dos, dgs = [], []
        for h in range(d // HEAD_DIM):
            sl = slice(h * HEAD_DIM, (h + 1) * HEAD_DIM)
            dx, dg = _rms_bwd(o[:, sl], g[:, sl], don[:, sl])
            dos.append(dx)
            dgs.append(dg)
        do = jnp.concatenate(dos, axis=1)
        return do[:, :n_sb], do[:, n_sb:], jnp.concatenate(dgs, axis=1)
    return _rowwise(fn, [don, o_sb, o_dl], [g], [(n_sb, F32), (d - n_sb, F32)], [d], "headnorm_bwd", tr, after)


def _res_norm(x, y, g_post, g_pre, name, tr, after=()):
    d = x.shape[1]

    def fn(x, y, g_post, g_pre):
        x2 = x + y * _rms_r(y) * g_post
        return x2, x2 * _rms_r(x2) * g_pre
    return _rowwise(fn, [x, y], [g_post, g_pre], [(d, F32), (d, BF16)], [], name, tr, after)


def _res_norm_bwd(dx_res, dh, x2, y, g_pre, g_post, name, tr):
    d = x2.shape[1]

    def fn(dx_res, dh, x2, y, g_pre, g_post):
        dxa, dg_pre = _rms_bwd(x2, g_pre, dh)
        dx2 = dx_res + dxa
        dy, dg_post = _rms_bwd(y, g_post, dx2)
        return dx2, dy, dg_pre, dg_post
    return _rowwise(fn, [dx_res, dh, x2, y], [g_pre, g_post], [(d, F32), (d, BF16)], [d, d], name, tr)


def _swiglu(gu, tr):
    ff = gu.shape[1] // 2

    def fn(gu):
        g, u = gu[:, :ff].astype(F32), gu[:, ff:].astype(F32)
        return (g * _sigmoid(g) * u,)
    return _rowwise(fn, [gu], [], [(ff, BF16)], [], "swiglu", tr)[0]


def _swiglu_bwd(dact, gu, tr):
    ff = gu.shape[1] // 2

    def fn(dact, gu):
        g, u = gu[:, :ff].astype(F32), gu[:, ff:].astype(F32)
        sg = _sigmoid(g)
        dg = dact * u * sg * (1.0 + g * (1.0 - sg))
        du = dact * g * sg
        return (jnp.concatenate([dg, du], axis=1),)
    return _rowwise(fn, [dact, gu], [], [(2 * ff, BF16)], [], "swiglu_bwd", tr)[0]


def _pli_out(x3, gl, pp, g_next, tr):
    d = x3.shape[1]

    def fn(x3, gl, pp, g):
        x4 = x3 + _sigmoid(gl) * pp
        return x4, x4 * _rms_r(x4) * g
    return _rowwise(fn, [x3, gl, pp], [g_next], [(d, F32), (d, BF16)], [], "pli_out", tr)


def _loss_head(x3, gl, pp, target, tr):
    d = x3.shape[1]

    def fn(x3, gl, pp, t):
        err = x3 + _sigmoid(gl) * pp - t
        sq = err * err
        part = sq[:, :LANE]
        for k in range(1, d // LANE):
            part = part + sq[:, k * LANE:(k + 1) * LANE]
        return err * (1.0 / d), part
    return _rowwise(fn, [x3, gl, pp, target], [], [(d, F32)], [LANE], "loss_head", tr)


def _pli_bwd(dx, gl, pp, tr, after=()):
    d = dx.shape[1]

    def fn(dx, gl, pp):
        gate = _sigmoid(gl)
        return dx * gate, dx * pp * gate * (1.0 - gate)
    return _rowwise(fn, [dx, gl, pp], [], [(d, BF16), (d, BF16)], [], "pli_bwd", tr, after)


def _my_chip():
    return 2 * lax.axis_index("x") + lax.axis_index("y")


def _cast_layer(w, layer, name):
    _, r, c = w.shape
    tr = r
    while tr * c * 4 > (4 << 20) and tr % 32 == 0:
        tr //= 2

    def body(w_ref, o_ref):
        o_ref[...] = w_ref[...].astype(o_ref.dtype)

    return pl.pallas_call(
        body, grid=(r // tr,),
        in_specs=[pl.BlockSpec((None, tr, c), lambda i: (layer, i, 0))],
        out_specs=pl.BlockSpec((None, tr, c), lambda i: (_my_chip(), i, 0)),
        out_shape=jax.ShapeDtypeStruct((N_CHIPS, r, c), BF16),
        compiler_params=_params(("parallel",)), name="cast_" + name,
    )(w)


def _reduce_piece(dw, recv, name, layer, depth, into):
    _, r, c = dw.shape
    h = r // 2
    tr = h
    while tr * c * 2 * N_DEV > (8 << 20) and tr % 32 == 0:
        tr //= 2
    per = h // tr

    def body(d_ref, r_ref, *rest):
        acc = d_ref[...].astype(F32)
        for i in range(N_DEV - 1):
            acc = acc + r_ref[i].astype(F32)
        rest[-1][...] = acc

    kept = [] if into is None else [into]
    return pl.pallas_call(
        body, grid=(per,),
        in_specs=[pl.BlockSpec((None, tr, c), lambda i: (_my_chip(), lax.axis_index("c") * per + i, 0)),
                  pl.BlockSpec((N_DEV - 1, tr, c), lambda i: (0, i, 0))] + [pl.BlockSpec(memory_space=pl.ANY)] * len(kept),
        out_specs=pl.BlockSpec((None, tr, c), lambda i: (layer, lax.axis_index("c") * per + i, 0)),
        out_shape=jax.ShapeDtypeStruct((depth, r, c), F32),
        input_output_aliases={2: 0} if kept else {},
        compiler_params=_params(("parallel",)), name="reduce_" + name,
    )(dw, recv, *kept)


def _adamw(w, g, m, v, name):
    shape = w.shape
    if w.ndim == 3:
        w, g, m, v = (a.reshape(shape[0] * shape[1], shape[2]) for a in (w, g, m, v))
    r, c = w.shape
    tr = r
    while tr * c * 4 > (1 << 20) and tr % 16 == 0:
        tr //= 2

    def body(w_ref, g_ref, m_ref, v_ref, d_ref, m2_ref, v2_ref, g2_ref):
        g = g_ref[...]
        m2 = ADAM_B1 * m_ref[...] + (1.0 - ADAM_B1) * g
        v2 = ADAM_B2 * v_ref[...] + (1.0 - ADAM_B2) * (g * g)
        m_hat = m2 / (1.0 - ADAM_B1 ** ADAM_STEP)
        v_hat = v2 / (1.0 - ADAM_B2 ** ADAM_STEP)
        d_ref[...] = -ADAM_LR * (m_hat / (jnp.sqrt(v_hat) + ADAM_EPS) + ADAM_WD * w_ref[...])
        m2_ref[...] = m2
        v2_ref[...] = v2
        g2_ref[...] = g

    spec = pl.BlockSpec((tr, c), lambda i: (i, 0))
    res = pl.pallas_call(
        body, grid=(r // tr,), in_specs=[spec] * 4, out_specs=[spec] * 4,
        out_shape=[jax.ShapeDtypeStruct((r, c), F32)] * 4,
        compiler_params=_params(("parallel",)), name="adamw_" + name,
    )(w, g, m, v)
    return tuple(a.reshape(shape) for a in res)


def _mm(a, b, grid, a_spec, b_spec, o_spec, o_shape, o_dtype, dims, acc_shape, name):
    nk = grid[2]

    def whole(a_ref, b_ref, o_ref):
        b = b_ref[...]
        if b.ndim == 3:
            b = b.reshape(b.shape[0] * b.shape[1], b.shape[2])
        o_ref[...] = lax.dot_general(a_ref[...].astype(BF16), b.astype(BF16), dims,
                                     preferred_element_type=F32).astype(o_ref.dtype)

    def body(a_ref, b_ref, o_ref, acc_ref):
        k = pl.program_id(2)

        @pl.when(k == 0)
        def _():
            acc_ref[...] = jnp.zeros_like(acc_ref)

        acc_ref[...] += lax.dot_general(a_ref[...].astype(BF16), b_ref[...].astype(BF16), dims,
                                        preferred_element_type=F32)

        @pl.when(k == nk - 1)
        def _():
            o_ref[...] = acc_ref[...].astype(o_ref.dtype)

    return pl.pallas_call(
        whole if nk == 1 else body, grid=grid, in_specs=[a_spec, b_spec], out_specs=o_spec,
        out_shape=jax.ShapeDtypeStruct(o_shape, o_dtype),
        scratch_shapes=[] if nk == 1 else [pltpu.VMEM(acc_shape, F32)],
        compiler_params=_params(("parallel", "parallel", "arbitrary")), name=name,
    )(a, b)


def _mm_fwd(a, wg, col, name, tm=1024, tn_cap=1536, tk_cap=1024, out_dtype=F32):
    m, k = a.shape
    ns, r, c = wg.shape
    tm = min(tm, m)
    if col:
        n, tn, tk = ns * c, _tile(c, tn_cap), _tile(k, tk_cap)
        per = c // tn
        b_spec = pl.BlockSpec((None, tk, tn), lambda j, i, kk: (j // per, kk, j % per))
    elif tk_cap >= k:
        n, tn, tk = c, _tile(c, tn_cap), k
        b_spec = pl.BlockSpec((ns, r, tn), lambda j, i, kk: (0, 0, j))
    else:
        n, tn, tk = c, _tile(c, tn_cap), _tile(r, tk_cap)
        per = r // tk
        b_spec = pl.BlockSpec((None, tk, tn), lambda j, i, kk: (kk // per, kk % per, j))
    return _mm(a, wg, (n // tn, m // tm, k // tk), pl.BlockSpec((tm, tk), lambda j, i, kk: (i, kk)), b_spec,
               pl.BlockSpec((tm, tn), lambda j, i, kk: (i, j)), (m, n), out_dtype, NN, (tm, tn), name)


def _mm_dgrad(dc, wg, col, name, tm=1024, to_cap=1536, tc_cap=2048):
    m, n = dc.shape
    ns, r, c = wg.shape
    tm = min(tm, m)
    if col:
        kout, to, tc = r, _tile(r, to_cap), _tile(c, tc_cap)
        per = c // tc
        b_spec = pl.BlockSpec((None, to, tc), lambda j, i, kk: (kk // per, j, kk % per))
    elif to_cap >= ns * r and tc_cap >= c:
        kout, to, tc, tm = ns * r, ns * r, c, min(tm, 512)
        b_spec = pl.BlockSpec((ns, r, tc), lambda j, i, kk: (0, 0, 0))
    else:
        kout, to, tc = ns * r, _tile(r, to_cap), _tile(c, tc_cap)
        per = r // to
        b_spec = pl.BlockSpec((None, to, tc), lambda j, i, kk: (j // per, j % per, kk))
    return _mm(dc, wg, (kout // to, m // tm, n // tc), pl.BlockSpec((tm, tc), lambda j, i, kk: (i, kk)), b_spec,
               pl.BlockSpec((tm, to), lambda j, i, kk: (i, j)), (m, kout), F32, NT, (tm, to), name)


def _mm_wgrad(a, dc, col, name, ti_cap=1536, tn_cap=1536, tkm=2048):
    m, k = a.shape
    n = dc.shape[1]
    tkm = min(tkm, m)
    if col:
        r, c = k, n // N_CHIPS
        ti, tn = _tile(r, ti_cap), _tile(c, tn_cap)
        per = c // tn
        o_spec = pl.BlockSpec((None, ti, tn), lambda i, j, kk: (j // per, i, j % per))
    else:
        r, c = k // N_CHIPS, n
        ti, tn = _tile(r, ti_cap), _tile(c, tn_cap)
        per = r // ti
        o_spec = pl.BlockSpec((None, ti, tn), lambda i, j, kk: (i // per, i % per, j))
    return _mm(a, dc, (k // ti, n // tn, m // tkm), pl.BlockSpec((tkm, ti), lambda i, j, kk: (kk, i)),
               pl.BlockSpec((tkm, tn), lambda i, j, kk: (kk, j)), o_spec, (N_CHIPS, r, c), BF16, TN, (ti, tn), name)


def _log_keep(z):
    return -(jnp.maximum(z, 0.0) + jnp.log(1.0 + jnp.exp(-jnp.abs(z))))


def _split_dot(x, t):
    hi = x.astype(BF16)
    lo = (x - hi.astype(F32)).astype(BF16)
    return (lax.dot_general(hi, t, NN, preferred_element_type=F32)
            + lax.dot_general(lo, t, NN, preferred_element_type=F32))


def _sb_fwd(proj, n_sb, bq):
    s = proj.shape[0]
    scale = 1.0 / math.sqrt(HEAD_DIM)

    def body(q_ref, k_ref, v_ref, o_ref, lt_ref):
        i = pl.program_id(1)
        q = q_ref[...].astype(BF16)
        row = lax.broadcasted_iota(jnp.int32, (bq, bq), 0)
        col = lax.broadcasted_iota(jnp.int32, (bq, bq), 1)
        later_in_block = (row > col).astype(BF16)
        keep = col < row

        def blocks(js, carry, diagonal=False):
            c, acc = carry
            sl = [pl.ds(pl.multiple_of(j * bq, bq), bq) for j in js]
            masked = [diagonal and n == 0 for n in range(len(js))]
            kbs = [k_ref[s_, :].astype(BF16) for s_ in sl]
            zs = [lax.dot_general(q, kb, NT, preferred_element_type=F32) * scale for kb in kbs]
            lks = [jnp.where(keep, _log_keep(z), 0.0) if m else _log_keep(z) for z, m in zip(zs, masked)]
            within = [_split_dot(lk, later_in_block) for lk in lks]
            es = []
            for z, lk, w, m in zip(zs, lks, within, masked):
                e = z + lk + w + c
                es.append(jnp.where(keep, e, NEG) if m else e)
                c = c + jnp.sum(lk, axis=1, keepdims=True)
            for e, s_ in zip(es, sl):
                acc = acc + lax.dot_general(jnp.exp(e).astype(BF16), v_ref[s_, :].astype(BF16), NN,
                                            preferred_element_type=F32)
            return c, acc

        first = [functools.partial(blocks, [i - u for u in range(g + 1)], diagonal=True) for g in range(4)]
        carry = lax.switch(i % 4, first, (jnp.zeros((bq, 1), F32), jnp.zeros((bq, HEAD_DIM), F32)))
        top = i - i % 4 - 1
        carry = lax.fori_loop(0, i // 4, lambda t, cr: blocks([top - 4 * t - u for u in range(4)], cr), carry)
        o_ref[...] = carry[1]
        lt_ref[...] = jnp.broadcast_to(carry[0], (bq, HEAD_DIM))

    blk = pl.BlockSpec((bq, HEAD_DIM), lambda h, i: (i, h))
    shp = jax.ShapeDtypeStruct((s, n_sb * HEAD_DIM), F32)
    return pl.pallas_call(
        body, grid=(n_sb, s // bq),
        in_specs=[blk,
                  pl.BlockSpec((s, HEAD_DIM), lambda h, i: (0, n_sb + h)),
                  pl.BlockSpec((s, HEAD_DIM), lambda h, i: (0, 2 * n_sb + h))],
        out_specs=[blk, blk], out_shape=[shp, shp],
        compiler_params=_params(("parallel", "parallel")), name="sb_fwd",
    )(proj, proj, proj)


def _sb_bwd(proj, lt, do, n_sb, bq):
    s = proj.shape[0]
    nq = s // bq
    scale = 1.0 / math.sqrt(HEAD_DIM)

    def body(q_ref, k_ref, v_ref, lt_ref, do_ref, dq_ref, dk_ref, dv_ref, dk_acc, dv_acc):
        i = pl.program_id(1)

        @pl.when(i == 0)
        def _():
            dk_acc[...] = jnp.zeros_like(dk_acc)
            dv_acc[...] = jnp.zeros_like(dv_acc)

        q = q_ref[...].astype(BF16)
        do_b = do_ref[...].astype(BF16)
        ltot = jnp.max(lt_ref[...], axis=1, keepdims=True)
        row = lax.broadcasted_iota(jnp.int32, (bq, bq), 0)
        col = lax.broadcasted_iota(jnp.int32, (bq, bq), 1)
        upto_in_block = (row <= col).astype(BF16)
        before_in_block = (row < col).astype(BF16)
        keep = col < row

        def blocks(js, carry, diagonal=False):
            pk, pg, dq = carry
            sl = [pl.ds(pl.multiple_of(j * bq, bq), bq) for j in js]
            masked = [diagonal and n == len(js) - 1 for n in range(len(js))]
            kbs = [k_ref[s_, :].astype(BF16) for s_ in sl]
            zs = [lax.dot_general(q, kb, NT, preferred_element_type=F32) * scale for kb in kbs]
            das = [lax.dot_general(do_b, v_ref[s_, :].astype(BF16), NT, preferred_element_type=F32) for s_ in sl]
            lks = [jnp.where(keep, _log_keep(z), 0.0) if m else _log_keep(z) for z, m in zip(zs, masked)]
            upto = [_split_dot(lk, upto_in_block) for lk in lks]
            gs, abs_ = [], []
            for z, lk, u, da, m in zip(zs, lks, upto, das, masked):
                e = z + lk + ((ltot - pk) - u)
                a = jnp.exp(jnp.where(keep, e, NEG) if m else e)
                gs.append(a * da)
                abs_.append(a.astype(BF16))
                pk = pk + jnp.sum(lk, axis=1, keepdims=True)
            for a_b, s_ in zip(abs_, sl):
                dv_acc[s_, :] += lax.dot_general(a_b, do_b, TN, preferred_element_type=F32)
            before = [lax.dot_general(g.astype(BF16), before_in_block, NN, preferred_element_type=F32) for g in gs]
            dzs = []
            for z, lk, g, bf, m in zip(zs, lks, gs, before, masked):
                keep_p = jnp.exp(lk)
                dz = g * keep_p - (pg + bf) * (1.0 - keep_p)
                dzs.append(((jnp.where(keep, dz, 0.0) if m else dz) * scale).astype(BF16))
                pg = pg + jnp.sum(g, axis=1, keepdims=True)
            for dz_b, kb in zip(dzs, kbs):
                dq = dq + lax.dot_general(dz_b, kb, NN, preferred_element_type=F32)
            for dz_b, s_ in zip(dzs, sl):
                dk_acc[s_, :] += lax.dot_general(dz_b, q, TN, preferred_element_type=F32)
            return pk, pg, dq

        zero = jnp.zeros((bq, 1), F32)
        carry = lax.fori_loop(0, i // 4, lambda t, cr: blocks([4 * t + u for u in range(4)], cr),
                              (zero, zero, jnp.zeros((bq, HEAD_DIM), F32)))
        last = [functools.partial(blocks, [i - g + u for u in range(g + 1)], diagonal=True) for g in range(4)]
        carry = lax.switch(i % 4, last, carry)
        dq_ref[...] = carry[2].astype(dq_ref.dtype)

        @pl.when(i == nq - 1)
        def _():
            dk_ref[...] = dk_acc[...].astype(dk_ref.dtype)
            dv_ref[...] = dv_acc[...].astype(dv_ref.dtype)

    blk = pl.BlockSpec((bq, HEAD_DIM), lambda h, i: (i, h))
    full = pl.BlockSpec((s, HEAD_DIM), lambda h, i: (0, h))
    shp = jax.ShapeDtypeStruct((s, n_sb * HEAD_DIM), BF16)
    return pl.pallas_call(
        body, grid=(n_sb, nq),
        in_specs=[blk,
                  pl.BlockSpec((s, HEAD_DIM), lambda h, i: (0, n_sb + h)),
                  pl.BlockSpec((s, HEAD_DIM), lambda h, i: (0, 2 * n_sb + h)),
                  blk, blk],
        out_specs=[blk, full, full], out_shape=[shp, shp, shp],
        scratch_shapes=[pltpu.VMEM((s, HEAD_DIM), F32), pltpu.VMEM((s, HEAD_DIM), F32)],
        compiler_params=_params(("parallel", "arbitrary")), name="sb_bwd",
    )(proj, proj, proj, lt, do)


def _t5_bucket(dist):
    max_exact = NUM_BUCKETS // 2
    d = jnp.maximum(dist, 1).astype(F32)
    large = max_exact + (jnp.log(d / max_exact) / math.log(MAX_DISTANCE / max_exact)
                         * (NUM_BUCKETS - max_exact)).astype(jnp.int32)
    large = jnp.minimum(large, NUM_BUCKETS - 1)
    return jnp.where(dist < max_exact, dist, large)


def _dil_tables(rel_bias):
    qi = jnp.arange(HEAD_DIM, dtype=jnp.int32)[:, None]
    ki = jnp.arange(2 * HEAD_DIM, dtype=jnp.int32)[None, :]
    rel = HEAD_DIM + qi - ki
    band = (rel >= 0) & (rel <= HEAD_DIM)
    biases, buckets = [], []
    for d in DILATIONS:
        bucket = _t5_bucket(jnp.maximum(rel, 0) * d)
        onehot = (bucket[:, :, None] == jnp.arange(NUM_BUCKETS, dtype=jnp.int32)).astype(F32)
        bias = jnp.einsum("qkb,bh->hqk", onehot, rel_bias.astype(F32), precision=lax.Precision.HIGHEST)
        biases.append(jnp.where(band[None], bias, NEG))
        buckets.append(jnp.where(band, bucket, -1).astype(F32))
    return jnp.stack(biases, axis=1), jnp.stack(buckets, axis=0)


def _sub_rows(ref, start, d):
    if d == 1:
        return ref[pl.ds(pl.multiple_of(start, HEAD_DIM), HEAD_DIM), :]
    return ref[pl.ds(start, HEAD_DIM, stride=d), :]


def _sub_idx(start, d):
    if d == 1:
        return pl.ds(pl.multiple_of(start, HEAD_DIM), HEAD_DIM)
    return pl.ds(start, HEAD_DIM, stride=d)


def _dil_logits(q_ref, k_ref, bm, n, cur, prv, d, scale):
    qb = _sub_rows(q_ref, cur, d).astype(BF16)
    kk = jnp.concatenate([_sub_rows(k_ref, prv, d), _sub_rows(k_ref, cur, d)], axis=0).astype(BF16)
    sc = lax.dot_general(qb, kk, NT, preferred_element_type=F32) * scale + bm
    colk = lax.broadcasted_iota(jnp.int32, sc.shape, 1)
    sc = jnp.where((colk >= HEAD_DIM) | (n > 0), sc, NEG)
    return qb, kk, sc


def _dil_fwd(proj, bm, n_sb, n_dl, after=()):
    s = proj.shape[0]
    scale = 1.0 / math.sqrt(HEAD_DIM)
    chunk = min(s, 512)
    na = len(after)

    def body(q_ref, k_ref, v_ref, bm_ref, *rest):
        o_ref, l_ref, ob0, ob1, ob2, lb0, lb1, lb2 = rest[na:]
        obs, lbs = (ob0, ob1, ob2), (lb0, lb1, lb2)
        for b, d in enumerate(DILATIONS):
            nb = s // (HEAD_DIM * d)

            def group(g, _, b=b, d=d, nb=nb):
                where = []
                for u in range(DIL_UNROLL):
                    idx = g * DIL_UNROLL + u
                    r, n = idx // nb, idx % nb
                    where.append((n, n * (HEAD_DIM * d) + r, jnp.maximum(n - 1, 0) * (HEAD_DIM * d) + r))
                bm = bm_ref[b]
                scs = [_dil_logits(q_ref, k_ref, bm, n, cur, prv, d, scale)[2] for n, cur, prv in where]
                mxs = [jnp.max(sc, axis=1, keepdims=True) for sc in scs]
                prs = [jnp.exp(sc - mx) for sc, mx in zip(scs, mxs)]
                dens = [jnp.sum(pr, axis=1, keepdims=True) for pr in prs]
                outs = [lax.dot_general(
                    pr.astype(BF16),
                    jnp.concatenate([_sub_rows(v_ref, prv, d), _sub_rows(v_ref, cur, d)], axis=0).astype(BF16),
                    NN, preferred_element_type=F32) for pr, (n, cur, prv) in zip(prs, where)]
                for o, mx, den, (n, cur, prv) in zip(outs, mxs, dens, where):
                    obs[b][_sub_idx(cur, d), :] = o / den
                    lbs[b][_sub_idx(cur, d), :] = jnp.broadcast_to(mx + jnp.log(den), (HEAD_DIM, HEAD_DIM))
                return 0

            lax.fori_loop(0, s // HEAD_DIM // DIL_UNROLL, group, 0)

        for ci in range(s // chunk):
            sl = pl.ds(ci * chunk, chunk)
            l0, l1, l2 = lb0[sl, :], lb1[sl, :], lb2[sl, :]
            mx = jnp.maximum(jnp.maximum(l0, l1), l2)
            w0, w1, w2 = jnp.exp(l0 - mx), jnp.exp(l1 - mx), jnp.exp(l2 - mx)
            tot = w0 + w1 + w2
            o_ref[sl, :] = (w0 * ob0[sl, :] + w1 * ob1[sl, :] + w2 * ob2[sl, :]) / tot
            l_ref[sl, :] = mx + jnp.log(tot)

    base = 3 * n_sb
    full = pl.BlockSpec((s, HEAD_DIM), lambda h: (0, h))
    shp = jax.ShapeDtypeStruct((s, n_dl * HEAD_DIM), F32)
    return pl.pallas_call(
        body, grid=(n_dl,),
        in_specs=[pl.BlockSpec((s, HEAD_DIM), lambda h: (0, base + h)),
                  pl.BlockSpec((s, HEAD_DIM), lambda h: (0, base + n_dl + h)),
                  pl.BlockSpec((s, HEAD_DIM), lambda h: (0, base + 2 * n_dl + h)),
                  pl.BlockSpec((None, 3, HEAD_DIM, 2 * HEAD_DIM), lambda h: (h, 0, 0, 0))]
        + [pl.BlockSpec(memory_space=pl.ANY)] * na,
        out_specs=[full, full], out_shape=[shp, shp],
        scratch_shapes=[pltpu.VMEM((s, HEAD_DIM), F32)] * 6,
        compiler_params=_params(("parallel",)), name="dil_fwd",
    )(proj, proj, proj, bm, *after)


def _dil_bwd(proj, do, o, lse, bm, n_sb, n_dl):
    s = proj.shape[0]
    scale = 1.0 / math.sqrt(HEAD_DIM)
    chunk = min(s, 512)

    def body(q_ref, k_ref, v_ref, do_ref, o_ref, l_ref, bm_ref, dq_ref, dk_ref, dv_ref, ds_ref, dq_s, dk_s, dv_s):
        dq_s[...] = jnp.zeros_like(dq_s)
        dk_s[...] = jnp.zeros_like(dk_s)
        dv_s[...] = jnp.zeros_like(dv_s)
        ds_ref[...] = jnp.zeros_like(ds_ref)
        for b, d in enumerate(DILATIONS):
            nb = s // (HEAD_DIM * d)

            def group(g, _, b=b, d=d, nb=nb):
                where = []
                for u in range(DIL_UNROLL):
                    idx = g * DIL_UNROLL + u
                    r, n = idx // nb, idx % nb
                    where.append((n, n * (HEAD_DIM * d) + r, jnp.maximum(n - 1, 0) * (HEAD_DIM * d) + r))
                bm = bm_ref[b]
                logits = [_dil_logits(q_ref, k_ref, bm, n, cur, prv, d, scale) for n, cur, prv in where]
                do_fs = [_sub_rows(do_ref, cur, d) for n, cur, prv in where]
                do_bs = [do_f.astype(BF16) for do_f in do_fs]
                dps = [lax.dot_general(
                    do_b, jnp.concatenate([_sub_rows(v_ref, prv, d), _sub_rows(v_ref, cur, d)], axis=0).astype(BF16),
                    NT, preferred_element_type=F32) for do_b, (n, cur, prv) in zip(do_bs, where)]
                ws, dss = [], []
                for (qb, kk, sc), do_f, dp, (n, cur, prv) in zip(logits, do_fs, dps, where):
                    delta = jnp.sum(do_f * _sub_rows(o_ref, cur, d), axis=1, keepdims=True)
                    lr = _sub_rows(l_ref, cur, d)
                    w = jnp.exp(sc - jnp.concatenate([lr, lr], axis=1))
                    ws.append(w)
                    dss.append(w * (dp - delta))
                ds_bs = [(ds * scale).astype(BF16) for ds in dss]
                dv_blks = [lax.dot_general(w.astype(BF16), do_b, TN, preferred_element_type=F32)
                           for w, do_b in zip(ws, do_bs)]
                dk_blks = [lax.dot_general(ds_b, qb, TN, preferred_element_type=F32)
                           for ds_b, (qb, kk, sc) in zip(ds_bs, logits)]
                dq_blks = [lax.dot_general(ds_b, kk, NN, preferred_element_type=F32)
                           for ds_b, (qb, kk, sc) in zip(ds_bs, logits)]
                total = dss[0]
                for ds in dss[1:]:
                    total = total + ds
                ds_ref[b] += total
                for dq_blk, dk_blk, dv_blk, (n, cur, prv) in zip(dq_blks, dk_blks, dv_blks, where):
                    ci, pi = _sub_idx(cur, d), _sub_idx(prv, d)
                    dq_s[ci, :] += dq_blk
                    dk_s[ci, :] += dk_blk[HEAD_DIM:]
                    dv_s[ci, :] += dv_blk[HEAD_DIM:]
                    dk_s[pi, :] += dk_blk[:HEAD_DIM]
                    dv_s[pi, :] += dv_blk[:HEAD_DIM]
                return 0

            lax.fori_loop(0, s // HEAD_DIM // DIL_UNROLL, group, 0)

        for ci in range(s // chunk):
            sl = pl.ds(ci * chunk, chunk)
            dq_ref[sl, :] = dq_s[sl, :].astype(dq_ref.dtype)
            dk_ref[sl, :] = dk_s[sl, :].astype(dk_ref.dtype)
            dv_ref[sl, :] = dv_s[sl, :].astype(dv_ref.dtype)

    base = 3 * n_sb
    full = pl.BlockSpec((s, HEAD_DIM), lambda h: (0, h))
    tab = pl.BlockSpec((None, 3, HEAD_DIM, 2 * HEAD_DIM), lambda h: (h, 0, 0, 0))
    shp = jax.ShapeDtypeStruct((s, n_dl * HEAD_DIM), BF16)
    return pl.pallas_call(
        body, grid=(n_dl,),
        in_specs=[pl.BlockSpec((s, HEAD_DIM), lambda h: (0, base + h)),
                  pl.BlockSpec((s, HEAD_DIM), lambda h: (0, base + n_dl + h)),
                  pl.BlockSpec((s, HEAD_DIM), lambda h: (0, base + 2 * n_dl + h)),
                  full, full, full, tab],
        out_specs=[full, full, full, tab],
        out_shape=[shp, shp, shp, jax.ShapeDtypeStruct((n_dl, 3, HEAD_DIM, 2 * HEAD_DIM), F32)],
        scratch_shapes=[pltpu.VMEM((s, HEAD_DIM), F32)] * 3,
        compiler_params=_params(("parallel",)), name="dil_bwd",
    )(proj, proj, proj, do, o, lse, bm)


def _rel_bias_grad(ds_all, buckets):
    depth, n_dl = ds_all.shape[:2]
    rows = -(-n_dl // 8) * 8

    def body(ds_ref, bk_ref, o_ref):
        lane = lax.broadcasted_iota(jnp.int32, (1, LANE), 1)

        def one_bucket(bkt, acc):
            fb = bkt.astype(F32)
            out = []
            for h in range(n_dl):
                val = jnp.zeros((1, 1), F32)
                for b in range(3):
                    tot = ds_ref[0, h, b]
                    for l in range(1, depth):
                        tot = tot + ds_ref[l, h, b]
                    val = val + jnp.sum(jnp.where(bk_ref[b] == fb, tot, 0.0), keepdims=True)
                out.append(jnp.where(lane == bkt, val, 0.0))
            out += [jnp.zeros((1, LANE), F32)] * (rows - n_dl)
            return acc + jnp.concatenate(out, axis=0)

        o_ref[...] = lax.fori_loop(0, NUM_BUCKETS, one_bucket, jnp.zeros((rows, LANE), F32))

    return pl.pallas_call(
        body, out_shape=jax.ShapeDtypeStruct((rows, LANE), F32),
        in_specs=[pl.BlockSpec(memory_space=pltpu.VMEM)] * 2, out_specs=pl.BlockSpec(memory_space=pltpu.VMEM),
        compiler_params=_params(), name="rel_bias_grad",
    )(ds_all, buckets)


def _place():
    return lax.axis_index("x"), lax.axis_index("y"), lax.axis_index("c")


def _flip(v, bit):
    return 1 - v if bit else v


HBM_SPEC = pl.BlockSpec(memory_space=pl.ANY)


HBM_ONLY = pl.BlockSpec(memory_space=pltpu.HBM)
SEM_SPEC = pl.BlockSpec(memory_space=pltpu.SEMAPHORE)
DATAFLOW = pltpu.SideEffectType.DATAFLOW_SIDE_EFFECTING


def _in_hbm(a):
    return pltpu.with_memory_space_constraint(a, pltpu.HBM)


def _split_start(arrays, token, copies_of, n_sem, name):
    na = len(arrays)

    def body(*refs):
        for cp in copies_of(refs[:na], refs[na + 1], refs[na + 2]):
            cp.start()
        refs[-1][...] = jnp.zeros_like(refs[-1])

    sems = pltpu.SemaphoreType.DMA((n_sem,))
    res = pl.pallas_call(
        body, name=name,
        out_shape=(sems, sems, *[pltpu.HBM(a.shape, a.dtype) for a in arrays], jax.ShapeDtypeStruct((8, LANE), F32)),
        in_specs=[HBM_ONLY] * na + [HBM_SPEC],
        out_specs=(SEM_SPEC, SEM_SPEC, *[HBM_ONLY] * na, pl.BlockSpec(memory_space=pltpu.VMEM)),
        input_output_aliases={i: 2 + i for i in range(na)},
        compiler_params=pltpu.CompilerParams(has_side_effects=DATAFLOW),
    )(*[_in_hbm(a) for a in arrays], token)
    return res[0], res[1], res[2:2 + na], res[-1]


def _split_wait(started, after, copies_of, name):
    send, recv, arrays, _ = started
    na = len(arrays)

    def body(*refs):
        for cp in copies_of(refs[:na], refs[na], refs[na + 1]):
            cp.wait_send()
            cp.wait_recv()

    return pl.pallas_call(
        body, name=name, out_shape=[pltpu.HBM(a.shape, a.dtype) for a in arrays],
        in_specs=[HBM_ONLY] * na + [SEM_SPEC, SEM_SPEC, HBM_SPEC], out_specs=[HBM_ONLY] * na,
        input_output_aliases={i: i for i in range(na)},
        compiler_params=pltpu.CompilerParams(has_side_effects=DATAFLOW),
    )(*arrays, send, recv, after)


def _gather_ici_copies(buf, send, recv):
    x, y, c = _place()
    out = []
    for w in range(len(buf)):
        h = buf[w].shape[1] // 2
        mine = buf[w].at[2 * x + y, pl.ds(c * h, h)]
        for k in (1, 2, 3):
            out.append(pltpu.make_async_remote_copy(
                src_ref=mine, dst_ref=mine, send_sem=send.at[4 * w + k], recv_sem=recv.at[4 * w + k],
                device_id=(_flip(x, k >> 1), _flip(y, k & 1), c), device_id_type=MESH))
    return out


def _gather_d2d_copies(buf, send, recv):
    x, y, c = _place()
    out = []
    for w in range(len(buf)):
        h = buf[w].shape[1] // 2
        for k in (1, 2, 3):
            got = buf[w].at[2 * _flip(x, k >> 1) + _flip(y, k & 1), pl.ds(c * h, h)]
            out.append(pltpu.make_async_remote_copy(
                src_ref=got, dst_ref=got, send_sem=send.at[4 * w + k], recv_sem=recv.at[4 * w + k],
                device_id=(x, y, 1 - c), device_id_type=MESH))
    return out


def _scatter_copies(refs, send, recv):
    nw = len(refs) // 2
    src, buf = refs[:nw], refs[nw:]
    x, y, c = _place()
    out = []
    for w in range(nw):
        h = src[w].shape[1] // 2
        for k in range(1, N_DEV):
            px, py, pc = _flip(x, k >> 2), _flip(y, (k >> 1) & 1), _flip(c, k & 1)
            out.append(pltpu.make_async_remote_copy(
                src_ref=src[w].at[2 * px + py, pl.ds(pc * h, h)], dst_ref=buf[w].at[k - 1],
                send_sem=send.at[N_DEV * w + k], recv_sem=recv.at[N_DEV * w + k], device_id=(px, py, pc),
                device_id_type=MESH))
    return out


def _join_halves(grads, layer):
    nw = len(grads)

    def body(*refs):
        ins, outs = refs[:nw], refs[nw:2 * nw]
        send, recv = refs[2 * nw:]
        x, y, c = _place()
        remote = []
        for w in range(nw):
            h = ins[w].shape[1] // 2
            hc = h // JOIN_CHUNKS
            for j in range(JOIN_CHUNKS):
                rows = pl.ds(c * h + j * hc, hc)
                cp = pltpu.make_async_remote_copy(
                    src_ref=ins[w].at[layer, rows], dst_ref=outs[w].at[layer, rows],
                    send_sem=send.at[w, j], recv_sem=recv.at[w, j], device_id=(x, y, 1 - c), device_id_type=MESH)
                cp.start()
                remote.append(cp)
        for w in range(nw):
            h = ins[w].shape[1] // 2
            hc = h // JOIN_CHUNKS
            for j in range(JOIN_CHUNKS):
                theirs = outs[w].at[layer, pl.ds((1 - c) * h + j * hc, hc)]
                pltpu.make_async_remote_copy(
                    src_ref=theirs, dst_ref=theirs, send_sem=send.at[w, j], recv_sem=recv.at[w, j],
                    device_id=(x, y, c), device_id_type=MESH).wait_recv()
        for cp in remote:
            cp.wait_send()

    sems = [pltpu.SemaphoreType.DMA((nw, JOIN_CHUNKS))] * 2
    return pl.pallas_call(
        body, out_shape=[jax.ShapeDtypeStruct(a.shape, a.dtype) for a in grads],
        in_specs=[HBM_SPEC] * nw, out_specs=[HBM_SPEC] * nw, scratch_shapes=sems,
        input_output_aliases={i: i for i in range(nw)}, name="join_halves",
    )(*grads)


def _allreduce_small(v):
    rows, c = v.shape

    def body(v_ref, o_ref, buf, local_sem, send, recv):
        x, y, cc = _place()
        me = 4 * x + 2 * y + cc
        own = pltpu.make_async_copy(v_ref, buf.at[me], local_sem)
        own.start()
        sends = []
        for k in range(1, N_DEV):
            px, py, pc = _flip(x, k >> 2), _flip(y, (k >> 1) & 1), _flip(cc, k & 1)
            cp = pltpu.make_async_remote_copy(
                src_ref=v_ref, dst_ref=buf.at[me], send_sem=send.at[k], recv_sem=recv.at[k],
                device_id=(px, py, pc), device_id_type=MESH)
            cp.start()
            sends.append(cp)
        for k in range(1, N_DEV):
            px, py, pc = _flip(x, k >> 2), _flip(y, (k >> 1) & 1), _flip(cc, k & 1)
            slot = buf.at[4 * px + 2 * py + pc]
            pltpu.make_async_remote_copy(
                src_ref=slot, dst_ref=slot, send_sem=send.at[k], recv_sem=recv.at[k],
                device_id=(x, y, cc), device_id_type=MESH).wait_recv()
        for cp in sends:
            cp.wait_send()
        own.wait()
        acc = buf[0]
        for i in range(1, N_DEV):
            acc = acc + buf[i]
        o_ref[...] = acc

    return pl.pallas_call(
        body, out_shape=jax.ShapeDtypeStruct((rows, c), F32),
        in_specs=[pl.BlockSpec(memory_space=pltpu.VMEM)], out_specs=pl.BlockSpec(memory_space=pltpu.VMEM),
        scratch_shapes=[pltpu.VMEM((N_DEV, rows, c), F32), pltpu.SemaphoreType.DMA,
                        pltpu.SemaphoreType.DMA((N_DEV,)), pltpu.SemaphoreType.DMA((N_DEV,))],
        compiler_params=_params(),
        name="allreduce_small",
    )(v)


def kernel(x, p, ln_mix_pre, w_in, ln_head, w_out, ln_mix_post, rel_bias, ln_ffn_pre, w_gate_up, w_down, ln_ffn_post, ln_pli, w_pli_gate, w_pli_proj, loss_target, m_ln_mix_pre, m_w_in, m_ln_head, m_w_out, m_ln_mix_post, m_rel_bias, m_ln_ffn_pre, m_w_gate_up, m_w_down, m_ln_ffn_post, m_ln_pli, m_w_pli_gate, m_w_pli_proj, v_ln_mix_pre, v_w_in, v_ln_head, v_w_out, v_ln_mix_post, v_rel_bias, v_ln_ffn_pre, v_w_gate_up, v_w_down, v_ln_ffn_post, v_ln_pli, v_w_pli_gate, v_w_pli_proj):
    weights = dict(ln_mix_pre=ln_mix_pre, w_in=w_in, ln_head=ln_head, w_out=w_out, ln_mix_post=ln_mix_post,
                   rel_bias=rel_bias, ln_ffn_pre=ln_ffn_pre, w_gate_up=w_gate_up, w_down=w_down,
                   ln_ffn_post=ln_ffn_post, ln_pli=ln_pli, w_pli_gate=w_pli_gate, w_pli_proj=w_pli_proj)
    mom1 = dict(ln_mix_pre=m_ln_mix_pre, w_in=m_w_in, ln_head=m_ln_head, w_out=m_w_out, ln_mix_post=m_ln_mix_post,
                rel_bias=m_rel_bias, ln_ffn_pre=m_ln_ffn_pre, w_gate_up=m_w_gate_up, w_down=m_w_down,
                ln_ffn_post=m_ln_ffn_post, ln_pli=m_ln_pli, w_pli_gate=m_w_pli_gate, w_pli_proj=m_w_pli_proj)
    mom2 = dict(ln_mix_pre=v_ln_mix_pre, w_in=v_w_in, ln_head=v_ln_head, w_out=v_w_out, ln_mix_post=v_ln_mix_post,
                rel_bias=v_rel_bias, ln_ffn_pre=v_ln_ffn_pre, w_gate_up=v_w_gate_up, w_down=v_w_down,
                ln_ffn_post=v_ln_ffn_post, ln_pli=v_ln_pli, w_pli_gate=v_w_pli_gate, w_pli_proj=v_w_pli_proj)

    _, seq, d_model = x.shape
    depth = w_in.shape[0]
    n_heads = d_model // HEAD_DIM
    n_sb = n_heads // 2
    n_dl = n_heads - n_sb
    assert seq % (HEAD_DIM * DILATIONS[-1]) == 0 and d_model % (2 * HEAD_DIM) == 0
    bq = 256
    tr = 256
    tr_ff = 128

    xs = x[0]
    target = loss_target[0]
    gain = {n: [weights[n][l][None, :] for l in range(depth)] for n in SMALL}
    bias_mask, buckets = _dil_tables(rel_bias)

    zero_token = jnp.zeros((8, LANE), F32)
    token = zero_token
    gathers = []
    for l in range(depth):
        parts = []
        for names in ((FIRST_USED, REST) if l == 0 else (BIG,)):
            slots = [_cast_layer(weights[n], l, n) for n in names]
            tag = f"{l}" if names is BIG else f"{l}_{names[0]}"
            parts.append((names, tag, _split_start(slots, token, _gather_ici_copies, 4 * len(names),
                                                   f"gather_start_{tag}")))
            token = parts[-1][2][3]
        gathers.append(parts)

    def gather_pass_on(part, after):
        names, tag, started = part
        arrays = _split_wait(started, after, _gather_ici_copies, f"gather_wait_{tag}")
        started = _split_start(arrays, zero_token, _gather_d2d_copies, 4 * len(names), f"gather_pass_{tag}")
        return names, tag, started, started[3]

    def gather_done(passed, after):
        names, tag, started, _ = passed
        return dict(zip(names, _split_wait(started, after, _gather_d2d_copies, f"gather_done_{tag}")))

    saved = []
    h1 = _norm_in(xs, gain["ln_mix_pre"][0], tr)
    xin = xs
    passed = gather_pass_on(gathers[0][0], token)
    wg = gather_done(passed, passed[3])
    for l in range(depth):
        proj = _mm_fwd(h1, wg["w_in"], True, "mm_in", tn_cap=768, tk_cap=2048)
        o_sb, lt_sb = _sb_fwd(proj, n_sb, bq)
        if l == 0:
            passed = gather_pass_on(gathers[0][1], o_sb)
            o_dl, lse_dl = _dil_fwd(proj, bias_mask, n_sb, n_dl, (passed[3],))
            wg.update(gather_done(passed, o_dl))
        else:
            o_dl, lse_dl = _dil_fwd(proj, bias_mask, n_sb, n_dl)
        on = _headnorm(o_sb, o_dl, gain["ln_head"][l], tr)
        y = _mm_fwd(on, wg["w_out"], False, "mm_out", tn_cap=1024, tk_cap=2048)
        x2, h2 = _res_norm(xin, y, gain["ln_mix_post"][l], gain["ln_ffn_pre"][l], "post_attn", tr)
        gu = _mm_fwd(h2, wg["w_gate_up"], True, "mm_gate_up", tk_cap=2048, out_dtype=BF16)
        act = _swiglu(gu, tr_ff)
        f = _mm_fwd(act, wg["w_down"], False, "mm_down", tn_cap=1024, tk_cap=1536)
        passed = gather_pass_on(gathers[l + 1][0], f) if l + 1 < depth else None
        x3, h3 = _res_norm(x2, f, gain["ln_ffn_post"][l], gain["ln_pli"][l], "post_ffn", tr,
                           () if passed is None else (passed[3],))
        gl = _mm_fwd(h3, wg["w_pli_gate"], False, "mm_pli_gate", tn_cap=1024, tk_cap=2048)
        pl_in = p[l, 0]
        pp = _mm_fwd(pl_in, wg["w_pli_proj"], True, "mm_pli_proj")
        saved.append(dict(wg=wg, x=xin, h1=h1, proj=proj, o_sb=o_sb, lt_sb=lt_sb, o_dl=o_dl, lse_dl=lse_dl, on=on, y=y, x2=x2,
                          h2=h2, gu=gu, act=act, f=f, x3=x3, h3=h3, gl=gl, pp=pp, p=pl_in))
        if l + 1 < depth:
            xin, h1 = _pli_out(x3, gl, pp, gain["ln_mix_pre"][l + 1], tr)
            wg = gather_done(passed, pp)
        else:
            dx, loss_part = _loss_head(x3, gl, pp, target, tr)

    grad_big = {n: None for n in BIG}
    grad_gain = {n: [None] * depth for n in SMALL}
    ds_layers = [None] * depth

    def start_scatter(layer, names, dw):
        dws = [dw[n] for n in names]
        landing = [lax.empty((N_DEV - 1, a.shape[1] // 2, a.shape[2]), a.dtype) for a in dws]
        tag = f"{layer}" if names is BIG else f"{layer}_{names[0]}"
        return layer, names, tag, _split_start(dws + landing, zero_token, _scatter_copies, N_DEV * len(names),
                                               f"scatter_start_{tag}")

    def reduce_layer(layer, names, tag, started, after):
        arrays = _split_wait(started, after, _scatter_copies, f"scatter_wait_{tag}")
        nw = len(names)
        halves = [_reduce_piece(arrays[w], arrays[nw + w], names[w], layer, depth, grad_big[names[w]])
                  for w in range(nw)]
        joined = _join_halves(halves, layer)
        grad_big.update(zip(names, joined))
        return joined[0]

    pending = None
    for l in reversed(range(depth)):
        sv = saved[l]
        wg = sv["wg"]
        dpp, dgl = _pli_bwd(dx, sv["gl"], sv["pp"], tr, () if pending is None else (pending[3][3],))
        dw = {}
        dw["w_pli_proj"] = _mm_wgrad(sv["p"], dpp, True, "wg_pli_proj")
        dw["w_pli_gate"] = _mm_wgrad(sv["h3"], dgl, False, "wg_pli_gate")
        dh3 = _mm_dgrad(dgl, wg["w_pli_gate"], False, "dg_pli_gate", to_cap=2048)
        dx3, df, grad_gain["ln_pli"][l], grad_gain["ln_ffn_post"][l] = _res_norm_bwd(
            dx, dh3, sv["x3"], sv["f"], gain["ln_pli"][l], gain["ln_ffn_post"][l], "post_ffn_bwd", tr)
        dw["w_down"] = _mm_wgrad(sv["act"], df, False, "wg_down")
        dact = _mm_dgrad(df, wg["w_down"], False, "dg_down")
        dgu = _swiglu_bwd(dact, sv["gu"], tr_ff)
        dw["w_gate_up"] = _mm_wgrad(sv["h2"], dgu, True, "wg_gate_up")
        dh2 = _mm_dgrad(dgu, wg["w_gate_up"], True, "dg_gate_up", tc_cap=2816)
        dx2, dy, grad_gain["ln_ffn_pre"][l], grad_gain["ln_mix_post"][l] = _res_norm_bwd(
            dx3, dh2, sv["x2"], sv["y"], gain["ln_ffn_pre"][l], gain["ln_mix_post"][l], "post_attn_bwd", tr)
        dw["w_out"] = _mm_wgrad(sv["on"], dy, False, "wg_out")
        don = _mm_dgrad(dy, wg["w_out"], False, "dg_out", to_cap=2048)
        early = start_scatter(l, REST, dw) if l == 0 else None
        do_sb, do_dl, grad_gain["ln_head"][l] = _headnorm_bwd(
            don, sv["o_sb"], sv["o_dl"], gain["ln_head"][l], tr, () if early is None else (early[3][3],))
        dq_s, dk_s, dv_s = _sb_bwd(sv["proj"], sv["lt_sb"], do_sb, n_sb, bq)
        dq_d, dk_d, dv_d, ds_layers[l] = _dil_bwd(sv["proj"], do_dl, sv["o_dl"], sv["lse_dl"], bias_mask, n_sb, n_dl)
        dproj = jnp.concatenate([dq_s, dk_s, dv_s, dq_d, dk_d, dv_d], axis=1)
        dw["w_in"] = _mm_wgrad(sv["h1"], dproj, True, "wg_in")
        dh1 = _mm_dgrad(dproj, wg["w_in"], True, "dg_in", tc_cap=1536)
        dx, grad_gain["ln_mix_pre"][l] = _norm_in_bwd(dx2, dh1, sv["x"], gain["ln_mix_pre"][l], tr)
        late = start_scatter(l, BIG if l > 0 else FIRST_USED, dw)
        if pending is not None:
            reduce_layer(*pending, dx if l > 0 else late[3][3])
        pending = late
    done = reduce_layer(*early, pending[3][3])
    reduce_layer(*pending, done)

    db = _rel_bias_grad(jnp.stack(ds_layers, axis=0), buckets)
    rb_flat = db[:n_dl, :NUM_BUCKETS].T.reshape(1, NUM_BUCKETS * n_dl)
    def widen(v):
        return jnp.pad(v, ((0, 0), (0, d_model - v.shape[1])))
    small_rows = [grad_gain[n][l] for n in SMALL for l in range(depth)] + [widen(rb_flat), widen(loss_part)]
    n_rows = len(small_rows)
    small = jnp.concatenate(small_rows + [jnp.zeros((-n_rows % 8, d_model), F32)], axis=0)
    total = _allreduce_small(small)
    grads = {}
    for i, n in enumerate(SMALL):
        grads[n] = total[i * depth:(i + 1) * depth]
    grads["rel_bias"] = total[len(SMALL) * depth, :NUM_BUCKETS * n_dl].reshape(NUM_BUCKETS, n_dl)
    loss = (0.5 / d_model) * jnp.sum(total[len(SMALL) * depth + 1, :LANE])
    for n in BIG:
        grads[n] = grad_big[n]

    delta, new_m, new_v = {}, {}, {}
    for n in WEIGHTS:
        delta[n], new_m[n], new_v[n], grads[n] = _adamw(weights[n], grads[n], mom1[n], mom2[n], n)
    return (loss, dx[None], *[grads[n] for n in WEIGHTS], *[delta[n] for n in WEIGHTS],
            *[new_m[n] for n in WEIGHTS], *[new_v[n] for n in WEIGHTS])
```

```python
import functools
import math

import jax
import jax.numpy as jnp
from jax import lax
from jax.experimental import pallas as pl
from jax.experimental.pallas import tpu as pltpu

F32 = jnp.float32
BF16 = jnp.bfloat16

HEAD_DIM = 128
RMS_EPS = 1e-6
DILATIONS = (1, 4, 16)
NUM_BUCKETS = 32
MAX_DISTANCE = 2048
NEG = -1e30
N_CHIPS = 4
N_DEV = 8
JOIN_CHUNKS = 8
DIL_UNROLL = 8

ADAM_LR = 0.001
ADAM_B1 = 0.9
ADAM_B2 = 0.999
ADAM_EPS = 1e-08
ADAM_WD = 0.01
ADAM_STEP = 10

V7X_VMEM_LIMIT = 48 * 1024 * 1024
LANE = 128

NN = (((1,), (0,)), ((), ()))
NT = (((1,), (1,)), ((), ()))
TN = (((0,), (0,)), ((), ()))
MESH = pl.DeviceIdType.MESH

BIG = ("w_in", "w_out", "w_gate_up", "w_down", "w_pli_gate", "w_pli_proj")
FIRST_USED = BIG[:1]
REST = BIG[1:]
COL_SHARDED = {"w_in": True, "w_out": False, "w_gate_up": True, "w_down": False,
               "w_pli_gate": False, "w_pli_proj": True}
SMALL = ("ln_mix_pre", "ln_head", "ln_mix_post", "ln_ffn_pre", "ln_ffn_post", "ln_pli")
WEIGHTS = ("ln_mix_pre", "w_in", "ln_head", "w_out", "ln_mix_post", "rel_bias", "ln_ffn_pre",
           "w_gate_up", "w_down", "ln_ffn_post", "ln_pli", "w_pli_gate", "w_pli_proj")


def _tile(n, cap):
    t = min(n, cap) // LANE * LANE
    while t >= LANE:
        if n % t == 0:
            return t
        t -= LANE
    return n


def _params(sem=None):
    return pltpu.CompilerParams(dimension_semantics=sem, vmem_limit_bytes=V7X_VMEM_LIMIT)


def _rowwise(fn, rows, vecs, outs, sums, name, tr, after=()):
    s = rows[0].shape[0]
    nr, nv, no, ns, na = len(rows), len(vecs), len(outs), len(sums), len(after)

    def body(*refs):
        ins = [r[...] for r in refs[:nr + nv]]
        res = fn(*ins)
        out_refs = refs[nr + nv + na:nr + nv + na + no]
        sum_refs = refs[nr + nv + na + no:]
        for o_ref, val in zip(out_refs, res[:no]):
            o_ref[...] = val.astype(o_ref.dtype)
        if ns:
            @pl.when(pl.program_id(0) == 0)
            def _():
                for s_ref in sum_refs:
                    s_ref[...] = jnp.zeros_like(s_ref)
            for s_ref, val in zip(sum_refs, res[no:]):
                s_ref[...] += jnp.sum(val, axis=0, keepdims=True)

    in_specs = [pl.BlockSpec((tr, r.shape[1]), lambda i: (i, 0)) for r in rows]
    in_specs += [pl.BlockSpec(v.shape, lambda i: (0, 0)) for v in vecs]
    in_specs += [pl.BlockSpec(memory_space=pl.ANY)] * na
    out_specs = [pl.BlockSpec((tr, c), lambda i: (i, 0)) for c, _ in outs]
    out_specs += [pl.BlockSpec((1, c), lambda i: (0, 0)) for c in sums]
    out_shape = [jax.ShapeDtypeStruct((s, c), dt) for c, dt in outs]
    out_shape += [jax.ShapeDtypeStruct((1, c), F32) for c in sums]
    return pl.pallas_call(
        body, grid=(s // tr,), in_specs=in_specs, out_specs=out_specs, out_shape=out_shape,
        compiler_params=_params(("arbitrary",) if ns else ("parallel",)), name=name,
    )(*rows, *vecs, *after)


def _rms_r(x):
    return lax.rsqrt(jnp.mean(x * x, axis=-1, keepdims=True) + RMS_EPS)


def _rms_bwd(x, g, dy):
    r = _rms_r(x)
    u = dy * g
    dx = r * (u - x * (r * r) * jnp.mean(u * x, axis=-1, keepdims=True))
    return dx, dy * x * r


def _sigmoid(z):
    return 1.0 / (1.0 + jnp.exp(-z))


def _norm_in(x, g, tr):
    d = x.shape[1]
    return _rowwise(lambda x, g: (x * _rms_r(x) * g,), [x], [g], [(d, BF16)], [], "norm_in", tr)[0]


def _norm_in_bwd(dx_res, dh, x, g, tr):
    d = x.shape[1]

    def fn(dx_res, dh, x, g):
        dx, dg = _rms_bwd(x, g, dh)
        return dx_res + dx, dg
    return _rowwise(fn, [dx_res, dh, x], [g], [(d, F32)], [d], "norm_in_bwd", tr)


def _headnorm(o_sb, o_dl, g, tr):
    d = g.shape[1]

    def fn(o_sb, o_dl, g):
        o = jnp.concatenate([o_sb, o_dl], axis=1)
        parts = []
        for h in range(d // HEAD_DIM):
            sl = slice(h * HEAD_DIM, (h + 1) * HEAD_DIM)
            oh = o[:, sl]
            parts.append(oh * _rms_r(oh) * g[:, sl])
        return (jnp.concatenate(parts, axis=1),)
    return _rowwise(fn, [o_sb, o_dl], [g], [(d, BF16)], [], "headnorm", tr)[0]


def _headnorm_bwd(don, o_sb, o_dl, g, tr, after=()):
    d = g.shape[1]
    n_sb = o_sb.shape[1]

    def fn(don, o_sb, o_dl, g):
        o = jnp.concatenate([o_sb, o_dl], axis=1)
        dos, dgs = [], []
        for h in range(d // HEAD_DIM):
            sl = slice(h * HEAD_DIM, (h + 1) * HEAD_DIM)
            dx, dg = _rms_bwd(o[:, sl], g[:, sl], don[:, sl])
            dos.append(dx)
            dgs.append(dg)
        do = jnp.concatenate(dos, axis=1)
        return do[:, :n_sb], do[:, n_sb:], jnp.concatenate(dgs, axis=1)
    return _rowwise(fn, [don, o_sb, o_dl], [g], [(n_sb, F32), (d - n_sb, F32)], [d], "headnorm_bwd", tr, after)


def _res_norm(x, y, g_post, g_pre, name, tr, after=()):
    d = x.shape[1]

    def fn(x, y, g_post, g_pre):
        x2 = x + y * _rms_r(y) * g_post
        return x2, x2 * _rms_r(x2) * g_pre
    return _rowwise(fn, [x, y], [g_post, g_pre], [(d, F32), (d, BF16)], [], name, tr, after)


def _res_norm_bwd(dx_res, dh, x2, y, g_pre, g_post, name, tr):
    d = x2.shape[1]

    def fn(dx_res, dh, x2, y, g_pre, g_post):
        dxa, dg_pre = _rms_bwd(x2, g_pre, dh)
        dx2 = dx_res + dxa
        dy, dg_post = _rms_bwd(y, g_post, dx2)
        return dx2, dy, dg_pre, dg_post
    return _rowwise(fn, [dx_res, dh, x2, y], [g_pre, g_post], [(d, F32), (d, BF16)], [d, d], name, tr)


def _swiglu(gu, tr):
    ff = gu.shape[1] // 2

    def fn(gu):
        g, u = gu[:, :ff].astype(F32), gu[:, ff:].astype(F32)
        return (g * _sigmoid(g) * u,)
    return _rowwise(fn, [gu], [], [(ff, BF16)], [], "swiglu", tr)[0]


def _swiglu_bwd(dact, gu, tr):
    ff = gu.shape[1] // 2

    def fn(dact, gu):
        g, u = gu[:, :ff].astype(F32), gu[:, ff:].astype(F32)
        sg = _sigmoid(g)
        dg = dact * u * sg * (1.0 + g * (1.0 - sg))
        du = dact * g * sg
        return (jnp.concatenate([dg, du], axis=1),)
    return _rowwise(fn, [dact, gu], [], [(2 * ff, BF16)], [], "swiglu_bwd", tr)[0]


def _pli_out(x3, gl, pp, g_next, tr):
    d = x3.shape[1]

    def fn(x3, gl, pp, g):
        x4 = x3 + _sigmoid(gl) * pp
        return x4, x4 * _rms_r(x4) * g
    return _rowwise(fn, [x3, gl, pp], [g_next], [(d, F32), (d, BF16)], [], "pli_out", tr)


def _loss_head(x3, gl, pp, target, tr):
    d = x3.shape[1]

    def fn(x3, gl, pp, t):
        err = x3 + _sigmoid(gl) * pp - t
        sq = err * err
        part = sq[:, :LANE]
        for k in range(1, d // LANE):
            part = part + sq[:, k * LANE:(k + 1) * LANE]
        return err * (1.0 / d), part
    return _rowwise(fn, [x3, gl, pp, target], [], [(d, F32)], [LANE], "loss_head", tr)


def _pli_bwd(dx, gl, pp, tr, after=()):
    d = dx.shape[1]

    def fn(dx, gl, pp):
        gate = _sigmoid(gl)
        return dx * gate, dx * pp * gate * (1.0 - gate)
    return _rowwise(fn, [dx, gl, pp], [], [(d, BF16), (d, BF16)], [], "pli_bwd", tr, after)


def _my_chip():
    return 2 * lax.axis_index("x") + lax.axis_index("y")


def _cast_layer(w, layer, name):
    _, r, c = w.shape
    tr = r
    while tr * c * 4 > (4 << 20) and tr % 32 == 0:
        tr //= 2

    def body(w_ref, o_ref):
        o_ref[...] = w_ref[...].astype(o_ref.dtype)

    return pl.pallas_call(
        body, grid=(r // tr,),
        in_specs=[pl.BlockSpec((None, tr, c), lambda i: (layer, i, 0))],
        out_specs=pl.BlockSpec((None, tr, c), lambda i: (_my_chip(), i, 0)),
        out_shape=jax.ShapeDtypeStruct((N_CHIPS, r, c), BF16),
        compiler_params=_params(("parallel",)), name="cast_" + name,
    )(w)


def _reduce_piece(dw, recv, name, layer, depth, into):
    _, r, c = dw.shape
    h = r // 2
    tr = h
    while tr * c * 2 * N_DEV > (8 << 20) and tr % 32 == 0:
        tr //= 2
    per = h // tr

    def body(d_ref, r_ref, *rest):
        acc = d_ref[...].astype(F32)
        for i in range(N_DEV - 1):
            acc = acc + r_ref[i].astype(F32)
        rest[-1][...] = acc

    kept = [] if into is None else [into]
    return pl.pallas_call(
        body, grid=(per,),
        in_specs=[pl.BlockSpec((None, tr, c), lambda i: (_my_chip(), lax.axis_index("c") * per + i, 0)),
                  pl.BlockSpec((N_DEV - 1, tr, c), lambda i: (0, i, 0))] + [pl.BlockSpec(memory_space=pl.ANY)] * len(kept),
        out_specs=pl.BlockSpec((None, tr, c), lambda i: (layer, lax.axis_index("c") * per + i, 0)),
        out_shape=jax.ShapeDtypeStruct((depth, r, c), F32),
        input_output_aliases={2: 0} if kept else {},
        compiler_params=_params(("parallel",)), name="reduce_" + name,
    )(dw, recv, *kept)


def _adamw(w, g, m, v, name):
    shape = w.shape
    if w.ndim == 3:
        w, g, m, v = (a.reshape(shape[0] * shape[1], shape[2]) for a in (w, g, m, v))
    r, c = w.shape
    tr = r
    while tr * c * 4 > (1 << 20) and tr % 16 == 0:
        tr //= 2

    def body(w_ref, g_ref, m_ref, v_ref, d_ref, m2_ref, v2_ref, g2_ref):
        g = g_ref[...]
        m2 = ADAM_B1 * m_ref[...] + (1.0 - ADAM_B1) * g
        v2 = ADAM_B2 * v_ref[...] + (1.0 - ADAM_B2) * (g * g)
        m_hat = m2 / (1.0 - ADAM_B1 ** ADAM_STEP)
        v_hat = v2 / (1.0 - ADAM_B2 ** ADAM_STEP)
        d_ref[...] = -ADAM_LR * (m_hat / (jnp.sqrt(v_hat) + ADAM_EPS) + ADAM_WD * w_ref[...])
        m2_ref[...] = m2
        v2_ref[...] = v2
        g2_ref[...] = g

    spec = pl.BlockSpec((tr, c), lambda i: (i, 0))
    res = pl.pallas_call(
        body, grid=(r // tr,), in_specs=[spec] * 4, out_specs=[spec] * 4,
        out_shape=[jax.ShapeDtypeStruct((r, c), F32)] * 4,
        compiler_params=_params(("parallel",)), name="adamw_" + name,
    )(w, g, m, v)
    return tuple(a.reshape(shape) for a in res)


def _mm(a, b, grid, a_spec, b_spec, o_spec, o_shape, o_dtype, dims, acc_shape, name):
    nk = grid[2]

    def whole(a_ref, b_ref, o_ref):
        b = b_ref[...]
        if b.ndim == 3:
            b = b.reshape(b.shape[0] * b.shape[1], b.shape[2])
        o_ref[...] = lax.dot_general(a_ref[...].astype(BF16), b.astype(BF16), dims,
                                     preferred_element_type=F32).astype(o_ref.dtype)

    def body(a_ref, b_ref, o_ref, acc_ref):
        k = pl.program_id(2)

        @pl.when(k == 0)
        def _():
            acc_ref[...] = jnp.zeros_like(acc_ref)

        acc_ref[...] += lax.dot_general(a_ref[...].astype(BF16), b_ref[...].astype(BF16), dims,
                                        preferred_element_type=F32)

        @pl.when(k == nk - 1)
        def _():
            o_ref[...] = acc_ref[...].astype(o_ref.dtype)

    return pl.pallas_call(
        whole if nk == 1 else body, grid=grid, in_specs=[a_spec, b_spec], out_specs=o_spec,
        out_shape=jax.ShapeDtypeStruct(o_shape, o_dtype),
        scratch_shapes=[] if nk == 1 else [pltpu.VMEM(acc_shape, F32)],
        compiler_params=_params(("parallel", "parallel", "arbitrary")), name=name,
    )(a, b)


def _mm_fwd(a, wg, col, name, tm=1024, tn_cap=1536, tk_cap=1024, out_dtype=F32):
    m, k = a.shape
    ns, r, c = wg.shape
    tm = min(tm, m)
    if col:
        n, tn, tk = ns * c, _tile(c, tn_cap), _tile(k, tk_cap)
        per = c // tn
        b_spec = pl.BlockSpec((None, tk, tn), lambda j, i, kk: (j // per, kk, j % per))
    elif tk_cap >= k:
        n, tn, tk = c, _tile(c, tn_cap), k
        b_spec = pl.BlockSpec((ns, r, tn), lambda j, i, kk: (0, 0, j))
    else:
        n, tn, tk = c, _tile(c, tn_cap), _tile(r, tk_cap)
        per = r // tk
        b_spec = pl.BlockSpec((None, tk, tn), lambda j, i, kk: (kk // per, kk % per, j))
    return _mm(a, wg, (n // tn, m // tm, k // tk), pl.BlockSpec((tm, tk), lambda j, i, kk: (i, kk)), b_spec,
               pl.BlockSpec((tm, tn), lambda j, i, kk: (i, j)), (m, n), out_dtype, NN, (tm, tn), name)


def _mm_dgrad(dc, wg, col, name, tm=1024, to_cap=1536, tc_cap=2048):
    m, n = dc.shape
    ns, r, c = wg.shape
    tm = min(tm, m)
    if col:
        kout, to, tc = r, _tile(r, to_cap), _tile(c, tc_cap)
        per = c // tc
        b_spec = pl.BlockSpec((None, to, tc), lambda j, i, kk: (kk // per, j, kk % per))
    elif to_cap >= ns * r and tc_cap >= c:
        kout, to, tc, tm = ns * r, ns * r, c, min(tm, 512)
        b_spec = pl.BlockSpec((ns, r, tc), lambda j, i, kk: (0, 0, 0))
    else:
        kout, to, tc = ns * r, _tile(r, to_cap), _tile(c, tc_cap)
        per = r // to
        b_spec = pl.BlockSpec((None, to, tc), lambda j, i, kk: (j // per, j % per, kk))
    return _mm(dc, wg, (kout // to, m // tm, n // tc), pl.BlockSpec((tm, tc), lambda j, i, kk: (i, kk)), b_spec,
               pl.BlockSpec((tm, to), lambda j, i, kk: (i, j)), (m, kout), F32, NT, (tm, to), name)


def _mm_wgrad(a, dc, col, name, ti_cap=1536, tn_cap=1536, tkm=2048):
    m, k = a.shape
    n = dc.shape[1]
    tkm = min(tkm, m)
    if col:
        r, c = k, n // N_CHIPS
        ti, tn = _tile(r, ti_cap), _tile(c, tn_cap)
        per = c // tn
        o_spec = pl.BlockSpec((None, ti, tn), lambda i, j, kk: (j // per, i, j % per))
    else:
        r, c = k // N_CHIPS, n
        ti, tn = _tile(r, ti_cap), _tile(c, tn_cap)
        per = r // ti
        o_spec = pl.BlockSpec((None, ti, tn), lambda i, j, kk: (i // per, i % per, j))
    return _mm(a, dc, (k // ti, n // tn, m // tkm), pl.BlockSpec((tkm, ti), lambda i, j, kk: (kk, i)),
               pl.BlockSpec((tkm, tn), lambda i, j, kk: (kk, j)), o_spec, (N_CHIPS, r, c), BF16, TN, (ti, tn), name)


def _log_keep(z):
    return -(jnp.maximum(z, 0.0) + jnp.log(1.0 + jnp.exp(-jnp.abs(z))))


def _split_dot(x, t):
    hi = x.astype(BF16)
    lo = (x - hi.astype(F32)).astype(BF16)
    return (lax.dot_general(hi, t, NN, preferred_element_type=F32)
            + lax.dot_general(lo, t, NN, preferred_element_type=F32))


def _sb_fwd(proj, n_sb, bq):
    s = proj.shape[0]
    scale = 1.0 / math.sqrt(HEAD_DIM)

    def body(q_ref, k_ref, v_ref, o_ref, lt_ref):
        i = pl.program_id(1)
        q = q_ref[...].astype(BF16)
        row = lax.broadcasted_iota(jnp.int32, (bq, bq), 0)
        col = lax.broadcasted_iota(jnp.int32, (bq, bq), 1)
        later_in_block = (row > col).astype(BF16)
        keep = col < row

        def blocks(js, carry, diagonal=False):
            c, acc = carry
            sl = [pl.ds(pl.multiple_of(j * bq, bq), bq) for j in js]
            masked = [diagonal and n == 0 for n in range(len(js))]
            kbs = [k_ref[s_, :].astype(BF16) for s_ in sl]
            zs = [lax.dot_general(q, kb, NT, preferred_element_type=F32) * scale for kb in kbs]
            lks = [jnp.where(keep, _log_keep(z), 0.0) if m else _log_keep(z) for z, m in zip(zs, masked)]
            within = [_split_dot(lk, later_in_block) for lk in lks]
            es = []
            for z, lk, w, m in zip(zs, lks, within, masked):
                e = z + lk + w + c
                es.append(jnp.where(keep, e, NEG) if m else e)
                c = c + jnp.sum(lk, axis=1, keepdims=True)
            for e, s_ in zip(es, sl):
                acc = acc + lax.dot_general(jnp.exp(e).astype(BF16), v_ref[s_, :].astype(BF16), NN,
                                            preferred_element_type=F32)
            return c, acc

        first = [functools.partial(blocks, [i - u for u in range(g + 1)], diagonal=True) for g in range(4)]
        carry = lax.switch(i % 4, first, (jnp.zeros((bq, 1), F32), jnp.zeros((bq, HEAD_DIM), F32)))
        top = i - i % 4 - 1
        carry = lax.fori_loop(0, i // 4, lambda t, cr: blocks([top - 4 * t - u for u in range(4)], cr), carry)
        o_ref[...] = carry[1]
        lt_ref[...] = jnp.broadcast_to(carry[0], (bq, HEAD_DIM))

    blk = pl.BlockSpec((bq, HEAD_DIM), lambda h, i: (i, h))
    shp = jax.ShapeDtypeStruct((s, n_sb * HEAD_DIM), F32)
    return pl.pallas_call(
        body, grid=(n_sb, s // bq),
        in_specs=[blk,
                  pl.BlockSpec((s, HEAD_DIM), lambda h, i: (0, n_sb + h)),
                  pl.BlockSpec((s, HEAD_DIM), lambda h, i: (0, 2 * n_sb + h))],
        out_specs=[blk, blk], out_shape=[shp, shp],
        compiler_params=_params(("parallel", "parallel")), name="sb_fwd",
    )(proj, proj, proj)


def _sb_bwd(proj, lt, do, n_sb, bq):
    s = proj.shape[0]
    nq = s // bq
    scale = 1.0 / math.sqrt(HEAD_DIM)

    def body(q_ref, k_ref, v_ref, lt_ref, do_ref, dq_ref, dk_ref, dv_ref, dk_acc, dv_acc):
        i = pl.program_id(1)

        @pl.when(i == 0)
        def _():
            dk_acc[...] = jnp.zeros_like(dk_acc)
            dv_acc[...] = jnp.zeros_like(dv_acc)

        q = q_ref[...].astype(BF16)
        do_b = do_ref[...].astype(BF16)
        ltot = jnp.max(lt_ref[...], axis=1, keepdims=True)
        row = lax.broadcasted_iota(jnp.int32, (bq, bq), 0)
        col = lax.broadcasted_iota(jnp.int32, (bq, bq), 1)
        upto_in_block = (row <= col).astype(BF16)
        before_in_block = (row < col).astype(BF16)
        keep = col < row

        def blocks(js, carry, diagonal=False):
            pk, pg, dq = carry
            sl = [pl.ds(pl.multiple_of(j * bq, bq), bq) for j in js]
            masked = [diagonal and n == len(js) - 1 for n in range(len(js))]
            kbs = [k_ref[s_, :].astype(BF16) for s_ in sl]
            zs = [lax.dot_general(q, kb, NT, preferred_element_type=F32) * scale for kb in kbs]
            das = [lax.dot_general(do_b, v_ref[s_, :].astype(BF16), NT, preferred_element_type=F32) for s_ in sl]
            lks = [jnp.where(keep, _log_keep(z), 0.0) if m else _log_keep(z) for z, m in zip(zs, masked)]
            upto = [_split_dot(lk, upto_in_block) for lk in lks]
            gs, abs_ = [], []
            for z, lk, u, da, m in zip(zs, lks, upto, das, masked):
                e = z + lk + ((ltot - pk) - u)
                a = jnp.exp(jnp.where(keep, e, NEG) if m else e)
                gs.append(a * da)
                abs_.append(a.astype(BF16))
                pk = pk + jnp.sum(lk, axis=1, keepdims=True)
            for a_b, s_ in zip(abs_, sl):
                dv_acc[s_, :] += lax.dot_general(a_b, do_b, TN, preferred_element_type=F32)
            before = [lax.dot_general(g.astype(BF16), before_in_block, NN, preferred_element_type=F32) for g in gs]
            dzs = []
            for z, lk, g, bf, m in zip(zs, lks, gs, before, masked):
                keep_p = jnp.exp(lk)
                dz = g * keep_p - (pg + bf) * (1.0 - keep_p)
                dzs.append(((jnp.where(keep, dz, 0.0) if m else dz) * scale).astype(BF16))
                pg = pg + jnp.sum(g, axis=1, keepdims=True)
            for dz_b, kb in zip(dzs, kbs):
                dq = dq + lax.dot_general(dz_b, kb, NN, preferred_element_type=F32)
            for dz_b, s_ in zip(dzs, sl):
                dk_acc[s_, :] += lax.dot_general(dz_b, q, TN, preferred_element_type=F32)
            return pk, pg, dq

        zero = jnp.zeros((bq, 1), F32)
        carry = lax.fori_loop(0, i // 4, lambda t, cr: blocks([4 * t + u for u in range(4)], cr),
                              (zero, zero, jnp.zeros((bq, HEAD_DIM), F32)))
        last = [functools.partial(blocks, [i - g + u for u in range(g + 1)], diagonal=True) for g in range(4)]
        carry = lax.switch(i % 4, last, carry)
        dq_ref[...] = carry[2].astype(dq_ref.dtype)

        @pl.when(i == nq - 1)
        def _():
            dk_ref[...] = dk_acc[...].astype(dk_ref.dtype)
            dv_ref[...] = dv_acc[...].astype(dv_ref.dtype)

    blk = pl.BlockSpec((bq, HEAD_DIM), lambda h, i: (i, h))
    full = pl.BlockSpec((s, HEAD_DIM), lambda h, i: (0, h))
    shp = jax.ShapeDtypeStruct((s, n_sb * HEAD_DIM), BF16)
    return pl.pallas_call(
        body, grid=(n_sb, nq),
        in_specs=[blk,
                  pl.BlockSpec((s, HEAD_DIM), lambda h, i: (0, n_sb + h)),
                  pl.BlockSpec((s, HEAD_DIM), lambda h, i: (0, 2 * n_sb + h)),
                  blk, blk],
        out_specs=[blk, full, full], out_shape=[shp, shp, shp],
        scratch_shapes=[pltpu.VMEM((s, HEAD_DIM), F32), pltpu.VMEM((s, HEAD_DIM), F32)],
        compiler_params=_params(("parallel", "arbitrary")), name="sb_bwd",
    )(proj, proj, proj, lt, do)


def _t5_bucket(dist):
    max_exact = NUM_BUCKETS // 2
    d = jnp.maximum(dist, 1).astype(F32)
    large = max_exact + (jnp.log(d / max_exact) / math.log(MAX_DISTANCE / max_exact)
                         * (NUM_BUCKETS - max_exact)).astype(jnp.int32)
    large = jnp.minimum(large, NUM_BUCKETS - 1)
    return jnp.where(dist < max_exact, dist, large)


def _dil_tables(rel_bias):
    qi = jnp.arange(HEAD_DIM, dtype=jnp.int32)[:, None]
    ki = jnp.arange(2 * HEAD_DIM, dtype=jnp.int32)[None, :]
    rel = HEAD_DIM + qi - ki
    band = (rel >= 0) & (rel <= HEAD_DIM)
    biases, buckets = [], []
    for d in DILATIONS:
        bucket = _t5_bucket(jnp.maximum(rel, 0) * d)
        onehot = (bucket[:, :, None] == jnp.arange(NUM_BUCKETS, dtype=jnp.int32)).astype(F32)
        bias = jnp.einsum("qkb,bh->hqk", onehot, rel_bias.astype(F32), precision=lax.Precision.HIGHEST)
        biases.append(jnp.where(band[None], bias, NEG))
        buckets.append(jnp.where(band, bucket, -1).astype(F32))
    return jnp.stack(biases, axis=1), jnp.stack(buckets, axis=0)


def _sub_rows(ref, start, d):
    if d == 1:
        return ref[pl.ds(pl.multiple_of(start, HEAD_DIM), HEAD_DIM), :]
    return ref[pl.ds(start, HEAD_DIM, stride=d), :]


def _sub_idx(start, d):
    if d == 1:
        return pl.ds(pl.multiple_of(start, HEAD_DIM), HEAD_DIM)
    return pl.ds(start, HEAD_DIM, stride=d)


def _dil_logits(q_ref, k_ref, bm, n, cur, prv, d, scale):
    qb = _sub_rows(q_ref, cur, d).astype(BF16)
    kk = jnp.concatenate([_sub_rows(k_ref, prv, d), _sub_rows(k_ref, cur, d)], axis=0).astype(BF16)
    sc = lax.dot_general(qb, kk, NT, preferred_element_type=F32) * scale + bm
    colk = lax.broadcasted_iota(jnp.int32, sc.shape, 1)
    sc = jnp.where((colk >= HEAD_DIM) | (n > 0), sc, NEG)
    return qb, kk, sc


def _dil_fwd(proj, bm, n_sb, n_dl, after=()):
    s = proj.shape[0]
    scale = 1.0 / math.sqrt(HEAD_DIM)
    chunk = min(s, 512)
    na = len(after)

    def body(q_ref, k_ref, v_ref, bm_ref, *rest):
        o_ref, l_ref, ob0, ob1, ob2, lb0, lb1, lb2 = rest[na:]
        obs, lbs = (ob0, ob1, ob2), (lb0, lb1, lb2)
        for b, d in enumerate(DILATIONS):
            nb = s // (HEAD_DIM * d)

            def group(g, _, b=b, d=d, nb=nb):
                where = []
                for u in range(DIL_UNROLL):
                    idx = g * DIL_UNROLL + u
                    r, n = idx // nb, idx % nb
                    where.append((n, n * (HEAD_DIM * d) + r, jnp.maximum(n - 1, 0) * (HEAD_DIM * d) + r))
                bm = bm_ref[b]
                scs = [_dil_logits(q_ref, k_ref, bm, n, cur, prv, d, scale)[2] for n, cur, prv in where]
                mxs = [jnp.max(sc, axis=1, keepdims=True) for sc in scs]
                prs = [jnp.exp(sc - mx) for sc, mx in zip(scs, mxs)]
                dens = [jnp.sum(pr, axis=1, keepdims=True) for pr in prs]
                outs = [lax.dot_general(
                    pr.astype(BF16),
                    jnp.concatenate([_sub_rows(v_ref, prv, d), _sub_rows(v_ref, cur, d)], axis=0).astype(BF16),
                    NN, preferred_element_type=F32) for pr, (n, cur, prv) in zip(prs, where)]
                for o, mx, den, (n, cur, prv) in zip(outs, mxs, dens, where):
                    obs[b][_sub_idx(cur, d), :] = o / den
                    lbs[b][_sub_idx(cur, d), :] = jnp.broadcast_to(mx + jnp.log(den), (HEAD_DIM, HEAD_DIM))
                return 0

            lax.fori_loop(0, s // HEAD_DIM // DIL_UNROLL, group, 0)

        for ci in range(s // chunk):
            sl = pl.ds(ci * chunk, chunk)
            l0, l1, l2 = lb0[sl, :], lb1[sl, :], lb2[sl, :]
            mx = jnp.maximum(jnp.maximum(l0, l1), l2)
            w0, w1, w2 = jnp.exp(l0 - mx), jnp.exp(l1 - mx), jnp.exp(l2 - mx)
            tot = w0 + w1 + w2
            o_ref[sl, :] = (w0 * ob0[sl, :] + w1 * ob1[sl, :] + w2 * ob2[sl, :]) / tot
            l_ref[sl, :] = mx + jnp.log(tot)

    base = 3 * n_sb
    full = pl.BlockSpec((s, HEAD_DIM), lambda h: (0, h))
    shp = jax.ShapeDtypeStruct((s, n_dl * HEAD_DIM), F32)
    return pl.pallas_call(
        body, grid=(n_dl,),
        in_specs=[pl.BlockSpec((s, HEAD_DIM), lambda h: (0, base + h)),
                  pl.BlockSpec((s, HEAD_DIM), lambda h: (0, base + n_dl + h)),
                  pl.BlockSpec((s, HEAD_DIM), lambda h: (0, base + 2 * n_dl + h)),
                  pl.BlockSpec((None, 3, HEAD_DIM, 2 * HEAD_DIM), lambda h: (h, 0, 0, 0))]
        + [pl.BlockSpec(memory_space=pl.ANY)] * na,
        out_specs=[full, full], out_shape=[shp, shp],
        scratch_shapes=[pltpu.VMEM((s, HEAD_DIM), F32)] * 6,
        compiler_params=_params(("parallel",)), name="dil_fwd",
    )(proj, proj, proj, bm, *after)


def _dil_bwd(proj, do, o, lse, bm, n_sb, n_dl):
    s = proj.shape[0]
    scale = 1.0 / math.sqrt(HEAD_DIM)
    chunk = min(s, 512)

    def body(q_ref, k_ref, v_ref, do_ref, o_ref, l_ref, bm_ref, dq_ref, dk_ref, dv_ref, ds_ref, dq_s, dk_s, dv_s):
        dq_s[...] = jnp.zeros_like(dq_s)
        dk_s[...] = jnp.zeros_like(dk_s)
        dv_s[...] = jnp.zeros_like(dv_s)
        ds_ref[...] = jnp.zeros_like(ds_ref)
        for b, d in enumerate(DILATIONS):
            nb = s // (HEAD_DIM * d)

            def group(g, _, b=b, d=d, nb=nb):
                where = []
                for u in range(DIL_UNROLL):
                    idx = g * DIL_UNROLL + u
                    r, n = idx // nb, idx % nb
                    where.append((n, n * (HEAD_DIM * d) + r, jnp.maximum(n - 1, 0) * (HEAD_DIM * d) + r))
                bm = bm_ref[b]
                logits = [_dil_logits(q_ref, k_ref, bm, n, cur, prv, d, scale) for n, cur, prv in where]
                do_fs = [_sub_rows(do_ref, cur, d) for n, cur, prv in where]
                do_bs = [do_f.astype(BF16) for do_f in do_fs]
                dps = [lax.dot_general(
                    do_b, jnp.concatenate([_sub_rows(v_ref, prv, d), _sub_rows(v_ref, cur, d)], axis=0).astype(BF16),
                    NT, preferred_element_type=F32) for do_b, (n, cur, prv) in zip(do_bs, where)]
                ws, dss = [], []
                for (qb, kk, sc), do_f, dp, (n, cur, prv) in zip(logits, do_fs, dps, where):
                    delta = jnp.sum(do_f * _sub_rows(o_ref, cur, d), axis=1, keepdims=True)
                    lr = _sub_rows(l_ref, cur, d)
                    w = jnp.exp(sc - jnp.concatenate([lr, lr], axis=1))
                    ws.append(w)
                    dss.append(w * (dp - delta))
                ds_bs = [(ds * scale).astype(BF16) for ds in dss]
                dv_blks = [lax.dot_general(w.astype(BF16), do_b, TN, preferred_element_type=F32)
                           for w, do_b in zip(ws, do_bs)]
                dk_blks = [lax.dot_general(ds_b, qb, TN, preferred_element_type=F32)
                           for ds_b, (qb, kk, sc) in zip(ds_bs, logits)]
                dq_blks = [lax.dot_general(ds_b, kk, NN, preferred_element_type=F32)
                           for ds_b, (qb, kk, sc) in zip(ds_bs, logits)]
                total = dss[0]
                for ds in dss[1:]:
                    total = total + ds
                ds_ref[b] += total
                for dq_blk, dk_blk, dv_blk, (n, cur, prv) in zip(dq_blks, dk_blks, dv_blks, where):
                    ci, pi = _sub_idx(cur, d), _sub_idx(prv, d)
                    dq_s[ci, :] += dq_blk
                    dk_s[ci, :] += dk_blk[HEAD_DIM:]
                    dv_s[ci, :] += dv_blk[HEAD_DIM:]
                    dk_s[pi, :] += dk_blk[:HEAD_DIM]
                    dv_s[pi, :] += dv_blk[:HEAD_DIM]
                return 0

            lax.fori_loop(0, s // HEAD_DIM // DIL_UNROLL, group, 0)

        for ci in range(s // chunk):
            sl = pl.ds(ci * chunk, chunk)
            dq_ref[sl, :] = dq_s[sl, :].astype(dq_ref.dtype)
            dk_ref[sl, :] = dk_s[sl, :].astype(dk_ref.dtype)
            dv_ref[sl, :] = dv_s[sl, :].astype(dv_ref.dtype)

    base = 3 * n_sb
    full = pl.BlockSpec((s, HEAD_DIM), lambda h: (0, h))
    tab = pl.BlockSpec((None, 3, HEAD_DIM, 2 * HEAD_DIM), lambda h: (h, 0, 0, 0))
    shp = jax.ShapeDtypeStruct((s, n_dl * HEAD_DIM), BF16)
    return pl.pallas_call(
        body, grid=(n_dl,),
        in_specs=[pl.BlockSpec((s, HEAD_DIM), lambda h: (0, base + h)),
                  pl.BlockSpec((s, HEAD_DIM), lambda h: (0, base + n_dl + h)),
                  pl.BlockSpec((s, HEAD_DIM), lambda h: (0, base + 2 * n_dl + h)),
                  full, full, full, tab],
        out_specs=[full, full, full, tab],
        out_shape=[shp, shp, shp, jax.ShapeDtypeStruct((n_dl, 3, HEAD_DIM, 2 * HEAD_DIM), F32)],
        scratch_shapes=[pltpu.VMEM((s, HEAD_DIM), F32)] * 3,
        compiler_params=_params(("parallel",)), name="dil_bwd",
    )(proj, proj, proj, do, o, lse, bm)


def _rel_bias_grad(ds_all, buckets):
    depth, n_dl = ds_all.shape[:2]
    rows = -(-n_dl // 8) * 8

    def body(ds_ref, bk_ref, o_ref):
        lane = lax.broadcasted_iota(jnp.int32, (1, LANE), 1)

        def one_bucket(bkt, acc):
            fb = bkt.astype(F32)
            out = []
            for h in range(n_dl):
                val = jnp.zeros((1, 1), F32)
                for b in range(3):
                    tot = ds_ref[0, h, b]
                    for l in range(1, depth):
                        tot = tot + ds_ref[l, h, b]
                    val = val + jnp.sum(jnp.where(bk_ref[b] == fb, tot, 0.0), keepdims=True)
                out.append(jnp.where(lane == bkt, val, 0.0))
            out += [jnp.zeros((1, LANE), F32)] * (rows - n_dl)
            return acc + jnp.concatenate(out, axis=0)

        o_ref[...] = lax.fori_loop(0, NUM_BUCKETS, one_bucket, jnp.zeros((rows, LANE), F32))

    return pl.pallas_call(
        body, out_shape=jax.ShapeDtypeStruct((rows, LANE), F32),
        in_specs=[pl.BlockSpec(memory_space=pltpu.VMEM)] * 2, out_specs=pl.BlockSpec(memory_space=pltpu.VMEM),
        compiler_params=_params(), name="rel_bias_grad",
    )(ds_all, buckets)


def _place():
    return lax.axis_index("x"), lax.axis_index("y"), lax.axis_index("c")


def _flip(v, bit):
    return 1 - v if bit else v


HBM_SPEC = pl.BlockSpec(memory_space=pl.ANY)


HBM_ONLY = pl.BlockSpec(memory_space=pltpu.HBM)
SEM_SPEC = pl.BlockSpec(memory_space=pltpu.SEMAPHORE)
DATAFLOW = pltpu.SideEffectType.DATAFLOW_SIDE_EFFECTING


def _in_hbm(a):
    return pltpu.with_memory_space_constraint(a, pltpu.HBM)


def _split_start(arrays, token, copies_of, n_sem, name):
    na = len(arrays)

    def body(*refs):
        for cp in copies_of(refs[:na], refs[na + 1], refs[na + 2]):
            cp.start()
        refs[-1][...] = jnp.zeros_like(refs[-1])

    sems = pltpu.SemaphoreType.DMA((n_sem,))
    res = pl.pallas_call(
        body, name=name,
        out_shape=(sems, sems, *[pltpu.HBM(a.shape, a.dtype) for a in arrays], jax.ShapeDtypeStruct((8, LANE), F32)),
        in_specs=[HBM_ONLY] * na + [HBM_SPEC],
        out_specs=(SEM_SPEC, SEM_SPEC, *[HBM_ONLY] * na, pl.BlockSpec(memory_space=pltpu.VMEM)),
        input_output_aliases={i: 2 + i for i in range(na)},
        compiler_params=pltpu.CompilerParams(has_side_effects=DATAFLOW),
    )(*[_in_hbm(a) for a in arrays], token)
    return res[0], res[1], res[2:2 + na], res[-1]


def _split_wait(started, after, copies_of, name):
    send, recv, arrays, _ = started
    na = len(arrays)

    def body(*refs):
        for cp in copies_of(refs[:na], refs[na], refs[na + 1]):
            cp.wait_send()
            cp.wait_recv()

    return pl.pallas_call(
        body, name=name, out_shape=[pltpu.HBM(a.shape, a.dtype) for a in arrays],
        in_specs=[HBM_ONLY] * na + [SEM_SPEC, SEM_SPEC, HBM_SPEC], out_specs=[HBM_ONLY] * na,
        input_output_aliases={i: i for i in range(na)},
        compiler_params=pltpu.CompilerParams(has_side_effects=DATAFLOW),
    )(*arrays, send, recv, after)


def _gather_ici_copies(buf, send, recv):
    x, y, c = _place()
    out = []
    for w in range(len(buf)):
        h = buf[w].shape[1] // 2
        mine = buf[w].at[2 * x + y, pl.ds(c * h, h)]
        for k in (1, 2, 3):
            out.append(pltpu.make_async_remote_copy(
                src_ref=mine, dst_ref=mine, send_sem=send.at[4 * w + k], recv_sem=recv.at[4 * w + k],
                device_id=(_flip(x, k >> 1), _flip(y, k & 1), c), device_id_type=MESH))
    return out


def _gather_d2d_copies(buf, send, recv):
    x, y, c = _place()
    out = []
    for w in range(len(buf)):
        h = buf[w].shape[1] // 2
        for k in (1, 2, 3):
            got = buf[w].at[2 * _flip(x, k >> 1) + _flip(y, k & 1), pl.ds(c * h, h)]
            out.append(pltpu.make_async_remote_copy(
                src_ref=got, dst_ref=got, send_sem=send.at[4 * w + k], recv_sem=recv.at[4 * w + k],
                device_id=(x, y, 1 - c), device_id_type=MESH))
    return out


def _scatter_copies(refs, send, recv):
    nw = len(refs) // 2
    src, buf = refs[:nw], refs[nw:]
    x, y, c = _place()
    out = []
    for w in range(nw):
        h = src[w].shape[1] // 2
        for k in range(1, N_DEV):
            px, py, pc = _flip(x, k >> 2), _flip(y, (k >> 1) & 1), _flip(c, k & 1)
            out.append(pltpu.make_async_remote_copy(
                src_ref=src[w].at[2 * px + py, pl.ds(pc * h, h)], dst_ref=buf[w].at[k - 1],
                send_sem=send.at[N_DEV * w + k], recv_sem=recv.at[N_DEV * w + k], device_id=(px, py, pc),
                device_id_type=MESH))
    return out


def _join_halves(grads, layer):
    nw = len(grads)

    def body(*refs):
        ins, outs = refs[:nw], refs[nw:2 * nw]
        send, recv = refs[2 * nw:]
        x, y, c = _place()
        remote = []
        for w in range(nw):
            h = ins[w].shape[1] // 2
            hc = h // JOIN_CHUNKS
            for j in range(JOIN_CHUNKS):
                rows = pl.ds(c * h + j * hc, hc)
                cp = pltpu.make_async_remote_copy(
                    src_ref=ins[w].at[layer, rows], dst_ref=outs[w].at[layer, rows],
                    send_sem=send.at[w, j], recv_sem=recv.at[w, j], device_id=(x, y, 1 - c), device_id_type=MESH)
                cp.start()
                remote.append(cp)
        for w in range(nw):
            h = ins[w].shape[1] // 2
            hc = h // JOIN_CHUNKS
            for j in range(JOIN_CHUNKS):
                theirs = outs[w].at[layer, pl.ds((1 - c) * h + j * hc, hc)]
                pltpu.make_async_remote_copy(
                    src_ref=theirs, dst_ref=theirs, send_sem=send.at[w, j], recv_sem=recv.at[w, j],
                    device_id=(x, y, c), device_id_type=MESH).wait_recv()
        for cp in remote:
            cp.wait_send()

    sems = [pltpu.SemaphoreType.DMA((nw, JOIN_CHUNKS))] * 2
    return pl.pallas_call(
        body, out_shape=[jax.ShapeDtypeStruct(a.shape, a.dtype) for a in grads],
        in_specs=[HBM_SPEC] * nw, out_specs=[HBM_SPEC] * nw, scratch_shapes=sems,
        input_output_aliases={i: i for i in range(nw)}, name="join_halves",
    )(*grads)


def _allreduce_small(v):
    rows, c = v.shape

    def body(v_ref, o_ref, buf, local_sem, send, recv):
        x, y, cc = _place()
        me = 4 * x + 2 * y + cc
        own = pltpu.make_async_copy(v_ref, buf.at[me], local_sem)
        own.start()
        sends = []
        for k in range(1, N_DEV):
            px, py, pc = _flip(x, k >> 2), _flip(y, (k >> 1) & 1), _flip(cc, k & 1)
            cp = pltpu.make_async_remote_copy(
                src_ref=v_ref, dst_ref=buf.at[me], send_sem=send.at[k], recv_sem=recv.at[k],
                device_id=(px, py, pc), device_id_type=MESH)
            cp.start()
            sends.append(cp)
        for k in range(1, N_DEV):
            px, py, pc = _flip(x, k >> 2), _flip(y, (k >> 1) & 1), _flip(cc, k & 1)
            slot = buf.at[4 * px + 2 * py + pc]
            pltpu.make_async_remote_copy(
                src_ref=slot, dst_ref=slot, send_sem=send.at[k], recv_sem=recv.at[k],
                device_id=(x, y, cc), device_id_type=MESH).wait_recv()
        for cp in sends:
            cp.wait_send()
        own.wait()
        acc = buf[0]
        for i in range(1, N_DEV):
            acc = acc + buf[i]
        o_ref[...] = acc

    return pl.pallas_call(
        body, out_shape=jax.ShapeDtypeStruct((rows, c), F32),
        in_specs=[pl.BlockSpec(memory_space=pltpu.VMEM)], out_specs=pl.BlockSpec(memory_space=pltpu.VMEM),
        scratch_shapes=[pltpu.VMEM((N_DEV, rows, c), F32), pltpu.SemaphoreType.DMA,
                        pltpu.SemaphoreType.DMA((N_DEV,)), pltpu.SemaphoreType.DMA((N_DEV,))],
        compiler_params=_params(),
        name="allreduce_small",
    )(v)


def kernel(x, p, ln_mix_pre, w_in, ln_head, w_out, ln_mix_post, rel_bias, ln_ffn_pre, w_gate_up, w_down, ln_ffn_post, ln_pli, w_pli_gate, w_pli_proj, loss_target, m_ln_mix_pre, m_w_in, m_ln_head, m_w_out, m_ln_mix_post, m_rel_bias, m_ln_ffn_pre, m_w_gate_up, m_w_down, m_ln_ffn_post, m_ln_pli, m_w_pli_gate, m_w_pli_proj, v_ln_mix_pre, v_w_in, v_ln_head, v_w_out, v_ln_mix_post, v_rel_bias, v_ln_ffn_pre, v_w_gate_up, v_w_down, v_ln_ffn_post, v_ln_pli, v_w_pli_gate, v_w_pli_proj):
    weights = dict(ln_mix_pre=ln_mix_pre, w_in=w_in, ln_head=ln_head, w_out=w_out, ln_mix_post=ln_mix_post,
                   rel_bias=rel_bias, ln_ffn_pre=ln_ffn_pre, w_gate_up=w_gate_up, w_down=w_down,
                   ln_ffn_post=ln_ffn_post, ln_pli=ln_pli, w_pli_gate=w_pli_gate, w_pli_proj=w_pli_proj)
    mom1 = dict(ln_mix_pre=m_ln_mix_pre, w_in=m_w_in, ln_head=m_ln_head, w_out=m_w_out, ln_mix_post=m_ln_mix_post,
                rel_bias=m_rel_bias, ln_ffn_pre=m_ln_ffn_pre, w_gate_up=m_w_gate_up, w_down=m_w_down,
                ln_ffn_post=m_ln_ffn_post, ln_pli=m_ln_pli, w_pli_gate=m_w_pli_gate, w_pli_proj=m_w_pli_proj)
    mom2 = dict(ln_mix_pre=v_ln_mix_pre, w_in=v_w_in, ln_head=v_ln_head, w_out=v_w_out, ln_mix_post=v_ln_mix_post,
                rel_bias=v_rel_bias, ln_ffn_pre=v_ln_ffn_pre, w_gate_up=v_w_gate_up, w_down=v_w_down,
                ln_ffn_post=v_ln_ffn_post, ln_pli=v_ln_pli, w_pli_gate=v_w_pli_gate, w_pli_proj=v_w_pli_proj)

    _, seq, d_model = x.shape
    depth = w_in.shape[0]
    n_heads = d_model // HEAD_DIM
    n_sb = n_heads // 2
    n_dl = n_heads - n_sb
    assert seq % (HEAD_DIM * DILATIONS[-1]) == 0 and d_model % (2 * HEAD_DIM) == 0
    bq = 256
    tr = 256
    tr_ff = 256

    xs = x[0]
    target = loss_target[0]
    gain = {n: [weights[n][l][None, :] for l in range(depth)] for n in SMALL}
    bias_mask, buckets = _dil_tables(rel_bias)

    zero_token = jnp.zeros((8, LANE), F32)
    token = zero_token
    gathers = []
    for l in range(depth):
        parts = []
        for names in ((FIRST_USED, REST) if l == 0 else (BIG,)):
            slots = [_cast_layer(weights[n], l, n) for n in names]
            tag = f"{l}" if names is BIG else f"{l}_{names[0]}"
            parts.append((names, tag, _split_start(slots, token, _gather_ici_copies, 4 * len(names),
                                                   f"gather_start_{tag}")))
            token = parts[-1][2][3]
        gathers.append(parts)

    def gather_pass_on(part, after):
        names, tag, started = part
        arrays = _split_wait(started, after, _gather_ici_copies, f"gather_wait_{tag}")
        started = _split_start(arrays, zero_token, _gather_d2d_copies, 4 * len(names), f"gather_pass_{tag}")
        return names, tag, started, started[3]

    def gather_done(passed, after):
        names, tag, started, _ = passed
        return dict(zip(names, _split_wait(started, after, _gather_d2d_copies, f"gather_done_{tag}")))

    saved = []
    h1 = _norm_in(xs, gain["ln_mix_pre"][0], tr)
    xin = xs
    passed = gather_pass_on(gathers[0][0], token)
    wg = gather_done(passed, passed[3])
    for l in range(depth):
        proj = _mm_fwd(h1, wg["w_in"], True, "mm_in", tn_cap=768, tk_cap=2048)
        o_sb, lt_sb = _sb_fwd(proj, n_sb, bq)
        if l == 0:
            passed = gather_pass_on(gathers[0][1], o_sb)
            o_dl, lse_dl = _dil_fwd(proj, bias_mask, n_sb, n_dl, (passed[3],))
            wg.update(gather_done(passed, o_dl))
        else:
            o_dl, lse_dl = _dil_fwd(proj, bias_mask, n_sb, n_dl)
        on = _headnorm(o_sb, o_dl, gain["ln_head"][l], tr)
        y = _mm_fwd(on, wg["w_out"], False, "mm_out", tn_cap=1024, tk_cap=2048)
        x2, h2 = _res_norm(xin, y, gain["ln_mix_post"][l], gain["ln_ffn_pre"][l], "post_attn", tr)
        gu = _mm_fwd(h2, wg["w_gate_up"], True, "mm_gate_up", tk_cap=2048, out_dtype=BF16)
        act = _swiglu(gu, tr_ff)
        f = _mm_fwd(act, wg["w_down"], False, "mm_down", tn_cap=1024, tk_cap=1536)
        passed = gather_pass_on(gathers[l + 1][0], f) if l + 1 < depth else None
        x3, h3 = _res_norm(x2, f, gain["ln_ffn_post"][l], gain["ln_pli"][l], "post_ffn", tr,
                           () if passed is None else (passed[3],))
        gl = _mm_fwd(h3, wg["w_pli_gate"], False, "mm_pli_gate", tn_cap=1024, tk_cap=2048)
        pl_in = p[l, 0]
        pp = _mm_fwd(pl_in, wg["w_pli_proj"], True, "mm_pli_proj")
        saved.append(dict(wg=wg, x=xin, h1=h1, proj=proj, o_sb=o_sb, lt_sb=lt_sb, o_dl=o_dl, lse_dl=lse_dl, on=on, y=y, x2=x2,
                          h2=h2, gu=gu, act=act, f=f, x3=x3, h3=h3, gl=gl, pp=pp, p=pl_in))
        if l + 1 < depth:
            xin, h1 = _pli_out(x3, gl, pp, gain["ln_mix_pre"][l + 1], tr)
            wg = gather_done(passed, pp)
        else:
            dx, loss_part = _loss_head(x3, gl, pp, target, tr)

    grad_big = {n: None for n in BIG}
    grad_gain = {n: [None] * depth for n in SMALL}
    ds_layers = [None] * depth

    def start_scatter(layer, names, dw):
        dws = [dw[n] for n in names]
        landing = [lax.empty((N_DEV - 1, a.shape[1] // 2, a.shape[2]), a.dtype) for a in dws]
        tag = f"{layer}" if names is BIG else f"{layer}_{names[0]}"
        return layer, names, tag, _split_start(dws + landing, zero_token, _scatter_copies, N_DEV * len(names),
                                               f"scatter_start_{tag}")

    def reduce_layer(layer, names, tag, started, after):
        arrays = _split_wait(started, after, _scatter_copies, f"scatter_wait_{tag}")
        nw = len(names)
        halves = [_reduce_piece(arrays[w], arrays[nw + w], names[w], layer, depth, grad_big[names[w]])
                  for w in range(nw)]
        joined = _join_halves(halves, layer)
        grad_big.update(zip(names, joined))
        return joined[0]

    pending = None
    for l in reversed(range(depth)):
        sv = saved[l]
        wg = sv["wg"]
        dpp, dgl = _pli_bwd(dx, sv["gl"], sv["pp"], tr, () if pending is None else (pending[3][3],))
        dw = {}
        dw["w_pli_proj"] = _mm_wgrad(sv["p"], dpp, True, "wg_pli_proj")
        dw["w_pli_gate"] = _mm_wgrad(sv["h3"], dgl, False, "wg_pli_gate")
        dh3 = _mm_dgrad(dgl, wg["w_pli_gate"], False, "dg_pli_gate", to_cap=2048)
        dx3, df, grad_gain["ln_pli"][l], grad_gain["ln_ffn_post"][l] = _res_norm_bwd(
            dx, dh3, sv["x3"], sv["f"], gain["ln_pli"][l], gain["ln_ffn_post"][l], "post_ffn_bwd", tr)
        dw["w_down"] = _mm_wgrad(sv["act"], df, False, "wg_down")
        dact = _mm_dgrad(df, wg["w_down"], False, "dg_down")
        dgu = _swiglu_bwd(dact, sv["gu"], tr_ff)
        dw["w_gate_up"] = _mm_wgrad(sv["h2"], dgu, True, "wg_gate_up")
        dh2 = _mm_dgrad(dgu, wg["w_gate_up"], True, "dg_gate_up", tc_cap=2816)
        dx2, dy, grad_gain["ln_ffn_pre"][l], grad_gain["ln_mix_post"][l] = _res_norm_bwd(
            dx3, dh2, sv["x2"], sv["y"], gain["ln_ffn_pre"][l], gain["ln_mix_post"][l], "post_attn_bwd", tr)
        dw["w_out"] = _mm_wgrad(sv["on"], dy, False, "wg_out")
        don = _mm_dgrad(dy, wg["w_out"], False, "dg_out", to_cap=2048)
        early = start_scatter(l, REST, dw) if l == 0 else None
        do_sb, do_dl, grad_gain["ln_head"][l] = _headnorm_bwd(
            don, sv["o_sb"], sv["o_dl"], gain["ln_head"][l], tr, () if early is None else (early[3][3],))
        dq_s, dk_s, dv_s = _sb_bwd(sv["proj"], sv["lt_sb"], do_sb, n_sb, bq)
        dq_d, dk_d, dv_d, ds_layers[l] = _dil_bwd(sv["proj"], do_dl, sv["o_dl"], sv["lse_dl"], bias_mask, n_sb, n_dl)
        dproj = jnp.concatenate([dq_s, dk_s, dv_s, dq_d, dk_d, dv_d], axis=1)
        dw["w_in"] = _mm_wgrad(sv["h1"], dproj, True, "wg_in")
        dh1 = _mm_dgrad(dproj, wg["w_in"], True, "dg_in", tc_cap=1536)
        dx, grad_gain["ln_mix_pre"][l] = _norm_in_bwd(dx2, dh1, sv["x"], gain["ln_mix_pre"][l], tr)
        late = start_scatter(l, BIG if l > 0 else FIRST_USED, dw)
        if pending is not None:
            reduce_layer(*pending, dx if l > 0 else late[3][3])
        pending = late
    done = reduce_layer(*early, pending[3][3])
    reduce_layer(*pending, done)

    db = _rel_bias_grad(jnp.stack(ds_layers, axis=0), buckets)
    rb_flat = db[:n_dl, :NUM_BUCKETS].T.reshape(1, NUM_BUCKETS * n_dl)
    def widen(v):
        return jnp.pad(v, ((0, 0), (0, d_model - v.shape[1])))
    small_rows = [grad_gain[n][l] for n in SMALL for l in range(depth)] + [widen(rb_flat), widen(loss_part)]
    n_rows = len(small_rows)
    small = jnp.concatenate(small_rows + [jnp.zeros((-n_rows % 8, d_model), F32)], axis=0)
    total = _allreduce_small(small)
    grads = {}
    for i, n in enumerate(SMALL):
        grads[n] = total[i * depth:(i + 1) * depth]
    grads["rel_bias"] = total[len(SMALL) * depth, :NUM_BUCKETS * n_dl].reshape(NUM_BUCKETS, n_dl)
    loss = (0.5 / d_model) * jnp.sum(total[len(SMALL) * depth + 1, :LANE])
    for n in BIG:
        grads[n] = grad_big[n]

    delta, new_m, new_v = {}, {}, {}
    for n in WEIGHTS:
        delta[n], new_m[n], new_v[n], grads[n] = _adamw(weights[n], grads[n], mom1[n], mom2[n], n)
    return (loss, dx[None], *[grads[n] for n in WEIGHTS], *[delta[n] for n in WEIGHTS],
            *[new_m[n] for n in WEIGHTS], *[new_v[n] for n in WEIGHTS])
```
